```python
import jax, jax.numpy as jnp
from jax import lax
import numpy as np

D_MODEL = 2048
BATCH = 8
SEQ = 2048
DEPTH = 1

N_META = 16
BLOCK_Q = 128
CONV_CH = 2048
CONV_WIDTH = 31
N_HEADS = 16
Q_LORA = 512
KV_LORA = 512
QK_NOPE = 128
QK_ROPE = 64
V_HEAD = 128
ROPE_THETA = 10000.0
D_FF = -(-8 * D_MODEL // (3 * 256)) * 256
EPS = 1e-6
SPLITS = (2 * CONV_CH, Q_LORA, KV_LORA, QK_ROPE, 2 * D_MODEL)
IN_COLS = sum(SPLITS)
SPLIT_IDX = tuple(int(v) for v in np.cumsum(SPLITS)[:-1])

kernel_name = "hybrid_conformer_conv_mla_gated_block"


def rmsnorm(x, g):
    xf = x.astype(jnp.float32)
    y = xf * lax.rsqrt(jnp.mean(xf * xf, axis=-1, keepdims=True) + EPS)
    return y.astype(x.dtype) * g


def layernorm(x, g, b):
    xf = x.astype(jnp.float32)
    mu = jnp.mean(xf, axis=-1, keepdims=True)
    var = jnp.mean(jnp.square(xf - mu), axis=-1, keepdims=True)
    return ((xf - mu) * lax.rsqrt(var + EPS)).astype(x.dtype) * g + b


def rope(x, cos, sin):
    x1, x2 = jnp.split(x, 2, axis=-1)
    return jnp.concatenate([x1 * cos - x2 * sin, x1 * sin + x2 * cos], axis=-1)


def causal_depthwise_conv(x, w):
    return lax.conv_general_dilated(
        x, w[:, None, :], window_strides=(1,), padding=[(CONV_WIDTH - 1, 0)],
        dimension_numbers=("NWC", "WIO", "NWC"), feature_group_count=x.shape[-1])


def mla_attention(q_nope, q_rope, k_nope, k_rope, v):
    b, lp = q_nope.shape[0], q_nope.shape[1]
    n_blocks = lp // BLOCK_Q
    scale = (QK_NOPE + QK_ROPE) ** -0.5
    kpos = jnp.arange(lp)

    def one_block(i):
        start = i * BLOCK_Q
        qn = lax.dynamic_slice_in_dim(q_nope, start, BLOCK_Q, axis=1)
        qr = lax.dynamic_slice_in_dim(q_rope, start, BLOCK_Q, axis=1)
        s = (jnp.einsum("bqhd,bkhd->bhqk", qn, k_nope)
             + jnp.einsum("bqhr,bkr->bhqk", qr, k_rope)).astype(jnp.float32) * scale
        qpos = start + jnp.arange(BLOCK_Q)
        s = jnp.where(kpos[None, :] <= qpos[:, None], s, -jnp.inf)
        p = jax.nn.softmax(s, axis=-1).astype(v.dtype)
        return jnp.einsum("bhqk,bkhd->bqhd", p, v)

    o = lax.map(one_block, jnp.arange(n_blocks))
    return jnp.moveaxis(o, 0, 1).reshape(b, lp, N_HEADS * V_HEAD)


def _fwd_setup_inputs(seed: int = 0) -> dict:
    key = jax.random.key(seed)
    ks = iter(jax.random.split(key, 32))
    f32 = jnp.float32

    def w(shape, fan_in):
        return jax.random.normal(next(ks), shape, f32) * (fan_in ** -0.5)

    def gain(shape):
        return 1.0 + 0.02 * jax.random.normal(next(ks), shape, f32)

    def bias(shape):
        return 0.02 * jax.random.normal(next(ks), shape, f32)

    Ld = DEPTH
    return {
        "x": jax.random.normal(next(ks), (BATCH, SEQ, D_MODEL), f32),
        "meta_tokens": jax.random.normal(next(ks), (N_META, D_MODEL), f32),
        "g_mix": gain((Ld, D_MODEL)),
        "w_in": w((Ld, D_MODEL, IN_COLS), D_MODEL),
        "b_glu": bias((Ld, 2 * CONV_CH)),
        "b_gate": bias((Ld, 2 * D_MODEL)),
        "w_dw": w((Ld, CONV_WIDTH, CONV_CH), CONV_WIDTH),
        "b_dw": bias((Ld, CONV_CH)),
        "g_conv_ln": gain((Ld, CONV_CH)),
        "b_conv_ln": bias((Ld, CONV_CH)),
        "w_conv_out": w((Ld, CONV_CH, D_MODEL), CONV_CH),
        "b_conv_out": bias((Ld, D_MODEL)),
        "g_q_lora": gain((Ld, Q_LORA)),
        "w_uq": w((Ld, Q_LORA, N_HEADS * (QK_NOPE + QK_ROPE)), Q_LORA),
        "g_kv_lora": gain((Ld, KV_LORA)),
        "w_uk": w((Ld, KV_LORA, N_HEADS * QK_NOPE), KV_LORA),
        "w_uv": w((Ld, KV_LORA, N_HEADS * V_HEAD), KV_LORA),
        "w_attn_out": w((Ld, N_HEADS * V_HEAD, D_MODEL), N_HEADS * V_HEAD),
        "w_out": w((Ld, D_MODEL, D_MODEL), D_MODEL),
        "g_ffn": gain((Ld, D_MODEL)),
        "w_ffn_gate": w((Ld, D_MODEL, D_FF), D_MODEL),
        "w_ffn_up": w((Ld, D_MODEL, D_FF), D_MODEL),
        "w_ffn_down": w((Ld, D_FF, D_MODEL), D_FF),
        "g_final": gain((D_MODEL,)),
    }


def _fwd_reference(x, meta_tokens, g_mix, w_in, b_glu, b_gate, w_dw, b_dw, g_conv_ln, b_conv_ln,
              w_conv_out, b_conv_out, g_q_lora, w_uq, g_kv_lora, w_uk, w_uv, w_attn_out,
              w_out, g_ffn, w_ffn_gate, w_ffn_up, w_ffn_down, g_final):
    b, seq, d = x.shape
    length = N_META + seq
    lp = -(-length // BLOCK_Q) * BLOCK_Q
    meta = jnp.broadcast_to(meta_tokens[None].astype(x.dtype), (b, N_META, d))
    h = jnp.concatenate([meta, x], axis=1)
    h = jnp.pad(h, ((0, 0), (0, lp - length), (0, 0)))

    pos = jnp.arange(lp, dtype=jnp.float32)
    inv_freq = ROPE_THETA ** (-jnp.arange(0, QK_ROPE, 2, dtype=jnp.float32) / QK_ROPE)
    ang = pos[:, None] * inv_freq[None, :]
    cos, sin = jnp.cos(ang).astype(h.dtype), jnp.sin(ang).astype(h.dtype)

    for l in range(DEPTH):
        u = rmsnorm(h, g_mix[l])
        z = u @ w_in[l]
        z_glu, z_cq, z_ckv, z_kr, z_gate = jnp.split(z, SPLIT_IDX, axis=-1)

        za, zb = jnp.split(z_glu + b_glu[l], 2, axis=-1)
        c = za * jax.nn.sigmoid(zb)
        c = causal_depthwise_conv(c, w_dw[l]) + b_dw[l]
        c = jax.nn.silu(layernorm(c, g_conv_ln[l], b_conv_ln[l]))
        y_conv = c @ w_conv_out[l] + b_conv_out[l]

        cq = rmsnorm(z_cq, g_q_lora[l])
        q = (cq @ w_uq[l]).reshape(b, lp, N_HEADS, QK_NOPE + QK_ROPE)
        q_nope = q[..., :QK_NOPE]
        q_rope = rope(q[..., QK_NOPE:], cos[:, None, :], sin[:, None, :])
        ckv = rmsnorm(z_ckv, g_kv_lora[l])
        k_nope = (ckv @ w_uk[l]).reshape(b, lp, N_HEADS, QK_NOPE)
        v = (ckv @ w_uv[l]).reshape(b, lp, N_HEADS, V_HEAD)
        k_rope = rope(z_kr, cos, sin)
        y_attn = mla_attention(q_nope, q_rope, k_nope, k_rope, v) @ w_attn_out[l]

        g_c, g_a = jnp.split(jax.nn.sigmoid(z_gate + b_gate[l]), 2, axis=-1)
        h = h + (g_c * y_conv + g_a * y_attn) @ w_out[l]

        hn = rmsnorm(h, g_ffn[l])
        h = h + (jax.nn.silu(hn @ w_ffn_gate[l]) * (hn @ w_ffn_up[l])) @ w_ffn_down[l]

    h = rmsnorm(h, g_final)
    return h[:, N_META:N_META + seq]


import jax as _jax
import jax.numpy as _jnp

TWIN_FORMAT = 'train_step'
FWD_PARAMS = ['x', 'meta_tokens', 'g_mix', 'w_in', 'b_glu', 'b_gate', 'w_dw', 'b_dw', 'g_conv_ln', 'b_conv_ln', 'w_conv_out', 'b_conv_out', 'g_q_lora', 'w_uq', 'g_kv_lora', 'w_uk', 'w_uv', 'w_attn_out', 'w_out', 'g_ffn', 'w_ffn_gate', 'w_ffn_up', 'w_ffn_down', 'g_final']
TWIN_WEIGHTS = ['meta_tokens', 'g_mix', 'w_in', 'b_glu', 'b_gate', 'w_dw', 'b_dw', 'g_conv_ln', 'b_conv_ln', 'w_conv_out', 'b_conv_out', 'g_q_lora', 'w_uq', 'g_kv_lora', 'w_uk', 'w_uv', 'w_attn_out', 'w_out', 'g_ffn', 'w_ffn_gate', 'w_ffn_up', 'w_ffn_down', 'g_final']
TWIN_DIFF_INPUT = 'x'
TWIN_INPUTS = ['x', 'meta_tokens', 'g_mix', 'w_in', 'b_glu', 'b_gate', 'w_dw', 'b_dw', 'g_conv_ln', 'b_conv_ln', 'w_conv_out', 'b_conv_out', 'g_q_lora', 'w_uq', 'g_kv_lora', 'w_uk', 'w_uv', 'w_attn_out', 'w_out', 'g_ffn', 'w_ffn_gate', 'w_ffn_up', 'w_ffn_down', 'g_final', 'loss_target', 'm_meta_tokens', 'm_g_mix', 'm_w_in', 'm_b_glu', 'm_b_gate', 'm_w_dw', 'm_b_dw', 'm_g_conv_ln', 'm_b_conv_ln', 'm_w_conv_out', 'm_b_conv_out', 'm_g_q_lora', 'm_w_uq', 'm_g_kv_lora', 'm_w_uk', 'm_w_uv', 'm_w_attn_out', 'm_w_out', 'm_g_ffn', 'm_w_ffn_gate', 'm_w_ffn_up', 'm_w_ffn_down', 'm_g_final', 'v_meta_tokens', 'v_g_mix', 'v_w_in', 'v_b_glu', 'v_b_gate', 'v_w_dw', 'v_b_dw', 'v_g_conv_ln', 'v_b_conv_ln', 'v_w_conv_out', 'v_b_conv_out', 'v_g_q_lora', 'v_w_uq', 'v_g_kv_lora', 'v_w_uk', 'v_w_uv', 'v_w_attn_out', 'v_w_out', 'v_g_ffn', 'v_w_ffn_gate', 'v_w_ffn_up', 'v_w_ffn_down', 'v_g_final']
TWIN_OUTPUTS = ['loss', 'grad_x', 'grad_meta_tokens', 'grad_g_mix', 'grad_w_in', 'grad_b_glu', 'grad_b_gate', 'grad_w_dw', 'grad_b_dw', 'grad_g_conv_ln', 'grad_b_conv_ln', 'grad_w_conv_out', 'grad_b_conv_out', 'grad_g_q_lora', 'grad_w_uq', 'grad_g_kv_lora', 'grad_w_uk', 'grad_w_uv', 'grad_w_attn_out', 'grad_w_out', 'grad_g_ffn', 'grad_w_ffn_gate', 'grad_w_ffn_up', 'grad_w_ffn_down', 'grad_g_final', 'delta_meta_tokens', 'delta_g_mix', 'delta_w_in', 'delta_b_glu', 'delta_b_gate', 'delta_w_dw', 'delta_b_dw', 'delta_g_conv_ln', 'delta_b_conv_ln', 'delta_w_conv_out', 'delta_b_conv_out', 'delta_g_q_lora', 'delta_w_uq', 'delta_g_kv_lora', 'delta_w_uk', 'delta_w_uv', 'delta_w_attn_out', 'delta_w_out', 'delta_g_ffn', 'delta_w_ffn_gate', 'delta_w_ffn_up', 'delta_w_ffn_down', 'delta_g_final', 'new_m_meta_tokens', 'new_m_g_mix', 'new_m_w_in', 'new_m_b_glu', 'new_m_b_gate', 'new_m_w_dw', 'new_m_b_dw', 'new_m_g_conv_ln', 'new_m_b_conv_ln', 'new_m_w_conv_out', 'new_m_b_conv_out', 'new_m_g_q_lora', 'new_m_w_uq', 'new_m_g_kv_lora', 'new_m_w_uk', 'new_m_w_uv', 'new_m_w_attn_out', 'new_m_w_out', 'new_m_g_ffn', 'new_m_w_ffn_gate', 'new_m_w_ffn_up', 'new_m_w_ffn_down', 'new_m_g_final', 'new_v_meta_tokens', 'new_v_g_mix', 'new_v_w_in', 'new_v_b_glu', 'new_v_b_gate', 'new_v_w_dw', 'new_v_b_dw', 'new_v_g_conv_ln', 'new_v_b_conv_ln', 'new_v_w_conv_out', 'new_v_b_conv_out', 'new_v_g_q_lora', 'new_v_w_uq', 'new_v_g_kv_lora', 'new_v_w_uk', 'new_v_w_uv', 'new_v_w_attn_out', 'new_v_w_out', 'new_v_g_ffn', 'new_v_w_ffn_gate', 'new_v_w_ffn_up', 'new_v_w_ffn_down', 'new_v_g_final']
TWIN_LEAF_KINDS = {'loss': 'loss', 'grad_x': 'grad_x', 'grad_meta_tokens': 'grad_w', 'grad_g_mix': 'grad_w', 'grad_w_in': 'grad_w', 'grad_b_glu': 'grad_w', 'grad_b_gate': 'grad_w', 'grad_w_dw': 'grad_w', 'grad_b_dw': 'grad_w', 'grad_g_conv_ln': 'grad_w', 'grad_b_conv_ln': 'grad_w', 'grad_w_conv_out': 'grad_w', 'grad_b_conv_out': 'grad_w', 'grad_g_q_lora': 'grad_w', 'grad_w_uq': 'grad_w', 'grad_g_kv_lora': 'grad_w', 'grad_w_uk': 'grad_w', 'grad_w_uv': 'grad_w', 'grad_w_attn_out': 'grad_w', 'grad_w_out': 'grad_w', 'grad_g_ffn': 'grad_w', 'grad_w_ffn_gate': 'grad_w', 'grad_w_ffn_up': 'grad_w', 'grad_w_ffn_down': 'grad_w', 'grad_g_final': 'grad_w', 'delta_meta_tokens': 'delta_w', 'delta_g_mix': 'delta_w', 'delta_w_in': 'delta_w', 'delta_b_glu': 'delta_w', 'delta_b_gate': 'delta_w', 'delta_w_dw': 'delta_w', 'delta_b_dw': 'delta_w', 'delta_g_conv_ln': 'delta_w', 'delta_b_conv_ln': 'delta_w', 'delta_w_conv_out': 'delta_w', 'delta_b_conv_out': 'delta_w', 'delta_g_q_lora': 'delta_w', 'delta_w_uq': 'delta_w', 'delta_g_kv_lora': 'delta_w', 'delta_w_uk': 'delta_w', 'delta_w_uv': 'delta_w', 'delta_w_attn_out': 'delta_w', 'delta_w_out': 'delta_w', 'delta_g_ffn': 'delta_w', 'delta_w_ffn_gate': 'delta_w', 'delta_w_ffn_up': 'delta_w', 'delta_w_ffn_down': 'delta_w', 'delta_g_final': 'delta_w', 'new_m_meta_tokens': 'new_m', 'new_m_g_mix': 'new_m', 'new_m_w_in': 'new_m', 'new_m_b_glu': 'new_m', 'new_m_b_gate': 'new_m', 'new_m_w_dw': 'new_m', 'new_m_b_dw': 'new_m', 'new_m_g_conv_ln': 'new_m', 'new_m_b_conv_ln': 'new_m', 'new_m_w_conv_out': 'new_m', 'new_m_b_conv_out': 'new_m', 'new_m_g_q_lora': 'new_m', 'new_m_w_uq': 'new_m', 'new_m_g_kv_lora': 'new_m', 'new_m_w_uk': 'new_m', 'new_m_w_uv': 'new_m', 'new_m_w_attn_out': 'new_m', 'new_m_w_out': 'new_m', 'new_m_g_ffn': 'new_m', 'new_m_w_ffn_gate': 'new_m', 'new_m_w_ffn_up': 'new_m', 'new_m_w_ffn_down': 'new_m', 'new_m_g_final': 'new_m', 'new_v_meta_tokens': 'new_v', 'new_v_g_mix': 'new_v', 'new_v_w_in': 'new_v', 'new_v_b_glu': 'new_v', 'new_v_b_gate': 'new_v', 'new_v_w_dw': 'new_v', 'new_v_b_dw': 'new_v', 'new_v_g_conv_ln': 'new_v', 'new_v_b_conv_ln': 'new_v', 'new_v_w_conv_out': 'new_v', 'new_v_b_conv_out': 'new_v', 'new_v_g_q_lora': 'new_v', 'new_v_w_uq': 'new_v', 'new_v_g_kv_lora': 'new_v', 'new_v_w_uk': 'new_v', 'new_v_w_uv': 'new_v', 'new_v_w_attn_out': 'new_v', 'new_v_w_out': 'new_v', 'new_v_g_ffn': 'new_v', 'new_v_w_ffn_gate': 'new_v', 'new_v_w_ffn_up': 'new_v', 'new_v_w_ffn_down': 'new_v', 'new_v_g_final': 'new_v'}


def _forward(args):
    return _fwd_reference(*[args[k] for k in FWD_PARAMS])


def _output_shape():
    out = _jax.eval_shape(lambda: _forward(_fwd_setup_inputs(0)))
    return out.shape, out.dtype

N_MICROBATCH = 1
ADAM_LR = 0.001
ADAM_B1 = 0.9
ADAM_B2 = 0.999
ADAM_EPS = 1e-08
ADAM_WD = 0.01
ADAM_STEP = 10
PER_EXAMPLE_BATCH_AXIS = {'x': 0, 'loss_target': 0}
SHARED_INPUTS = []
_WEIGHT_DTYPES = {'meta_tokens': _jnp.float32, 'g_mix': _jnp.float32, 'w_in': _jnp.float32, 'b_glu': _jnp.float32, 'b_gate': _jnp.float32, 'w_dw': _jnp.float32, 'b_dw': _jnp.float32, 'g_conv_ln': _jnp.float32, 'b_conv_ln': _jnp.float32, 'w_conv_out': _jnp.float32, 'b_conv_out': _jnp.float32, 'g_q_lora': _jnp.float32, 'w_uq': _jnp.float32, 'g_kv_lora': _jnp.float32, 'w_uk': _jnp.float32, 'w_uv': _jnp.float32, 'w_attn_out': _jnp.float32, 'w_out': _jnp.float32, 'g_ffn': _jnp.float32, 'w_ffn_gate': _jnp.float32, 'w_ffn_up': _jnp.float32, 'w_ffn_down': _jnp.float32, 'g_final': _jnp.float32}
MOMENT_SCALE = {'meta_tokens': 1.749795e-03, 'g_mix': 2.697226e-02, 'w_in': 1.286991e-02, 'b_glu': 1.814976e-02, 'b_gate': 6.435698e-03, 'w_dw': 2.263514e-02, 'b_dw': 4.371163e-02, 'g_conv_ln': 2.729394e-02, 'b_conv_ln': 2.374018e-02, 'w_conv_out': 2.194224e-02, 'b_conv_out': 3.884110e-02, 'g_q_lora': 1.183461e-02, 'w_uq': 4.641952e-03, 'g_kv_lora': 1.452242e-02, 'w_uk': 4.724387e-03, 'w_uv': 6.101195e-03, 'w_attn_out': 6.094241e-03, 'w_out': 2.248317e-02, 'g_ffn': 4.488443e-02, 'w_ffn_gate': 1.920014e-02, 'w_ffn_up': 1.864754e-02, 'w_ffn_down': 3.092376e-02, 'g_final': 7.990496e+00}


def _to_microbatches(a, axis):
    t = _jnp.moveaxis(a, axis, 0)
    t = t.reshape((N_MICROBATCH, t.shape[0] // N_MICROBATCH) + t.shape[1:])
    return _jnp.moveaxis(t, 1, axis + 1)


def setup_inputs(seed: int = 0) -> dict:
    inp = _fwd_setup_inputs(seed)
    key = _jax.random.fold_in(_jax.random.key(seed), 7919)
    shape, _ = _output_shape()
    out = dict(inp)
    out["loss_target"] = _jax.random.normal(_jax.random.fold_in(key, 0), shape, _jnp.float32)
    for i, name in enumerate(TWIN_WEIGHTS):
        w = inp[name].astype(_jnp.float32)
        if MOMENT_SCALE is None:
            s = _jnp.sqrt(_jnp.mean(_jnp.square(w)) + 1e-30)
        else:
            s = MOMENT_SCALE[name]
        km, kv = _jax.random.split(_jax.random.fold_in(key, i + 1))
        out[name] = w
        out["m_" + name] = s * _jax.random.normal(km, w.shape, _jnp.float32)
        out["v_" + name] = (s * s) * _jax.random.uniform(kv, w.shape, _jnp.float32, 0.5, 1.5)
    if N_MICROBATCH > 1:
        for name, axis in PER_EXAMPLE_BATCH_AXIS.items():
            out[name] = _to_microbatches(out[name], axis)
    return {'x': out['x'], 'meta_tokens': out['meta_tokens'], 'g_mix': out['g_mix'], 'w_in': out['w_in'], 'b_glu': out['b_glu'], 'b_gate': out['b_gate'], 'w_dw': out['w_dw'], 'b_dw': out['b_dw'], 'g_conv_ln': out['g_conv_ln'], 'b_conv_ln': out['b_conv_ln'], 'w_conv_out': out['w_conv_out'], 'b_conv_out': out['b_conv_out'], 'g_q_lora': out['g_q_lora'], 'w_uq': out['w_uq'], 'g_kv_lora': out['g_kv_lora'], 'w_uk': out['w_uk'], 'w_uv': out['w_uv'], 'w_attn_out': out['w_attn_out'], 'w_out': out['w_out'], 'g_ffn': out['g_ffn'], 'w_ffn_gate': out['w_ffn_gate'], 'w_ffn_up': out['w_ffn_up'], 'w_ffn_down': out['w_ffn_down'], 'g_final': out['g_final'], 'loss_target': out['loss_target'], 'm_meta_tokens': out['m_meta_tokens'], 'm_g_mix': out['m_g_mix'], 'm_w_in': out['m_w_in'], 'm_b_glu': out['m_b_glu'], 'm_b_gate': out['m_b_gate'], 'm_w_dw': out['m_w_dw'], 'm_b_dw': out['m_b_dw'], 'm_g_conv_ln': out['m_g_conv_ln'], 'm_b_conv_ln': out['m_b_conv_ln'], 'm_w_conv_out': out['m_w_conv_out'], 'm_b_conv_out': out['m_b_conv_out'], 'm_g_q_lora': out['m_g_q_lora'], 'm_w_uq': out['m_w_uq'], 'm_g_kv_lora': out['m_g_kv_lora'], 'm_w_uk': out['m_w_uk'], 'm_w_uv': out['m_w_uv'], 'm_w_attn_out': out['m_w_attn_out'], 'm_w_out': out['m_w_out'], 'm_g_ffn': out['m_g_ffn'], 'm_w_ffn_gate': out['m_w_ffn_gate'], 'm_w_ffn_up': out['m_w_ffn_up'], 'm_w_ffn_down': out['m_w_ffn_down'], 'm_g_final': out['m_g_final'], 'v_meta_tokens': out['v_meta_tokens'], 'v_g_mix': out['v_g_mix'], 'v_w_in': out['v_w_in'], 'v_b_glu': out['v_b_glu'], 'v_b_gate': out['v_b_gate'], 'v_w_dw': out['v_w_dw'], 'v_b_dw': out['v_b_dw'], 'v_g_conv_ln': out['v_g_conv_ln'], 'v_b_conv_ln': out['v_b_conv_ln'], 'v_w_conv_out': out['v_w_conv_out'], 'v_b_conv_out': out['v_b_conv_out'], 'v_g_q_lora': out['v_g_q_lora'], 'v_w_uq': out['v_w_uq'], 'v_g_kv_lora': out['v_g_kv_lora'], 'v_w_uk': out['v_w_uk'], 'v_w_uv': out['v_w_uv'], 'v_w_attn_out': out['v_w_attn_out'], 'v_w_out': out['v_w_out'], 'v_g_ffn': out['v_g_ffn'], 'v_w_ffn_gate': out['v_w_ffn_gate'], 'v_w_ffn_up': out['v_w_ffn_up'], 'v_w_ffn_down': out['v_w_ffn_down'], 'v_g_final': out['v_g_final']}


def _loss(weights, diff, rest, loss_target):
    with _jax.named_scope("forward"):
        args = {**rest, TWIN_DIFF_INPUT: diff, **{k: w.astype(_WEIGHT_DTYPES[k]) for k, w in weights.items()}}
        y = _forward(args)
    with _jax.named_scope("loss_head"):
        err = _jnp.square(y.astype(_jnp.float32) - loss_target)
        return 0.5 * _jnp.sum(_jnp.mean(err, axis=-1)) if err.ndim else 0.5 * err


def _adamw(w, g, m, v):
    m = ADAM_B1 * m + (1.0 - ADAM_B1) * g
    v = ADAM_B2 * v + (1.0 - ADAM_B2) * _jnp.square(g)
    m_hat = m / (1.0 - ADAM_B1 ** ADAM_STEP)
    v_hat = v / (1.0 - ADAM_B2 ** ADAM_STEP)
    delta = -ADAM_LR * (m_hat / (_jnp.sqrt(v_hat) + ADAM_EPS) + ADAM_WD * w)
    return delta, m, v


def reference(x, meta_tokens, g_mix, w_in, b_glu, b_gate, w_dw, b_dw, g_conv_ln, b_conv_ln, w_conv_out, b_conv_out, g_q_lora, w_uq, g_kv_lora, w_uk, w_uv, w_attn_out, w_out, g_ffn, w_ffn_gate, w_ffn_up, w_ffn_down, g_final, loss_target, m_meta_tokens, m_g_mix, m_w_in, m_b_glu, m_b_gate, m_w_dw, m_b_dw, m_g_conv_ln, m_b_conv_ln, m_w_conv_out, m_b_conv_out, m_g_q_lora, m_w_uq, m_g_kv_lora, m_w_uk, m_w_uv, m_w_attn_out, m_w_out, m_g_ffn, m_w_ffn_gate, m_w_ffn_up, m_w_ffn_down, m_g_final, v_meta_tokens, v_g_mix, v_w_in, v_b_glu, v_b_gate, v_w_dw, v_b_dw, v_g_conv_ln, v_b_conv_ln, v_w_conv_out, v_b_conv_out, v_g_q_lora, v_w_uq, v_g_kv_lora, v_w_uk, v_w_uv, v_w_attn_out, v_w_out, v_g_ffn, v_w_ffn_gate, v_w_ffn_up, v_w_ffn_down, v_g_final):
    given = dict(x=x, meta_tokens=meta_tokens, g_mix=g_mix, w_in=w_in, b_glu=b_glu, b_gate=b_gate, w_dw=w_dw, b_dw=b_dw, g_conv_ln=g_conv_ln, b_conv_ln=b_conv_ln, w_conv_out=w_conv_out, b_conv_out=b_conv_out, g_q_lora=g_q_lora, w_uq=w_uq, g_kv_lora=g_kv_lora, w_uk=w_uk, w_uv=w_uv, w_attn_out=w_attn_out, w_out=w_out, g_ffn=g_ffn, w_ffn_gate=w_ffn_gate, w_ffn_up=w_ffn_up, w_ffn_down=w_ffn_down, g_final=g_final, loss_target=loss_target, m_meta_tokens=m_meta_tokens, m_g_mix=m_g_mix, m_w_in=m_w_in, m_b_glu=m_b_glu, m_b_gate=m_b_gate, m_w_dw=m_w_dw, m_b_dw=m_b_dw, m_g_conv_ln=m_g_conv_ln, m_b_conv_ln=m_b_conv_ln, m_w_conv_out=m_w_conv_out, m_b_conv_out=m_b_conv_out, m_g_q_lora=m_g_q_lora, m_w_uq=m_w_uq, m_g_kv_lora=m_g_kv_lora, m_w_uk=m_w_uk, m_w_uv=m_w_uv, m_w_attn_out=m_w_attn_out, m_w_out=m_w_out, m_g_ffn=m_g_ffn, m_w_ffn_gate=m_w_ffn_gate, m_w_ffn_up=m_w_ffn_up, m_w_ffn_down=m_w_ffn_down, m_g_final=m_g_final, v_meta_tokens=v_meta_tokens, v_g_mix=v_g_mix, v_w_in=v_w_in, v_b_glu=v_b_glu, v_b_gate=v_b_gate, v_w_dw=v_w_dw, v_b_dw=v_b_dw, v_g_conv_ln=v_g_conv_ln, v_b_conv_ln=v_b_conv_ln, v_w_conv_out=v_w_conv_out, v_b_conv_out=v_b_conv_out, v_g_q_lora=v_g_q_lora, v_w_uq=v_w_uq, v_g_kv_lora=v_g_kv_lora, v_w_uk=v_w_uk, v_w_uv=v_w_uv, v_w_attn_out=v_w_attn_out, v_w_out=v_w_out, v_g_ffn=v_g_ffn, v_w_ffn_gate=v_w_ffn_gate, v_w_ffn_up=v_w_ffn_up, v_w_ffn_down=v_w_ffn_down, v_g_final=v_g_final)
    weights = {n: given[n] for n in TWIN_WEIGHTS}
    shared = {n: given[n] for n in SHARED_INPUTS}
    per_example = {n: given[n] for n in ['x']}
    grad_fn = _jax.value_and_grad(_loss, argnums=(0, 1))

    def one_microbatch(ex, loss_target):
        ex = dict(ex)
        diff = ex.pop(TWIN_DIFF_INPUT)
        return grad_fn(weights, diff, {**shared, **ex}, loss_target)

    if N_MICROBATCH == 1:
        loss, (grad_w, grad_x) = one_microbatch(per_example, given["loss_target"])
    else:
        def body(carry, xs):
            loss_sum, grad_sum = carry
            l_k, (gw_k, gx_k) = one_microbatch(xs[0], xs[1])
            with _jax.named_scope("update"):
                return (loss_sum + l_k, _jax.tree.map(_jnp.add, grad_sum, gw_k)), gx_k

        init = (_jnp.zeros((), _jnp.float32), _jax.tree.map(_jnp.zeros_like, weights))
        (loss, grad_w), grad_x = _jax.lax.scan(body, init, (per_example, given["loss_target"]))
    with _jax.named_scope("update"):
        delta_w, new_m, new_v = {}, {}, {}
        for n in TWIN_WEIGHTS:
            delta_w[n], new_m[n], new_v[n] = _adamw(weights[n], grad_w[n], given["m_" + n], given["v_" + n])
    return (loss, grad_x, *[grad_w[n] for n in TWIN_WEIGHTS], *[delta_w[n] for n in TWIN_WEIGHTS],
            *[new_m[n] for n in TWIN_WEIGHTS], *[new_v[n] for n in TWIN_WEIGHTS])
```

```python
import functools

import jax
import jax.numpy as jnp
from jax import lax
from jax.experimental import pallas as pl
from jax.experimental.pallas import tpu as pltpu

F32, BF16 = jnp.float32, jnp.bfloat16
SDS = jax.ShapeDtypeStruct

N_DEV = 8
N_META = 16
BLOCK_Q = 128
CONV_WIDTH = 31
QK_NOPE, QK_ROPE, V_HEAD = 128, 64, 128
QK = QK_NOPE + QK_ROPE
ROPE_THETA = 10000.0
EPS = 1e-6
ADAM_LR, ADAM_B1, ADAM_B2, ADAM_EPS, ADAM_WD, ADAM_STEP = 0.001, 0.9, 0.999, 1e-08, 0.01, 10

LANES = 128
ROW_TILE = 128
PACK_W = 1024
VMEM_LIMIT = 56 * 1024 * 1024

BIG = ("w_in", "w_conv_out", "w_uq", "w_uk", "w_uv", "w_attn_out", "w_out", "w_ffn_gate", "w_ffn_up", "w_ffn_down")
F32_GATHERED = ("meta_tokens", "w_dw")
SHARDED = BIG + F32_GATHERED
ROW_SHARDED = ("w_conv_out", "w_attn_out", "w_out", "w_ffn_down")
SMALL = ("g_mix", "b_glu", "b_gate", "b_dw", "g_conv_ln", "b_conv_ln", "b_conv_out", "g_q_lora", "g_kv_lora",
         "g_ffn", "g_final")
WEIGHTS = ("meta_tokens", "g_mix", "w_in", "b_glu", "b_gate", "w_dw", "b_dw", "g_conv_ln", "b_conv_ln", "w_conv_out",
           "b_conv_out", "g_q_lora", "w_uq", "g_kv_lora", "w_uk", "w_uv", "w_attn_out", "w_out", "g_ffn", "w_ffn_gate",
           "w_ffn_up", "w_ffn_down", "g_final")


def _params():
    return pltpu.CompilerParams(vmem_limit_bytes=VMEM_LIMIT)


def _tile(dim, limit):
    best = None
    t = LANES
    while t <= min(dim, limit):
        if dim % t == 0:
            best = t
        t += LANES
    return best if best is not None else dim


def _mm(a, b, *, mode, out_dtype, name, add=None):
    if mode == "nn":
        (m, kc), n = a.shape, b.shape[1]
    elif mode == "nt":
        (m, kc), n = a.shape, b.shape[0]
    else:
        (kc, m), n = a.shape, b.shape[1]
    if mode == "tn":
        tm, tn, tk = _tile(m, 1024), _tile(n, 512), kc
    else:
        tm, tn, tk = m, _tile(n, 512), _tile(kc, 512)
    nk = kc // tk
    if mode == "nn":
        a_spec = pl.BlockSpec((tm, tk), lambda i, j, k: (i, k))
        b_spec = pl.BlockSpec((tk, tn), lambda i, j, k: (k, j))
        dims = (((1,), (0,)), ((), ()))
    elif mode == "nt":
        a_spec = pl.BlockSpec((tm, tk), lambda i, j, k: (i, k))
        b_spec = pl.BlockSpec((tn, tk), lambda i, j, k: (j, k))
        dims = (((1,), (1,)), ((), ()))
    else:
        a_spec = pl.BlockSpec((tk, tm), lambda i, j, k: (k, i))
        b_spec = pl.BlockSpec((tk, tn), lambda i, j, k: (k, j))
        dims = (((0,), (0,)), ((), ()))
    o_spec = pl.BlockSpec((tm, tn), lambda i, j, k: (i, j))
    has_add = add is not None

    def body(*refs):
        if has_add:
            a_ref, b_ref, add_ref, o_ref, acc_ref = refs
        else:
            a_ref, b_ref, o_ref, acc_ref = refs
        k = pl.program_id(2)
        p = lax.dot_general(a_ref[...], b_ref[...], dims, preferred_element_type=F32)
        if nk == 1:
            o_ref[...] = ((p + add_ref[...]) if has_add else p).astype(o_ref.dtype)
            return

        @pl.when(k == 0)
        def _():
            acc_ref[...] = (p + add_ref[...]) if has_add else p

        @pl.when(jnp.logical_and(k > 0, k < nk - 1))
        def _():
            acc_ref[...] += p

        @pl.when(k == nk - 1)
        def _():
            o_ref[...] = (acc_ref[...] + p).astype(o_ref.dtype)

    in_specs = [a_spec, b_spec] + ([o_spec] if has_add else [])
    args = (a, b) + ((add,) if has_add else ())
    acc_shape = (tm, tn) if nk > 1 else (8, LANES)
    return pl.pallas_call(
        body, name=name, grid=(m // tm, n // tn, nk), in_specs=in_specs, out_specs=o_spec,
        out_shape=SDS((m, n), out_dtype), scratch_shapes=[pltpu.VMEM(acc_shape, F32)],
        compiler_params=_params(),
    )(*args)


def _rows(width, col=0, tr=ROW_TILE):
    return pl.BlockSpec((tr, width), lambda i: (i, col))


def _whole(arr):
    nd = arr.ndim
    return pl.BlockSpec(arr.shape, lambda i: (0,) * nd)


def _rowwise(fn, ins, outs, accs, *, name, n_rows, tr=ROW_TILE):
    n_in, n_out = len(ins), len(outs)

    def body(*refs):
        i = pl.program_id(0)
        res = fn(i, *[r[...] for r in refs[:n_in]])
        res = res if isinstance(res, (tuple, list)) else (res,)
        for o_ref, v in zip(refs[n_in:n_in + n_out], res[:n_out]):
            o_ref[...] = v.astype(o_ref.dtype)
        for a_ref, v in zip(refs[n_in + n_out:], res[n_out:]):
            @pl.when(i == 0)
            def _(a_ref=a_ref, v=v):
                a_ref[...] = v

            @pl.when(i > 0)
            def _(a_ref=a_ref, v=v):
                a_ref[...] += v

    acc_specs = [pl.BlockSpec(s.shape, lambda i, nd=len(s.shape): (0,) * nd) for s in accs]
    res = pl.pallas_call(
        body, name=name, grid=(n_rows // tr,),
        in_specs=[s for _, s in ins], out_specs=[s for _, s in outs] + acc_specs,
        out_shape=[s for s, _ in outs] + list(accs), compiler_params=_params(),
    )(*[a for a, _ in ins])
    return res


def _rms(x, g):
    return x * lax.rsqrt(jnp.mean(x * x, axis=-1, keepdims=True) + EPS) * g


def _sigmoid(x):
    return 1.0 / (1.0 + jnp.exp(-x))


def _silu(x):
    return x * _sigmoid(x)


def _rms_fwd(h, g, name):
    t, d = h.shape
    (u,) = _rowwise(lambda i, h, g: _rms(h, g), [(h, _rows(d)), (g, _whole(g))], [(SDS((t, d), BF16), _rows(d))], [],
                    name=name, n_rows=t)
    return u


def _rms_bwd(h, g, du, dres, name):
    t, d = h.shape

    def fn(i, h, g, du, dres):
        _, vjp = jax.vjp(_rms, h, g)
        dh, dg = vjp(du)
        dh = dh + dres
        return dh, dh, dg

    return _rowwise(fn, [(h, _rows(d)), (g, _whole(g)), (du, _rows(d)), (dres, _rows(d))],
                    [(SDS((t, d), F32), _rows(d)), (SDS((t, d), BF16), _rows(d))], [SDS((1, d), F32)],
                    name=name, n_rows=t)


def _conv_fwd(c0, w_dw, b_dw):
    t, ch = c0.shape
    tc = _tile(ch, 256)
    halo = 32
    shift = halo - (CONV_WIDTH - 1)

    def body(x_ref, w_ref, b_ref, o_ref, pad_ref):
        pad_ref[0:halo, :] = jnp.zeros((halo, tc), F32)
        pad_ref[halo:halo + t, :] = x_ref[...]
        for r0 in range(0, t, ROW_TILE):
            acc = jnp.zeros((ROW_TILE, tc), F32) + b_ref[...]
            for j in range(CONV_WIDTH):
                acc = acc + pad_ref[r0 + shift + j:r0 + shift + j + ROW_TILE, :] * w_ref[j:j + 1, :]
            o_ref[r0:r0 + ROW_TILE, :] = acc

    col = lambda i: (0, i)
    return pl.pallas_call(
        body, name="conv_fwd", grid=(ch // tc,),
        in_specs=[pl.BlockSpec((t, tc), col), pl.BlockSpec((CONV_WIDTH, tc), col), pl.BlockSpec((1, tc), col)],
        out_specs=pl.BlockSpec((t, tc), col), out_shape=SDS((t, ch), F32),
        scratch_shapes=[pltpu.VMEM((halo + t, tc), F32)], compiler_params=_params(),
    )(c0, w_dw, b_dw)


def _conv_bwd(dc1, c0, w_dw):
    t, ch = c0.shape
    tc = _tile(ch, 256)
    halo = 32
    shift = halo - (CONV_WIDTH - 1)

    def body(d_ref, x_ref, w_ref, dx_ref, dw_ref, db_ref, xpad_ref, dpad_ref):
        xpad_ref[0:halo, :] = jnp.zeros((halo, tc), F32)
        xpad_ref[halo:halo + t, :] = x_ref[...]
        dpad_ref[0:t, :] = d_ref[...]
        dpad_ref[t:t + halo, :] = jnp.zeros((halo, tc), F32)
        for r0 in range(0, t, ROW_TILE):
            acc = jnp.zeros((ROW_TILE, tc), F32)
            for j in range(CONV_WIDTH):
                off = r0 + (CONV_WIDTH - 1) - j
                acc = acc + dpad_ref[off:off + ROW_TILE, :] * w_ref[j:j + 1, :]
            dx_ref[r0:r0 + ROW_TILE, :] = acc
        for j in range(CONV_WIDTH):
            acc = jnp.zeros((1, tc), F32)
            for r0 in range(0, t, ROW_TILE):
                prod = d_ref[r0:r0 + ROW_TILE, :] * xpad_ref[r0 + shift + j:r0 + shift + j + ROW_TILE, :]
                acc = acc + jnp.sum(prod, axis=0, keepdims=True)
            dw_ref[j:j + 1, :] = acc
        db_ref[...] = jnp.sum(d_ref[...], axis=0, keepdims=True)

    col = lambda i: (0, i)
    return pl.pallas_call(
        body, name="conv_bwd", grid=(ch // tc,),
        in_specs=[pl.BlockSpec((t, tc), col), pl.BlockSpec((t, tc), col), pl.BlockSpec((CONV_WIDTH, tc), col)],
        out_specs=[pl.BlockSpec((t, tc), col), pl.BlockSpec((CONV_WIDTH, tc), col), pl.BlockSpec((1, tc), col)],
        out_shape=[SDS((t, ch), F32), SDS((CONV_WIDTH, ch), F32), SDS((1, ch), F32)],
        scratch_shapes=[pltpu.VMEM((halo + t, tc), F32), pltpu.VMEM((halo + t, tc), F32)], compiler_params=_params(),
    )(dc1, c0, w_dw)


def _rope(x1, x2, cos, sin):
    return x1 * cos - x2 * sin, x1 * sin + x2 * cos


def _attn_prep(q, kv, z_l, kr_col, cos, sin, n_heads):
    t = q.shape[0]
    hn = n_heads * QK_NOPE
    half = QK_ROPE // 2

    def body(q_ref, kv_ref, kr_ref, cos_ref, sin_ref, qo_ref, ko_ref, vo_ref):
        cos, sin = cos_ref[...], sin_ref[...]
        kr = kr_ref[...]
        k1, k2 = _rope(kr[:, 0:half], kr[:, half:QK_ROPE], cos, sin)
        for h in range(n_heads):
            b = h * QK
            q1, q2 = _rope(q_ref[:, b + QK_NOPE:b + QK_NOPE + half], q_ref[:, b + QK_NOPE + half:b + QK], cos, sin)
            qo_ref[h] = jnp.concatenate([q_ref[:, b:b + QK_NOPE], q1, q2], axis=-1).astype(BF16)
            ko_ref[h] = jnp.concatenate([kv_ref[:, h * QK_NOPE:(h + 1) * QK_NOPE], k1, k2], axis=-1).astype(BF16)
            vo_ref[h] = kv_ref[:, hn + h * V_HEAD:hn + (h + 1) * V_HEAD].astype(BF16)

    tr = ROW_TILE
    hm = lambda w: pl.BlockSpec((n_heads, tr, w), lambda i: (0, i, 0))
    return pl.pallas_call(
        body, name="attn_prep", grid=(t // tr,),
        in_specs=[_rows(q.shape[1]), _rows(kv.shape[1]), _rows(LANES, kr_col), _rows(half), _rows(half)],
        out_specs=[hm(QK), hm(QK), hm(V_HEAD)],
        out_shape=[SDS((n_heads, t, QK), BF16), SDS((n_heads, t, QK), BF16), SDS((n_heads, t, V_HEAD), BF16)],
        compiler_params=_params(),
    )(q, kv, z_l, cos, sin)


def _attn_post(dq_hm, dk_hm, dv_hm, cos, sin):
    n_heads, t, _ = dq_hm.shape
    hn = n_heads * QK_NOPE
    half = QK_ROPE // 2

    def unrope(d1, d2, cos, sin):
        return d1 * cos + d2 * sin, d2 * cos - d1 * sin

    def body(dq_ref, dk_ref, dv_ref, cos_ref, sin_ref, qo_ref, kvo_ref, kro_ref):
        cos, sin = cos_ref[...], sin_ref[...]
        dkr = jnp.zeros((ROW_TILE, QK_ROPE), F32)
        for h in range(n_heads):
            dq = dq_ref[h]
            d1, d2 = unrope(dq[:, QK_NOPE:QK_NOPE + half], dq[:, QK_NOPE + half:QK], cos, sin)
            qo_ref[:, h * QK:(h + 1) * QK] = jnp.concatenate([dq[:, 0:QK_NOPE], d1, d2], axis=-1).astype(BF16)
            dk = dk_ref[h]
            kvo_ref[:, h * QK_NOPE:(h + 1) * QK_NOPE] = dk[:, 0:QK_NOPE].astype(BF16)
            kvo_ref[:, hn + h * V_HEAD:hn + (h + 1) * V_HEAD] = dv_ref[h].astype(BF16)
            dkr = dkr + dk[:, QK_NOPE:QK]
        d1, d2 = unrope(dkr[:, 0:half], dkr[:, half:QK_ROPE], cos, sin)
        kro_ref[...] = jnp.concatenate([d1, d2, jnp.zeros((ROW_TILE, LANES - QK_ROPE), F32)], axis=-1)

    tr = ROW_TILE
    hm = lambda w: pl.BlockSpec((n_heads, tr, w), lambda i: (0, i, 0))
    return pl.pallas_call(
        body, name="attn_post", grid=(t // tr,),
        in_specs=[hm(QK), hm(QK), hm(V_HEAD), _rows(half), _rows(half)],
        out_specs=[_rows(n_heads * QK), _rows(2 * hn), _rows(LANES)],
        out_shape=[SDS((t, n_heads * QK), BF16), SDS((t, 2 * hn), BF16), SDS((t, LANES), F32)],
        compiler_params=_params(),
    )(dq_hm, dk_hm, dv_hm, cos, sin)


N_QBLK = 4
_NT = (((1,), (1,)), ((), ()))
_TN = (((0,), (0,)), ((), ()))


def _scores(q, k, r0, scale):
    s = lax.dot_general(q, k, _NT, preferred_element_type=F32) * scale
    row = r0 + lax.broadcasted_iota(jnp.int32, s.shape, 0)
    col = lax.broadcasted_iota(jnp.int32, s.shape, 1)
    return jnp.where(col <= row, s, -jnp.inf)


def _attn_fwd(q_hm, k_hm, v_hm):
    n_heads, t, _ = q_hm.shape
    bq = t // N_QBLK
    scale = QK ** -0.5

    def body(q_ref, k_ref, v_ref, o_ref, lse_ref):
        for i in range(N_QBLK):
            r0, n_k = i * bq, (i + 1) * bq
            s = _scores(q_ref[0, r0:r0 + bq, :], k_ref[0, 0:n_k, :], r0, scale)
            m = jnp.max(s, axis=-1, keepdims=True)
            p = jnp.exp(s - m)
            l = jnp.sum(p, axis=-1, keepdims=True)
            p = (p / l).astype(BF16)
            o_ref[r0:r0 + bq, :] = jnp.dot(p, v_ref[0, 0:n_k, :], preferred_element_type=F32).astype(BF16)
            lse_ref[0, r0:r0 + bq, :] = m + jnp.log(l)

    head = lambda w: pl.BlockSpec((1, t, w), lambda h: (h, 0, 0))
    return pl.pallas_call(
        body, name="attn_fwd", grid=(n_heads,),
        in_specs=[head(QK), head(QK), head(V_HEAD)],
        out_specs=[pl.BlockSpec((t, V_HEAD), lambda h: (0, h)), head(1)],
        out_shape=[SDS((t, n_heads * V_HEAD), BF16), SDS((n_heads, t, 1), F32)],
        compiler_params=_params(),
    )(q_hm, k_hm, v_hm)


def _attn_bwd(q_hm, k_hm, v_hm, lse, d_o):
    n_heads, t, _ = q_hm.shape
    bq = t // N_QBLK
    scale = QK ** -0.5

    def body(q_ref, k_ref, v_ref, lse_ref, do_ref, dq_ref, dk_ref, dv_ref):
        dk_ref[...] = jnp.zeros(dk_ref.shape, F32)
        dv_ref[...] = jnp.zeros(dv_ref.shape, F32)
        for i in range(N_QBLK):
            r0, n_k = i * bq, (i + 1) * bq
            q = q_ref[0, r0:r0 + bq, :]
            k = k_ref[0, 0:n_k, :]
            d_o = do_ref[r0:r0 + bq, :]
            s = _scores(q, k, r0, scale)
            p = jnp.exp(s - lse_ref[0, r0:r0 + bq, :])
            dp = lax.dot_general(d_o, v_ref[0, 0:n_k, :], _NT, preferred_element_type=F32)
            ds = (p * (dp - jnp.sum(dp * p, axis=-1, keepdims=True)) * scale).astype(BF16)
            dq_ref[0, r0:r0 + bq, :] = jnp.dot(ds, k, preferred_element_type=F32)
            dk_ref[0, 0:n_k, :] += lax.dot_general(ds, q, _TN, preferred_element_type=F32)
            dv_ref[0, 0:n_k, :] += lax.dot_general(p.astype(BF16), d_o, _TN, preferred_element_type=F32)

    head = lambda w: pl.BlockSpec((1, t, w), lambda h: (h, 0, 0))
    return pl.pallas_call(
        body, name="attn_bwd", grid=(n_heads,),
        in_specs=[head(QK), head(QK), head(V_HEAD), head(1), pl.BlockSpec((t, V_HEAD), lambda h: (0, h))],
        out_specs=[head(QK), head(QK), head(V_HEAD)],
        out_shape=[SDS((n_heads, t, QK), F32), SDS((n_heads, t, QK), F32), SDS((n_heads, t, V_HEAD), F32)],
        compiler_params=_params(),
    )(q_hm, k_hm, v_hm, lse, d_o)


def _glu(za, zb, ba, bb):
    return (za + ba) * _sigmoid(zb + bb)


def _ln_silu(c, g, b):
    mu = jnp.mean(c, axis=-1, keepdims=True)
    var = jnp.mean(jnp.square(c - mu), axis=-1, keepdims=True)
    return _silu((c - mu) * lax.rsqrt(var + EPS) * g + b)


def _mix(yc, bco, ya, zc, za, bgc, bga):
    return _sigmoid(zc + bgc) * (yc + bco) + _sigmoid(za + bga) * ya


def _swiglu(a, b):
    return _silu(a) * b


def _local_step(x, target, meta, w_dw, wts, sp):
    seq, d = x.shape
    length = N_META + seq
    t = -(-length // BLOCK_Q) * BLOCK_Q
    ch = w_dw.shape[1]
    ql = wts["w_uq"].shape[0]
    n_heads = wts["w_uq"].shape[1] // QK
    hn = n_heads * QK_NOPE
    dff = wts["w_ffn_down"].shape[0]
    assert wts["w_ukv"].shape[0] == ql and ql % LANES == 0 and t % (N_QBLK * 16) == 0
    pad_rows = lambda a: jnp.concatenate([jnp.zeros((N_META, d), F32), a, jnp.zeros((t - length, d), F32)], axis=0)
    h0 = jnp.concatenate([meta, x, jnp.zeros((t - length, d), F32)], axis=0)
    target_p = pad_rows(target)

    pos = jnp.arange(t, dtype=F32)
    inv_freq = ROPE_THETA ** (-jnp.arange(0, QK_ROPE, 2, dtype=F32) / QK_ROPE)
    ang = pos[:, None] * inv_freq[None, :]
    cos, sin = jnp.cos(ang), jnp.sin(ang)

    b_glu_a, b_glu_b = sp["b_glu"][:, :ch], sp["b_glu"][:, ch:]
    b_gate_c, b_gate_a = sp["b_gate"][:, :d], sp["b_gate"][:, d:]
    kr_col = 2 * ql // LANES

    u = _rms_fwd(h0, sp["g_mix"], "rms_mix")
    z_glu = _mm(u, wts["w_glu"], mode="nn", out_dtype=F32, name="mm_z_glu")
    z_l = _mm(u, wts["w_lora"], mode="nn", out_dtype=F32, name="mm_z_lora")
    z_gate = _mm(u, wts["w_gate"], mode="nn", out_dtype=F32, name="mm_z_gate")

    glu_ins = [(z_glu, _rows(ch, 0)), (z_glu, _rows(ch, 1)), (b_glu_a, _whole(b_glu_a)), (b_glu_b, _whole(b_glu_b))]
    (c0,) = _rowwise(lambda i, za, zb, ba, bb: _glu(za, zb, ba, bb), glu_ins, [(SDS((t, ch), F32), _rows(ch))], [],
                     name="glu_fwd", n_rows=t)
    c1 = _conv_fwd(c0, w_dw, sp["b_dw"])
    ln_ins = [(c1, _rows(ch)), (sp["g_conv_ln"], _whole(sp["g_conv_ln"])), (sp["b_conv_ln"], _whole(sp["b_conv_ln"]))]
    (c3,) = _rowwise(lambda i, c, g, b: _ln_silu(c, g, b), ln_ins, [(SDS((t, ch), BF16), _rows(ch))], [],
                     name="ln_silu_fwd", n_rows=t)
    yc = _mm(c3, wts["w_conv_out"], mode="nn", out_dtype=F32, name="mm_conv_out")

    lora_ins = [(z_l, _rows(ql, 0)), (z_l, _rows(ql, 1)), (sp["g_q_lora"], _whole(sp["g_q_lora"])),
                (sp["g_kv_lora"], _whole(sp["g_kv_lora"]))]
    cq, ckv = _rowwise(lambda i, zq, zk, gq, gk: (_rms(zq, gq), _rms(zk, gk)), lora_ins,
                       [(SDS((t, ql), BF16), _rows(ql)), (SDS((t, ql), BF16), _rows(ql))], [],
                       name="lora_norm_fwd", n_rows=t)
    q = _mm(cq, wts["w_uq"], mode="nn", out_dtype=F32, name="mm_q")
    kv = _mm(ckv, wts["w_ukv"], mode="nn", out_dtype=F32, name="mm_kv")
    q_hm, k_hm, v_hm = _attn_prep(q, kv, z_l, kr_col, cos, sin, n_heads)
    o, lse = _attn_fwd(q_hm, k_hm, v_hm)
    ya = _mm(o, wts["w_attn_out"], mode="nn", out_dtype=F32, name="mm_attn_out")

    mix_ins = [(yc, _rows(d)), (sp["b_conv_out"], _whole(sp["b_conv_out"])), (ya, _rows(d)), (z_gate, _rows(d, 0)),
               (z_gate, _rows(d, 1)), (b_gate_c, _whole(b_gate_c)), (b_gate_a, _whole(b_gate_a))]
    (mix,) = _rowwise(lambda i, *a: _mix(*a), mix_ins, [(SDS((t, d), BF16), _rows(d))], [], name="mix_fwd", n_rows=t)
    h1 = _mm(mix, wts["w_out"], mode="nn", out_dtype=F32, name="mm_out", add=h0)

    hn_ = _rms_fwd(h1, sp["g_ffn"], "rms_ffn")
    ab = _mm(hn_, wts["w_gu"], mode="nn", out_dtype=F32, name="mm_ffn_in")
    tr_ffn = 64
    (f,) = _rowwise(lambda i, a, b: _swiglu(a, b), [(ab, _rows(dff, 0, tr_ffn)), (ab, _rows(dff, 1, tr_ffn))],
                    [(SDS((t, dff), BF16), _rows(dff, 0, tr_ffn))], [], name="swiglu_fwd", n_rows=t, tr=tr_ffn)
    h2 = _mm(f, wts["w_ffn_down"], mode="nn", out_dtype=F32, name="mm_ffn_down", add=h1)

    def head(i, h, g, tgt):
        y, vjp = jax.vjp(_rms, h, g)
        row = i * ROW_TILE + lax.broadcasted_iota(jnp.int32, (ROW_TILE, 1), 0)
        valid = jnp.logical_and(row >= N_META, row < length)
        err = jnp.where(valid, y - tgt, 0.0)
        dh, dg = vjp(err / d)
        loss = 0.5 * jnp.sum(jnp.sum(err * err, axis=-1, keepdims=True), axis=0, keepdims=True) / d
        return dh, dh, dg, jnp.broadcast_to(loss, (1, LANES))

    dh2, dh2_b, g_final, loss_v = _rowwise(
        head, [(h2, _rows(d)), (sp["g_final"], _whole(sp["g_final"])), (target_p, _rows(d))],
        [(SDS((t, d), F32), _rows(d)), (SDS((t, d), BF16), _rows(d))], [SDS((1, d), F32), SDS((1, LANES), F32)],
        name="loss_head", n_rows=t)
    loss = loss_v[0, 0]

    g_ffn_down = _mm(f, dh2_b, mode="tn", out_dtype=BF16, name="mm_g_ffn_down")
    df = _mm(dh2_b, wts["w_ffn_down"], mode="nt", out_dtype=F32, name="mm_d_f")

    def swiglu_bwd(i, a, b, df):
        _, vjp = jax.vjp(_swiglu, a, b)
        da, db = vjp(df)
        return jnp.concatenate([da, db], axis=-1)

    (dab,) = _rowwise(swiglu_bwd, [(ab, _rows(dff, 0, tr_ffn)), (ab, _rows(dff, 1, tr_ffn)), (df, _rows(dff, 0, tr_ffn))],
                      [(SDS((t, 2 * dff), BF16), _rows(2 * dff, 0, tr_ffn))], [], name="swiglu_bwd", n_rows=t, tr=tr_ffn)
    g_gu = _mm(hn_, dab, mode="tn", out_dtype=BF16, name="mm_g_ffn_in")
    dhn = _mm(dab, wts["w_gu"], mode="nt", out_dtype=F32, name="mm_d_hn")
    dh1, dh1_b, g_g_ffn = _rms_bwd(h1, sp["g_ffn"], dhn, dh2, "rms_ffn_bwd")

    g_w_out = _mm(mix, dh1_b, mode="tn", out_dtype=BF16, name="mm_g_out")
    dmix = _mm(dh1_b, wts["w_out"], mode="nt", out_dtype=F32, name="mm_d_mix")

    def mix_bwd(i, yc, bco, ya, zc, za, bgc, bga, dmix):
        _, vjp = jax.vjp(_mix, yc, bco, ya, zc, za, bgc, bga)
        dyc, dbco, dya, dzc, dza, dbgc, dbga = vjp(dmix)
        return dyc, dya, jnp.concatenate([dzc, dza], axis=-1), dbco, dbgc, dbga

    dyc, dya, dz_gate, g_b_conv_out, g_bgc, g_bga = _rowwise(
        mix_bwd, mix_ins + [(dmix, _rows(d))],
        [(SDS((t, d), BF16), _rows(d)), (SDS((t, d), BF16), _rows(d)), (SDS((t, 2 * d), BF16), _rows(2 * d))],
        [SDS((1, d), F32)] * 3, name="mix_bwd", n_rows=t)

    g_attn_out = _mm(o, dya, mode="tn", out_dtype=BF16, name="mm_g_attn_out")
    d_o = _mm(dya, wts["w_attn_out"], mode="nt", out_dtype=BF16, name="mm_d_o")
    dq_hm, dk_hm, dv_hm = _attn_bwd(q_hm, k_hm, v_hm, lse, d_o)
    dq, dkv, dkr = _attn_post(dq_hm, dk_hm, dv_hm, cos, sin)
    g_uq = _mm(cq, dq, mode="tn", out_dtype=BF16, name="mm_g_uq")
    g_ukv = _mm(ckv, dkv, mode="tn", out_dtype=BF16, name="mm_g_ukv")
    dcq = _mm(dq, wts["w_uq"], mode="nt", out_dtype=F32, name="mm_d_cq")
    dckv = _mm(dkv, wts["w_ukv"], mode="nt", out_dtype=F32, name="mm_d_ckv")

    def lora_bwd(i, zq, zk, gq, gk, dcq, dckv, dkr):
        _, vq = jax.vjp(_rms, zq, gq)
        _, vk = jax.vjp(_rms, zk, gk)
        dzq, dgq = vq(dcq)
        dzk, dgk = vk(dckv)
        return jnp.concatenate([dzq, dzk, dkr], axis=-1), dgq, dgk

    dz_l, g_g_q, g_g_kv = _rowwise(
        lora_bwd, lora_ins + [(dcq, _rows(ql)), (dckv, _rows(ql)), (dkr, _rows(LANES))],
        [(SDS((t, 2 * ql + LANES), BF16), _rows(2 * ql + LANES))], [SDS((1, ql), F32)] * 2,
        name="lora_norm_bwd", n_rows=t)

    g_conv_out = _mm(c3, dyc, mode="tn", out_dtype=BF16, name="mm_g_conv_out")
    dc3 = _mm(dyc, wts["w_conv_out"], mode="nt", out_dtype=F32, name="mm_d_c3")

    def ln_bwd(i, c, g, b, dc3):
        _, vjp = jax.vjp(_ln_silu, c, g, b)
        return vjp(dc3)

    dc1, g_g_ln, g_b_ln = _rowwise(ln_bwd, ln_ins + [(dc3, _rows(ch))], [(SDS((t, ch), F32), _rows(ch))],
                                   [SDS((1, ch), F32)] * 2, name="ln_silu_bwd", n_rows=t)
    dc0, g_w_dw, g_b_dw = _conv_bwd(dc1, c0, w_dw)

    def glu_bwd(i, za, zb, ba, bb, dc0):
        _, vjp = jax.vjp(_glu, za, zb, ba, bb)
        dza, dzb, dba, dbb = vjp(dc0)
        return jnp.concatenate([dza, dzb], axis=-1), dba, dbb

    dz_glu, g_bga_, g_bgb_ = _rowwise(glu_bwd, glu_ins + [(dc0, _rows(ch))],
                                      [(SDS((t, 2 * ch), BF16), _rows(2 * ch))], [SDS((1, ch), F32)] * 2,
                                      name="glu_bwd", n_rows=t)

    g_glu = _mm(u, dz_glu, mode="tn", out_dtype=BF16, name="mm_g_w_glu")
    g_lora = _mm(u, dz_l, mode="tn", out_dtype=BF16, name="mm_g_w_lora")
    g_gate = _mm(u, dz_gate, mode="tn", out_dtype=BF16, name="mm_g_w_gate")
    du = _mm(dz_glu, wts["w_glu"], mode="nt", out_dtype=F32, name="mm_d_u0")
    du = _mm(dz_l, wts["w_lora"], mode="nt", out_dtype=F32, name="mm_d_u1", add=du)
    du = _mm(dz_gate, wts["w_gate"], mode="nt", out_dtype=F32, name="mm_d_u2", add=du)
    dh0, _, g_g_mix = _rms_bwd(h0, sp["g_mix"], du, dh1, "rms_mix_bwd")

    big = {
        "w_glu": g_glu, "w_lora": g_lora, "w_gate": g_gate, "w_conv_out": g_conv_out, "w_uq": g_uq, "w_ukv": g_ukv,
        "w_attn_out": g_attn_out, "w_out": g_w_out, "w_gu": g_gu, "w_ffn_down": g_ffn_down,
        "meta_tokens": dh0[:N_META], "w_dw": g_w_dw,
    }
    small = {
        "g_mix": g_g_mix, "b_glu": jnp.concatenate([g_bga_, g_bgb_], axis=1),
        "b_gate": jnp.concatenate([g_bgc, g_bga], axis=1), "b_dw": g_b_dw, "g_conv_ln": g_g_ln, "b_conv_ln": g_b_ln,
        "b_conv_out": g_b_conv_out, "g_q_lora": g_g_q, "g_kv_lora": g_g_kv, "g_ffn": g_g_ffn, "g_final": g_final,
    }
    return loss, dh0[N_META:length], big, small


def _slot(ref, dev):
    return ref.at[dev]


def _row_window(rows):
    return lambda ref, dev: ref.at[pl.ds(pl.multiple_of(dev * rows, 16), rows)]


def _col_window(width, offset=0):
    return lambda ref, dev: ref.at[:, pl.ds(pl.multiple_of(offset + dev * width, LANES), width)]


def _dev_index(x, y, c):
    return 4 * x + 2 * y + c


def _gather_weights(items, out_shapes):
    srcs = [it[0] for it in items]
    n, n_out = len(srcs), len(out_shapes)

    def body(*refs):
        src, out = refs[:n], refs[n:n + n_out]
        send_sems, recv_sems, local_sems = refs[n + n_out:]
        x, y, c = lax.axis_index("x"), lax.axis_index("y"), lax.axis_index("c")
        me, sibling = (x, y, c), (x, y, 1 - c)
        chips = [(1 - x, y), (x, 1 - y), (1 - x, 1 - y)]

        def place(i, block):
            _, o, window = items[i]
            return window(out[o], _dev_index(*block))

        def copy(k, i, block, to, from_src=False):
            return pltpu.make_async_remote_copy(
                src_ref=src[i] if from_src else place(i, block), dst_ref=place(i, block),
                send_sem=send_sems.at[k * n + i], recv_sem=recv_sems.at[k * n + i],
                device_id=to, device_id_type=pl.DeviceIdType.MESH)

        mine = [pltpu.make_async_copy(src[i], place(i, me), local_sems.at[i]) for i in range(n)]
        first = [copy(0, i, me, sibling, True) for i in range(n)]
        first += [copy(1 + j, i, me, (*chip, c), True) for j, chip in enumerate(chips) for i in range(n)]
        for cp in mine + first:
            cp.start()
        passed = [[copy(4 + j, i, (*chip, c), sibling) for i in range(n)] for j, chip in enumerate(chips)]
        for j, chip in enumerate(chips):
            for i in range(n):
                copy(1 + j, i, (*chip, c), me).wait_recv()
            for cp in passed[j]:
                cp.start()
        for i in range(n):
            copy(0, i, sibling, me).wait_recv()
        for j, chip in enumerate(chips):
            for i in range(n):
                copy(4 + j, i, (*chip, 1 - c), me).wait_recv()
        for cp in first + [cp for row in passed for cp in row]:
            cp.wait_send()
        for cp in mine:
            cp.wait()

    any_spec = pl.BlockSpec(memory_space=pl.ANY)
    return pl.pallas_call(
        body, name="gather_weights", in_specs=[any_spec] * n, out_specs=[any_spec] * n_out, out_shape=out_shapes,
        scratch_shapes=[pltpu.SemaphoreType.DMA((7 * n,)), pltpu.SemaphoreType.DMA((7 * n,)),
                        pltpu.SemaphoreType.DMA((n,))],
    )(*srcs)


def _exchange(srcs, out_shapes, items, *, masks, name):
    n_src, n_out, n = len(srcs), len(out_shapes), len(items)

    def body(*refs):
        src, out = refs[:n_src], refs[n_src:n_src + n_out]
        send_sems, recv_sems, local_sems = refs[n_src + n_out:]
        x, y, c = lax.axis_index("x"), lax.axis_index("y"), lax.axis_index("c")
        me = _dev_index(x, y, c)
        local, remote = [], []
        for i, (s, s_win, o, d_win) in enumerate(items):
            local.append(pltpu.make_async_copy(s_win(src[s], me), d_win(out[o], me), local_sems.at[i]))
        for k, mask in enumerate(masks):
            px, py, pc = x ^ ((mask >> 2) & 1), y ^ ((mask >> 1) & 1), c ^ (mask & 1)
            peer = _dev_index(px, py, pc)
            for i, (s, s_win, o, d_win) in enumerate(items):
                remote.append(pltpu.make_async_remote_copy(
                    src_ref=s_win(src[s], peer), dst_ref=d_win(out[o], me),
                    send_sem=send_sems.at[k * n + i], recv_sem=recv_sems.at[k * n + i],
                    device_id=(px, py, pc), device_id_type=pl.DeviceIdType.MESH))
        for cp in local:
            cp.start()
        for cp in remote:
            cp.start()
        for cp in remote:
            cp.wait()
        for cp in local:
            cp.wait()

    any_spec = pl.BlockSpec(memory_space=pl.ANY)
    return pl.pallas_call(
        body, name=name, in_specs=[any_spec] * n_src, out_specs=[any_spec] * n_out, out_shape=out_shapes,
        scratch_shapes=[pltpu.SemaphoreType.DMA((len(masks) * n,)), pltpu.SemaphoreType.DMA((len(masks) * n,)),
                        pltpu.SemaphoreType.DMA((n,))],
    )(*srcs)


def _regroup(srcs, dsts, *, name, tr=256):
    def segments(shape, valid):
        if len(shape) == 3:
            return [(p, shape[2]) for p in range(shape[0])]
        return [(None, valid)]

    src_arrays = [s[0] if isinstance(s, tuple) else s for s in srcs]
    src_valid = [s[1] if isinstance(s, tuple) else s.shape[-1] for s in srcs]
    k_rows = src_arrays[0].shape[-2]
    src_segs = [(i, p, w) for i, a in enumerate(src_arrays) for p, w in segments(a.shape, src_valid[i])]
    dst_segs = [(j, p, w) for j, (shape, _, valid) in enumerate(dsts) for p, w in segments(shape, valid)]
    pieces = []
    si, so, di, do = 0, 0, 0, 0
    while si < len(src_segs) and di < len(dst_segs):
        n = min(src_segs[si][2] - so, dst_segs[di][2] - do)
        pieces.append((src_segs[si][0], src_segs[si][1], so, dst_segs[di][0], dst_segs[di][1], do, n))
        so, do = so + n, do + n
        if so == src_segs[si][2]:
            si, so = si + 1, 0
        if do == dst_segs[di][2]:
            di, do = di + 1, 0
    assert si == len(src_segs) and di == len(dst_segs), "source and destination columns differ in number"
    n_src = len(src_arrays)

    def body(*refs):
        src, dst = refs[:n_src], refs[n_src:]
        for j, (shape, dtype, valid) in enumerate(dsts):
            if len(shape) == 2 and valid < shape[1]:
                dst[j][:, valid:shape[1]] = jnp.zeros((tr, shape[1] - valid), dtype)
        for i, sp, so, j, dp, do, n in pieces:
            val = src[i][:, so:so + n] if sp is None else src[i][sp, :, so:so + n]
            if dp is None:
                dst[j][:, do:do + n] = val.astype(dst[j].dtype)
            else:
                dst[j][dp, :, do:do + n] = val.astype(dst[j].dtype)

    def spec(shape):
        if len(shape) == 3:
            return pl.BlockSpec((shape[0], tr, shape[2]), lambda i: (0, i, 0))
        return pl.BlockSpec((tr, shape[1]), lambda i: (i, 0))

    return pl.pallas_call(
        body, name=name, grid=(k_rows // tr,), in_specs=[spec(a.shape) for a in src_arrays],
        out_specs=[spec(shape) for shape, _, _ in dsts], out_shape=[SDS(shape, dtype) for shape, dtype, _ in dsts],
        compiler_params=_params(),
    )(*src_arrays)


def _cast_bf16(a, name):
    r, c = a.shape
    tr = _row_tile(r, c * 4)
    (out,) = _rowwise(lambda i, v: v, [(a, _rows(c, 0, tr))], [(SDS((r, c), BF16), _rows(c, 0, tr))], [], name=name,
                      n_rows=r, tr=tr)
    return out


def _row_tile(rows, row_bytes, limit=2 * 1024 * 1024):
    best = None
    for t in range(16, rows + 1, 16):
        if rows % t == 0 and t * row_bytes <= limit:
            best = t
    return best if best is not None else rows


def _adamw(recv, w, m, v, *, name):
    n_rows, width = w.shape
    tr = _row_tile(n_rows, width * 4, limit=1024 * 1024)

    def body(r_ref, w_ref, m_ref, v_ref, g_ref, d_ref, mo_ref, vo_ref):
        g = r_ref[0].astype(F32)
        for q in range(1, N_DEV):
            g = g + r_ref[q].astype(F32)
        m_new = ADAM_B1 * m_ref[...] + (1.0 - ADAM_B1) * g
        v_new = ADAM_B2 * v_ref[...] + (1.0 - ADAM_B2) * jnp.square(g)
        m_hat = m_new / (1.0 - ADAM_B1 ** ADAM_STEP)
        v_hat = v_new / (1.0 - ADAM_B2 ** ADAM_STEP)
        g_ref[...] = g
        d_ref[...] = -ADAM_LR * (m_hat / (jnp.sqrt(v_hat) + ADAM_EPS) + ADAM_WD * w_ref[...])
        mo_ref[...] = m_new
        vo_ref[...] = v_new

    row = pl.BlockSpec((tr, width), lambda i: (i, 0))
    return pl.pallas_call(
        body, name=name, grid=(n_rows // tr,),
        in_specs=[pl.BlockSpec((N_DEV, tr, width), lambda i: (0, i, 0)), row, row, row],
        out_specs=[row] * 4, out_shape=[SDS((n_rows, width), F32)] * 4, compiler_params=_params(),
    )(recv, w, m, v)


def _offsets(sizes):
    offs, o = [], 0
    for n in sizes:
        offs.append(o)
        o += n
    return offs, o


def kernel(x, meta_tokens, g_mix, w_in, b_glu, b_gate, w_dw, b_dw, g_conv_ln, b_conv_ln, w_conv_out, b_conv_out, g_q_lora, w_uq, g_kv_lora, w_uk, w_uv, w_attn_out, w_out, g_ffn, w_ffn_gate, w_ffn_up, w_ffn_down, g_final, loss_target, m_meta_tokens, m_g_mix, m_w_in, m_b_glu, m_b_gate, m_w_dw, m_b_dw, m_g_conv_ln, m_b_conv_ln, m_w_conv_out, m_b_conv_out, m_g_q_lora, m_w_uq, m_g_kv_lora, m_w_uk, m_w_uv, m_w_attn_out, m_w_out, m_g_ffn, m_w_ffn_gate, m_w_ffn_up, m_w_ffn_down, m_g_final, v_meta_tokens, v_g_mix, v_w_in, v_b_glu, v_b_gate, v_w_dw, v_b_dw, v_g_conv_ln, v_b_conv_ln, v_w_conv_out, v_b_conv_out, v_g_q_lora, v_w_uq, v_g_kv_lora, v_w_uk, v_w_uv, v_w_attn_out, v_w_out, v_g_ffn, v_w_ffn_gate, v_w_ffn_up, v_w_ffn_down, v_g_final):
    w_all = dict(meta_tokens=meta_tokens, g_mix=g_mix, w_in=w_in, b_glu=b_glu, b_gate=b_gate, w_dw=w_dw, b_dw=b_dw, g_conv_ln=g_conv_ln, b_conv_ln=b_conv_ln, w_conv_out=w_conv_out, b_conv_out=b_conv_out, g_q_lora=g_q_lora, w_uq=w_uq, g_kv_lora=g_kv_lora, w_uk=w_uk, w_uv=w_uv, w_attn_out=w_attn_out, w_out=w_out, g_ffn=g_ffn, w_ffn_gate=w_ffn_gate, w_ffn_up=w_ffn_up, w_ffn_down=w_ffn_down, g_final=g_final)
    m_all = dict(meta_tokens=m_meta_tokens, g_mix=m_g_mix, w_in=m_w_in, b_glu=m_b_glu, b_gate=m_b_gate, w_dw=m_w_dw, b_dw=m_b_dw, g_conv_ln=m_g_conv_ln, b_conv_ln=m_b_conv_ln, w_conv_out=m_w_conv_out, b_conv_out=m_b_conv_out, g_q_lora=m_g_q_lora, w_uq=m_w_uq, g_kv_lora=m_g_kv_lora, w_uk=m_w_uk, w_uv=m_w_uv, w_attn_out=m_w_attn_out, w_out=m_w_out, g_ffn=m_g_ffn, w_ffn_gate=m_w_ffn_gate, w_ffn_up=m_w_ffn_up, w_ffn_down=m_w_ffn_down, g_final=m_g_final)
    v_all = dict(meta_tokens=v_meta_tokens, g_mix=v_g_mix, w_in=v_w_in, b_glu=v_b_glu, b_gate=v_b_gate, w_dw=v_w_dw, b_dw=v_b_dw, g_conv_ln=v_g_conv_ln, b_conv_ln=v_b_conv_ln, w_conv_out=v_w_conv_out, b_conv_out=v_b_conv_out, g_q_lora=v_g_q_lora, w_uq=v_w_uq, g_kv_lora=v_g_kv_lora, w_uk=v_w_uk, w_uv=v_w_uv, w_attn_out=v_w_attn_out, w_out=v_w_out, g_ffn=v_g_ffn, w_ffn_gate=v_w_ffn_gate, w_ffn_up=v_w_ffn_up, w_ffn_down=v_w_ffn_down, g_final=v_g_final)

    two_d = lambda a: a.reshape(a.shape[-2:]) if a.ndim >= 2 else a.reshape(1, -1)
    sh = {n: two_d(w_all[n]) for n in SHARDED}
    d = x.shape[-1]
    k_in, c_in = sh["w_in"].shape
    r_co, r_ao, r_wo, r_fd = (sh[n].shape[0] for n in ROW_SHARDED)
    ql, c_uq = sh["w_uq"].shape
    c_uk = sh["w_uk"].shape[1]
    c_ff = sh["w_ffn_gate"].shape[1]
    n_meta, c_meta = sh["meta_tokens"].shape
    n_taps, c_dw = sh["w_dw"].shape
    ch, dff = N_DEV * c_dw, N_DEV * c_ff
    n_lora = 2 * ql + QK_ROPE

    bf = {n: _cast_bf16(sh[n], "cast_" + n) for n in BIG}
    gathered = _gather_weights(
        [(bf["w_in"], 0, _slot), (bf["w_conv_out"], 1, _row_window(r_co)), (bf["w_uq"], 2, _col_window(c_uq)),
         (bf["w_uk"], 3, _col_window(c_uk)), (bf["w_uv"], 3, _col_window(c_uk, N_DEV * c_uk)),
         (bf["w_attn_out"], 4, _row_window(r_ao)), (bf["w_out"], 5, _row_window(r_wo)),
         (bf["w_ffn_gate"], 6, _slot), (bf["w_ffn_up"], 7, _slot), (bf["w_ffn_down"], 8, _row_window(r_fd)),
         (sh["meta_tokens"], 9, _col_window(c_meta)), (sh["w_dw"], 10, _col_window(c_dw))],
        [SDS((N_DEV, k_in, c_in), BF16), SDS((N_DEV * r_co, d), BF16), SDS((ql, N_DEV * c_uq), BF16),
         SDS((ql, 2 * N_DEV * c_uk), BF16), SDS((N_DEV * r_ao, d), BF16), SDS((N_DEV * r_wo, d), BF16),
         SDS((N_DEV, d, c_ff), BF16), SDS((N_DEV, d, c_ff), BF16), SDS((N_DEV * r_fd, d), BF16),
         SDS((n_meta, N_DEV * c_meta), F32), SDS((n_taps, ch), F32)])
    w_glu, w_lora, w_gate = _regroup(
        [gathered[0]], [((k_in, 2 * ch), BF16, 2 * ch), ((k_in, n_lora + LANES - QK_ROPE), BF16, n_lora),
                        ((k_in, 2 * d), BF16, 2 * d)], name="unpack_w_in")
    (w_gu,) = _regroup([gathered[6], gathered[7]], [((d, 2 * dff), BF16, 2 * dff)], name="unpack_w_ffn_in")
    wts = {"w_glu": w_glu, "w_lora": w_lora, "w_gate": w_gate, "w_conv_out": gathered[1], "w_uq": gathered[2],
           "w_ukv": gathered[3], "w_attn_out": gathered[4], "w_out": gathered[5], "w_gu": w_gu,
           "w_ffn_down": gathered[8]}
    sp = {n: w_all[n].reshape(1, -1) for n in SMALL}

    loss, grad_x, g_big, g_small = _local_step(x[0], loss_target[0], gathered[9], gathered[10], wts, sp)

    (s_in,) = _regroup([g_big["w_glu"], (g_big["w_lora"], n_lora), g_big["w_gate"]],
                       [((N_DEV, k_in, c_in), BF16, None)], name="pack_g_w_in")
    s_gate, s_up = _regroup([g_big["w_gu"]], [((N_DEV, d, c_ff), BF16, None)] * 2, name="pack_g_w_ffn_in")
    s_sizes = [w_all[n].size for n in SMALL]
    s_offs, n_s = _offsets(s_sizes)
    cat_small = lambda src: jnp.concatenate([src[n].reshape(1, -1) for n in SMALL], axis=1)
    whole_ref = lambda ref, dev: ref
    recv_names = ("w_in", "w_conv_out", "w_uq", "w_uk", "w_uv", "w_attn_out", "w_out", "w_ffn_gate", "w_ffn_up",
                  "w_ffn_down", "meta_tokens", "w_dw")
    recv = _exchange(
        [s_in, g_big["w_conv_out"], g_big["w_uq"], g_big["w_ukv"], g_big["w_attn_out"], g_big["w_out"], s_gate, s_up,
         g_big["w_ffn_down"], g_big["meta_tokens"], g_big["w_dw"], cat_small(g_small)],
        [SDS((N_DEV,) + sh[n].shape, F32 if n in F32_GATHERED else BF16) for n in recv_names]
        + [SDS((N_DEV, 1, n_s), F32)],
        [(0, _slot, 0, _slot), (1, _row_window(r_co), 1, _slot), (2, _col_window(c_uq), 2, _slot),
         (3, _col_window(c_uk), 3, _slot), (3, _col_window(c_uk, N_DEV * c_uk), 4, _slot),
         (4, _row_window(r_ao), 5, _slot), (5, _row_window(r_wo), 6, _slot), (6, _slot, 7, _slot), (7, _slot, 8, _slot),
         (8, _row_window(r_fd), 9, _slot), (9, _col_window(c_meta), 10, _slot), (10, _col_window(c_dw), 11, _slot),
         (11, whole_ref, 12, _slot)],
        masks=tuple(range(1, N_DEV)), name="exchange_grads")

    by_name = {}
    for i, n in enumerate(recv_names):
        outs = _adamw(recv[i], sh[n], two_d(m_all[n]), two_d(v_all[n]), name="adamw_" + n)
        by_name[n] = [o.reshape(w_all[n].shape) for o in outs]
    outs = _adamw(recv[len(recv_names)], cat_small(w_all), cat_small(m_all), cat_small(v_all), name="adamw_replicated")
    for n, o, s in zip(SMALL, s_offs, s_sizes):
        by_name[n] = [out[:, o:o + s].reshape(w_all[n].shape) for out in outs]
    result = [[by_name[n][k] for n in WEIGHTS] for k in range(4)]
    loss = lax.psum(loss, ("x", "y", "c"))
    return (loss, grad_x[None], *result[0], *result[1], *result[2], *result[3])
```

```python
import functools

import jax
import jax.numpy as jnp
from jax import lax
from jax.experimental import pallas as pl
from jax.experimental.pallas import tpu as pltpu

F32, BF16 = jnp.float32, jnp.bfloat16
SDS = jax.ShapeDtypeStruct

N_DEV = 8
N_META = 16
BLOCK_Q = 128
CONV_WIDTH = 31
QK_NOPE, QK_ROPE, V_HEAD = 128, 64, 128
QK = QK_NOPE + QK_ROPE
ROPE_THETA = 10000.0
EPS = 1e-6
ADAM_LR, ADAM_B1, ADAM_B2, ADAM_EPS, ADAM_WD, ADAM_STEP = 0.001, 0.9, 0.999, 1e-08, 0.01, 10

LANES = 128
ROW_TILE = 128
PACK_W = 1024
VMEM_LIMIT = 56 * 1024 * 1024

BIG = ("w_in", "w_conv_out", "w_uq", "w_uk", "w_uv", "w_attn_out", "w_out", "w_ffn_gate", "w_ffn_up", "w_ffn_down")
F32_GATHERED = ("meta_tokens", "w_dw")
SHARDED = BIG + F32_GATHERED
ROW_SHARDED = ("w_conv_out", "w_attn_out", "w_out", "w_ffn_down")
SMALL = ("g_mix", "b_glu", "b_gate", "b_dw", "g_conv_ln", "b_conv_ln", "b_conv_out", "g_q_lora", "g_kv_lora",
         "g_ffn", "g_final")
WEIGHTS = ("meta_tokens", "g_mix", "w_in", "b_glu", "b_gate", "w_dw", "b_dw", "g_conv_ln", "b_conv_ln", "w_conv_out",
           "b_conv_out", "g_q_lora", "w_uq", "g_kv_lora", "w_uk", "w_uv", "w_attn_out", "w_out", "g_ffn", "w_ffn_gate",
           "w_ffn_up", "w_ffn_down", "g_final")


def _params():
    return pltpu.CompilerParams(vmem_limit_bytes=VMEM_LIMIT)


def _tile(dim, limit):
    best = None
    t = LANES
    while t <= min(dim, limit):
        if dim % t == 0:
            best = t
        t += LANES
    return best if best is not None else dim


def _mm(a, b, *, mode, out_dtype, name, add=None):
    if mode == "nn":
        (m, kc), n = a.shape, b.shape[1]
    elif mode == "nt":
        (m, kc), n = a.shape, b.shape[0]
    else:
        (kc, m), n = a.shape, b.shape[1]
    if mode == "tn":
        tm, tn, tk = _tile(m, 1024), _tile(n, 512), kc
    else:
        tm, tn, tk = m, _tile(n, 512), _tile(kc, 512)
    nk = kc // tk
    if mode == "nn":
        a_spec = pl.BlockSpec((tm, tk), lambda i, j, k: (i, k))
        b_spec = pl.BlockSpec((tk, tn), lambda i, j, k: (k, j))
        dims = (((1,), (0,)), ((), ()))
    elif mode == "nt":
        a_spec = pl.BlockSpec((tm, tk), lambda i, j, k: (i, k))
        b_spec = pl.BlockSpec((tn, tk), lambda i, j, k: (j, k))
        dims = (((1,), (1,)), ((), ()))
    else:
        a_spec = pl.BlockSpec((tk, tm), lambda i, j, k: (k, i))
        b_spec = pl.BlockSpec((tk, tn), lambda i, j, k: (k, j))
        dims = (((0,), (0,)), ((), ()))
    o_spec = pl.BlockSpec((tm, tn), lambda i, j, k: (i, j))
    has_add = add is not None

    def body(*refs):
        if has_add:
            a_ref, b_ref, add_ref, o_ref, acc_ref = refs
        else:
            a_ref, b_ref, o_ref, acc_ref = refs
        k = pl.program_id(2)
        p = lax.dot_general(a_ref[...], b_ref[...], dims, preferred_element_type=F32)
        if nk == 1:
            o_ref[...] = ((p + add_ref[...]) if has_add else p).astype(o_ref.dtype)
            return

        @pl.when(k == 0)
        def _():
            acc_ref[...] = (p + add_ref[...]) if has_add else p

        @pl.when(jnp.logical_and(k > 0, k < nk - 1))
        def _():
            acc_ref[...] += p

        @pl.when(k == nk - 1)
        def _():
            o_ref[...] = (acc_ref[...] + p).astype(o_ref.dtype)

    in_specs = [a_spec, b_spec] + ([o_spec] if has_add else [])
    args = (a, b) + ((add,) if has_add else ())
    acc_shape = (tm, tn) if nk > 1 else (8, LANES)
    return pl.pallas_call(
        body, name=name, grid=(m // tm, n // tn, nk), in_specs=in_specs, out_specs=o_spec,
        out_shape=SDS((m, n), out_dtype), scratch_shapes=[pltpu.VMEM(acc_shape, F32)],
        compiler_params=_params(),
    )(*args)


def _rows(width, col=0, tr=ROW_TILE):
    return pl.BlockSpec((tr, width), lambda i: (i, col))


def _whole(arr):
    nd = arr.ndim
    return pl.BlockSpec(arr.shape, lambda i: (0,) * nd)


def _rowwise(fn, ins, outs, accs, *, name, n_rows, tr=ROW_TILE):
    n_in, n_out = len(ins), len(outs)

    def body(*refs):
        i = pl.program_id(0)
        res = fn(i, *[r[...] for r in refs[:n_in]])
        res = res if isinstance(res, (tuple, list)) else (res,)
        for o_ref, v in zip(refs[n_in:n_in + n_out], res[:n_out]):
            o_ref[...] = v.astype(o_ref.dtype)
        for a_ref, v in zip(refs[n_in + n_out:], res[n_out:]):
            @pl.when(i == 0)
            def _(a_ref=a_ref, v=v):
                a_ref[...] = v

            @pl.when(i > 0)
            def _(a_ref=a_ref, v=v):
                a_ref[...] += v

    acc_specs = [pl.BlockSpec(s.shape, lambda i, nd=len(s.shape): (0,) * nd) for s in accs]
    res = pl.pallas_call(
        body, name=name, grid=(n_rows // tr,),
        in_specs=[s for _, s in ins], out_specs=[s for _, s in outs] + acc_specs,
        out_shape=[s for s, _ in outs] + list(accs), compiler_params=_params(),
    )(*[a for a, _ in ins])
    return res


def _rms(x, g):
    return x * lax.rsqrt(jnp.mean(x * x, axis=-1, keepdims=True) + EPS) * g


def _sigmoid(x):
    return 1.0 / (1.0 + jnp.exp(-x))


def _silu(x):
    return x * _sigmoid(x)


def _rms_fwd(h, g, name):
    t, d = h.shape
    (u,) = _rowwise(lambda i, h, g: _rms(h, g), [(h, _rows(d)), (g, _whole(g))], [(SDS((t, d), BF16), _rows(d))], [],
                    name=name, n_rows=t)
    return u


def _rms_bwd(h, g, du, dres, name):
    t, d = h.shape

    def fn(i, h, g, du, dres):
        _, vjp = jax.vjp(_rms, h, g)
        dh, dg = vjp(du)
        dh = dh + dres
        return dh, dh, dg

    return _rowwise(fn, [(h, _rows(d)), (g, _whole(g)), (du, _rows(d)), (dres, _rows(d))],
                    [(SDS((t, d), F32), _rows(d)), (SDS((t, d), BF16), _rows(d))], [SDS((1, d), F32)],
                    name=name, n_rows=t)


def _conv_fwd(c0, w_dw, b_dw):
    t, ch = c0.shape
    tc = _tile(ch, 256)
    halo = 32
    shift = halo - (CONV_WIDTH - 1)

    def body(x_ref, w_ref, b_ref, o_ref, pad_ref):
        pad_ref[0:halo, :] = jnp.zeros((halo, tc), F32)
        pad_ref[halo:halo + t, :] = x_ref[...]
        for r0 in range(0, t, ROW_TILE):
            acc = jnp.zeros((ROW_TILE, tc), F32) + b_ref[...]
            for j in range(CONV_WIDTH):
                acc = acc + pad_ref[r0 + shift + j:r0 + shift + j + ROW_TILE, :] * w_ref[j:j + 1, :]
            o_ref[r0:r0 + ROW_TILE, :] = acc

    col = lambda i: (0, i)
    return pl.pallas_call(
        body, name="conv_fwd", grid=(ch // tc,),
        in_specs=[pl.BlockSpec((t, tc), col), pl.BlockSpec((CONV_WIDTH, tc), col), pl.BlockSpec((1, tc), col)],
        out_specs=pl.BlockSpec((t, tc), col), out_shape=SDS((t, ch), F32),
        scratch_shapes=[pltpu.VMEM((halo + t, tc), F32)], compiler_params=_params(),
    )(c0, w_dw, b_dw)


def _conv_bwd(dc1, c0, w_dw):
    t, ch = c0.shape
    tc = _tile(ch, 256)
    halo = 32
    shift = halo - (CONV_WIDTH - 1)

    def body(d_ref, x_ref, w_ref, dx_ref, dw_ref, db_ref, xpad_ref, dpad_ref):
        xpad_ref[0:halo, :] = jnp.zeros((halo, tc), F32)
        xpad_ref[halo:halo + t, :] = x_ref[...]
        dpad_ref[0:t, :] = d_ref[...]
        dpad_ref[t:t + halo, :] = jnp.zeros((halo, tc), F32)
        for r0 in range(0, t, ROW_TILE):
            acc = jnp.zeros((ROW_TILE, tc), F32)
            for j in range(CONV_WIDTH):
                off = r0 + (CONV_WIDTH - 1) - j
                acc = acc + dpad_ref[off:off + ROW_TILE, :] * w_ref[j:j + 1, :]
            dx_ref[r0:r0 + ROW_TILE, :] = acc
        for j in range(CONV_WIDTH):
            acc = jnp.zeros((1, tc), F32)
            for r0 in range(0, t, ROW_TILE):
                prod = d_ref[r0:r0 + ROW_TILE, :] * xpad_ref[r0 + shift + j:r0 + shift + j + ROW_TILE, :]
                acc = acc + jnp.sum(prod, axis=0, keepdims=True)
            dw_ref[j:j + 1, :] = acc
        db_ref[...] = jnp.sum(d_ref[...], axis=0, keepdims=True)

    col = lambda i: (0, i)
    return pl.pallas_call(
        body, name="conv_bwd", grid=(ch // tc,),
        in_specs=[pl.BlockSpec((t, tc), col), pl.BlockSpec((t, tc), col), pl.BlockSpec((CONV_WIDTH, tc), col)],
        out_specs=[pl.BlockSpec((t, tc), col), pl.BlockSpec((CONV_WIDTH, tc), col), pl.BlockSpec((1, tc), col)],
        out_shape=[SDS((t, ch), F32), SDS((CONV_WIDTH, ch), F32), SDS((1, ch), F32)],
        scratch_shapes=[pltpu.VMEM((halo + t, tc), F32), pltpu.VMEM((halo + t, tc), F32)], compiler_params=_params(),
    )(dc1, c0, w_dw)


def _rope(x1, x2, cos, sin):
    return x1 * cos - x2 * sin, x1 * sin + x2 * cos


def _attn_prep(q, kv, z_l, kr_col, cos, sin, n_heads):
    t = q.shape[0]
    hn = n_heads * QK_NOPE
    half = QK_ROPE // 2

    def body(q_ref, kv_ref, kr_ref, cos_ref, sin_ref, qo_ref, ko_ref, vo_ref):
        cos, sin = cos_ref[...], sin_ref[...]
        kr = kr_ref[...]
        k1, k2 = _rope(kr[:, 0:half], kr[:, half:QK_ROPE], cos, sin)
        for h in range(n_heads):
            b = h * QK
            q1, q2 = _rope(q_ref[:, b + QK_NOPE:b + QK_NOPE + half], q_ref[:, b + QK_NOPE + half:b + QK], cos, sin)
            qo_ref[h] = jnp.concatenate([q_ref[:, b:b + QK_NOPE], q1, q2], axis=-1).astype(BF16)
            ko_ref[h] = jnp.concatenate([kv_ref[:, h * QK_NOPE:(h + 1) * QK_NOPE], k1, k2], axis=-1).astype(BF16)
            vo_ref[h] = kv_ref[:, hn + h * V_HEAD:hn + (h + 1) * V_HEAD].astype(BF16)

    tr = ROW_TILE
    hm = lambda w: pl.BlockSpec((n_heads, tr, w), lambda i: (0, i, 0))
    return pl.pallas_call(
        body, name="attn_prep", grid=(t // tr,),
        in_specs=[_rows(q.shape[1]), _rows(kv.shape[1]), _rows(LANES, kr_col), _rows(half), _rows(half)],
        out_specs=[hm(QK), hm(QK), hm(V_HEAD)],
        out_shape=[SDS((n_heads, t, QK), BF16), SDS((n_heads, t, QK), BF16), SDS((n_heads, t, V_HEAD), BF16)],
        compiler_params=_params(),
    )(q, kv, z_l, cos, sin)


def _attn_post(dq_hm, dk_hm, dv_hm, cos, sin):
    n_heads, t, _ = dq_hm.shape
    hn = n_heads * QK_NOPE
    half = QK_ROPE // 2

    def unrope(d1, d2, cos, sin):
        return d1 * cos + d2 * sin, d2 * cos - d1 * sin

    def body(dq_ref, dk_ref, dv_ref, cos_ref, sin_ref, qo_ref, kvo_ref, kro_ref):
        cos, sin = cos_ref[...], sin_ref[...]
        dkr = jnp.zeros((ROW_TILE, QK_ROPE), F32)
        for h in range(n_heads):
            dq = dq_ref[h]
            d1, d2 = unrope(dq[:, QK_NOPE:QK_NOPE + half], dq[:, QK_NOPE + half:QK], cos, sin)
            qo_ref[:, h * QK:(h + 1) * QK] = jnp.concatenate([dq[:, 0:QK_NOPE], d1, d2], axis=-1).astype(BF16)
            dk = dk_ref[h]
            kvo_ref[:, h * QK_NOPE:(h + 1) * QK_NOPE] = dk[:, 0:QK_NOPE].astype(BF16)
            kvo_ref[:, hn + h * V_HEAD:hn + (h + 1) * V_HEAD] = dv_ref[h].astype(BF16)
            dkr = dkr + dk[:, QK_NOPE:QK]
        d1, d2 = unrope(dkr[:, 0:half], dkr[:, half:QK_ROPE], cos, sin)
        kro_ref[...] = jnp.concatenate([d1, d2, jnp.zeros((ROW_TILE, LANES - QK_ROPE), F32)], axis=-1)

    tr = ROW_TILE
    hm = lambda w: pl.BlockSpec((n_heads, tr, w), lambda i: (0, i, 0))
    return pl.pallas_call(
        body, name="attn_post", grid=(t // tr,),
        in_specs=[hm(QK), hm(QK), hm(V_HEAD), _rows(half), _rows(half)],
        out_specs=[_rows(n_heads * QK), _rows(2 * hn), _rows(LANES)],
        out_shape=[SDS((t, n_heads * QK), BF16), SDS((t, 2 * hn), BF16), SDS((t, LANES), F32)],
        compiler_params=_params(),
    )(dq_hm, dk_hm, dv_hm, cos, sin)


N_QBLK = 4
_NT = (((1,), (1,)), ((), ()))
_TN = (((0,), (0,)), ((), ()))


def _scores(q, k, r0, scale):
    s = lax.dot_general(q, k, _NT, preferred_element_type=F32) * scale
    row = r0 + lax.broadcasted_iota(jnp.int32, s.shape, 0)
    col = lax.broadcasted_iota(jnp.int32, s.shape, 1)
    return jnp.where(col <= row, s, -jnp.inf)


def _attn_fwd(q_hm, k_hm, v_hm):
    n_heads, t, _ = q_hm.shape
    bq = t // N_QBLK
    scale = QK ** -0.5

    def body(q_ref, k_ref, v_ref, o_ref, lse_ref):
        for i in range(N_QBLK):
            r0, n_k = i * bq, (i + 1) * bq
            s = _scores(q_ref[0, r0:r0 + bq, :], k_ref[0, 0:n_k, :], r0, scale)
            m = jnp.max(s, axis=-1, keepdims=True)
            p = jnp.exp(s - m)
            l = jnp.sum(p, axis=-1, keepdims=True)
            p = (p / l).astype(BF16)
            o_ref[r0:r0 + bq, :] = jnp.dot(p, v_ref[0, 0:n_k, :], preferred_element_type=F32).astype(BF16)
            lse_ref[0, r0:r0 + bq, :] = m + jnp.log(l)

    head = lambda w: pl.BlockSpec((1, t, w), lambda h: (h, 0, 0))
    return pl.pallas_call(
        body, name="attn_fwd", grid=(n_heads,),
        in_specs=[head(QK), head(QK), head(V_HEAD)],
        out_specs=[pl.BlockSpec((t, V_HEAD), lambda h: (0, h)), head(1)],
        out_shape=[SDS((t, n_heads * V_HEAD), BF16), SDS((n_heads, t, 1), F32)],
        compiler_params=_params(),
    )(q_hm, k_hm, v_hm)


def _attn_bwd(q_hm, k_hm, v_hm, lse, d_o):
    n_heads, t, _ = q_hm.shape
    bq = t // N_QBLK
    scale = QK ** -0.5

    def body(q_ref, k_ref, v_ref, lse_ref, do_ref, dq_ref, dk_ref, dv_ref):
        dk_ref[...] = jnp.zeros(dk_ref.shape, F32)
        dv_ref[...] = jnp.zeros(dv_ref.shape, F32)
        for i in range(N_QBLK):
            r0, n_k = i * bq, (i + 1) * bq
            q = q_ref[0, r0:r0 + bq, :]
            k = k_ref[0, 0:n_k, :]
            d_o = do_ref[r0:r0 + bq, :]
            s = _scores(q, k, r0, scale)
            p = jnp.exp(s - lse_ref[0, r0:r0 + bq, :])
            dp = lax.dot_general(d_o, v_ref[0, 0:n_k, :], _NT, preferred_element_type=F32)
            ds = (p * (dp - jnp.sum(dp * p, axis=-1, keepdims=True)) * scale).astype(BF16)
            dq_ref[0, r0:r0 + bq, :] = jnp.dot(ds, k, preferred_element_type=F32)
            dk_ref[0, 0:n_k, :] += lax.dot_general(ds, q, _TN, preferred_element_type=F32)
            dv_ref[0, 0:n_k, :] += lax.dot_general(p.astype(BF16), d_o, _TN, preferred_element_type=F32)

    head = lambda w: pl.BlockSpec((1, t, w), lambda h: (h, 0, 0))
    return pl.pallas_call(
        body, name="attn_bwd", grid=(n_heads,),
        in_specs=[head(QK), head(QK), head(V_HEAD), head(1), pl.BlockSpec((t, V_HEAD), lambda h: (0, h))],
        out_specs=[head(QK), head(QK), head(V_HEAD)],
        out_shape=[SDS((n_heads, t, QK), F32), SDS((n_heads, t, QK), F32), SDS((n_heads, t, V_HEAD), F32)],
        compiler_params=_params(),
    )(q_hm, k_hm, v_hm, lse, d_o)


def _glu(za, zb, ba, bb):
    return (za + ba) * _sigmoid(zb + bb)


def _ln_silu(c, g, b):
    mu = jnp.mean(c, axis=-1, keepdims=True)
    var = jnp.mean(jnp.square(c - mu), axis=-1, keepdims=True)
    return _silu((c - mu) * lax.rsqrt(var + EPS) * g + b)


def _mix(yc, bco, ya, zc, za, bgc, bga):
    return _sigmoid(zc + bgc) * (yc + bco) + _sigmoid(za + bga) * ya


def _swiglu(a, b):
    return _silu(a) * b


def _local_step(x, target, meta, w_dw, wts, sp, ship=None):
    if ship is None:
        ship = lambda group, grads: jnp.zeros((), F32)
    seq, d = x.shape
    length = N_META + seq
    t = -(-length // BLOCK_Q) * BLOCK_Q
    ch = w_dw.shape[1]
    ql = wts["w_uq"].shape[0]
    n_heads = wts["w_uq"].shape[1] // QK
    hn = n_heads * QK_NOPE
    dff = wts["w_ffn_down"].shape[0]
    assert wts["w_ukv"].shape[0] == ql and ql % LANES == 0 and t % (N_QBLK * 16) == 0
    pad_rows = lambda a: jnp.concatenate([jnp.zeros((N_META, d), F32), a, jnp.zeros((t - length, d), F32)], axis=0)
    h0 = jnp.concatenate([meta, x, jnp.zeros((t - length, d), F32)], axis=0)
    target_p = pad_rows(target)

    pos = jnp.arange(t, dtype=F32)
    inv_freq = ROPE_THETA ** (-jnp.arange(0, QK_ROPE, 2, dtype=F32) / QK_ROPE)
    ang = pos[:, None] * inv_freq[None, :]
    cos, sin = jnp.cos(ang), jnp.sin(ang)

    b_glu_a, b_glu_b = sp["b_glu"][:, :ch], sp["b_glu"][:, ch:]
    b_gate_c, b_gate_a = sp["b_gate"][:, :d], sp["b_gate"][:, d:]
    kr_col = 2 * ql // LANES

    u = _rms_fwd(h0, sp["g_mix"], "rms_mix")
    z_glu = _mm(u, wts["w_glu"], mode="nn", out_dtype=F32, name="mm_z_glu")
    z_l = _mm(u, wts["w_lora"], mode="nn", out_dtype=F32, name="mm_z_lora")
    z_gate = _mm(u, wts["w_gate"], mode="nn", out_dtype=F32, name="mm_z_gate")

    glu_ins = [(z_glu, _rows(ch, 0)), (z_glu, _rows(ch, 1)), (b_glu_a, _whole(b_glu_a)), (b_glu_b, _whole(b_glu_b))]
    (c0,) = _rowwise(lambda i, za, zb, ba, bb: _glu(za, zb, ba, bb), glu_ins, [(SDS((t, ch), F32), _rows(ch))], [],
                     name="glu_fwd", n_rows=t)
    c1 = _conv_fwd(c0, w_dw, sp["b_dw"])
    ln_ins = [(c1, _rows(ch)), (sp["g_conv_ln"], _whole(sp["g_conv_ln"])), (sp["b_conv_ln"], _whole(sp["b_conv_ln"]))]
    (c3,) = _rowwise(lambda i, c, g, b: _ln_silu(c, g, b), ln_ins, [(SDS((t, ch), BF16), _rows(ch))], [],
                     name="ln_silu_fwd", n_rows=t)
    yc = _mm(c3, wts["w_conv_out"], mode="nn", out_dtype=F32, name="mm_conv_out")

    lora_ins = [(z_l, _rows(ql, 0)), (z_l, _rows(ql, 1)), (sp["g_q_lora"], _whole(sp["g_q_lora"])),
                (sp["g_kv_lora"], _whole(sp["g_kv_lora"]))]
    cq, ckv = _rowwise(lambda i, zq, zk, gq, gk: (_rms(zq, gq), _rms(zk, gk)), lora_ins,
                       [(SDS((t, ql), BF16), _rows(ql)), (SDS((t, ql), BF16), _rows(ql))], [],
                       name="lora_norm_fwd", n_rows=t)
    q = _mm(cq, wts["w_uq"], mode="nn", out_dtype=F32, name="mm_q")
    kv = _mm(ckv, wts["w_ukv"], mode="nn", out_dtype=F32, name="mm_kv")
    q_hm, k_hm, v_hm = _attn_prep(q, kv, z_l, kr_col, cos, sin, n_heads)
    o, lse = _attn_fwd(q_hm, k_hm, v_hm)
    ya = _mm(o, wts["w_attn_out"], mode="nn", out_dtype=F32, name="mm_attn_out")

    mix_ins = [(yc, _rows(d)), (sp["b_conv_out"], _whole(sp["b_conv_out"])), (ya, _rows(d)), (z_gate, _rows(d, 0)),
               (z_gate, _rows(d, 1)), (b_gate_c, _whole(b_gate_c)), (b_gate_a, _whole(b_gate_a))]
    (mix,) = _rowwise(lambda i, *a: _mix(*a), mix_ins, [(SDS((t, d), BF16), _rows(d))], [], name="mix_fwd", n_rows=t)
    h1 = _mm(mix, wts["w_out"], mode="nn", out_dtype=F32, name="mm_out", add=h0)

    hn_ = _rms_fwd(h1, sp["g_ffn"], "rms_ffn")
    ab = _mm(hn_, wts["w_gu"], mode="nn", out_dtype=F32, name="mm_ffn_in")
    tr_ffn = 64
    (f,) = _rowwise(lambda i, a, b: _swiglu(a, b), [(ab, _rows(dff, 0, tr_ffn)), (ab, _rows(dff, 1, tr_ffn))],
                    [(SDS((t, dff), BF16), _rows(dff, 0, tr_ffn))], [], name="swiglu_fwd", n_rows=t, tr=tr_ffn)
    h2 = _mm(f, wts["w_ffn_down"], mode="nn", out_dtype=F32, name="mm_ffn_down", add=h1)

    def head(i, h, g, tgt):
        y, vjp = jax.vjp(_rms, h, g)
        row = i * ROW_TILE + lax.broadcasted_iota(jnp.int32, (ROW_TILE, 1), 0)
        valid = jnp.logical_and(row >= N_META, row < length)
        err = jnp.where(valid, y - tgt, 0.0)
        dh, dg = vjp(err / d)
        loss = 0.5 * jnp.sum(jnp.sum(err * err, axis=-1, keepdims=True), axis=0, keepdims=True) / d
        return dh, dh, dg, jnp.broadcast_to(loss, (1, LANES))

    dh2, dh2_b, g_final, loss_v = _rowwise(
        head, [(h2, _rows(d)), (sp["g_final"], _whole(sp["g_final"])), (target_p, _rows(d))],
        [(SDS((t, d), F32), _rows(d)), (SDS((t, d), BF16), _rows(d))], [SDS((1, d), F32), SDS((1, LANES), F32)],
        name="loss_head", n_rows=t)
    loss = loss_v[0, 0]

    g_ffn_down = _mm(f, dh2_b, mode="tn", out_dtype=BF16, name="mm_g_ffn_down")
    df = _mm(dh2_b, wts["w_ffn_down"], mode="nt", out_dtype=F32, name="mm_d_f")

    def swiglu_bwd(i, a, b, df):
        _, vjp = jax.vjp(_swiglu, a, b)
        da, db = vjp(df)
        return jnp.concatenate([da, db], axis=-1)

    (dab,) = _rowwise(swiglu_bwd, [(ab, _rows(dff, 0, tr_ffn)), (ab, _rows(dff, 1, tr_ffn)), (df, _rows(dff, 0, tr_ffn))],
                      [(SDS((t, 2 * dff), BF16), _rows(2 * dff, 0, tr_ffn))], [], name="swiglu_bwd", n_rows=t, tr=tr_ffn)
    g_gu = _mm(hn_, dab, mode="tn", out_dtype=BF16, name="mm_g_ffn_in")
    dhn = _mm(dab, wts["w_gu"], mode="nt", out_dtype=F32, name="mm_d_hn")
    sent = ship("ffn", {"w_gu": g_gu, "w_ffn_down": g_ffn_down})
    dh1, dh1_b, g_g_ffn = _rms_bwd(h1, sp["g_ffn"] + sent, dhn, dh2, "rms_ffn_bwd")

    g_w_out = _mm(mix, dh1_b, mode="tn", out_dtype=BF16, name="mm_g_out")
    dmix = _mm(dh1_b, wts["w_out"], mode="nt", out_dtype=F32, name="mm_d_mix")

    def mix_bwd(i, yc, bco, ya, zc, za, bgc, bga, dmix):
        _, vjp = jax.vjp(_mix, yc, bco, ya, zc, za, bgc, bga)
        dyc, dbco, dya, dzc, dza, dbgc, dbga = vjp(dmix)
        return dyc, dya, jnp.concatenate([dzc, dza], axis=-1), dbco, dbgc, dbga

    dyc, dya, dz_gate, g_b_conv_out, g_bgc, g_bga = _rowwise(
        mix_bwd, mix_ins + [(dmix, _rows(d))],
        [(SDS((t, d), BF16), _rows(d)), (SDS((t, d), BF16), _rows(d)), (SDS((t, 2 * d), BF16), _rows(2 * d))],
        [SDS((1, d), F32)] * 3, name="mix_bwd", n_rows=t)

    g_attn_out = _mm(o, dya, mode="tn", out_dtype=BF16, name="mm_g_attn_out")
    d_o = _mm(dya, wts["w_attn_out"], mode="nt", out_dtype=BF16, name="mm_d_o")
    dq_hm, dk_hm, dv_hm = _attn_bwd(q_hm, k_hm, v_hm, lse, d_o)
    dq, dkv, dkr = _attn_post(dq_hm, dk_hm, dv_hm, cos, sin)
    g_uq = _mm(cq, dq, mode="tn", out_dtype=BF16, name="mm_g_uq")
    g_ukv = _mm(ckv, dkv, mode="tn", out_dtype=BF16, name="mm_g_ukv")
    dcq = _mm(dq, wts["w_uq"], mode="nt", out_dtype=F32, name="mm_d_cq")
    dckv = _mm(dkv, wts["w_ukv"], mode="nt", out_dtype=F32, name="mm_d_ckv")

    def lora_bwd(i, zq, zk, gq, gk, dcq, dckv, dkr):
        _, vq = jax.vjp(_rms, zq, gq)
        _, vk = jax.vjp(_rms, zk, gk)
        dzq, dgq = vq(dcq)
        dzk, dgk = vk(dckv)
        return jnp.concatenate([dzq, dzk, dkr], axis=-1), dgq, dgk

    dz_l, g_g_q, g_g_kv = _rowwise(
        lora_bwd, lora_ins + [(dcq, _rows(ql)), (dckv, _rows(ql)), (dkr, _rows(LANES))],
        [(SDS((t, 2 * ql + LANES), BF16), _rows(2 * ql + LANES))], [SDS((1, ql), F32)] * 2,
        name="lora_norm_bwd", n_rows=t)

    g_conv_out = _mm(c3, dyc, mode="tn", out_dtype=BF16, name="mm_g_conv_out")
    dc3 = _mm(dyc, wts["w_conv_out"], mode="nt", out_dtype=F32, name="mm_d_c3")
    sent = ship("mixers", {"w_conv_out": g_conv_out, "w_uq": g_uq, "w_ukv": g_ukv, "w_attn_out": g_attn_out,
                           "w_out": g_w_out})

    def ln_bwd(i, c, g, b, dc3):
        _, vjp = jax.vjp(_ln_silu, c, g, b)
        return vjp(dc3)

    g_ln_sent = sp["g_conv_ln"] + sent
    dc1, g_g_ln, g_b_ln = _rowwise(
        ln_bwd, [ln_ins[0], (g_ln_sent, _whole(g_ln_sent)), ln_ins[2], (dc3, _rows(ch))],
        [(SDS((t, ch), F32), _rows(ch))], [SDS((1, ch), F32)] * 2, name="ln_silu_bwd", n_rows=t)
    dc0, g_w_dw, g_b_dw = _conv_bwd(dc1, c0, w_dw)

    def glu_bwd(i, za, zb, ba, bb, dc0):
        _, vjp = jax.vjp(_glu, za, zb, ba, bb)
        dza, dzb, dba, dbb = vjp(dc0)
        return jnp.concatenate([dza, dzb], axis=-1), dba, dbb

    dz_glu, g_bga_, g_bgb_ = _rowwise(glu_bwd, glu_ins + [(dc0, _rows(ch))],
                                      [(SDS((t, 2 * ch), BF16), _rows(2 * ch))], [SDS((1, ch), F32)] * 2,
                                      name="glu_bwd", n_rows=t)

    g_glu = _mm(u, dz_glu, mode="tn", out_dtype=BF16, name="mm_g_w_glu")
    g_lora = _mm(u, dz_l, mode="tn", out_dtype=BF16, name="mm_g_w_lora")
    g_gate = _mm(u, dz_gate, mode="tn", out_dtype=BF16, name="mm_g_w_gate")
    du = _mm(dz_glu, wts["w_glu"], mode="nt", out_dtype=F32, name="mm_d_u0")
    du = _mm(dz_l, wts["w_lora"], mode="nt", out_dtype=F32, name="mm_d_u1", add=du)
    du = _mm(dz_gate, wts["w_gate"], mode="nt", out_dtype=F32, name="mm_d_u2", add=du)
    dh0, _, g_g_mix = _rms_bwd(h0, sp["g_mix"], du, dh1, "rms_mix_bwd")

    big = {"w_glu": g_glu, "w_lora": g_lora, "w_gate": g_gate, "meta_tokens": dh0[:N_META], "w_dw": g_w_dw}
    small = {
        "g_mix": g_g_mix, "b_glu": jnp.concatenate([g_bga_, g_bgb_], axis=1),
        "b_gate": jnp.concatenate([g_bgc, g_bga], axis=1), "b_dw": g_b_dw, "g_conv_ln": g_g_ln, "b_conv_ln": g_b_ln,
        "b_conv_out": g_b_conv_out, "g_q_lora": g_g_q, "g_kv_lora": g_g_kv, "g_ffn": g_g_ffn, "g_final": g_final,
    }
    return loss, dh0[N_META:length], big, small


def _slot(ref, dev):
    return ref.at[dev]


def _row_window(rows):
    return lambda ref, dev: ref.at[pl.ds(pl.multiple_of(dev * rows, 16), rows)]


def _col_window(width, offset=0):
    return lambda ref, dev: ref.at[:, pl.ds(pl.multiple_of(offset + dev * width, LANES), width)]


def _dev_index(x, y, c):
    return 4 * x + 2 * y + c


def _gather_weights(items, out_shapes):
    srcs = [it[0] for it in items]
    n, n_out = len(srcs), len(out_shapes)

    def body(*refs):
        src, out = refs[:n], refs[n:n + n_out]
        send_sems, recv_sems, local_sems = refs[n + n_out:]
        x, y, c = lax.axis_index("x"), lax.axis_index("y"), lax.axis_index("c")
        me, sibling = (x, y, c), (x, y, 1 - c)
        chips = [(1 - x, y), (x, 1 - y), (1 - x, 1 - y)]

        def place(i, block):
            _, o, window = items[i]
            return window(out[o], _dev_index(*block))

        def copy(k, i, block, to, from_src=False):
            return pltpu.make_async_remote_copy(
                src_ref=src[i] if from_src else place(i, block), dst_ref=place(i, block),
                send_sem=send_sems.at[k * n + i], recv_sem=recv_sems.at[k * n + i],
                device_id=to, device_id_type=pl.DeviceIdType.MESH)

        mine = [pltpu.make_async_copy(src[i], place(i, me), local_sems.at[i]) for i in range(n)]
        first = [copy(0, i, me, sibling, True) for i in range(n)]
        first += [copy(1 + j, i, me, (*chip, c), True) for j, chip in enumerate(chips) for i in range(n)]
        for cp in mine + first:
            cp.start()
        passed = [[copy(4 + j, i, (*chip, c), sibling) for i in range(n)] for j, chip in enumerate(chips)]
        for j, chip in enumerate(chips):
            for i in range(n):
                copy(1 + j, i, (*chip, c), me).wait_recv()
            for cp in passed[j]:
                cp.start()
        for i in range(n):
            copy(0, i, sibling, me).wait_recv()
        for j, chip in enumerate(chips):
            for i in range(n):
                copy(4 + j, i, (*chip, 1 - c), me).wait_recv()
        for cp in first + [cp for row in passed for cp in row]:
            cp.wait_send()
        for cp in mine:
            cp.wait()

    any_spec = pl.BlockSpec(memory_space=pl.ANY)
    return pl.pallas_call(
        body, name="gather_weights", in_specs=[any_spec] * n, out_specs=[any_spec] * n_out, out_shape=out_shapes,
        scratch_shapes=[pltpu.SemaphoreType.DMA((7 * n,)), pltpu.SemaphoreType.DMA((7 * n,)),
                        pltpu.SemaphoreType.DMA((n,))],
    )(*srcs)


def _exchange(srcs, out_shapes, items, *, masks, name):
    n_src, n_out, n = len(srcs), len(out_shapes), len(items)

    def body(*refs):
        src, out = refs[:n_src], refs[n_src:n_src + n_out]
        send_sems, recv_sems = refs[n_src + n_out:]
        remote = _peer_copies(src, out, send_sems, recv_sems, items, masks)
        for cp in remote:
            cp.start()
        for cp in remote:
            cp.wait()

    any_spec = pl.BlockSpec(memory_space=pl.ANY)
    return pl.pallas_call(
        body, name=name, in_specs=[any_spec] * n_src, out_specs=[any_spec] * n_out, out_shape=out_shapes,
        scratch_shapes=[pltpu.SemaphoreType.DMA((len(masks) * n,)), pltpu.SemaphoreType.DMA((len(masks) * n,))],
    )(*srcs)


def _peer_copies(src, out, send_sems, recv_sems, items, masks):
    x, y, c = lax.axis_index("x"), lax.axis_index("y"), lax.axis_index("c")
    me = _dev_index(x, y, c)
    n = len(items)
    copies = []
    for k, mask in enumerate(masks):
        px, py, pc = x ^ ((mask >> 2) & 1), y ^ ((mask >> 1) & 1), c ^ (mask & 1)
        peer = _dev_index(px, py, pc)
        for i, (s, s_win, o, d_win) in enumerate(items):
            copies.append(pltpu.make_async_remote_copy(
                src_ref=s_win(src[s], peer), dst_ref=d_win(out[o], me),
                send_sem=send_sems.at[k * n + i], recv_sem=recv_sems.at[k * n + i],
                device_id=(px, py, pc), device_id_type=pl.DeviceIdType.MESH))
    return copies


def _exchange_start(srcs, out_shapes, items, *, masks, name):
    n_src, n_out, n = len(srcs), len(out_shapes), len(items)
    n_sem = len(masks) * n
    n_buf = n_src + n_out

    def body(*refs):
        src, land = refs[:n_src], refs[n_src:n_buf]
        send_sems, recv_sems = refs[n_buf], refs[n_buf + 1]
        token = refs[-1]
        for cp in _peer_copies(src, land, send_sems, recv_sems, items, masks):
            cp.start()
        token[...] = jnp.zeros_like(token)

    hbm = pl.BlockSpec(memory_space=pltpu.HBM)
    sem = pl.BlockSpec(memory_space=pltpu.SEMAPHORE)
    bufs = [pltpu.with_memory_space_constraint(a, pltpu.HBM) for a in srcs]
    bufs += [pltpu.with_memory_space_constraint(lax.empty(s.shape, s.dtype), pltpu.HBM) for s in out_shapes]
    res = pl.pallas_call(
        body, name=name,
        out_shape=(pltpu.SemaphoreType.DMA((n_sem,)), pltpu.SemaphoreType.DMA((n_sem,)),
                   *[pltpu.HBM(b.shape, b.dtype) for b in bufs], SDS((8, LANES), F32)),
        in_specs=[hbm] * n_buf, out_specs=(sem, sem, *[hbm] * n_buf, pl.BlockSpec(memory_space=pltpu.VMEM)),
        input_output_aliases={i: 2 + i for i in range(n_buf)},
        compiler_params=pltpu.CompilerParams(has_side_effects=pltpu.SideEffectType.DATAFLOW_SIDE_EFFECTING),
    )(*bufs)
    return res[0], res[1], list(res[2:2 + n_src]), list(res[2 + n_src:2 + n_buf]), res[-1][0, 0]


def _exchange_wait(send_sems, recv_sems, srcs, lands, items, after, *, masks, name):
    n_src, n_out = len(srcs), len(lands)
    n_buf = n_src + n_out

    def body(*refs):
        src, land = refs[:n_src], refs[n_src:n_buf]
        send_sems, recv_sems = refs[n_buf], refs[n_buf + 1]
        for cp in _peer_copies(src, land, send_sems, recv_sems, items, masks):
            cp.wait_send()
            cp.wait_recv()

    hbm = pl.BlockSpec(memory_space=pltpu.HBM)
    sem = pl.BlockSpec(memory_space=pltpu.SEMAPHORE)
    bufs = list(srcs) + list(lands)
    res = pl.pallas_call(
        body, name=name, out_shape=tuple(pltpu.HBM(b.shape, b.dtype) for b in bufs),
        in_specs=[hbm] * n_buf + [sem, sem, pl.BlockSpec(memory_space=pl.ANY)], out_specs=tuple([hbm] * n_buf),
        input_output_aliases={i: i for i in range(n_buf)},
        compiler_params=pltpu.CompilerParams(has_side_effects=pltpu.SideEffectType.DATAFLOW_SIDE_EFFECTING),
    )(*bufs, send_sems, recv_sems, after)
    return list(res[:n_src]), list(res[n_src:])


def _regroup(srcs, dsts, *, name, tr=256):
    def segments(shape, valid):
        if len(shape) == 3:
            return [(p, shape[2]) for p in range(shape[0])]
        return [(None, valid)]

    src_arrays = [s[0] if isinstance(s, tuple) else s for s in srcs]
    src_valid = [s[1] if isinstance(s, tuple) else s.shape[-1] for s in srcs]
    k_rows = src_arrays[0].shape[-2]
    src_segs = [(i, p, w) for i, a in enumerate(src_arrays) for p, w in segments(a.shape, src_valid[i])]
    dst_segs = [(j, p, w) for j, (shape, _, valid) in enumerate(dsts) for p, w in segments(shape, valid)]
    pieces = []
    si, so, di, do = 0, 0, 0, 0
    while si < len(src_segs) and di < len(dst_segs):
        n = min(src_segs[si][2] - so, dst_segs[di][2] - do)
        pieces.append((src_segs[si][0], src_segs[si][1], so, dst_segs[di][0], dst_segs[di][1], do, n))
        so, do = so + n, do + n
        if so == src_segs[si][2]:
            si, so = si + 1, 0
        if do == dst_segs[di][2]:
            di, do = di + 1, 0
    assert si == len(src_segs) and di == len(dst_segs), "source and destination columns differ in number"
    n_src = len(src_arrays)

    def body(*refs):
        src, dst = refs[:n_src], refs[n_src:]
        for j, (shape, dtype, valid) in enumerate(dsts):
            if len(shape) == 2 and valid < shape[1]:
                dst[j][:, valid:shape[1]] = jnp.zeros((tr, shape[1] - valid), dtype)
        for i, sp, so, j, dp, do, n in pieces:
            val = src[i][:, so:so + n] if sp is None else src[i][sp, :, so:so + n]
            if dp is None:
                dst[j][:, do:do + n] = val.astype(dst[j].dtype)
            else:
                dst[j][dp, :, do:do + n] = val.astype(dst[j].dtype)

    def spec(shape):
        if len(shape) == 3:
            return pl.BlockSpec((shape[0], tr, shape[2]), lambda i: (0, i, 0))
        return pl.BlockSpec((tr, shape[1]), lambda i: (i, 0))

    return pl.pallas_call(
        body, name=name, grid=(k_rows // tr,), in_specs=[spec(a.shape) for a in src_arrays],
        out_specs=[spec(shape) for shape, _, _ in dsts], out_shape=[SDS(shape, dtype) for shape, dtype, _ in dsts],
        compiler_params=_params(),
    )(*src_arrays)


def _cast_bf16(a, name):
    r, c = a.shape
    tr = _row_tile(r, c * 4)
    (out,) = _rowwise(lambda i, v: v, [(a, _rows(c, 0, tr))], [(SDS((r, c), BF16), _rows(c, 0, tr))], [], name=name,
                      n_rows=r, tr=tr)
    return out


def _row_tile(rows, row_bytes, limit=2 * 1024 * 1024):
    best = None
    for t in range(16, rows + 1, 16):
        if rows % t == 0 and t * row_bytes <= limit:
            best = t
    return best if best is not None else rows


def _adamw(me, recv, own, own_kind, w, m, v, *, name):
    n_rows, width = w.shape
    tr = _row_tile(n_rows, width * 4, limit=1024 * 1024)
    n_tiles = n_rows // tr
    if own_kind[0] == "slot":
        own_spec = pl.BlockSpec((None, tr, width), lambda i, me: (me[0], i, 0))
    elif own_kind[0] == "rows":
        own_spec = pl.BlockSpec((tr, width), lambda i, me: (me[0] * n_tiles + i, 0))
    elif own_kind[0] == "cols":
        own_spec = pl.BlockSpec((tr, width), lambda i, me: (i, own_kind[1] + me[0]))
    else:
        own_spec = pl.BlockSpec((tr, width), lambda i, me: (i, 0))

    def body(me_ref, r_ref, own_ref, w_ref, m_ref, v_ref, g_ref, d_ref, mo_ref, vo_ref):
        mine = own_ref[...].astype(F32)
        g = None
        for q in range(N_DEV):
            term = jnp.where(me_ref[0] == q, mine, r_ref[q].astype(F32))
            g = term if g is None else g + term
        m_new = ADAM_B1 * m_ref[...] + (1.0 - ADAM_B1) * g
        v_new = ADAM_B2 * v_ref[...] + (1.0 - ADAM_B2) * jnp.square(g)
        m_hat = m_new / (1.0 - ADAM_B1 ** ADAM_STEP)
        v_hat = v_new / (1.0 - ADAM_B2 ** ADAM_STEP)
        g_ref[...] = g
        d_ref[...] = -ADAM_LR * (m_hat / (jnp.sqrt(v_hat) + ADAM_EPS) + ADAM_WD * w_ref[...])
        mo_ref[...] = m_new
        vo_ref[...] = v_new

    row = pl.BlockSpec((tr, width), lambda i, me: (i, 0))
    return pl.pallas_call(
        body, name=name,
        grid_spec=pltpu.PrefetchScalarGridSpec(
            num_scalar_prefetch=1, grid=(n_tiles,),
            in_specs=[pl.BlockSpec((N_DEV, tr, width), lambda i, me: (0, i, 0)), own_spec, row, row, row],
            out_specs=[row] * 4),
        out_shape=[SDS((n_rows, width), F32)] * 4, compiler_params=_params(),
    )(me, recv, own, w, m, v)


def _offsets(sizes):
    offs, o = [], 0
    for n in sizes:
        offs.append(o)
        o += n
    return offs, o


def kernel(x, meta_tokens, g_mix, w_in, b_glu, b_gate, w_dw, b_dw, g_conv_ln, b_conv_ln, w_conv_out, b_conv_out, g_q_lora, w_uq, g_kv_lora, w_uk, w_uv, w_attn_out, w_out, g_ffn, w_ffn_gate, w_ffn_up, w_ffn_down, g_final, loss_target, m_meta_tokens, m_g_mix, m_w_in, m_b_glu, m_b_gate, m_w_dw, m_b_dw, m_g_conv_ln, m_b_conv_ln, m_w_conv_out, m_b_conv_out, m_g_q_lora, m_w_uq, m_g_kv_lora, m_w_uk, m_w_uv, m_w_attn_out, m_w_out, m_g_ffn, m_w_ffn_gate, m_w_ffn_up, m_w_ffn_down, m_g_final, v_meta_tokens, v_g_mix, v_w_in, v_b_glu, v_b_gate, v_w_dw, v_b_dw, v_g_conv_ln, v_b_conv_ln, v_w_conv_out, v_b_conv_out, v_g_q_lora, v_w_uq, v_g_kv_lora, v_w_uk, v_w_uv, v_w_attn_out, v_w_out, v_g_ffn, v_w_ffn_gate, v_w_ffn_up, v_w_ffn_down, v_g_final):
    w_all = dict(meta_tokens=meta_tokens, g_mix=g_mix, w_in=w_in, b_glu=b_glu, b_gate=b_gate, w_dw=w_dw, b_dw=b_dw, g_conv_ln=g_conv_ln, b_conv_ln=b_conv_ln, w_conv_out=w_conv_out, b_conv_out=b_conv_out, g_q_lora=g_q_lora, w_uq=w_uq, g_kv_lora=g_kv_lora, w_uk=w_uk, w_uv=w_uv, w_attn_out=w_attn_out, w_out=w_out, g_ffn=g_ffn, w_ffn_gate=w_ffn_gate, w_ffn_up=w_ffn_up, w_ffn_down=w_ffn_down, g_final=g_final)
    m_all = dict(meta_tokens=m_meta_tokens, g_mix=m_g_mix, w_in=m_w_in, b_glu=m_b_glu, b_gate=m_b_gate, w_dw=m_w_dw, b_dw=m_b_dw, g_conv_ln=m_g_conv_ln, b_conv_ln=m_b_conv_ln, w_conv_out=m_w_conv_out, b_conv_out=m_b_conv_out, g_q_lora=m_g_q_lora, w_uq=m_w_uq, g_kv_lora=m_g_kv_lora, w_uk=m_w_uk, w_uv=m_w_uv, w_attn_out=m_w_attn_out, w_out=m_w_out, g_ffn=m_g_ffn, w_ffn_gate=m_w_ffn_gate, w_ffn_up=m_w_ffn_up, w_ffn_down=m_w_ffn_down, g_final=m_g_final)
    v_all = dict(meta_tokens=v_meta_tokens, g_mix=v_g_mix, w_in=v_w_in, b_glu=v_b_glu, b_gate=v_b_gate, w_dw=v_w_dw, b_dw=v_b_dw, g_conv_ln=v_g_conv_ln, b_conv_ln=v_b_conv_ln, w_conv_out=v_w_conv_out, b_conv_out=v_b_conv_out, g_q_lora=v_g_q_lora, w_uq=v_w_uq, g_kv_lora=v_g_kv_lora, w_uk=v_w_uk, w_uv=v_w_uv, w_attn_out=v_w_attn_out, w_out=v_w_out, g_ffn=v_g_ffn, w_ffn_gate=v_w_ffn_gate, w_ffn_up=v_w_ffn_up, w_ffn_down=v_w_ffn_down, g_final=v_g_final)

    two_d = lambda a: a.reshape(a.shape[-2:]) if a.ndim >= 2 else a.reshape(1, -1)
    sh = {n: two_d(w_all[n]) for n in SHARDED}
    d = x.shape[-1]
    k_in, c_in = sh["w_in"].shape
    r_co, r_ao, r_wo, r_fd = (sh[n].shape[0] for n in ROW_SHARDED)
    ql, c_uq = sh["w_uq"].shape
    c_uk = sh["w_uk"].shape[1]
    c_ff = sh["w_ffn_gate"].shape[1]
    n_meta, c_meta = sh["meta_tokens"].shape
    n_taps, c_dw = sh["w_dw"].shape
    ch, dff = N_DEV * c_dw, N_DEV * c_ff
    n_lora = 2 * ql + QK_ROPE

    bf = {n: _cast_bf16(sh[n], "cast_" + n) for n in BIG}
    gathered = _gather_weights(
        [(bf["w_in"], 0, _slot), (bf["w_conv_out"], 1, _row_window(r_co)), (bf["w_uq"], 2, _col_window(c_uq)),
         (bf["w_uk"], 3, _col_window(c_uk)), (bf["w_uv"], 3, _col_window(c_uk, N_DEV * c_uk)),
         (bf["w_attn_out"], 4, _row_window(r_ao)), (bf["w_out"], 5, _row_window(r_wo)),
         (bf["w_ffn_gate"], 6, _slot), (bf["w_ffn_up"], 7, _slot), (bf["w_ffn_down"], 8, _row_window(r_fd)),
         (sh["meta_tokens"], 9, _col_window(c_meta)), (sh["w_dw"], 10, _col_window(c_dw))],
        [SDS((N_DEV, k_in, c_in), BF16), SDS((N_DEV * r_co, d), BF16), SDS((ql, N_DEV * c_uq), BF16),
         SDS((ql, 2 * N_DEV * c_uk), BF16), SDS((N_DEV * r_ao, d), BF16), SDS((N_DEV * r_wo, d), BF16),
         SDS((N_DEV, d, c_ff), BF16), SDS((N_DEV, d, c_ff), BF16), SDS((N_DEV * r_fd, d), BF16),
         SDS((n_meta, N_DEV * c_meta), F32), SDS((n_taps, ch), F32)])
    w_glu, w_lora, w_gate = _regroup(
        [gathered[0]], [((k_in, 2 * ch), BF16, 2 * ch), ((k_in, n_lora + LANES - QK_ROPE), BF16, n_lora),
                        ((k_in, 2 * d), BF16, 2 * d)], name="unpack_w_in")
    (w_gu,) = _regroup([gathered[6], gathered[7]], [((d, 2 * dff), BF16, 2 * dff)], name="unpack_w_ffn_in")
    wts = {"w_glu": w_glu, "w_lora": w_lora, "w_gate": w_gate, "w_conv_out": gathered[1], "w_uq": gathered[2],
           "w_ukv": gathered[3], "w_attn_out": gathered[4], "w_out": gathered[5], "w_gu": w_gu,
           "w_ffn_down": gathered[8]}
    sp = {n: w_all[n].reshape(1, -1) for n in SMALL}

    masks = tuple(range(1, N_DEV))
    recv_shape = lambda n: SDS((N_DEV,) + sh[n].shape, F32 if n in F32_GATHERED else BF16)
    in_flight = {}

    def ship(group, grads):
        if group == "ffn":
            s_gate, s_up = _regroup([grads["w_gu"]], [((N_DEV, d, c_ff), BF16, None)] * 2, name="pack_g_w_ffn_in")
            srcs = [s_gate, s_up, grads["w_ffn_down"]]
            names = ("w_ffn_gate", "w_ffn_up", "w_ffn_down")
            items = [(0, _slot, 0, _slot), (1, _slot, 1, _slot), (2, _row_window(r_fd), 2, _slot)]
            own = [(0, ("slot",)), (1, ("slot",)), (2, ("rows",))]
        else:
            srcs = [grads["w_conv_out"], grads["w_uq"], grads["w_ukv"], grads["w_attn_out"], grads["w_out"]]
            names = ("w_conv_out", "w_uq", "w_uk", "w_uv", "w_attn_out", "w_out")
            items = [(0, _row_window(r_co), 0, _slot), (1, _col_window(c_uq), 1, _slot),
                     (2, _col_window(c_uk), 2, _slot), (2, _col_window(c_uk, N_DEV * c_uk), 3, _slot),
                     (3, _row_window(r_ao), 4, _slot), (4, _row_window(r_wo), 5, _slot)]
            own = [(0, ("rows",)), (1, ("cols", 0)), (2, ("cols", 0)), (2, ("cols", N_DEV)), (3, ("rows",)),
                   (4, ("rows",))]
        send_sems, recv_sems, srcs, lands, zero = _exchange_start(
            srcs, [recv_shape(n) for n in names], items, masks=masks, name="send_grads_" + group)
        in_flight[group] = (send_sems, recv_sems, srcs, lands, items, names, own)
        return zero

    loss, grad_x, g_big, g_small = _local_step(x[0], loss_target[0], gathered[9], gathered[10], wts, sp, ship)

    (s_in,) = _regroup([g_big["w_glu"], (g_big["w_lora"], n_lora), g_big["w_gate"]],
                       [((N_DEV, k_in, c_in), BF16, None)], name="pack_g_w_in")
    s_sizes = [w_all[n].size for n in SMALL]
    s_offs, n_s = _offsets(s_sizes)
    cat_small = lambda src: jnp.concatenate([src[n].reshape(1, -1) for n in SMALL], axis=1)
    whole_ref = lambda ref, dev: ref
    last_names = ("w_in", "meta_tokens", "w_dw")
    last_srcs = [s_in, g_big["meta_tokens"], g_big["w_dw"], cat_small(g_small)]
    last_own = [(0, ("slot",)), (1, ("cols", 0)), (2, ("cols", 0))]
    last = _exchange(
        last_srcs, [recv_shape(n) for n in last_names] + [SDS((N_DEV, 1, n_s), F32)],
        [(0, _slot, 0, _slot), (1, _col_window(c_meta), 1, _slot), (2, _col_window(c_dw), 2, _slot),
         (3, whole_ref, 3, _slot)], masks=masks, name="exchange_grads_last")

    me = (4 * lax.axis_index("x") + 2 * lax.axis_index("y") + lax.axis_index("c")).astype(jnp.int32).reshape(1)
    by_name = {}

    def update(n, recv, own, kind):
        outs = _adamw(me, recv, own, kind, sh[n], two_d(m_all[n]), two_d(v_all[n]), name="adamw_" + n)
        by_name[n] = [o.reshape(w_all[n].shape) for o in outs]

    for group in ("ffn", "mixers"):
        send_sems, recv_sems, srcs, lands, items, names, own = in_flight[group]
        srcs, lands = _exchange_wait(send_sems, recv_sems, srcs, lands, items, last[0], masks=masks,
                                     name="wait_grads_" + group)
        for n, land, (s, kind) in zip(names, lands, own):
            update(n, land, srcs[s], kind)
    for n, recv, (s, kind) in zip(last_names, last, last_own):
        update(n, recv, last_srcs[s], kind)
    outs = _adamw(me, last[3], last_srcs[3], ("whole",), cat_small(w_all), cat_small(m_all), cat_small(v_all),
                  name="adamw_replicated")
    for n, o, s in zip(SMALL, s_offs, s_sizes):
        by_name[n] = [out[:, o:o + s].reshape(w_all[n].shape) for out in outs]
    result = [[by_name[n][k] for n in WEIGHTS] for k in range(4)]
    loss = lax.psum(loss, ("x", "y", "c"))
    return (loss, grad_x[None], *result[0], *result[1], *result[2], *result[3])
```

```python
import functools

import jax
import jax.numpy as jnp
from jax import lax
from jax.experimental import pallas as pl
from jax.experimental.pallas import tpu as pltpu

F32, BF16 = jnp.float32, jnp.bfloat16
SDS = jax.ShapeDtypeStruct

N_DEV = 8
N_META = 16
BLOCK_Q = 128
CONV_WIDTH = 31
QK_NOPE, QK_ROPE, V_HEAD = 128, 64, 128
QK = QK_NOPE + QK_ROPE
ROPE_THETA = 10000.0
EPS = 1e-6
ADAM_LR, ADAM_B1, ADAM_B2, ADAM_EPS, ADAM_WD, ADAM_STEP = 0.001, 0.9, 0.999, 1e-08, 0.01, 10

LANES = 128
ROW_TILE = 128
PACK_W = 1024
VMEM_LIMIT = 56 * 1024 * 1024

BIG = ("w_in", "w_conv_out", "w_uq", "w_uk", "w_uv", "w_attn_out", "w_out", "w_ffn_gate", "w_ffn_up", "w_ffn_down")
F32_GATHERED = ("meta_tokens", "w_dw")
SHARDED = BIG + F32_GATHERED
ROW_SHARDED = ("w_conv_out", "w_attn_out", "w_out", "w_ffn_down")
SMALL = ("g_mix", "b_glu", "b_gate", "b_dw", "g_conv_ln", "b_conv_ln", "b_conv_out", "g_q_lora", "g_kv_lora",
         "g_ffn", "g_final")
WEIGHTS = ("meta_tokens", "g_mix", "w_in", "b_glu", "b_gate", "w_dw", "b_dw", "g_conv_ln", "b_conv_ln", "w_conv_out",
           "b_conv_out", "g_q_lora", "w_uq", "g_kv_lora", "w_uk", "w_uv", "w_attn_out", "w_out", "g_ffn", "w_ffn_gate",
           "w_ffn_up", "w_ffn_down", "g_final")


def _params():
    return pltpu.CompilerParams(vmem_limit_bytes=VMEM_LIMIT)


def _tile(dim, limit):
    best = None
    t = LANES
    while t <= min(dim, limit):
        if dim % t == 0:
            best = t
        t += LANES
    return best if best is not None else dim


def _mm(a, b, *, mode, out_dtype, name, add=None, after=None):
    if mode == "nn":
        (m, kc), n = a.shape, b.shape[1]
    elif mode == "nt":
        (m, kc), n = a.shape, b.shape[0]
    else:
        (kc, m), n = a.shape, b.shape[1]
    if mode == "tn":
        tm, tn, tk = _tile(m, 1024), _tile(n, 512), kc
    else:
        tm, tn, tk = m, _tile(n, 512), _tile(kc, 512)
    nk = kc // tk
    if mode == "nn":
        a_spec = pl.BlockSpec((tm, tk), lambda i, j, k: (i, k))
        b_spec = pl.BlockSpec((tk, tn), lambda i, j, k: (k, j))
        dims = (((1,), (0,)), ((), ()))
    elif mode == "nt":
        a_spec = pl.BlockSpec((tm, tk), lambda i, j, k: (i, k))
        b_spec = pl.BlockSpec((tn, tk), lambda i, j, k: (j, k))
        dims = (((1,), (1,)), ((), ()))
    else:
        a_spec = pl.BlockSpec((tk, tm), lambda i, j, k: (k, i))
        b_spec = pl.BlockSpec((tk, tn), lambda i, j, k: (k, j))
        dims = (((0,), (0,)), ((), ()))
    o_spec = pl.BlockSpec((tm, tn), lambda i, j, k: (i, j))
    has_add = add is not None

    def body(*refs):
        if after is not None:
            refs = refs[:-3] + refs[-2:]
        if has_add:
            a_ref, b_ref, add_ref, o_ref, acc_ref = refs
        else:
            a_ref, b_ref, o_ref, acc_ref = refs
        k = pl.program_id(2)
        p = lax.dot_general(a_ref[...], b_ref[...], dims, preferred_element_type=F32)
        if nk == 1:
            o_ref[...] = ((p + add_ref[...]) if has_add else p).astype(o_ref.dtype)
            return

        @pl.when(k == 0)
        def _():
            acc_ref[...] = (p + add_ref[...]) if has_add else p

        @pl.when(jnp.logical_and(k > 0, k < nk - 1))
        def _():
            acc_ref[...] += p

        @pl.when(k == nk - 1)
        def _():
            o_ref[...] = (acc_ref[...] + p).astype(o_ref.dtype)

    in_specs = [a_spec, b_spec] + ([o_spec] if has_add else [])
    args = (a, b) + ((add,) if has_add else ())
    if after is not None:
        in_specs, args = in_specs + [pl.BlockSpec(memory_space=pl.ANY)], args + (after,)
    acc_shape = (tm, tn) if nk > 1 else (8, LANES)
    return pl.pallas_call(
        body, name=name, grid=(m // tm, n // tn, nk), in_specs=in_specs, out_specs=o_spec,
        out_shape=SDS((m, n), out_dtype), scratch_shapes=[pltpu.VMEM(acc_shape, F32)],
        compiler_params=_params(),
    )(*args)


def _rows(width, col=0, tr=ROW_TILE):
    return pl.BlockSpec((tr, width), lambda i: (i, col))


def _whole(arr):
    nd = arr.ndim
    return pl.BlockSpec(arr.shape, lambda i: (0,) * nd)


def _rowwise(fn, ins, outs, accs, *, name, n_rows, tr=ROW_TILE, after=None):
    n_in, n_out = len(ins), len(outs)
    if after is not None:
        ins = list(ins) + [(after, pl.BlockSpec(memory_space=pl.ANY))]

    def body(*refs):
        i = pl.program_id(0)
        res = fn(i, *[r[...] for r in refs[:n_in]])
        refs = refs[:n_in] + refs[len(ins):]
        res = res if isinstance(res, (tuple, list)) else (res,)
        for o_ref, v in zip(refs[n_in:n_in + n_out], res[:n_out]):
            o_ref[...] = v.astype(o_ref.dtype)
        for a_ref, v in zip(refs[n_in + n_out:], res[n_out:]):
            @pl.when(i == 0)
            def _(a_ref=a_ref, v=v):
                a_ref[...] = v

            @pl.when(i > 0)
            def _(a_ref=a_ref, v=v):
                a_ref[...] += v

    acc_specs = [pl.BlockSpec(s.shape, lambda i, nd=len(s.shape): (0,) * nd) for s in accs]
    res = pl.pallas_call(
        body, name=name, grid=(n_rows // tr,),
        in_specs=[s for _, s in ins], out_specs=[s for _, s in outs] + acc_specs,
        out_shape=[s for s, _ in outs] + list(accs), compiler_params=_params(),
    )(*[a for a, _ in ins])
    return res


def _rms(x, g):
    return x * lax.rsqrt(jnp.mean(x * x, axis=-1, keepdims=True) + EPS) * g


def _sigmoid(x):
    return 1.0 / (1.0 + jnp.exp(-x))


def _silu(x):
    return x * _sigmoid(x)


def _rms_fwd(h, g, name):
    t, d = h.shape
    (u,) = _rowwise(lambda i, h, g: _rms(h, g), [(h, _rows(d)), (g, _whole(g))], [(SDS((t, d), BF16), _rows(d))], [],
                    name=name, n_rows=t)
    return u


def _rms_bwd(h, g, du, dres, name):
    t, d = h.shape

    def fn(i, h, g, du, dres):
        _, vjp = jax.vjp(_rms, h, g)
        dh, dg = vjp(du)
        dh = dh + dres
        return dh, dh, dg

    return _rowwise(fn, [(h, _rows(d)), (g, _whole(g)), (du, _rows(d)), (dres, _rows(d))],
                    [(SDS((t, d), F32), _rows(d)), (SDS((t, d), BF16), _rows(d))], [SDS((1, d), F32)],
                    name=name, n_rows=t)


def _conv_fwd(c0, w_dw, b_dw):
    t, ch = c0.shape
    tc = _tile(ch, 256)
    halo = 32
    shift = halo - (CONV_WIDTH - 1)

    def body(x_ref, w_ref, b_ref, o_ref, pad_ref):
        pad_ref[0:halo, :] = jnp.zeros((halo, tc), F32)
        pad_ref[halo:halo + t, :] = x_ref[...]
        for r0 in range(0, t, ROW_TILE):
            acc = jnp.zeros((ROW_TILE, tc), F32) + b_ref[...]
            for j in range(CONV_WIDTH):
                acc = acc + pad_ref[r0 + shift + j:r0 + shift + j + ROW_TILE, :] * w_ref[j:j + 1, :]
            o_ref[r0:r0 + ROW_TILE, :] = acc

    col = lambda i: (0, i)
    return pl.pallas_call(
        body, name="conv_fwd", grid=(ch // tc,),
        in_specs=[pl.BlockSpec((t, tc), col), pl.BlockSpec((CONV_WIDTH, tc), col), pl.BlockSpec((1, tc), col)],
        out_specs=pl.BlockSpec((t, tc), col), out_shape=SDS((t, ch), F32),
        scratch_shapes=[pltpu.VMEM((halo + t, tc), F32)], compiler_params=_params(),
    )(c0, w_dw, b_dw)


def _conv_bwd(dc1, c0, w_dw):
    t, ch = c0.shape
    tc = _tile(ch, 256)
    halo = 32
    shift = halo - (CONV_WIDTH - 1)

    def body(d_ref, x_ref, w_ref, dx_ref, dw_ref, db_ref, xpad_ref, dpad_ref):
        xpad_ref[0:halo, :] = jnp.zeros((halo, tc), F32)
        xpad_ref[halo:halo + t, :] = x_ref[...]
        dpad_ref[0:t, :] = d_ref[...]
        dpad_ref[t:t + halo, :] = jnp.zeros((halo, tc), F32)
        for r0 in range(0, t, ROW_TILE):
            acc = jnp.zeros((ROW_TILE, tc), F32)
            for j in range(CONV_WIDTH):
                off = r0 + (CONV_WIDTH - 1) - j
                acc = acc + dpad_ref[off:off + ROW_TILE, :] * w_ref[j:j + 1, :]
            dx_ref[r0:r0 + ROW_TILE, :] = acc
        for j in range(CONV_WIDTH):
            acc = jnp.zeros((1, tc), F32)
            for r0 in range(0, t, ROW_TILE):
                prod = d_ref[r0:r0 + ROW_TILE, :] * xpad_ref[r0 + shift + j:r0 + shift + j + ROW_TILE, :]
                acc = acc + jnp.sum(prod, axis=0, keepdims=True)
            dw_ref[j:j + 1, :] = acc
        db_ref[...] = jnp.sum(d_ref[...], axis=0, keepdims=True)

    col = lambda i: (0, i)
    return pl.pallas_call(
        body, name="conv_bwd", grid=(ch // tc,),
        in_specs=[pl.BlockSpec((t, tc), col), pl.BlockSpec((t, tc), col), pl.BlockSpec((CONV_WIDTH, tc), col)],
        out_specs=[pl.BlockSpec((t, tc), col), pl.BlockSpec((CONV_WIDTH, tc), col), pl.BlockSpec((1, tc), col)],
        out_shape=[SDS((t, ch), F32), SDS((CONV_WIDTH, ch), F32), SDS((1, ch), F32)],
        scratch_shapes=[pltpu.VMEM((halo + t, tc), F32), pltpu.VMEM((halo + t, tc), F32)], compiler_params=_params(),
    )(dc1, c0, w_dw)


def _rope(x1, x2, cos, sin):
    return x1 * cos - x2 * sin, x1 * sin + x2 * cos


def _attn_prep(q, kv, z_l, kr_col, cos, sin, n_heads):
    t = q.shape[0]
    hn = n_heads * QK_NOPE
    half = QK_ROPE // 2

    def body(q_ref, kv_ref, kr_ref, cos_ref, sin_ref, qo_ref, ko_ref, vo_ref):
        cos, sin = cos_ref[...], sin_ref[...]
        kr = kr_ref[...]
        k1, k2 = _rope(kr[:, 0:half], kr[:, half:QK_ROPE], cos, sin)
        for h in range(n_heads):
            b = h * QK
            q1, q2 = _rope(q_ref[:, b + QK_NOPE:b + QK_NOPE + half], q_ref[:, b + QK_NOPE + half:b + QK], cos, sin)
            qo_ref[h] = jnp.concatenate([q_ref[:, b:b + QK_NOPE], q1, q2], axis=-1).astype(BF16)
            ko_ref[h] = jnp.concatenate([kv_ref[:, h * QK_NOPE:(h + 1) * QK_NOPE], k1, k2], axis=-1).astype(BF16)
            vo_ref[h] = kv_ref[:, hn + h * V_HEAD:hn + (h + 1) * V_HEAD].astype(BF16)

    tr = ROW_TILE
    hm = lambda w: pl.BlockSpec((n_heads, tr, w), lambda i: (0, i, 0))
    return pl.pallas_call(
        body, name="attn_prep", grid=(t // tr,),
        in_specs=[_rows(q.shape[1]), _rows(kv.shape[1]), _rows(LANES, kr_col), _rows(half), _rows(half)],
        out_specs=[hm(QK), hm(QK), hm(V_HEAD)],
        out_shape=[SDS((n_heads, t, QK), BF16), SDS((n_heads, t, QK), BF16), SDS((n_heads, t, V_HEAD), BF16)],
        compiler_params=_params(),
    )(q, kv, z_l, cos, sin)


def _attn_post(dq_hm, dk_hm, dv_hm, cos, sin):
    n_heads, t, _ = dq_hm.shape
    hn = n_heads * QK_NOPE
    half = QK_ROPE // 2

    def unrope(d1, d2, cos, sin):
        return d1 * cos + d2 * sin, d2 * cos - d1 * sin

    def body(dq_ref, dk_ref, dv_ref, cos_ref, sin_ref, qo_ref, kvo_ref, kro_ref):
        cos, sin = cos_ref[...], sin_ref[...]
        dkr = jnp.zeros((ROW_TILE, QK_ROPE), F32)
        for h in range(n_heads):
            dq = dq_ref[h]
            d1, d2 = unrope(dq[:, QK_NOPE:QK_NOPE + half], dq[:, QK_NOPE + half:QK], cos, sin)
            qo_ref[:, h * QK:(h + 1) * QK] = jnp.concatenate([dq[:, 0:QK_NOPE], d1, d2], axis=-1).astype(BF16)
            dk = dk_ref[h]
            kvo_ref[:, h * QK_NOPE:(h + 1) * QK_NOPE] = dk[:, 0:QK_NOPE].astype(BF16)
            kvo_ref[:, hn + h * V_HEAD:hn + (h + 1) * V_HEAD] = dv_ref[h].astype(BF16)
            dkr = dkr + dk[:, QK_NOPE:QK]
        d1, d2 = unrope(dkr[:, 0:half], dkr[:, half:QK_ROPE], cos, sin)
        kro_ref[...] = jnp.concatenate([d1, d2, jnp.zeros((ROW_TILE, LANES - QK_ROPE), F32)], axis=-1)

    tr = ROW_TILE
    hm = lambda w: pl.BlockSpec((n_heads, tr, w), lambda i: (0, i, 0))
    return pl.pallas_call(
        body, name="attn_post", grid=(t // tr,),
        in_specs=[hm(QK), hm(QK), hm(V_HEAD), _rows(half), _rows(half)],
        out_specs=[_rows(n_heads * QK), _rows(2 * hn), _rows(LANES)],
        out_shape=[SDS((t, n_heads * QK), BF16), SDS((t, 2 * hn), BF16), SDS((t, LANES), F32)],
        compiler_params=_params(),
    )(dq_hm, dk_hm, dv_hm, cos, sin)


N_QBLK = 4
_NT = (((1,), (1,)), ((), ()))
_TN = (((0,), (0,)), ((), ()))


def _scores(q, k, r0, scale):
    s = lax.dot_general(q, k, _NT, preferred_element_type=F32) * scale
    row = r0 + lax.broadcasted_iota(jnp.int32, s.shape, 0)
    col = lax.broadcasted_iota(jnp.int32, s.shape, 1)
    return jnp.where(col <= row, s, -jnp.inf)


def _attn_fwd(q_hm, k_hm, v_hm):
    n_heads, t, _ = q_hm.shape
    bq = t // N_QBLK
    scale = QK ** -0.5

    def body(q_ref, k_ref, v_ref, o_ref, lse_ref):
        for i in range(N_QBLK):
            r0, n_k = i * bq, (i + 1) * bq
            s = _scores(q_ref[0, r0:r0 + bq, :], k_ref[0, 0:n_k, :], r0, scale)
            m = jnp.max(s, axis=-1, keepdims=True)
            p = jnp.exp(s - m)
            l = jnp.sum(p, axis=-1, keepdims=True)
            p = (p / l).astype(BF16)
            o_ref[r0:r0 + bq, :] = jnp.dot(p, v_ref[0, 0:n_k, :], preferred_element_type=F32).astype(BF16)
            lse_ref[0, r0:r0 + bq, :] = m + jnp.log(l)

    head = lambda w: pl.BlockSpec((1, t, w), lambda h: (h, 0, 0))
    return pl.pallas_call(
        body, name="attn_fwd", grid=(n_heads,),
        in_specs=[head(QK), head(QK), head(V_HEAD)],
        out_specs=[pl.BlockSpec((t, V_HEAD), lambda h: (0, h)), head(1)],
        out_shape=[SDS((t, n_heads * V_HEAD), BF16), SDS((n_heads, t, 1), F32)],
        compiler_params=_params(),
    )(q_hm, k_hm, v_hm)


def _attn_bwd(q_hm, k_hm, v_hm, lse, d_o):
    n_heads, t, _ = q_hm.shape
    bq = t // N_QBLK
    scale = QK ** -0.5

    def body(q_ref, k_ref, v_ref, lse_ref, do_ref, dq_ref, dk_ref, dv_ref):
        dk_ref[...] = jnp.zeros(dk_ref.shape, F32)
        dv_ref[...] = jnp.zeros(dv_ref.shape, F32)
        for i in range(N_QBLK):
            r0, n_k = i * bq, (i + 1) * bq
            q = q_ref[0, r0:r0 + bq, :]
            k = k_ref[0, 0:n_k, :]
            d_o = do_ref[r0:r0 + bq, :]
            s = _scores(q, k, r0, scale)
            p = jnp.exp(s - lse_ref[0, r0:r0 + bq, :])
            dp = lax.dot_general(d_o, v_ref[0, 0:n_k, :], _NT, preferred_element_type=F32)
            ds = (p * (dp - jnp.sum(dp * p, axis=-1, keepdims=True)) * scale).astype(BF16)
            dq_ref[0, r0:r0 + bq, :] = jnp.dot(ds, k, preferred_element_type=F32)
            dk_ref[0, 0:n_k, :] += lax.dot_general(ds, q, _TN, preferred_element_type=F32)
            dv_ref[0, 0:n_k, :] += lax.dot_general(p.astype(BF16), d_o, _TN, preferred_element_type=F32)

    head = lambda w: pl.BlockSpec((1, t, w), lambda h: (h, 0, 0))
    return pl.pallas_call(
        body, name="attn_bwd", grid=(n_heads,),
        in_specs=[head(QK), head(QK), head(V_HEAD), head(1), pl.BlockSpec((t, V_HEAD), lambda h: (0, h))],
        out_specs=[head(QK), head(QK), head(V_HEAD)],
        out_shape=[SDS((n_heads, t, QK), F32), SDS((n_heads, t, QK), F32), SDS((n_heads, t, V_HEAD), F32)],
        compiler_params=_params(),
    )(q_hm, k_hm, v_hm, lse, d_o)


def _glu(za, zb, ba, bb):
    return (za + ba) * _sigmoid(zb + bb)


def _ln_silu(c, g, b):
    mu = jnp.mean(c, axis=-1, keepdims=True)
    var = jnp.mean(jnp.square(c - mu), axis=-1, keepdims=True)
    return _silu((c - mu) * lax.rsqrt(var + EPS) * g + b)


def _mix(yc, bco, ya, zc, za, bgc, bga):
    return _sigmoid(zc + bgc) * (yc + bco) + _sigmoid(za + bga) * ya


def _swiglu(a, b):
    return _silu(a) * b


def _local_step(x, target, meta, w_dw, wts, sp, ship=None, fetch=None, dims=None):
    wts = dict(wts)
    if ship is None:
        ship = lambda group, grads: jnp.zeros((8, LANES), F32)
    if fetch is None:
        fetch = lambda group, after: {}
    seq, d = x.shape
    length = N_META + seq
    t = -(-length // BLOCK_Q) * BLOCK_Q
    ch = w_dw.shape[1]
    ql = sp["g_q_lora"].shape[1]
    n_heads = dims["n_heads"] if dims else wts["w_uq"].shape[1] // QK
    hn = n_heads * QK_NOPE
    dff = dims["dff"] if dims else wts["w_ffn_down"].shape[0]
    assert sp["g_kv_lora"].shape[1] == ql and ql % LANES == 0 and t % (N_QBLK * 16) == 0
    pad_rows = lambda a: jnp.concatenate([jnp.zeros((N_META, d), F32), a, jnp.zeros((t - length, d), F32)], axis=0)
    h0 = jnp.concatenate([meta, x, jnp.zeros((t - length, d), F32)], axis=0)
    target_p = pad_rows(target)

    pos = jnp.arange(t, dtype=F32)
    inv_freq = ROPE_THETA ** (-jnp.arange(0, QK_ROPE, 2, dtype=F32) / QK_ROPE)
    ang = pos[:, None] * inv_freq[None, :]
    cos, sin = jnp.cos(ang), jnp.sin(ang)

    b_glu_a, b_glu_b = sp["b_glu"][:, :ch], sp["b_glu"][:, ch:]
    b_gate_c, b_gate_a = sp["b_gate"][:, :d], sp["b_gate"][:, d:]
    kr_col = 2 * ql // LANES

    u = _rms_fwd(h0, sp["g_mix"], "rms_mix")
    z_glu = _mm(u, wts["w_glu"], mode="nn", out_dtype=F32, name="mm_z_glu")
    z_l = _mm(u, wts["w_lora"], mode="nn", out_dtype=F32, name="mm_z_lora")
    z_gate = _mm(u, wts["w_gate"], mode="nn", out_dtype=F32, name="mm_z_gate")

    glu_ins = [(z_glu, _rows(ch, 0)), (z_glu, _rows(ch, 1)), (b_glu_a, _whole(b_glu_a)), (b_glu_b, _whole(b_glu_b))]
    (c0,) = _rowwise(lambda i, za, zb, ba, bb: _glu(za, zb, ba, bb), glu_ins, [(SDS((t, ch), F32), _rows(ch))], [],
                     name="glu_fwd", n_rows=t)
    c1 = _conv_fwd(c0, w_dw, sp["b_dw"])
    ln_ins = [(c1, _rows(ch)), (sp["g_conv_ln"], _whole(sp["g_conv_ln"])), (sp["b_conv_ln"], _whole(sp["b_conv_ln"]))]
    (c3,) = _rowwise(lambda i, c, g, b: _ln_silu(c, g, b), ln_ins, [(SDS((t, ch), BF16), _rows(ch))], [],
                     name="ln_silu_fwd", n_rows=t)
    wts.update(fetch("mixers", c3))
    yc = _mm(c3, wts["w_conv_out"], mode="nn", out_dtype=F32, name="mm_conv_out")

    lora_ins = [(z_l, _rows(ql, 0)), (z_l, _rows(ql, 1)), (sp["g_q_lora"], _whole(sp["g_q_lora"])),
                (sp["g_kv_lora"], _whole(sp["g_kv_lora"]))]
    cq, ckv = _rowwise(lambda i, zq, zk, gq, gk: (_rms(zq, gq), _rms(zk, gk)), lora_ins,
                       [(SDS((t, ql), BF16), _rows(ql)), (SDS((t, ql), BF16), _rows(ql))], [],
                       name="lora_norm_fwd", n_rows=t)
    q = _mm(cq, wts["w_uq"], mode="nn", out_dtype=F32, name="mm_q")
    kv = _mm(ckv, wts["w_ukv"], mode="nn", out_dtype=F32, name="mm_kv")
    q_hm, k_hm, v_hm = _attn_prep(q, kv, z_l, kr_col, cos, sin, n_heads)
    o, lse = _attn_fwd(q_hm, k_hm, v_hm)
    ya = _mm(o, wts["w_attn_out"], mode="nn", out_dtype=F32, name="mm_attn_out")

    mix_ins = [(yc, _rows(d)), (sp["b_conv_out"], _whole(sp["b_conv_out"])), (ya, _rows(d)), (z_gate, _rows(d, 0)),
               (z_gate, _rows(d, 1)), (b_gate_c, _whole(b_gate_c)), (b_gate_a, _whole(b_gate_a))]
    (mix,) = _rowwise(lambda i, *a: _mix(*a), mix_ins, [(SDS((t, d), BF16), _rows(d))], [], name="mix_fwd", n_rows=t)
    h1 = _mm(mix, wts["w_out"], mode="nn", out_dtype=F32, name="mm_out", add=h0)

    hn_ = _rms_fwd(h1, sp["g_ffn"], "rms_ffn")
    wts.update(fetch("ffn", hn_))
    ab = _mm(hn_, wts["w_gu"], mode="nn", out_dtype=F32, name="mm_ffn_in")
    tr_ffn = 64
    (f,) = _rowwise(lambda i, a, b: _swiglu(a, b), [(ab, _rows(dff, 0, tr_ffn)), (ab, _rows(dff, 1, tr_ffn))],
                    [(SDS((t, dff), BF16), _rows(dff, 0, tr_ffn))], [], name="swiglu_fwd", n_rows=t, tr=tr_ffn)
    h2 = _mm(f, wts["w_ffn_down"], mode="nn", out_dtype=F32, name="mm_ffn_down", add=h1)

    def head(i, h, g, tgt):
        y, vjp = jax.vjp(_rms, h, g)
        row = i * ROW_TILE + lax.broadcasted_iota(jnp.int32, (ROW_TILE, 1), 0)
        valid = jnp.logical_and(row >= N_META, row < length)
        err = jnp.where(valid, y - tgt, 0.0)
        dh, dg = vjp(err / d)
        loss = 0.5 * jnp.sum(jnp.sum(err * err, axis=-1, keepdims=True), axis=0, keepdims=True) / d
        return dh, dh, dg, jnp.broadcast_to(loss, (1, LANES))

    dh2, dh2_b, g_final, loss_v = _rowwise(
        head, [(h2, _rows(d)), (sp["g_final"], _whole(sp["g_final"])), (target_p, _rows(d))],
        [(SDS((t, d), F32), _rows(d)), (SDS((t, d), BF16), _rows(d))], [SDS((1, d), F32), SDS((1, LANES), F32)],
        name="loss_head", n_rows=t)
    loss = loss_v[0, 0]

    g_ffn_down = _mm(f, dh2_b, mode="tn", out_dtype=BF16, name="mm_g_ffn_down")
    df = _mm(dh2_b, wts["w_ffn_down"], mode="nt", out_dtype=F32, name="mm_d_f")

    def swiglu_bwd(i, a, b, df):
        _, vjp = jax.vjp(_swiglu, a, b)
        da, db = vjp(df)
        return jnp.concatenate([da, db], axis=-1)

    (dab,) = _rowwise(swiglu_bwd, [(ab, _rows(dff, 0, tr_ffn)), (ab, _rows(dff, 1, tr_ffn)), (df, _rows(dff, 0, tr_ffn))],
                      [(SDS((t, 2 * dff), BF16), _rows(2 * dff, 0, tr_ffn))], [], name="swiglu_bwd", n_rows=t, tr=tr_ffn)
    g_gu = _mm(hn_, dab, mode="tn", out_dtype=BF16, name="mm_g_ffn_in")
    dhn = _mm(dab, wts["w_gu"], mode="nt", out_dtype=F32, name="mm_d_hn")
    sent = ship("ffn", {"w_gu": g_gu, "w_ffn_down": g_ffn_down})
    dh1, dh1_b, g_g_ffn = _rms_bwd(h1, sp["g_ffn"] + sent[0, 0], dhn, dh2, "rms_ffn_bwd")

    g_w_out = _mm(mix, dh1_b, mode="tn", out_dtype=BF16, name="mm_g_out")
    dmix = _mm(dh1_b, wts["w_out"], mode="nt", out_dtype=F32, name="mm_d_mix")

    def mix_bwd(i, yc, bco, ya, zc, za, bgc, bga, dmix):
        _, vjp = jax.vjp(_mix, yc, bco, ya, zc, za, bgc, bga)
        dyc, dbco, dya, dzc, dza, dbgc, dbga = vjp(dmix)
        return dyc, dya, jnp.concatenate([dzc, dza], axis=-1), dbco, dbgc, dbga

    dyc, dya, dz_gate, g_b_conv_out, g_bgc, g_bga = _rowwise(
        mix_bwd, mix_ins + [(dmix, _rows(d))],
        [(SDS((t, d), BF16), _rows(d)), (SDS((t, d), BF16), _rows(d)), (SDS((t, 2 * d), BF16), _rows(2 * d))],
        [SDS((1, d), F32)] * 3, name="mix_bwd", n_rows=t)

    g_attn_out = _mm(o, dya, mode="tn", out_dtype=BF16, name="mm_g_attn_out")
    d_o = _mm(dya, wts["w_attn_out"], mode="nt", out_dtype=BF16, name="mm_d_o")
    dq_hm, dk_hm, dv_hm = _attn_bwd(q_hm, k_hm, v_hm, lse, d_o)
    dq, dkv, dkr = _attn_post(dq_hm, dk_hm, dv_hm, cos, sin)
    g_uq = _mm(cq, dq, mode="tn", out_dtype=BF16, name="mm_g_uq")
    g_ukv = _mm(ckv, dkv, mode="tn", out_dtype=BF16, name="mm_g_ukv")
    dcq = _mm(dq, wts["w_uq"], mode="nt", out_dtype=F32, name="mm_d_cq")
    dckv = _mm(dkv, wts["w_ukv"], mode="nt", out_dtype=F32, name="mm_d_ckv")

    def lora_bwd(i, zq, zk, gq, gk, dcq, dckv, dkr):
        _, vq = jax.vjp(_rms, zq, gq)
        _, vk = jax.vjp(_rms, zk, gk)
        dzq, dgq = vq(dcq)
        dzk, dgk = vk(dckv)
        return jnp.concatenate([dzq, dzk, dkr], axis=-1), dgq, dgk

    dz_l, g_g_q, g_g_kv = _rowwise(
        lora_bwd, lora_ins + [(dcq, _rows(ql)), (dckv, _rows(ql)), (dkr, _rows(LANES))],
        [(SDS((t, 2 * ql + LANES), BF16), _rows(2 * ql + LANES))], [SDS((1, ql), F32)] * 2,
        name="lora_norm_bwd", n_rows=t)

    g_conv_out = _mm(c3, dyc, mode="tn", out_dtype=BF16, name="mm_g_conv_out")
    dc3 = _mm(dyc, wts["w_conv_out"], mode="nt", out_dtype=F32, name="mm_d_c3")
    sent = ship("mixers", {"w_conv_out": g_conv_out, "w_uq": g_uq, "w_ukv": g_ukv, "w_attn_out": g_attn_out,
                           "w_out": g_w_out})

    def ln_bwd(i, c, g, b, dc3):
        _, vjp = jax.vjp(_ln_silu, c, g, b)
        return vjp(dc3)

    g_ln_sent = sp["g_conv_ln"] + sent[0, 0]
    dc1, g_g_ln, g_b_ln = _rowwise(
        ln_bwd, [ln_ins[0], (g_ln_sent, _whole(g_ln_sent)), ln_ins[2], (dc3, _rows(ch))],
        [(SDS((t, ch), F32), _rows(ch))], [SDS((1, ch), F32)] * 2, name="ln_silu_bwd", n_rows=t)
    dc0, g_w_dw, g_b_dw = _conv_bwd(dc1, c0, w_dw)

    def glu_bwd(i, za, zb, ba, bb, dc0):
        _, vjp = jax.vjp(_glu, za, zb, ba, bb)
        dza, dzb, dba, dbb = vjp(dc0)
        return jnp.concatenate([dza, dzb], axis=-1), dba, dbb

    dz_glu, g_bga_, g_bgb_ = _rowwise(glu_bwd, glu_ins + [(dc0, _rows(ch))],
                                      [(SDS((t, 2 * ch), BF16), _rows(2 * ch))], [SDS((1, ch), F32)] * 2,
                                      name="glu_bwd", n_rows=t)

    g_glu = _mm(u, dz_glu, mode="tn", out_dtype=BF16, name="mm_g_w_glu")
    g_lora = _mm(u, dz_l, mode="tn", out_dtype=BF16, name="mm_g_w_lora")
    g_gate = _mm(u, dz_gate, mode="tn", out_dtype=BF16, name="mm_g_w_gate")
    sent = ship("input", {"w_glu": g_glu, "w_lora": g_lora, "w_gate": g_gate})
    du = _mm(dz_glu, wts["w_glu"], mode="nt", out_dtype=F32, name="mm_d_u0", after=sent)
    du = _mm(dz_l, wts["w_lora"], mode="nt", out_dtype=F32, name="mm_d_u1", add=du)
    du = _mm(dz_gate, wts["w_gate"], mode="nt", out_dtype=F32, name="mm_d_u2", add=du)
    dh0, _, g_g_mix = _rms_bwd(h0, sp["g_mix"], du, dh1, "rms_mix_bwd")

    big = {"meta_tokens": dh0[:N_META], "w_dw": g_w_dw}
    small = {
        "g_mix": g_g_mix, "b_glu": jnp.concatenate([g_bga_, g_bgb_], axis=1),
        "b_gate": jnp.concatenate([g_bgc, g_bga], axis=1), "b_dw": g_b_dw, "g_conv_ln": g_g_ln, "b_conv_ln": g_b_ln,
        "b_conv_out": g_b_conv_out, "g_q_lora": g_g_q, "g_kv_lora": g_g_kv, "g_ffn": g_g_ffn, "g_final": g_final,
    }
    return loss, dh0[N_META:length], big, small


def _slot(ref, dev):
    return ref.at[dev]


def _row_window(rows):
    return lambda ref, dev: ref.at[pl.ds(pl.multiple_of(dev * rows, 16), rows)]


def _col_window(width, offset=0):
    return lambda ref, dev: ref.at[:, pl.ds(pl.multiple_of(offset + dev * width, LANES), width)]


def _dev_index(x, y, c):
    return 4 * x + 2 * y + c


def _gather_weights(items, out_shapes):
    srcs = [it[0] for it in items]
    n, n_out = len(srcs), len(out_shapes)

    def body(*refs):
        src, out = refs[:n], refs[n:n + n_out]
        send_sems, recv_sems, local_sems = refs[n + n_out:]
        x, y, c = lax.axis_index("x"), lax.axis_index("y"), lax.axis_index("c")
        me, sibling = (x, y, c), (x, y, 1 - c)
        chips = [(1 - x, y), (x, 1 - y), (1 - x, 1 - y)]

        def place(i, block):
            _, o, window = items[i]
            return window(out[o], _dev_index(*block))

        def copy(k, i, block, to, from_src=False):
            return pltpu.make_async_remote_copy(
                src_ref=src[i] if from_src else place(i, block), dst_ref=place(i, block),
                send_sem=send_sems.at[k * n + i], recv_sem=recv_sems.at[k * n + i],
                device_id=to, device_id_type=pl.DeviceIdType.MESH)

        mine = [pltpu.make_async_copy(src[i], place(i, me), local_sems.at[i]) for i in range(n)]
        first = [copy(0, i, me, sibling, True) for i in range(n)]
        first += [copy(1 + j, i, me, (*chip, c), True) for j, chip in enumerate(chips) for i in range(n)]
        for cp in mine + first:
            cp.start()
        passed = [[copy(4 + j, i, (*chip, c), sibling) for i in range(n)] for j, chip in enumerate(chips)]
        for j, chip in enumerate(chips):
            for i in range(n):
                copy(1 + j, i, (*chip, c), me).wait_recv()
            for cp in passed[j]:
                cp.start()
        for i in range(n):
            copy(0, i, sibling, me).wait_recv()
        for j, chip in enumerate(chips):
            for i in range(n):
                copy(4 + j, i, (*chip, 1 - c), me).wait_recv()
        for cp in first + [cp for row in passed for cp in row]:
            cp.wait_send()
        for cp in mine:
            cp.wait()

    any_spec = pl.BlockSpec(memory_space=pl.ANY)
    return pl.pallas_call(
        body, name="gather_weights", in_specs=[any_spec] * n, out_specs=[any_spec] * n_out, out_shape=out_shapes,
        scratch_shapes=[pltpu.SemaphoreType.DMA((7 * n,)), pltpu.SemaphoreType.DMA((7 * n,)),
                        pltpu.SemaphoreType.DMA((n,))],
    )(*srcs)


def _exchange(srcs, out_shapes, items, *, masks, name):
    n_src, n_out, n = len(srcs), len(out_shapes), len(items)

    def body(*refs):
        src, out = refs[:n_src], refs[n_src:n_src + n_out]
        send_sems, recv_sems = refs[n_src + n_out:]
        remote = _peer_copies(src, out, send_sems, recv_sems, items, masks)
        for cp in remote:
            cp.start()
        for cp in remote:
            cp.wait()

    any_spec = pl.BlockSpec(memory_space=pl.ANY)
    return pl.pallas_call(
        body, name=name, in_specs=[any_spec] * n_src, out_specs=[any_spec] * n_out, out_shape=out_shapes,
        scratch_shapes=[pltpu.SemaphoreType.DMA((len(masks) * n,)), pltpu.SemaphoreType.DMA((len(masks) * n,))],
    )(*srcs)


def _peer_copies(src, out, send_sems, recv_sems, items, masks):
    x, y, c = lax.axis_index("x"), lax.axis_index("y"), lax.axis_index("c")
    me = _dev_index(x, y, c)
    n = len(items)
    copies = []
    for k, mask in enumerate(masks):
        px, py, pc = x ^ ((mask >> 2) & 1), y ^ ((mask >> 1) & 1), c ^ (mask & 1)
        peer = _dev_index(px, py, pc)
        for i, (s, s_win, o, d_win) in enumerate(items):
            copies.append(pltpu.make_async_remote_copy(
                src_ref=s_win(src[s], peer), dst_ref=d_win(out[o], me),
                send_sem=send_sems.at[k * n + i], recv_sem=recv_sems.at[k * n + i],
                device_id=(px, py, pc), device_id_type=pl.DeviceIdType.MESH))
    return copies


def _exchange_start(srcs, out_shapes, items, *, masks, name):
    n_src, n_out, n = len(srcs), len(out_shapes), len(items)
    n_sem = len(masks) * n
    n_buf = n_src + n_out

    def body(*refs):
        src, land = refs[:n_src], refs[n_src:n_buf]
        send_sems, recv_sems = refs[n_buf], refs[n_buf + 1]
        token = refs[-1]
        for cp in _peer_copies(src, land, send_sems, recv_sems, items, masks):
            cp.start()
        token[...] = jnp.zeros_like(token)

    hbm = pl.BlockSpec(memory_space=pltpu.HBM)
    sem = pl.BlockSpec(memory_space=pltpu.SEMAPHORE)
    bufs = [pltpu.with_memory_space_constraint(a, pltpu.HBM) for a in srcs]
    bufs += [pltpu.with_memory_space_constraint(lax.empty(s.shape, s.dtype), pltpu.HBM) for s in out_shapes]
    res = pl.pallas_call(
        body, name=name,
        out_shape=(pltpu.SemaphoreType.DMA((n_sem,)), pltpu.SemaphoreType.DMA((n_sem,)),
                   *[pltpu.HBM(b.shape, b.dtype) for b in bufs], SDS((8, LANES), F32)),
        in_specs=[hbm] * n_buf, out_specs=(sem, sem, *[hbm] * n_buf, pl.BlockSpec(memory_space=pltpu.VMEM)),
        input_output_aliases={i: 2 + i for i in range(n_buf)},
        compiler_params=pltpu.CompilerParams(has_side_effects=pltpu.SideEffectType.DATAFLOW_SIDE_EFFECTING),
    )(*bufs)
    return res[0], res[1], list(res[2:2 + n_src]), list(res[2 + n_src:2 + n_buf]), res[-1]


def _exchange_wait(send_sems, recv_sems, srcs, lands, items, after, *, masks, name):
    n_src, n_out = len(srcs), len(lands)
    n_buf = n_src + n_out

    def body(*refs):
        src, land = refs[:n_src], refs[n_src:n_buf]
        send_sems, recv_sems = refs[n_buf], refs[n_buf + 1]
        for cp in _peer_copies(src, land, send_sems, recv_sems, items, masks):
            cp.wait_send()
            cp.wait_recv()

    hbm = pl.BlockSpec(memory_space=pltpu.HBM)
    sem = pl.BlockSpec(memory_space=pltpu.SEMAPHORE)
    bufs = list(srcs) + list(lands)
    res = pl.pallas_call(
        body, name=name, out_shape=tuple(pltpu.HBM(b.shape, b.dtype) for b in bufs),
        in_specs=[hbm] * n_buf + [sem, sem, pl.BlockSpec(memory_space=pl.ANY)], out_specs=tuple([hbm] * n_buf),
        input_output_aliases={i: i for i in range(n_buf)},
        compiler_params=pltpu.CompilerParams(has_side_effects=pltpu.SideEffectType.DATAFLOW_SIDE_EFFECTING),
    )(*bufs, send_sems, recv_sems, after)
    return list(res[:n_src]), list(res[n_src:])


def _regroup(srcs, dsts, *, name, tr=256):
    def segments(shape, valid):
        if len(shape) == 3:
            return [(p, shape[2]) for p in range(shape[0])]
        return [(None, valid)]

    src_arrays = [s[0] if isinstance(s, tuple) else s for s in srcs]
    src_valid = [s[1] if isinstance(s, tuple) else s.shape[-1] for s in srcs]
    k_rows = src_arrays[0].shape[-2]
    src_segs = [(i, p, w) for i, a in enumerate(src_arrays) for p, w in segments(a.shape, src_valid[i])]
    dst_segs = [(j, p, w) for j, (shape, _, valid) in enumerate(dsts) for p, w in segments(shape, valid)]
    pieces = []
    si, so, di, do = 0, 0, 0, 0
    while si < len(src_segs) and di < len(dst_segs):
        n = min(src_segs[si][2] - so, dst_segs[di][2] - do)
        pieces.append((src_segs[si][0], src_segs[si][1], so, dst_segs[di][0], dst_segs[di][1], do, n))
        so, do = so + n, do + n
        if so == src_segs[si][2]:
            si, so = si + 1, 0
        if do == dst_segs[di][2]:
            di, do = di + 1, 0
    assert si == len(src_segs) and di == len(dst_segs), "source and destination columns differ in number"
    n_src = len(src_arrays)

    def body(*refs):
        src, dst = refs[:n_src], refs[n_src:]
        for j, (shape, dtype, valid) in enumerate(dsts):
            if len(shape) == 2 and valid < shape[1]:
                dst[j][:, valid:shape[1]] = jnp.zeros((tr, shape[1] - valid), dtype)
        for i, sp, so, j, dp, do, n in pieces:
            val = src[i][:, so:so + n] if sp is None else src[i][sp, :, so:so + n]
            if dp is None:
                dst[j][:, do:do + n] = val.astype(dst[j].dtype)
            else:
                dst[j][dp, :, do:do + n] = val.astype(dst[j].dtype)

    def spec(shape):
        if len(shape) == 3:
            return pl.BlockSpec((shape[0], tr, shape[2]), lambda i: (0, i, 0))
        return pl.BlockSpec((tr, shape[1]), lambda i: (i, 0))

    return pl.pallas_call(
        body, name=name, grid=(k_rows // tr,), in_specs=[spec(a.shape) for a in src_arrays],
        out_specs=[spec(shape) for shape, _, _ in dsts], out_shape=[SDS(shape, dtype) for shape, dtype, _ in dsts],
        compiler_params=_params(),
    )(*src_arrays)


def _cast_bf16(a, name, after=None):
    r, c = a.shape
    tr = _row_tile(r, c * 4)
    (out,) = _rowwise(lambda i, v: v, [(a, _rows(c, 0, tr))], [(SDS((r, c), BF16), _rows(c, 0, tr))], [], name=name,
                      n_rows=r, tr=tr, after=after)
    return out


def _insert_own(blocks, wholes, places, name):
    n_b, n_w = len(blocks), len(wholes)

    def body(*refs):
        own, out, sems = refs[:n_b], refs[n_b + n_w:n_b + 2 * n_w], refs[n_b + 2 * n_w]
        me = _dev_index(lax.axis_index("x"), lax.axis_index("y"), lax.axis_index("c"))
        copies = [pltpu.make_async_copy(own[b], win(out[w], me), sems.at[i]) for i, (b, w, win) in enumerate(places)]
        for cp in copies:
            cp.start()
        for cp in copies:
            cp.wait()

    any_spec = pl.BlockSpec(memory_space=pl.ANY)
    return pl.pallas_call(
        body, name=name, in_specs=[any_spec] * (n_b + n_w), out_specs=[any_spec] * n_w,
        out_shape=[SDS(w.shape, w.dtype) for w in wholes], input_output_aliases={n_b + i: i for i in range(n_w)},
        scratch_shapes=[pltpu.SemaphoreType.DMA((len(places),))],
    )(*blocks, *wholes)


def _row_tile(rows, row_bytes, limit=2 * 1024 * 1024):
    best = None
    for t in range(16, rows + 1, 16):
        if rows % t == 0 and t * row_bytes <= limit:
            best = t
    return best if best is not None else rows


def _adamw(me, recv, own, own_kind, w, m, v, *, name):
    n_rows, width = w.shape
    tr = _row_tile(n_rows, width * 4, limit=1024 * 1024)
    n_tiles = n_rows // tr
    if own_kind[0] == "slot":
        own_spec = pl.BlockSpec((None, tr, width), lambda i, me: (me[0], i, 0))
    elif own_kind[0] == "rows":
        own_spec = pl.BlockSpec((tr, width), lambda i, me: (me[0] * n_tiles + i, 0))
    elif own_kind[0] == "cols":
        own_spec = pl.BlockSpec((tr, width), lambda i, me: (i, own_kind[1] + me[0]))
    else:
        own_spec = pl.BlockSpec((tr, width), lambda i, me: (i, 0))

    def body(me_ref, r_ref, own_ref, w_ref, m_ref, v_ref, g_ref, d_ref, mo_ref, vo_ref):
        mine = own_ref[...].astype(F32)
        g = None
        for q in range(N_DEV):
            term = jnp.where(me_ref[0] == q, mine, r_ref[q].astype(F32))
            g = term if g is None else g + term
        m_new = ADAM_B1 * m_ref[...] + (1.0 - ADAM_B1) * g
        v_new = ADAM_B2 * v_ref[...] + (1.0 - ADAM_B2) * jnp.square(g)
        m_hat = m_new / (1.0 - ADAM_B1 ** ADAM_STEP)
        v_hat = v_new / (1.0 - ADAM_B2 ** ADAM_STEP)
        g_ref[...] = g
        d_ref[...] = -ADAM_LR * (m_hat / (jnp.sqrt(v_hat) + ADAM_EPS) + ADAM_WD * w_ref[...])
        mo_ref[...] = m_new
        vo_ref[...] = v_new

    row = pl.BlockSpec((tr, width), lambda i, me: (i, 0))
    return pl.pallas_call(
        body, name=name,
        grid_spec=pltpu.PrefetchScalarGridSpec(
            num_scalar_prefetch=1, grid=(n_tiles,),
            in_specs=[pl.BlockSpec((N_DEV, tr, width), lambda i, me: (0, i, 0)), own_spec, row, row, row],
            out_specs=[row] * 4),
        out_shape=[SDS((n_rows, width), F32)] * 4, compiler_params=_params(),
    )(me, recv, own, w, m, v)


def _offsets(sizes):
    offs, o = [], 0
    for n in sizes:
        offs.append(o)
        o += n
    return offs, o


def kernel(x, meta_tokens, g_mix, w_in, b_glu, b_gate, w_dw, b_dw, g_conv_ln, b_conv_ln, w_conv_out, b_conv_out, g_q_lora, w_uq, g_kv_lora, w_uk, w_uv, w_attn_out, w_out, g_ffn, w_ffn_gate, w_ffn_up, w_ffn_down, g_final, loss_target, m_meta_tokens, m_g_mix, m_w_in, m_b_glu, m_b_gate, m_w_dw, m_b_dw, m_g_conv_ln, m_b_conv_ln, m_w_conv_out, m_b_conv_out, m_g_q_lora, m_w_uq, m_g_kv_lora, m_w_uk, m_w_uv, m_w_attn_out, m_w_out, m_g_ffn, m_w_ffn_gate, m_w_ffn_up, m_w_ffn_down, m_g_final, v_meta_tokens, v_g_mix, v_w_in, v_b_glu, v_b_gate, v_w_dw, v_b_dw, v_g_conv_ln, v_b_conv_ln, v_w_conv_out, v_b_conv_out, v_g_q_lora, v_w_uq, v_g_kv_lora, v_w_uk, v_w_uv, v_w_attn_out, v_w_out, v_g_ffn, v_w_ffn_gate, v_w_ffn_up, v_w_ffn_down, v_g_final):
    w_all = dict(meta_tokens=meta_tokens, g_mix=g_mix, w_in=w_in, b_glu=b_glu, b_gate=b_gate, w_dw=w_dw, b_dw=b_dw, g_conv_ln=g_conv_ln, b_conv_ln=b_conv_ln, w_conv_out=w_conv_out, b_conv_out=b_conv_out, g_q_lora=g_q_lora, w_uq=w_uq, g_kv_lora=g_kv_lora, w_uk=w_uk, w_uv=w_uv, w_attn_out=w_attn_out, w_out=w_out, g_ffn=g_ffn, w_ffn_gate=w_ffn_gate, w_ffn_up=w_ffn_up, w_ffn_down=w_ffn_down, g_final=g_final)
    m_all = dict(meta_tokens=m_meta_tokens, g_mix=m_g_mix, w_in=m_w_in, b_glu=m_b_glu, b_gate=m_b_gate, w_dw=m_w_dw, b_dw=m_b_dw, g_conv_ln=m_g_conv_ln, b_conv_ln=m_b_conv_ln, w_conv_out=m_w_conv_out, b_conv_out=m_b_conv_out, g_q_lora=m_g_q_lora, w_uq=m_w_uq, g_kv_lora=m_g_kv_lora, w_uk=m_w_uk, w_uv=m_w_uv, w_attn_out=m_w_attn_out, w_out=m_w_out, g_ffn=m_g_ffn, w_ffn_gate=m_w_ffn_gate, w_ffn_up=m_w_ffn_up, w_ffn_down=m_w_ffn_down, g_final=m_g_final)
    v_all = dict(meta_tokens=v_meta_tokens, g_mix=v_g_mix, w_in=v_w_in, b_glu=v_b_glu, b_gate=v_b_gate, w_dw=v_w_dw, b_dw=v_b_dw, g_conv_ln=v_g_conv_ln, b_conv_ln=v_b_conv_ln, w_conv_out=v_w_conv_out, b_conv_out=v_b_conv_out, g_q_lora=v_g_q_lora, w_uq=v_w_uq, g_kv_lora=v_g_kv_lora, w_uk=v_w_uk, w_uv=v_w_uv, w_attn_out=v_w_attn_out, w_out=v_w_out, g_ffn=v_g_ffn, w_ffn_gate=v_w_ffn_gate, w_ffn_up=v_w_ffn_up, w_ffn_down=v_w_ffn_down, g_final=v_g_final)

    two_d = lambda a: a.reshape(a.shape[-2:]) if a.ndim >= 2 else a.reshape(1, -1)
    sh = {n: two_d(w_all[n]) for n in SHARDED}
    d = x.shape[-1]
    k_in, c_in = sh["w_in"].shape
    r_co, r_ao, r_wo, r_fd = (sh[n].shape[0] for n in ROW_SHARDED)
    ql, c_uq = sh["w_uq"].shape
    c_uk = sh["w_uk"].shape[1]
    c_ff = sh["w_ffn_gate"].shape[1]
    n_meta, c_meta = sh["meta_tokens"].shape
    n_taps, c_dw = sh["w_dw"].shape
    ch, dff = N_DEV * c_dw, N_DEV * c_ff
    n_lora = 2 * ql + QK_ROPE

    masks = tuple(range(1, N_DEV))
    whole_ref = lambda ref, dev: ref
    gathered = _gather_weights(
        [(_cast_bf16(sh["w_in"], "cast_w_in"), 0, _slot), (sh["meta_tokens"], 1, _col_window(c_meta)),
         (sh["w_dw"], 2, _col_window(c_dw))],
        [SDS((N_DEV, k_in, c_in), BF16), SDS((n_meta, N_DEV * c_meta), F32), SDS((n_taps, ch), F32)])
    bf = {n: _cast_bf16(sh[n], "cast_" + n, after=gathered[0]) for n in BIG if n != "w_in"}
    coming = {}
    for group, names, windows, shapes in (
            ("mixers", ("w_conv_out", "w_uq", "w_uk", "w_uv", "w_attn_out", "w_out"),
             ((0, _row_window(r_co)), (1, _col_window(c_uq)), (2, _col_window(c_uk)),
              (2, _col_window(c_uk, N_DEV * c_uk)), (3, _row_window(r_ao)), (4, _row_window(r_wo))),
             (SDS((N_DEV * r_co, d), BF16), SDS((ql, N_DEV * c_uq), BF16), SDS((ql, 2 * N_DEV * c_uk), BF16),
              SDS((N_DEV * r_ao, d), BF16), SDS((N_DEV * r_wo, d), BF16))),
            ("ffn", ("w_ffn_gate", "w_ffn_up", "w_ffn_down"),
             ((0, _slot), (1, _slot), (2, _row_window(r_fd))),
             (SDS((N_DEV, d, c_ff), BF16), SDS((N_DEV, d, c_ff), BF16), SDS((N_DEV * r_fd, d), BF16)))):
        items = [(s, whole_ref, o, win) for s, (o, win) in enumerate(windows)]
        coming[group] = _exchange_start([bf[n] for n in names], list(shapes), items, masks=masks,
                                        name="send_weights_" + group)[:4] + (items,)

    def fetch(group, after):
        send_sems, recv_sems, srcs, lands, items = coming[group]
        srcs, lands = _exchange_wait(send_sems, recv_sems, srcs, lands, items, after, masks=masks,
                                     name="wait_weights_" + group)
        lands = _insert_own(srcs, lands, [(s, o, win) for s, _, o, win in items], "own_weights_" + group)
        if group == "mixers":
            return {"w_conv_out": lands[0], "w_uq": lands[1], "w_ukv": lands[2], "w_attn_out": lands[3],
                    "w_out": lands[4]}
        (w_gu,) = _regroup([lands[0], lands[1]], [((d, 2 * dff), BF16, 2 * dff)], name="unpack_w_ffn_in")
        return {"w_gu": w_gu, "w_ffn_down": lands[2]}

    w_glu, w_lora, w_gate = _regroup(
        [gathered[0]], [((k_in, 2 * ch), BF16, 2 * ch), ((k_in, n_lora + LANES - QK_ROPE), BF16, n_lora),
                        ((k_in, 2 * d), BF16, 2 * d)], name="unpack_w_in")
    wts = {"w_glu": w_glu, "w_lora": w_lora, "w_gate": w_gate}
    sp = {n: w_all[n].reshape(1, -1) for n in SMALL}
    dims = {"n_heads": N_DEV * c_uq // QK, "dff": dff}

    recv_shape = lambda n: SDS((N_DEV,) + sh[n].shape, F32 if n in F32_GATHERED else BF16)
    in_flight = {}

    def ship(group, grads):
        if group == "input":
            (s_in,) = _regroup([grads["w_glu"], (grads["w_lora"], n_lora), grads["w_gate"]],
                               [((N_DEV, k_in, c_in), BF16, None)], name="pack_g_w_in")
            srcs, names, items, own = [s_in], ("w_in",), [(0, _slot, 0, _slot)], [(0, ("slot",))]
        elif group == "ffn":
            s_gate, s_up = _regroup([grads["w_gu"]], [((N_DEV, d, c_ff), BF16, None)] * 2, name="pack_g_w_ffn_in")
            srcs = [s_gate, s_up, grads["w_ffn_down"]]
            names = ("w_ffn_gate", "w_ffn_up", "w_ffn_down")
            items = [(0, _slot, 0, _slot), (1, _slot, 1, _slot), (2, _row_window(r_fd), 2, _slot)]
            own = [(0, ("slot",)), (1, ("slot",)), (2, ("rows",))]
        else:
            srcs = [grads["w_conv_out"], grads["w_uq"], grads["w_ukv"], grads["w_attn_out"], grads["w_out"]]
            names = ("w_conv_out", "w_uq", "w_uk", "w_uv", "w_attn_out", "w_out")
            items = [(0, _row_window(r_co), 0, _slot), (1, _col_window(c_uq), 1, _slot),
                     (2, _col_window(c_uk), 2, _slot), (2, _col_window(c_uk, N_DEV * c_uk), 3, _slot),
                     (3, _row_window(r_ao), 4, _slot), (4, _row_window(r_wo), 5, _slot)]
            own = [(0, ("rows",)), (1, ("cols", 0)), (2, ("cols", 0)), (2, ("cols", N_DEV)), (3, ("rows",)),
                   (4, ("rows",))]
        send_sems, recv_sems, srcs, lands, zero = _exchange_start(
            srcs, [recv_shape(n) for n in names], items, masks=masks, name="send_grads_" + group)
        in_flight[group] = (send_sems, recv_sems, srcs, lands, items, names, own)
        return zero

    loss, grad_x, g_big, g_small = _local_step(x[0], loss_target[0], gathered[1], gathered[2], wts, sp, ship, fetch,
                                               dims)

    s_sizes = [w_all[n].size for n in SMALL]
    s_offs, n_s = _offsets(s_sizes)
    cat_small = lambda src: jnp.concatenate([src[n].reshape(1, -1) for n in SMALL], axis=1)
    last_names = ("meta_tokens", "w_dw")
    last_srcs = [g_big["meta_tokens"], g_big["w_dw"], cat_small(g_small)]
    last = _exchange(
        last_srcs, [recv_shape(n) for n in last_names] + [SDS((N_DEV, 1, n_s), F32)],
        [(0, _col_window(c_meta), 0, _slot), (1, _col_window(c_dw), 1, _slot), (2, whole_ref, 2, _slot)],
        masks=masks, name="exchange_grads_last")

    me = (4 * lax.axis_index("x") + 2 * lax.axis_index("y") + lax.axis_index("c")).astype(jnp.int32).reshape(1)
    by_name = {}

    def update(n, recv, own, kind):
        outs = _adamw(me, recv, own, kind, sh[n], two_d(m_all[n]), two_d(v_all[n]), name="adamw_" + n)
        by_name[n] = [o.reshape(w_all[n].shape) for o in outs]
        return outs[0]

    for n, recv, src in zip(last_names, last, last_srcs):
        done = update(n, recv, src, ("cols", 0))
    outs = _adamw(me, last[2], last_srcs[2], ("whole",), cat_small(w_all), cat_small(m_all), cat_small(v_all),
                  name="adamw_replicated")
    for n, o, s in zip(SMALL, s_offs, s_sizes):
        by_name[n] = [out[:, o:o + s].reshape(w_all[n].shape) for out in outs]
    for group in ("ffn", "mixers", "input"):
        send_sems, recv_sems, srcs, lands, items, names, own = in_flight[group]
        srcs, lands = _exchange_wait(send_sems, recv_sems, srcs, lands, items, done, masks=masks,
                                     name="wait_grads_" + group)
        for n, land, (s, kind) in zip(names, lands, own):
            done = update(n, land, srcs[s], kind)
    result = [[by_name[n][k] for n in WEIGHTS] for k in range(4)]
    loss = lax.psum(loss, ("x", "y", "c"))
    return (loss, grad_x[None], *result[0], *result[1], *result[2], *result[3])
```

```python
import functools

import jax
import jax.numpy as jnp
from jax import lax
from jax.experimental import pallas as pl
from jax.experimental.pallas import tpu as pltpu

F32, BF16 = jnp.float32, jnp.bfloat16
SDS = jax.ShapeDtypeStruct

N_DEV = 8
N_META = 16
BLOCK_Q = 128
CONV_WIDTH = 31
QK_NOPE, QK_ROPE, V_HEAD = 128, 64, 128
QK = QK_NOPE + QK_ROPE
ROPE_THETA = 10000.0
EPS = 1e-6
ADAM_LR, ADAM_B1, ADAM_B2, ADAM_EPS, ADAM_WD, ADAM_STEP = 0.001, 0.9, 0.999, 1e-08, 0.01, 10

LANES = 128
ROW_TILE = 128
PACK_W = 1024
VMEM_LIMIT = 56 * 1024 * 1024

BIG = ("w_in", "w_conv_out", "w_uq", "w_uk", "w_uv", "w_attn_out", "w_out", "w_ffn_gate", "w_ffn_up", "w_ffn_down")
F32_GATHERED = ("meta_tokens", "w_dw")
SHARDED = BIG + F32_GATHERED
ROW_SHARDED = ("w_conv_out", "w_attn_out", "w_out", "w_ffn_down")
SMALL = ("g_mix", "b_glu", "b_gate", "b_dw", "g_conv_ln", "b_conv_ln", "b_conv_out", "g_q_lora", "g_kv_lora",
         "g_ffn", "g_final")
WEIGHTS = ("meta_tokens", "g_mix", "w_in", "b_glu", "b_gate", "w_dw", "b_dw", "g_conv_ln", "b_conv_ln", "w_conv_out",
           "b_conv_out", "g_q_lora", "w_uq", "g_kv_lora", "w_uk", "w_uv", "w_attn_out", "w_out", "g_ffn", "w_ffn_gate",
           "w_ffn_up", "w_ffn_down", "g_final")


def _params():
    return pltpu.CompilerParams(vmem_limit_bytes=VMEM_LIMIT)


def _tile(dim, limit):
    best = None
    t = LANES
    while t <= min(dim, limit):
        if dim % t == 0:
            best = t
        t += LANES
    return best if best is not None else dim


def _mm(a, b, *, mode, out_dtype, name, add=None, after=None):
    if mode == "nn":
        (m, kc), n = a.shape, b.shape[1]
    elif mode == "nt":
        (m, kc), n = a.shape, b.shape[0]
    else:
        (kc, m), n = a.shape, b.shape[1]
    if mode == "tn":
        tm, tn, tk = _tile(m, 1024), _tile(n, 512), kc
    else:
        tm, tn, tk = m, _tile(n, 512), _tile(kc, 512)
    nk = kc // tk
    if mode == "nn":
        a_spec = pl.BlockSpec((tm, tk), lambda i, j, k: (i, k))
        b_spec = pl.BlockSpec((tk, tn), lambda i, j, k: (k, j))
        dims = (((1,), (0,)), ((), ()))
    elif mode == "nt":
        a_spec = pl.BlockSpec((tm, tk), lambda i, j, k: (i, k))
        b_spec = pl.BlockSpec((tn, tk), lambda i, j, k: (j, k))
        dims = (((1,), (1,)), ((), ()))
    else:
        a_spec = pl.BlockSpec((tk, tm), lambda i, j, k: (k, i))
        b_spec = pl.BlockSpec((tk, tn), lambda i, j, k: (k, j))
        dims = (((0,), (0,)), ((), ()))
    o_spec = pl.BlockSpec((tm, tn), lambda i, j, k: (i, j))
    has_add = add is not None

    def body(*refs):
        if after is not None:
            refs = refs[:-3] + refs[-2:]
        if has_add:
            a_ref, b_ref, add_ref, o_ref, acc_ref = refs
        else:
            a_ref, b_ref, o_ref, acc_ref = refs
        k = pl.program_id(2)
        p = lax.dot_general(a_ref[...], b_ref[...], dims, preferred_element_type=F32)
        if nk == 1:
            o_ref[...] = ((p + add_ref[...]) if has_add else p).astype(o_ref.dtype)
            return

        @pl.when(k == 0)
        def _():
            acc_ref[...] = (p + add_ref[...]) if has_add else p

        @pl.when(jnp.logical_and(k > 0, k < nk - 1))
        def _():
            acc_ref[...] += p

        @pl.when(k == nk - 1)
        def _():
            o_ref[...] = (acc_ref[...] + p).astype(o_ref.dtype)

    in_specs = [a_spec, b_spec] + ([o_spec] if has_add else [])
    args = (a, b) + ((add,) if has_add else ())
    if after is not None:
        in_specs, args = in_specs + [pl.BlockSpec(memory_space=pl.ANY)], args + (after,)
    acc_shape = (tm, tn) if nk > 1 else (8, LANES)
    return pl.pallas_call(
        body, name=name, grid=(m // tm, n // tn, nk), in_specs=in_specs, out_specs=o_spec,
        out_shape=SDS((m, n), out_dtype), scratch_shapes=[pltpu.VMEM(acc_shape, F32)],
        compiler_params=_params(),
    )(*args)


def _rows(width, col=0, tr=ROW_TILE):
    return pl.BlockSpec((tr, width), lambda i: (i, col))


def _whole(arr):
    nd = arr.ndim
    return pl.BlockSpec(arr.shape, lambda i: (0,) * nd)


def _rowwise(fn, ins, outs, accs, *, name, n_rows, tr=ROW_TILE, after=None):
    n_in, n_out = len(ins), len(outs)
    if after is not None:
        ins = list(ins) + [(after, pl.BlockSpec(memory_space=pl.ANY))]

    def body(*refs):
        i = pl.program_id(0)
        res = fn(i, *[r[...] for r in refs[:n_in]])
        refs = refs[:n_in] + refs[len(ins):]
        res = res if isinstance(res, (tuple, list)) else (res,)
        for o_ref, v in zip(refs[n_in:n_in + n_out], res[:n_out]):
            o_ref[...] = v.astype(o_ref.dtype)
        for a_ref, v in zip(refs[n_in + n_out:], res[n_out:]):
            @pl.when(i == 0)
            def _(a_ref=a_ref, v=v):
                a_ref[...] = v

            @pl.when(i > 0)
            def _(a_ref=a_ref, v=v):
                a_ref[...] += v

    acc_specs = [pl.BlockSpec(s.shape, lambda i, nd=len(s.shape): (0,) * nd) for s in accs]
    res = pl.pallas_call(
        body, name=name, grid=(n_rows // tr,),
        in_specs=[s for _, s in ins], out_specs=[s for _, s in outs] + acc_specs,
        out_shape=[s for s, _ in outs] + list(accs), compiler_params=_params(),
    )(*[a for a, _ in ins])
    return res


def _rms(x, g):
    return x * lax.rsqrt(jnp.mean(x * x, axis=-1, keepdims=True) + EPS) * g


def _sigmoid(x):
    return 1.0 / (1.0 + jnp.exp(-x))


def _silu(x):
    return x * _sigmoid(x)


def _rms_fwd(h, g, name):
    t, d = h.shape
    (u,) = _rowwise(lambda i, h, g: _rms(h, g), [(h, _rows(d)), (g, _whole(g))], [(SDS((t, d), BF16), _rows(d))], [],
                    name=name, n_rows=t)
    return u


def _rms_bwd(h, g, du, dres, name):
    t, d = h.shape

    def fn(i, h, g, du, dres):
        _, vjp = jax.vjp(_rms, h, g)
        dh, dg = vjp(du)
        dh = dh + dres
        return dh, dh, dg

    return _rowwise(fn, [(h, _rows(d)), (g, _whole(g)), (du, _rows(d)), (dres, _rows(d))],
                    [(SDS((t, d), F32), _rows(d)), (SDS((t, d), BF16), _rows(d))], [SDS((1, d), F32)],
                    name=name, n_rows=t)


def _conv_fwd(c0, w_dw, b_dw):
    t, ch = c0.shape
    tc = _tile(ch, 256)
    halo = 32
    shift = halo - (CONV_WIDTH - 1)

    def body(x_ref, w_ref, b_ref, o_ref, pad_ref):
        pad_ref[0:halo, :] = jnp.zeros((halo, tc), F32)
        pad_ref[halo:halo + t, :] = x_ref[...]
        for r0 in range(0, t, ROW_TILE):
            acc = jnp.zeros((ROW_TILE, tc), F32) + b_ref[...]
            for j in range(CONV_WIDTH):
                acc = acc + pad_ref[r0 + shift + j:r0 + shift + j + ROW_TILE, :] * w_ref[j:j + 1, :]
            o_ref[r0:r0 + ROW_TILE, :] = acc

    col = lambda i: (0, i)
    return pl.pallas_call(
        body, name="conv_fwd", grid=(ch // tc,),
        in_specs=[pl.BlockSpec((t, tc), col), pl.BlockSpec((CONV_WIDTH, tc), col), pl.BlockSpec((1, tc), col)],
        out_specs=pl.BlockSpec((t, tc), col), out_shape=SDS((t, ch), F32),
        scratch_shapes=[pltpu.VMEM((halo + t, tc), F32)], compiler_params=_params(),
    )(c0, w_dw, b_dw)


def _conv_bwd(dc1, c0, w_dw):
    t, ch = c0.shape
    tc = _tile(ch, 256)
    halo = 32
    shift = halo - (CONV_WIDTH - 1)

    def body(d_ref, x_ref, w_ref, dx_ref, dw_ref, db_ref, xpad_ref, dpad_ref):
        xpad_ref[0:halo, :] = jnp.zeros((halo, tc), F32)
        xpad_ref[halo:halo + t, :] = x_ref[...]
        dpad_ref[0:t, :] = d_ref[...]
        dpad_ref[t:t + halo, :] = jnp.zeros((halo, tc), F32)
        for r0 in range(0, t, ROW_TILE):
            acc = jnp.zeros((ROW_TILE, tc), F32)
            for j in range(CONV_WIDTH):
                off = r0 + (CONV_WIDTH - 1) - j
                acc = acc + dpad_ref[off:off + ROW_TILE, :] * w_ref[j:j + 1, :]
            dx_ref[r0:r0 + ROW_TILE, :] = acc
        for j in range(CONV_WIDTH):
            acc = jnp.zeros((1, tc), F32)
            for r0 in range(0, t, ROW_TILE):
                prod = d_ref[r0:r0 + ROW_TILE, :] * xpad_ref[r0 + shift + j:r0 + shift + j + ROW_TILE, :]
                acc = acc + jnp.sum(prod, axis=0, keepdims=True)
            dw_ref[j:j + 1, :] = acc
        db_ref[...] = jnp.sum(d_ref[...], axis=0, keepdims=True)

    col = lambda i: (0, i)
    return pl.pallas_call(
        body, name="conv_bwd", grid=(ch // tc,),
        in_specs=[pl.BlockSpec((t, tc), col), pl.BlockSpec((t, tc), col), pl.BlockSpec((CONV_WIDTH, tc), col)],
        out_specs=[pl.BlockSpec((t, tc), col), pl.BlockSpec((CONV_WIDTH, tc), col), pl.BlockSpec((1, tc), col)],
        out_shape=[SDS((t, ch), F32), SDS((CONV_WIDTH, ch), F32), SDS((1, ch), F32)],
        scratch_shapes=[pltpu.VMEM((halo + t, tc), F32), pltpu.VMEM((halo + t, tc), F32)], compiler_params=_params(),
    )(dc1, c0, w_dw)


def _rope(x1, x2, cos, sin):
    return x1 * cos - x2 * sin, x1 * sin + x2 * cos


def _attn_prep(q, kv, z_l, kr_col, cos, sin, n_heads):
    t = q.shape[0]
    hn = n_heads * QK_NOPE
    half = QK_ROPE // 2

    def body(q_ref, kv_ref, kr_ref, cos_ref, sin_ref, qo_ref, ko_ref, vo_ref):
        cos, sin = cos_ref[...], sin_ref[...]
        kr = kr_ref[...]
        k1, k2 = _rope(kr[:, 0:half], kr[:, half:QK_ROPE], cos, sin)
        for h in range(n_heads):
            b = h * QK
            q1, q2 = _rope(q_ref[:, b + QK_NOPE:b + QK_NOPE + half], q_ref[:, b + QK_NOPE + half:b + QK], cos, sin)
            qo_ref[h] = jnp.concatenate([q_ref[:, b:b + QK_NOPE], q1, q2], axis=-1).astype(BF16)
            ko_ref[h] = jnp.concatenate([kv_ref[:, h * QK_NOPE:(h + 1) * QK_NOPE], k1, k2], axis=-1).astype(BF16)
            vo_ref[h] = kv_ref[:, hn + h * V_HEAD:hn + (h + 1) * V_HEAD].astype(BF16)

    tr = ROW_TILE
    hm = lambda w: pl.BlockSpec((n_heads, tr, w), lambda i: (0, i, 0))
    return pl.pallas_call(
        body, name="attn_prep", grid=(t // tr,),
        in_specs=[_rows(q.shape[1]), _rows(kv.shape[1]), _rows(LANES, kr_col), _rows(half), _rows(half)],
        out_specs=[hm(QK), hm(QK), hm(V_HEAD)],
        out_shape=[SDS((n_heads, t, QK), BF16), SDS((n_heads, t, QK), BF16), SDS((n_heads, t, V_HEAD), BF16)],
        compiler_params=_params(),
    )(q, kv, z_l, cos, sin)


def _attn_post(dq_hm, dk_hm, dv_hm, cos, sin):
    n_heads, t, _ = dq_hm.shape
    hn = n_heads * QK_NOPE
    half = QK_ROPE // 2

    def unrope(d1, d2, cos, sin):
        return d1 * cos + d2 * sin, d2 * cos - d1 * sin

    def body(dq_ref, dk_ref, dv_ref, cos_ref, sin_ref, qo_ref, kvo_ref, kro_ref):
        cos, sin = cos_ref[...], sin_ref[...]
        dkr = jnp.zeros((ROW_TILE, QK_ROPE), F32)
        for h in range(n_heads):
            dq = dq_ref[h]
            d1, d2 = unrope(dq[:, QK_NOPE:QK_NOPE + half], dq[:, QK_NOPE + half:QK], cos, sin)
            qo_ref[:, h * QK:(h + 1) * QK] = jnp.concatenate([dq[:, 0:QK_NOPE], d1, d2], axis=-1).astype(BF16)
            dk = dk_ref[h]
            kvo_ref[:, h * QK_NOPE:(h + 1) * QK_NOPE] = dk[:, 0:QK_NOPE].astype(BF16)
            kvo_ref[:, hn + h * V_HEAD:hn + (h + 1) * V_HEAD] = dv_ref[h].astype(BF16)
            dkr = dkr + dk[:, QK_NOPE:QK]
        d1, d2 = unrope(dkr[:, 0:half], dkr[:, half:QK_ROPE], cos, sin)
        kro_ref[...] = jnp.concatenate([d1, d2, jnp.zeros((ROW_TILE, LANES - QK_ROPE), F32)], axis=-1)

    tr = ROW_TILE
    hm = lambda w: pl.BlockSpec((n_heads, tr, w), lambda i: (0, i, 0))
    return pl.pallas_call(
        body, name="attn_post", grid=(t // tr,),
        in_specs=[hm(QK), hm(QK), hm(V_HEAD), _rows(half), _rows(half)],
        out_specs=[_rows(n_heads * QK), _rows(2 * hn), _rows(LANES)],
        out_shape=[SDS((t, n_heads * QK), BF16), SDS((t, 2 * hn), BF16), SDS((t, LANES), F32)],
        compiler_params=_params(),
    )(dq_hm, dk_hm, dv_hm, cos, sin)


N_QBLK = 4
_NT = (((1,), (1,)), ((), ()))
_TN = (((0,), (0,)), ((), ()))


def _scores(q, k, r0, scale):
    s = lax.dot_general(q, k, _NT, preferred_element_type=F32) * scale
    row = r0 + lax.broadcasted_iota(jnp.int32, s.shape, 0)
    col = lax.broadcasted_iota(jnp.int32, s.shape, 1)
    return jnp.where(col <= row, s, -jnp.inf)


def _attn_fwd(q_hm, k_hm, v_hm):
    n_heads, t, _ = q_hm.shape
    bq = t // N_QBLK
    scale = QK ** -0.5

    def body(q_ref, k_ref, v_ref, o_ref, lse_ref):
        for i in range(N_QBLK):
            r0, n_k = i * bq, (i + 1) * bq
            s = _scores(q_ref[0, r0:r0 + bq, :], k_ref[0, 0:n_k, :], r0, scale)
            m = jnp.max(s, axis=-1, keepdims=True)
            p = jnp.exp(s - m)
            l = jnp.sum(p, axis=-1, keepdims=True)
            p = (p / l).astype(BF16)
            o_ref[r0:r0 + bq, :] = jnp.dot(p, v_ref[0, 0:n_k, :], preferred_element_type=F32).astype(BF16)
            lse_ref[0, r0:r0 + bq, :] = m + jnp.log(l)

    head = lambda w: pl.BlockSpec((1, t, w), lambda h: (h, 0, 0))
    return pl.pallas_call(
        body, name="attn_fwd", grid=(n_heads,),
        in_specs=[head(QK), head(QK), head(V_HEAD)],
        out_specs=[pl.BlockSpec((t, V_HEAD), lambda h: (0, h)), head(1)],
        out_shape=[SDS((t, n_heads * V_HEAD), BF16), SDS((n_heads, t, 1), F32)],
        compiler_params=_params(),
    )(q_hm, k_hm, v_hm)


def _attn_bwd(q_hm, k_hm, v_hm, lse, d_o):
    n_heads, t, _ = q_hm.shape
    bq = t // N_QBLK
    scale = QK ** -0.5

    def body(q_ref, k_ref, v_ref, lse_ref, do_ref, dq_ref, dk_ref, dv_ref):
        dk_ref[...] = jnp.zeros(dk_ref.shape, F32)
        dv_ref[...] = jnp.zeros(dv_ref.shape, F32)
        for i in range(N_QBLK):
            r0, n_k = i * bq, (i + 1) * bq
            q = q_ref[0, r0:r0 + bq, :]
            k = k_ref[0, 0:n_k, :]
            d_o = do_ref[r0:r0 + bq, :]
            s = _scores(q, k, r0, scale)
            p = jnp.exp(s - lse_ref[0, r0:r0 + bq, :])
            dp = lax.dot_general(d_o, v_ref[0, 0:n_k, :], _NT, preferred_element_type=F32)
            ds = (p * (dp - jnp.sum(dp * p, axis=-1, keepdims=True)) * scale).astype(BF16)
            dq_ref[0, r0:r0 + bq, :] = jnp.dot(ds, k, preferred_element_type=F32)
            dk_ref[0, 0:n_k, :] += lax.dot_general(ds, q, _TN, preferred_element_type=F32)
            dv_ref[0, 0:n_k, :] += lax.dot_general(p.astype(BF16), d_o, _TN, preferred_element_type=F32)

    head = lambda w: pl.BlockSpec((1, t, w), lambda h: (h, 0, 0))
    return pl.pallas_call(
        body, name="attn_bwd", grid=(n_heads,),
        in_specs=[head(QK), head(QK), head(V_HEAD), head(1), pl.BlockSpec((t, V_HEAD), lambda h: (0, h))],
        out_specs=[head(QK), head(QK), head(V_HEAD)],
        out_shape=[SDS((n_heads, t, QK), F32), SDS((n_heads, t, QK), F32), SDS((n_heads, t, V_HEAD), F32)],
        compiler_params=_params(),
    )(q_hm, k_hm, v_hm, lse, d_o)


def _glu(za, zb, ba, bb):
    return (za + ba) * _sigmoid(zb + bb)


def _ln_silu(c, g, b):
    mu = jnp.mean(c, axis=-1, keepdims=True)
    var = jnp.mean(jnp.square(c - mu), axis=-1, keepdims=True)
    return _silu((c - mu) * lax.rsqrt(var + EPS) * g + b)


def _mix(yc, bco, ya, zc, za, bgc, bga):
    return _sigmoid(zc + bgc) * (yc + bco) + _sigmoid(za + bga) * ya


def _swiglu(a, b):
    return _silu(a) * b


def _local_step(x, target, meta, w_dw, wts, sp, ship=None, fetch=None, dims=None):
    wts = dict(wts)
    if ship is None:
        ship = lambda group, grads: jnp.zeros((8, LANES), F32)
    if fetch is None:
        fetch = lambda group, after: {}
    seq, d = x.shape
    length = N_META + seq
    t = -(-length // BLOCK_Q) * BLOCK_Q
    ch = w_dw.shape[1]
    ql = sp["g_q_lora"].shape[1]
    n_heads = dims["n_heads"] if dims else wts["w_uq"].shape[1] // QK
    hn = n_heads * QK_NOPE
    dff = dims["dff"] if dims else wts["w_ffn_down"].shape[0]
    assert sp["g_kv_lora"].shape[1] == ql and ql % LANES == 0 and t % (N_QBLK * 16) == 0
    pad_rows = lambda a: jnp.concatenate([jnp.zeros((N_META, d), F32), a, jnp.zeros((t - length, d), F32)], axis=0)
    h0 = jnp.concatenate([meta, x, jnp.zeros((t - length, d), F32)], axis=0)
    target_p = pad_rows(target)

    pos = jnp.arange(t, dtype=F32)
    inv_freq = ROPE_THETA ** (-jnp.arange(0, QK_ROPE, 2, dtype=F32) / QK_ROPE)
    ang = pos[:, None] * inv_freq[None, :]
    cos, sin = jnp.cos(ang), jnp.sin(ang)

    b_glu_a, b_glu_b = sp["b_glu"][:, :ch], sp["b_glu"][:, ch:]
    b_gate_c, b_gate_a = sp["b_gate"][:, :d], sp["b_gate"][:, d:]
    kr_col = 2 * ql // LANES

    u = _rms_fwd(h0, sp["g_mix"], "rms_mix")
    z_glu = _mm(u, wts["w_glu"], mode="nn", out_dtype=F32, name="mm_z_glu")
    z_l = _mm(u, wts["w_lora"], mode="nn", out_dtype=F32, name="mm_z_lora")
    z_gate = _mm(u, wts["w_gate"], mode="nn", out_dtype=F32, name="mm_z_gate")

    glu_ins = [(z_glu, _rows(ch, 0)), (z_glu, _rows(ch, 1)), (b_glu_a, _whole(b_glu_a)), (b_glu_b, _whole(b_glu_b))]
    (c0,) = _rowwise(lambda i, za, zb, ba, bb: _glu(za, zb, ba, bb), glu_ins, [(SDS((t, ch), F32), _rows(ch))], [],
                     name="glu_fwd", n_rows=t)
    c1 = _conv_fwd(c0, w_dw, sp["b_dw"])
    ln_ins = [(c1, _rows(ch)), (sp["g_conv_ln"], _whole(sp["g_conv_ln"])), (sp["b_conv_ln"], _whole(sp["b_conv_ln"]))]
    (c3,) = _rowwise(lambda i, c, g, b: _ln_silu(c, g, b), ln_ins, [(SDS((t, ch), BF16), _rows(ch))], [],
                     name="ln_silu_fwd", n_rows=t)
    wts.update(fetch("mixers", c3))
    yc = _mm(c3, wts["w_conv_out"], mode="nn", out_dtype=F32, name="mm_conv_out")

    lora_ins = [(z_l, _rows(ql, 0)), (z_l, _rows(ql, 1)), (sp["g_q_lora"], _whole(sp["g_q_lora"])),
                (sp["g_kv_lora"], _whole(sp["g_kv_lora"]))]
    cq, ckv = _rowwise(lambda i, zq, zk, gq, gk: (_rms(zq, gq), _rms(zk, gk)), lora_ins,
                       [(SDS((t, ql), BF16), _rows(ql)), (SDS((t, ql), BF16), _rows(ql))], [],
                       name="lora_norm_fwd", n_rows=t)
    q = _mm(cq, wts["w_uq"], mode="nn", out_dtype=F32, name="mm_q")
    kv = _mm(ckv, wts["w_ukv"], mode="nn", out_dtype=F32, name="mm_kv")
    q_hm, k_hm, v_hm = _attn_prep(q, kv, z_l, kr_col, cos, sin, n_heads)
    o, lse = _attn_fwd(q_hm, k_hm, v_hm)
    ya = _mm(o, wts["w_attn_out"], mode="nn", out_dtype=F32, name="mm_attn_out")

    mix_ins = [(yc, _rows(d)), (sp["b_conv_out"], _whole(sp["b_conv_out"])), (ya, _rows(d)), (z_gate, _rows(d, 0)),
               (z_gate, _rows(d, 1)), (b_gate_c, _whole(b_gate_c)), (b_gate_a, _whole(b_gate_a))]
    (mix,) = _rowwise(lambda i, *a: _mix(*a), mix_ins, [(SDS((t, d), BF16), _rows(d))], [], name="mix_fwd", n_rows=t)
    h1 = _mm(mix, wts["w_out"], mode="nn", out_dtype=F32, name="mm_out", add=h0)

    hn_ = _rms_fwd(h1, sp["g_ffn"], "rms_ffn")
    wts.update(fetch("ffn", hn_))
    ab = _mm(hn_, wts["w_gu"], mode="nn", out_dtype=F32, name="mm_ffn_in")
    tr_ffn = 64
    (f,) = _rowwise(lambda i, a, b: _swiglu(a, b), [(ab, _rows(dff, 0, tr_ffn)), (ab, _rows(dff, 1, tr_ffn))],
                    [(SDS((t, dff), BF16), _rows(dff, 0, tr_ffn))], [], name="swiglu_fwd", n_rows=t, tr=tr_ffn)
    h2 = _mm(f, wts["w_ffn_down"], mode="nn", out_dtype=F32, name="mm_ffn_down", add=h1)

    def head(i, h, g, tgt):
        y, vjp = jax.vjp(_rms, h, g)
        row = i * ROW_TILE + lax.broadcasted_iota(jnp.int32, (ROW_TILE, 1), 0)
        valid = jnp.logical_and(row >= N_META, row < length)
        err = jnp.where(valid, y - tgt, 0.0)
        dh, dg = vjp(err / d)
        loss = 0.5 * jnp.sum(jnp.sum(err * err, axis=-1, keepdims=True), axis=0, keepdims=True) / d
        return dh, dh, dg, jnp.broadcast_to(loss, (1, LANES))

    dh2, dh2_b, g_final, loss_v = _rowwise(
        head, [(h2, _rows(d)), (sp["g_final"], _whole(sp["g_final"])), (target_p, _rows(d))],
        [(SDS((t, d), F32), _rows(d)), (SDS((t, d), BF16), _rows(d))], [SDS((1, d), F32), SDS((1, LANES), F32)],
        name="loss_head", n_rows=t)
    loss = loss_v[0, 0]

    g_ffn_down = _mm(f, dh2_b, mode="tn", out_dtype=BF16, name="mm_g_ffn_down")
    df = _mm(dh2_b, wts["w_ffn_down"], mode="nt", out_dtype=F32, name="mm_d_f")

    def swiglu_bwd(i, a, b, df):
        _, vjp = jax.vjp(_swiglu, a, b)
        da, db = vjp(df)
        return jnp.concatenate([da, db], axis=-1)

    (dab,) = _rowwise(swiglu_bwd, [(ab, _rows(dff, 0, tr_ffn)), (ab, _rows(dff, 1, tr_ffn)), (df, _rows(dff, 0, tr_ffn))],
                      [(SDS((t, 2 * dff), BF16), _rows(2 * dff, 0, tr_ffn))], [], name="swiglu_bwd", n_rows=t, tr=tr_ffn)
    g_gu = _mm(hn_, dab, mode="tn", out_dtype=BF16, name="mm_g_ffn_in")
    dhn = _mm(dab, wts["w_gu"], mode="nt", out_dtype=F32, name="mm_d_hn")
    sent = ship("ffn", {"w_gu": g_gu, "w_ffn_down": g_ffn_down})
    dh1, dh1_b, g_g_ffn = _rms_bwd(h1, sp["g_ffn"] + sent[0, 0], dhn, dh2, "rms_ffn_bwd")

    g_w_out = _mm(mix, dh1_b, mode="tn", out_dtype=BF16, name="mm_g_out")
    dmix = _mm(dh1_b, wts["w_out"], mode="nt", out_dtype=F32, name="mm_d_mix")

    def mix_bwd(i, yc, bco, ya, zc, za, bgc, bga, dmix):
        _, vjp = jax.vjp(_mix, yc, bco, ya, zc, za, bgc, bga)
        dyc, dbco, dya, dzc, dza, dbgc, dbga = vjp(dmix)
        return dyc, dya, jnp.concatenate([dzc, dza], axis=-1), dbco, dbgc, dbga

    dyc, dya, dz_gate, g_b_conv_out, g_bgc, g_bga = _rowwise(
        mix_bwd, mix_ins + [(dmix, _rows(d))],
        [(SDS((t, d), BF16), _rows(d)), (SDS((t, d), BF16), _rows(d)), (SDS((t, 2 * d), BF16), _rows(2 * d))],
        [SDS((1, d), F32)] * 3, name="mix_bwd", n_rows=t)

    g_attn_out = _mm(o, dya, mode="tn", out_dtype=BF16, name="mm_g_attn_out")
    d_o = _mm(dya, wts["w_attn_out"], mode="nt", out_dtype=BF16, name="mm_d_o")
    dq_hm, dk_hm, dv_hm = _attn_bwd(q_hm, k_hm, v_hm, lse, d_o)
    dq, dkv, dkr = _attn_post(dq_hm, dk_hm, dv_hm, cos, sin)
    g_uq = _mm(cq, dq, mode="tn", out_dtype=BF16, name="mm_g_uq")
    g_ukv = _mm(ckv, dkv, mode="tn", out_dtype=BF16, name="mm_g_ukv")
    dcq = _mm(dq, wts["w_uq"], mode="nt", out_dtype=F32, name="mm_d_cq")
    dckv = _mm(dkv, wts["w_ukv"], mode="nt", out_dtype=F32, name="mm_d_ckv")

    def lora_bwd(i, zq, zk, gq, gk, dcq, dckv, dkr):
        _, vq = jax.vjp(_rms, zq, gq)
        _, vk = jax.vjp(_rms, zk, gk)
        dzq, dgq = vq(dcq)
        dzk, dgk = vk(dckv)
        return jnp.concatenate([dzq, dzk, dkr], axis=-1), dgq, dgk

    dz_l, g_g_q, g_g_kv = _rowwise(
        lora_bwd, lora_ins + [(dcq, _rows(ql)), (dckv, _rows(ql)), (dkr, _rows(LANES))],
        [(SDS((t, 2 * ql + LANES), BF16), _rows(2 * ql + LANES))], [SDS((1, ql), F32)] * 2,
        name="lora_norm_bwd", n_rows=t)

    g_conv_out = _mm(c3, dyc, mode="tn", out_dtype=BF16, name="mm_g_conv_out")
    dc3 = _mm(dyc, wts["w_conv_out"], mode="nt", out_dtype=F32, name="mm_d_c3")
    sent = ship("mixers", {"w_conv_out": g_conv_out, "w_uq": g_uq, "w_ukv": g_ukv, "w_attn_out": g_attn_out,
                           "w_out": g_w_out})

    def ln_bwd(i, c, g, b, dc3):
        _, vjp = jax.vjp(_ln_silu, c, g, b)
        return vjp(dc3)

    g_ln_sent = sp["g_conv_ln"] + sent[0, 0]
    dc1, g_g_ln, g_b_ln = _rowwise(
        ln_bwd, [ln_ins[0], (g_ln_sent, _whole(g_ln_sent)), ln_ins[2], (dc3, _rows(ch))],
        [(SDS((t, ch), F32), _rows(ch))], [SDS((1, ch), F32)] * 2, name="ln_silu_bwd", n_rows=t)
    dc0, g_w_dw, g_b_dw = _conv_bwd(dc1, c0, w_dw)

    def glu_bwd(i, za, zb, ba, bb, dc0):
        _, vjp = jax.vjp(_glu, za, zb, ba, bb)
        dza, dzb, dba, dbb = vjp(dc0)
        return jnp.concatenate([dza, dzb], axis=-1), dba, dbb

    dz_glu, g_bga_, g_bgb_ = _rowwise(glu_bwd, glu_ins + [(dc0, _rows(ch))],
                                      [(SDS((t, 2 * ch), BF16), _rows(2 * ch))], [SDS((1, ch), F32)] * 2,
                                      name="glu_bwd", n_rows=t)

    du = _mm(dz_glu, wts["w_glu"], mode="nt", out_dtype=F32, name="mm_d_u0")
    du = _mm(dz_l, wts["w_lora"], mode="nt", out_dtype=F32, name="mm_d_u1", add=du)
    du = _mm(dz_gate, wts["w_gate"], mode="nt", out_dtype=F32, name="mm_d_u2", add=du)
    dh0, _, g_g_mix = _rms_bwd(h0, sp["g_mix"], du, dh1, "rms_mix_bwd")
    big = {"meta_tokens": dh0[:N_META], "w_dw": g_w_dw}
    small = {
        "g_mix": g_g_mix, "b_glu": jnp.concatenate([g_bga_, g_bgb_], axis=1),
        "b_gate": jnp.concatenate([g_bgc, g_bga], axis=1), "b_dw": g_b_dw, "g_conv_ln": g_g_ln, "b_conv_ln": g_b_ln,
        "b_conv_out": g_b_conv_out, "g_q_lora": g_g_q, "g_kv_lora": g_g_kv, "g_ffn": g_g_ffn, "g_final": g_final,
    }
    sent = ship("small", {**big, **small})
    g_glu = _mm(u, dz_glu, mode="tn", out_dtype=BF16, name="mm_g_w_glu", after=sent)
    g_lora = _mm(u, dz_l, mode="tn", out_dtype=BF16, name="mm_g_w_lora", after=sent)
    g_gate = _mm(u, dz_gate, mode="tn", out_dtype=BF16, name="mm_g_w_gate", after=sent)
    big.update({"w_glu": g_glu, "w_lora": g_lora, "w_gate": g_gate})
    ship("input", big)
    return loss, dh0[N_META:length], big, small


def _slot(ref, dev):
    return ref.at[dev]


def _row_window(rows):
    return lambda ref, dev: ref.at[pl.ds(pl.multiple_of(dev * rows, 16), rows)]


def _col_window(width, offset=0):
    return lambda ref, dev: ref.at[:, pl.ds(pl.multiple_of(offset + dev * width, LANES), width)]


def _dev_index(x, y, c):
    return 4 * x + 2 * y + c


def _gather_weights(items, out_shapes):
    srcs = [it[0] for it in items]
    n, n_out = len(srcs), len(out_shapes)

    def body(*refs):
        src, out = refs[:n], refs[n:n + n_out]
        send_sems, recv_sems, local_sems = refs[n + n_out:]
        x, y, c = lax.axis_index("x"), lax.axis_index("y"), lax.axis_index("c")
        me, sibling = (x, y, c), (x, y, 1 - c)
        chips = [(1 - x, y), (x, 1 - y), (1 - x, 1 - y)]

        def place(i, block):
            _, o, window = items[i]
            return window(out[o], _dev_index(*block))

        def copy(k, i, block, to, from_src=False):
            return pltpu.make_async_remote_copy(
                src_ref=src[i] if from_src else place(i, block), dst_ref=place(i, block),
                send_sem=send_sems.at[k * n + i], recv_sem=recv_sems.at[k * n + i],
                device_id=to, device_id_type=pl.DeviceIdType.MESH)

        mine = [pltpu.make_async_copy(src[i], place(i, me), local_sems.at[i]) for i in range(n)]
        first = [copy(0, i, me, sibling, True) for i in range(n)]
        first += [copy(1 + j, i, me, (*chip, c), True) for j, chip in enumerate(chips) for i in range(n)]
        for cp in mine + first:
            cp.start()
        passed = [[copy(4 + j, i, (*chip, c), sibling) for i in range(n)] for j, chip in enumerate(chips)]
        for j, chip in enumerate(chips):
            for i in range(n):
                copy(1 + j, i, (*chip, c), me).wait_recv()
            for cp in passed[j]:
                cp.start()
        for i in range(n):
            copy(0, i, sibling, me).wait_recv()
        for j, chip in enumerate(chips):
            for i in range(n):
                copy(4 + j, i, (*chip, 1 - c), me).wait_recv()
        for cp in first + [cp for row in passed for cp in row]:
            cp.wait_send()
        for cp in mine:
            cp.wait()

    any_spec = pl.BlockSpec(memory_space=pl.ANY)
    return pl.pallas_call(
        body, name="gather_weights", in_specs=[any_spec] * n, out_specs=[any_spec] * n_out, out_shape=out_shapes,
        scratch_shapes=[pltpu.SemaphoreType.DMA((7 * n,)), pltpu.SemaphoreType.DMA((7 * n,)),
                        pltpu.SemaphoreType.DMA((n,))],
    )(*srcs)


def _exchange(srcs, out_shapes, items, *, masks, name):
    n_src, n_out, n = len(srcs), len(out_shapes), len(items)

    def body(*refs):
        src, out = refs[:n_src], refs[n_src:n_src + n_out]
        send_sems, recv_sems = refs[n_src + n_out:]
        remote = _peer_copies(src, out, send_sems, recv_sems, items, masks)
        for cp in remote:
            cp.start()
        for cp in remote:
            cp.wait()

    any_spec = pl.BlockSpec(memory_space=pl.ANY)
    return pl.pallas_call(
        body, name=name, in_specs=[any_spec] * n_src, out_specs=[any_spec] * n_out, out_shape=out_shapes,
        scratch_shapes=[pltpu.SemaphoreType.DMA((len(masks) * n,)), pltpu.SemaphoreType.DMA((len(masks) * n,))],
    )(*srcs)


def _peer_copies(src, out, send_sems, recv_sems, items, masks):
    x, y, c = lax.axis_index("x"), lax.axis_index("y"), lax.axis_index("c")
    me = _dev_index(x, y, c)
    n = len(items)
    copies = []
    for k, mask in enumerate(masks):
        px, py, pc = x ^ ((mask >> 2) & 1), y ^ ((mask >> 1) & 1), c ^ (mask & 1)
        peer = _dev_index(px, py, pc)
        for i, (s, s_win, o, d_win) in enumerate(items):
            copies.append(pltpu.make_async_remote_copy(
                src_ref=s_win(src[s], peer), dst_ref=d_win(out[o], me),
                send_sem=send_sems.at[k * n + i], recv_sem=recv_sems.at[k * n + i],
                device_id=(px, py, pc), device_id_type=pl.DeviceIdType.MESH))
    return copies


def _exchange_start(srcs, out_shapes, items, *, masks, name):
    n_src, n_out, n = len(srcs), len(out_shapes), len(items)
    n_sem = len(masks) * n
    n_buf = n_src + n_out

    def body(*refs):
        src, land = refs[:n_src], refs[n_src:n_buf]
        send_sems, recv_sems = refs[n_buf], refs[n_buf + 1]
        token = refs[-1]
        for cp in _peer_copies(src, land, send_sems, recv_sems, items, masks):
            cp.start()
        token[...] = jnp.zeros_like(token)

    hbm = pl.BlockSpec(memory_space=pltpu.HBM)
    sem = pl.BlockSpec(memory_space=pltpu.SEMAPHORE)
    bufs = [pltpu.with_memory_space_constraint(a, pltpu.HBM) for a in srcs]
    bufs += [pltpu.with_memory_space_constraint(lax.empty(s.shape, s.dtype), pltpu.HBM) for s in out_shapes]
    res = pl.pallas_call(
        body, name=name,
        out_shape=(pltpu.SemaphoreType.DMA((n_sem,)), pltpu.SemaphoreType.DMA((n_sem,)),
                   *[pltpu.HBM(b.shape, b.dtype) for b in bufs], SDS((8, LANES), F32)),
        in_specs=[hbm] * n_buf, out_specs=(sem, sem, *[hbm] * n_buf, pl.BlockSpec(memory_space=pltpu.VMEM)),
        input_output_aliases={i: 2 + i for i in range(n_buf)},
        compiler_params=pltpu.CompilerParams(has_side_effects=pltpu.SideEffectType.DATAFLOW_SIDE_EFFECTING),
    )(*bufs)
    return res[0], res[1], list(res[2:2 + n_src]), list(res[2 + n_src:2 + n_buf]), res[-1]


def _exchange_wait(send_sems, recv_sems, srcs, lands, items, after, *, masks, name):
    n_src, n_out = len(srcs), len(lands)
    n_buf = n_src + n_out

    def body(*refs):
        src, land = refs[:n_src], refs[n_src:n_buf]
        send_sems, recv_sems = refs[n_buf], refs[n_buf + 1]
        for cp in _peer_copies(src, land, send_sems, recv_sems, items, masks):
            cp.wait_send()
            cp.wait_recv()

    hbm = pl.BlockSpec(memory_space=pltpu.HBM)
    sem = pl.BlockSpec(memory_space=pltpu.SEMAPHORE)
    bufs = list(srcs) + list(lands)
    res = pl.pallas_call(
        body, name=name, out_shape=tuple(pltpu.HBM(b.shape, b.dtype) for b in bufs),
        in_specs=[hbm] * n_buf + [sem, sem, pl.BlockSpec(memory_space=pl.ANY)], out_specs=tuple([hbm] * n_buf),
        input_output_aliases={i: i for i in range(n_buf)},
        compiler_params=pltpu.CompilerParams(has_side_effects=pltpu.SideEffectType.DATAFLOW_SIDE_EFFECTING),
    )(*bufs, send_sems, recv_sems, after)
    return list(res[:n_src]), list(res[n_src:])


def _regroup(srcs, dsts, *, name, tr=256, after=()):
    def segments(shape, valid):
        if len(shape) == 3:
            return [(p, shape[2]) for p in range(shape[0])]
        return [(None, valid)]

    src_arrays = [s[0] if isinstance(s, tuple) else s for s in srcs]
    src_valid = [s[1] if isinstance(s, tuple) else s.shape[-1] for s in srcs]
    k_rows = src_arrays[0].shape[-2]
    src_segs = [(i, p, w) for i, a in enumerate(src_arrays) for p, w in segments(a.shape, src_valid[i])]
    dst_segs = [(j, p, w) for j, (shape, _, valid) in enumerate(dsts) for p, w in segments(shape, valid)]
    pieces = []
    si, so, di, do = 0, 0, 0, 0
    while si < len(src_segs) and di < len(dst_segs):
        n = min(src_segs[si][2] - so, dst_segs[di][2] - do)
        pieces.append((src_segs[si][0], src_segs[si][1], so, dst_segs[di][0], dst_segs[di][1], do, n))
        so, do = so + n, do + n
        if so == src_segs[si][2]:
            si, so = si + 1, 0
        if do == dst_segs[di][2]:
            di, do = di + 1, 0
    assert si == len(src_segs) and di == len(dst_segs), "source and destination columns differ in number"
    n_src = len(src_arrays)

    def body(*refs):
        src, dst = refs[:n_src], refs[n_src + len(after):]
        for j, (shape, dtype, valid) in enumerate(dsts):
            if len(shape) == 2 and valid < shape[1]:
                dst[j][:, valid:shape[1]] = jnp.zeros((tr, shape[1] - valid), dtype)
        for i, sp, so, j, dp, do, n in pieces:
            val = src[i][:, so:so + n] if sp is None else src[i][sp, :, so:so + n]
            if dp is None:
                dst[j][:, do:do + n] = val.astype(dst[j].dtype)
            else:
                dst[j][dp, :, do:do + n] = val.astype(dst[j].dtype)

    def spec(shape):
        if len(shape) == 3:
            return pl.BlockSpec((shape[0], tr, shape[2]), lambda i: (0, i, 0))
        return pl.BlockSpec((tr, shape[1]), lambda i: (i, 0))

    n_after = len(after)
    return pl.pallas_call(
        body, name=name, grid=(k_rows // tr,),
        in_specs=[spec(a.shape) for a in src_arrays] + [pl.BlockSpec(memory_space=pl.ANY)] * n_after,
        out_specs=[spec(shape) for shape, _, _ in dsts], out_shape=[SDS(shape, dtype) for shape, dtype, _ in dsts],
        compiler_params=_params(),
    )(*src_arrays, *after)


def _cast_bf16(a, name, after=None):
    r, c = a.shape
    tr = _row_tile(r, c * 4)
    (out,) = _rowwise(lambda i, v: v, [(a, _rows(c, 0, tr))], [(SDS((r, c), BF16), _rows(c, 0, tr))], [], name=name,
                      n_rows=r, tr=tr, after=after)
    return out


def _own_spec(kind, tr, width, n_tiles):
    if kind[0] == "slot":
        return pl.BlockSpec((None, tr, width), lambda i, me: (me[0], i, 0))
    if kind[0] == "rows":
        return pl.BlockSpec((tr, width), lambda i, me: (me[0] * n_tiles + i, 0))
    if kind[0] == "cols":
        return pl.BlockSpec((tr, width), lambda i, me: (i, kind[1] + me[0]))
    return pl.BlockSpec((tr, width), lambda i, me: (i, 0))


def _cast_into(me, a, whole, kind, name, into=None, after=None):
    r, c = a.shape
    tr = _row_tile(r, c * 4)
    extra = [x for x in (into, after) if x is not None]

    def body(me_ref, a_ref, *rest):
        rest[len(extra)][...] = a_ref[...].astype(BF16)

    return pl.pallas_call(
        body, name=name,
        grid_spec=pltpu.PrefetchScalarGridSpec(
            num_scalar_prefetch=1, grid=(r // tr,),
            in_specs=[pl.BlockSpec((tr, c), lambda i, me: (i, 0))] + [pl.BlockSpec(memory_space=pl.ANY)] * len(extra),
            out_specs=_own_spec(kind, tr, c, r // tr)),
        out_shape=whole, input_output_aliases={2: 0} if into is not None else {}, compiler_params=_params(),
    )(me, a, *extra)


def _own_block_copies(land, send_sems, recv_sems, items, masks):
    x, y, c = lax.axis_index("x"), lax.axis_index("y"), lax.axis_index("c")
    me = _dev_index(x, y, c)
    n = len(items)
    copies = []
    for k, mask in enumerate(masks):
        px, py, pc = x ^ ((mask >> 2) & 1), y ^ ((mask >> 1) & 1), c ^ (mask & 1)
        for i, (o, win) in enumerate(items):
            copies.append(pltpu.make_async_remote_copy(
                src_ref=win(land[o], me), dst_ref=win(land[o], me),
                send_sem=send_sems.at[k * n + i], recv_sem=recv_sems.at[k * n + i],
                device_id=(px, py, pc), device_id_type=pl.DeviceIdType.MESH))
    return copies


def _spread_start(lands, items, *, masks, name):
    n_buf, n_sem = len(lands), len(masks) * len(items)

    def body(*refs):
        land, send_sems, recv_sems, token = refs[:n_buf], refs[n_buf], refs[n_buf + 1], refs[-1]
        for cp in _own_block_copies(land, send_sems, recv_sems, items, masks):
            cp.start()
        token[...] = jnp.zeros_like(token)

    hbm = pl.BlockSpec(memory_space=pltpu.HBM)
    sem = pl.BlockSpec(memory_space=pltpu.SEMAPHORE)
    bufs = [pltpu.with_memory_space_constraint(a, pltpu.HBM) for a in lands]
    res = pl.pallas_call(
        body, name=name,
        out_shape=(pltpu.SemaphoreType.DMA((n_sem,)), pltpu.SemaphoreType.DMA((n_sem,)),
                   *[pltpu.HBM(b.shape, b.dtype) for b in bufs], SDS((8, LANES), F32)),
        in_specs=[hbm] * n_buf, out_specs=(sem, sem, *[hbm] * n_buf, pl.BlockSpec(memory_space=pltpu.VMEM)),
        input_output_aliases={i: 2 + i for i in range(n_buf)},
        compiler_params=pltpu.CompilerParams(has_side_effects=pltpu.SideEffectType.DATAFLOW_SIDE_EFFECTING),
    )(*bufs)
    return res[0], res[1], list(res[2:2 + n_buf]), res[-1]


def _spread_wait(send_sems, recv_sems, lands, items, after, *, masks, name):
    n_buf = len(lands)

    def body(*refs):
        land, send_sems, recv_sems = refs[:n_buf], refs[n_buf], refs[n_buf + 1]
        for cp in _own_block_copies(land, send_sems, recv_sems, items, masks):
            cp.wait_send()
            cp.wait_recv()

    hbm = pl.BlockSpec(memory_space=pltpu.HBM)
    sem = pl.BlockSpec(memory_space=pltpu.SEMAPHORE)
    res = pl.pallas_call(
        body, name=name, out_shape=tuple(pltpu.HBM(b.shape, b.dtype) for b in lands),
        in_specs=[hbm] * n_buf + [sem, sem, pl.BlockSpec(memory_space=pl.ANY)], out_specs=tuple([hbm] * n_buf),
        input_output_aliases={i: i for i in range(n_buf)},
        compiler_params=pltpu.CompilerParams(has_side_effects=pltpu.SideEffectType.DATAFLOW_SIDE_EFFECTING),
    )(*lands, send_sems, recv_sems, after)
    return list(res)


def _row_tile(rows, row_bytes, limit=2 * 1024 * 1024):
    best = None
    for t in range(16, rows + 1, 16):
        if rows % t == 0 and t * row_bytes <= limit:
            best = t
    return best if best is not None else rows


def _adamw(me, recv, own, own_kind, w, m, v, *, name):
    n_rows, width = w.shape
    tr = _row_tile(n_rows, width * 4, limit=1024 * 1024)
    n_tiles = n_rows // tr
    own_spec = _own_spec(own_kind, tr, width, n_tiles)

    def body(me_ref, r_ref, own_ref, w_ref, m_ref, v_ref, g_ref, d_ref, mo_ref, vo_ref):
        mine = own_ref[...].astype(F32)
        g = None
        for q in range(N_DEV):
            term = jnp.where(me_ref[0] == q, mine, r_ref[q].astype(F32))
            g = term if g is None else g + term
        m_new = ADAM_B1 * m_ref[...] + (1.0 - ADAM_B1) * g
        v_new = ADAM_B2 * v_ref[...] + (1.0 - ADAM_B2) * jnp.square(g)
        m_hat = m_new / (1.0 - ADAM_B1 ** ADAM_STEP)
        v_hat = v_new / (1.0 - ADAM_B2 ** ADAM_STEP)
        g_ref[...] = g
        d_ref[...] = -ADAM_LR * (m_hat / (jnp.sqrt(v_hat) + ADAM_EPS) + ADAM_WD * w_ref[...])
        mo_ref[...] = m_new
        vo_ref[...] = v_new

    row = pl.BlockSpec((tr, width), lambda i, me: (i, 0))
    return pl.pallas_call(
        body, name=name,
        grid_spec=pltpu.PrefetchScalarGridSpec(
            num_scalar_prefetch=1, grid=(n_tiles,),
            in_specs=[pl.BlockSpec((N_DEV, tr, width), lambda i, me: (0, i, 0)), own_spec, row, row, row],
            out_specs=[row] * 4),
        out_shape=[SDS((n_rows, width), F32)] * 4, compiler_params=_params(),
    )(me, recv, own, w, m, v)


def _offsets(sizes):
    offs, o = [], 0
    for n in sizes:
        offs.append(o)
        o += n
    return offs, o


def kernel(x, meta_tokens, g_mix, w_in, b_glu, b_gate, w_dw, b_dw, g_conv_ln, b_conv_ln, w_conv_out, b_conv_out, g_q_lora, w_uq, g_kv_lora, w_uk, w_uv, w_attn_out, w_out, g_ffn, w_ffn_gate, w_ffn_up, w_ffn_down, g_final, loss_target, m_meta_tokens, m_g_mix, m_w_in, m_b_glu, m_b_gate, m_w_dw, m_b_dw, m_g_conv_ln, m_b_conv_ln, m_w_conv_out, m_b_conv_out, m_g_q_lora, m_w_uq, m_g_kv_lora, m_w_uk, m_w_uv, m_w_attn_out, m_w_out, m_g_ffn, m_w_ffn_gate, m_w_ffn_up, m_w_ffn_down, m_g_final, v_meta_tokens, v_g_mix, v_w_in, v_b_glu, v_b_gate, v_w_dw, v_b_dw, v_g_conv_ln, v_b_conv_ln, v_w_conv_out, v_b_conv_out, v_g_q_lora, v_w_uq, v_g_kv_lora, v_w_uk, v_w_uv, v_w_attn_out, v_w_out, v_g_ffn, v_w_ffn_gate, v_w_ffn_up, v_w_ffn_down, v_g_final):
    w_all = dict(meta_tokens=meta_tokens, g_mix=g_mix, w_in=w_in, b_glu=b_glu, b_gate=b_gate, w_dw=w_dw, b_dw=b_dw, g_conv_ln=g_conv_ln, b_conv_ln=b_conv_ln, w_conv_out=w_conv_out, b_conv_out=b_conv_out, g_q_lora=g_q_lora, w_uq=w_uq, g_kv_lora=g_kv_lora, w_uk=w_uk, w_uv=w_uv, w_attn_out=w_attn_out, w_out=w_out, g_ffn=g_ffn, w_ffn_gate=w_ffn_gate, w_ffn_up=w_ffn_up, w_ffn_down=w_ffn_down, g_final=g_final)
    m_all = dict(meta_tokens=m_meta_tokens, g_mix=m_g_mix, w_in=m_w_in, b_glu=m_b_glu, b_gate=m_b_gate, w_dw=m_w_dw, b_dw=m_b_dw, g_conv_ln=m_g_conv_ln, b_conv_ln=m_b_conv_ln, w_conv_out=m_w_conv_out, b_conv_out=m_b_conv_out, g_q_lora=m_g_q_lora, w_uq=m_w_uq, g_kv_lora=m_g_kv_lora, w_uk=m_w_uk, w_uv=m_w_uv, w_attn_out=m_w_attn_out, w_out=m_w_out, g_ffn=m_g_ffn, w_ffn_gate=m_w_ffn_gate, w_ffn_up=m_w_ffn_up, w_ffn_down=m_w_ffn_down, g_final=m_g_final)
    v_all = dict(meta_tokens=v_meta_tokens, g_mix=v_g_mix, w_in=v_w_in, b_glu=v_b_glu, b_gate=v_b_gate, w_dw=v_w_dw, b_dw=v_b_dw, g_conv_ln=v_g_conv_ln, b_conv_ln=v_b_conv_ln, w_conv_out=v_w_conv_out, b_conv_out=v_b_conv_out, g_q_lora=v_g_q_lora, w_uq=v_w_uq, g_kv_lora=v_g_kv_lora, w_uk=v_w_uk, w_uv=v_w_uv, w_attn_out=v_w_attn_out, w_out=v_w_out, g_ffn=v_g_ffn, w_ffn_gate=v_w_ffn_gate, w_ffn_up=v_w_ffn_up, w_ffn_down=v_w_ffn_down, g_final=v_g_final)

    two_d = lambda a: a.reshape(a.shape[-2:]) if a.ndim >= 2 else a.reshape(1, -1)
    sh = {n: two_d(w_all[n]) for n in SHARDED}
    d = x.shape[-1]
    k_in, c_in = sh["w_in"].shape
    r_co, r_ao, r_wo, r_fd = (sh[n].shape[0] for n in ROW_SHARDED)
    ql, c_uq = sh["w_uq"].shape
    c_uk = sh["w_uk"].shape[1]
    c_ff = sh["w_ffn_gate"].shape[1]
    n_meta, c_meta = sh["meta_tokens"].shape
    n_taps, c_dw = sh["w_dw"].shape
    ch, dff = N_DEV * c_dw, N_DEV * c_ff
    n_lora = 2 * ql + QK_ROPE

    masks = tuple(range(1, N_DEV))
    whole_ref = lambda ref, dev: ref
    gathered = _gather_weights(
        [(_cast_bf16(sh["w_in"], "cast_w_in"), 0, _slot), (sh["meta_tokens"], 1, _col_window(c_meta)),
         (sh["w_dw"], 2, _col_window(c_dw))],
        [SDS((N_DEV, k_in, c_in), BF16), SDS((n_meta, N_DEV * c_meta), F32), SDS((n_taps, ch), F32)])
    me = (4 * lax.axis_index("x") + 2 * lax.axis_index("y") + lax.axis_index("c")).astype(jnp.int32).reshape(1)

    def placed(n, whole, kind, into=None):
        return _cast_into(me, sh[n], whole, kind, "cast_" + n, into=into, after=None if into is not None else gathered[0])

    w_ukv0 = placed("w_uk", SDS((ql, 2 * N_DEV * c_uk), BF16), ("cols", 0))
    coming = {}
    for group, lands, items in (
            ("mixers",
             [placed("w_conv_out", SDS((N_DEV * r_co, d), BF16), ("rows",)),
              placed("w_uq", SDS((ql, N_DEV * c_uq), BF16), ("cols", 0)),
              placed("w_uv", SDS((ql, 2 * N_DEV * c_uk), BF16), ("cols", N_DEV), into=w_ukv0),
              placed("w_attn_out", SDS((N_DEV * r_ao, d), BF16), ("rows",)),
              placed("w_out", SDS((N_DEV * r_wo, d), BF16), ("rows",))],
             [(0, _row_window(r_co)), (1, _col_window(c_uq)), (2, _col_window(c_uk)),
              (2, _col_window(c_uk, N_DEV * c_uk)), (3, _row_window(r_ao)), (4, _row_window(r_wo))]),
            ("ffn",
             [placed("w_ffn_gate", SDS((N_DEV, d, c_ff), BF16), ("slot",)),
              placed("w_ffn_up", SDS((N_DEV, d, c_ff), BF16), ("slot",)),
              placed("w_ffn_down", SDS((N_DEV * r_fd, d), BF16), ("rows",))],
             [(0, _slot), (1, _slot), (2, _row_window(r_fd))])):
        coming[group] = _spread_start(lands, items, masks=masks, name="send_weights_" + group) + (items,)

    def fetch(group, after):
        send_sems, recv_sems, lands, _, items = coming[group]
        lands = _spread_wait(send_sems, recv_sems, lands, items, after, masks=masks, name="wait_weights_" + group)
        if group == "mixers":
            return {"w_conv_out": lands[0], "w_uq": lands[1], "w_ukv": lands[2], "w_attn_out": lands[3],
                    "w_out": lands[4]}
        (w_gu,) = _regroup([lands[0], lands[1]], [((d, 2 * dff), BF16, 2 * dff)], name="unpack_w_ffn_in")
        return {"w_gu": w_gu, "w_ffn_down": lands[2]}

    w_glu, w_lora, w_gate = _regroup(
        [gathered[0]], [((k_in, 2 * ch), BF16, 2 * ch), ((k_in, n_lora + LANES - QK_ROPE), BF16, n_lora),
                        ((k_in, 2 * d), BF16, 2 * d)], name="unpack_w_in",
        after=(coming["mixers"][3], coming["ffn"][3]))
    wts = {"w_glu": w_glu, "w_lora": w_lora, "w_gate": w_gate}
    sp = {n: w_all[n].reshape(1, -1) for n in SMALL}
    dims = {"n_heads": N_DEV * c_uq // QK, "dff": dff}

    recv_shape = lambda n: SDS((N_DEV,) + sh[n].shape, F32 if n in F32_GATHERED else BF16)
    in_flight = {}

    s_sizes = [w_all[n].size for n in SMALL]
    s_offs, n_s = _offsets(s_sizes)
    cat_small = lambda src: jnp.concatenate([src[n].reshape(1, -1) for n in SMALL], axis=1)
    small_names = ("meta_tokens", "w_dw")

    def ship(group, grads):
        if group == "small":
            srcs = [grads["meta_tokens"], grads["w_dw"], cat_small(grads)]
            recv = _exchange(
                srcs, [recv_shape(n) for n in small_names] + [SDS((N_DEV, 1, n_s), F32)],
                [(0, _col_window(c_meta), 0, _slot), (1, _col_window(c_dw), 1, _slot), (2, whole_ref, 2, _slot)],
                masks=masks, name="exchange_grads_small")
            in_flight[group] = (srcs, recv)
            return recv[2]
        if group == "input":
            (s_in,) = _regroup([grads["w_glu"], (grads["w_lora"], n_lora), grads["w_gate"]],
                               [((N_DEV, k_in, c_in), BF16, None)], name="pack_g_w_in")
            srcs, names, items, own = [s_in], ("w_in",), [(0, _slot, 0, _slot)], [(0, ("slot",))]
        elif group == "ffn":
            s_gate, s_up = _regroup([grads["w_gu"]], [((N_DEV, d, c_ff), BF16, None)] * 2, name="pack_g_w_ffn_in")
            srcs = [s_gate, s_up, grads["w_ffn_down"]]
            names = ("w_ffn_gate", "w_ffn_up", "w_ffn_down")
            items = [(0, _slot, 0, _slot), (1, _slot, 1, _slot), (2, _row_window(r_fd), 2, _slot)]
            own = [(0, ("slot",)), (1, ("slot",)), (2, ("rows",))]
        else:
            srcs = [grads["w_conv_out"], grads["w_uq"], grads["w_ukv"], grads["w_attn_out"], grads["w_out"]]
            names = ("w_conv_out", "w_uq", "w_uk", "w_uv", "w_attn_out", "w_out")
            items = [(0, _row_window(r_co), 0, _slot), (1, _col_window(c_uq), 1, _slot),
                     (2, _col_window(c_uk), 2, _slot), (2, _col_window(c_uk, N_DEV * c_uk), 3, _slot),
                     (3, _row_window(r_ao), 4, _slot), (4, _row_window(r_wo), 5, _slot)]
            own = [(0, ("rows",)), (1, ("cols", 0)), (2, ("cols", 0)), (2, ("cols", N_DEV)), (3, ("rows",)),
                   (4, ("rows",))]
        send_sems, recv_sems, srcs, lands, zero = _exchange_start(
            srcs, [recv_shape(n) for n in names], items, masks=masks, name="send_grads_" + group)
        in_flight[group] = (send_sems, recv_sems, srcs, lands, items, names, own, zero)
        return zero

    loss, grad_x, _, _ = _local_step(x[0], loss_target[0], gathered[1], gathered[2], wts, sp, ship, fetch, dims)

    by_name = {}

    def update(n, recv, own, kind):
        outs = _adamw(me, recv, own, kind, sh[n], two_d(m_all[n]), two_d(v_all[n]), name="adamw_" + n)
        by_name[n] = [o.reshape(w_all[n].shape) for o in outs]
        return outs[0]

    small_srcs, small_recv = in_flight["small"]
    for n, recv, src in zip(small_names, small_recv, small_srcs):
        update(n, recv, src, ("cols", 0))
    outs = _adamw(me, small_recv[2], small_srcs[2], ("whole",), cat_small(w_all), cat_small(m_all), cat_small(v_all),
                  name="adamw_replicated")
    for n, o, s in zip(SMALL, s_offs, s_sizes):
        by_name[n] = [out[:, o:o + s].reshape(w_all[n].shape) for out in outs]
    done = in_flight["input"][-1]
    for group in ("ffn", "mixers", "input"):
        send_sems, recv_sems, srcs, lands, items, names, own, _ = in_flight[group]
        srcs, lands = _exchange_wait(send_sems, recv_sems, srcs, lands, items, done, masks=masks,
                                     name="wait_grads_" + group)
        for n, land, (s, kind) in zip(names, lands, own):
            done = update(n, land, srcs[s], kind)
    result = [[by_name[n][k] for n in WEIGHTS] for k in range(4)]
    loss = lax.psum(loss, ("x", "y", "c"))
    return (loss, grad_x[None], *result[0], *result[1], *result[2], *result[3])
```

```python
import functools

import jax
import jax.numpy as jnp
from jax import lax
from jax.experimental import pallas as pl
from jax.experimental.pallas import tpu as pltpu

F32, BF16 = jnp.float32, jnp.bfloat16
SDS = jax.ShapeDtypeStruct

N_DEV = 8
N_META = 16
BLOCK_Q = 128
CONV_WIDTH = 31
QK_NOPE, QK_ROPE, V_HEAD = 128, 64, 128
QK = QK_NOPE + QK_ROPE
ROPE_THETA = 10000.0
EPS = 1e-6
ADAM_LR, ADAM_B1, ADAM_B2, ADAM_EPS, ADAM_WD, ADAM_STEP = 0.001, 0.9, 0.999, 1e-08, 0.01, 10

LANES = 128
ROW_TILE = 128
PACK_W = 1024
VMEM_LIMIT = 56 * 1024 * 1024

BIG = ("w_in", "w_conv_out", "w_uq", "w_uk", "w_uv", "w_attn_out", "w_out", "w_ffn_gate", "w_ffn_up", "w_ffn_down")
F32_GATHERED = ("meta_tokens", "w_dw")
SHARDED = BIG + F32_GATHERED
ROW_SHARDED = ("w_conv_out", "w_attn_out", "w_out", "w_ffn_down")
SMALL = ("g_mix", "b_glu", "b_gate", "b_dw", "g_conv_ln", "b_conv_ln", "b_conv_out", "g_q_lora", "g_kv_lora",
         "g_ffn", "g_final")
WEIGHTS = ("meta_tokens", "g_mix", "w_in", "b_glu", "b_gate", "w_dw", "b_dw", "g_conv_ln", "b_conv_ln", "w_conv_out",
           "b_conv_out", "g_q_lora", "w_uq", "g_kv_lora", "w_uk", "w_uv", "w_attn_out", "w_out", "g_ffn", "w_ffn_gate",
           "w_ffn_up", "w_ffn_down", "g_final")


def _params():
    return pltpu.CompilerParams(vmem_limit_bytes=VMEM_LIMIT)


def _tile(dim, limit):
    best = None
    t = LANES
    while t <= min(dim, limit):
        if dim % t == 0:
            best = t
        t += LANES
    return best if best is not None else dim


def _mm(a, b, *, mode, out_dtype, name, add=None, after=None):
    if mode == "nn":
        (m, kc), n = a.shape, b.shape[1]
    elif mode == "nt":
        (m, kc), n = a.shape, b.shape[0]
    else:
        (kc, m), n = a.shape, b.shape[1]
    if mode == "tn":
        tm, tn, tk = _tile(m, 1024), _tile(n, 512), kc
    else:
        tm, tn, tk = m, _tile(n, 512), _tile(kc, 512)
    nk = kc // tk
    if mode == "nn":
        a_spec = pl.BlockSpec((tm, tk), lambda i, j, k: (i, k))
        b_spec = pl.BlockSpec((tk, tn), lambda i, j, k: (k, j))
        dims = (((1,), (0,)), ((), ()))
    elif mode == "nt":
        a_spec = pl.BlockSpec((tm, tk), lambda i, j, k: (i, k))
        b_spec = pl.BlockSpec((tn, tk), lambda i, j, k: (j, k))
        dims = (((1,), (1,)), ((), ()))
    else:
        a_spec = pl.BlockSpec((tk, tm), lambda i, j, k: (k, i))
        b_spec = pl.BlockSpec((tk, tn), lambda i, j, k: (k, j))
        dims = (((0,), (0,)), ((), ()))
    o_spec = pl.BlockSpec((tm, tn), lambda i, j, k: (i, j))
    has_add = add is not None

    def body(*refs):
        if after is not None:
            refs = refs[:-3] + refs[-2:]
        if has_add:
            a_ref, b_ref, add_ref, o_ref, acc_ref = refs
        else:
            a_ref, b_ref, o_ref, acc_ref = refs
        k = pl.program_id(2)
        p = lax.dot_general(a_ref[...], b_ref[...], dims, preferred_element_type=F32)
        if nk == 1:
            o_ref[...] = ((p + add_ref[...]) if has_add else p).astype(o_ref.dtype)
            return

        @pl.when(k == 0)
        def _():
            acc_ref[...] = (p + add_ref[...]) if has_add else p

        @pl.when(jnp.logical_and(k > 0, k < nk - 1))
        def _():
            acc_ref[...] += p

        @pl.when(k == nk - 1)
        def _():
            o_ref[...] = (acc_ref[...] + p).astype(o_ref.dtype)

    in_specs = [a_spec, b_spec] + ([o_spec] if has_add else [])
    args = (a, b) + ((add,) if has_add else ())
    if after is not None:
        in_specs, args = in_specs + [pl.BlockSpec(memory_space=pl.ANY)], args + (after,)
    acc_shape = (tm, tn) if nk > 1 else (8, LANES)
    return pl.pallas_call(
        body, name=name, grid=(m // tm, n // tn, nk), in_specs=in_specs, out_specs=o_spec,
        out_shape=SDS((m, n), out_dtype), scratch_shapes=[pltpu.VMEM(acc_shape, F32)],
        compiler_params=_params(),
    )(*args)


def _rows(width, col=0, tr=ROW_TILE):
    return pl.BlockSpec((tr, width), lambda i: (i, col))


def _whole(arr):
    nd = arr.ndim
    return pl.BlockSpec(arr.shape, lambda i: (0,) * nd)


def _rowwise(fn, ins, outs, accs, *, name, n_rows, tr=ROW_TILE, after=None):
    n_in, n_out = len(ins), len(outs)
    if after is not None:
        ins = list(ins) + [(after, pl.BlockSpec(memory_space=pl.ANY))]

    def body(*refs):
        i = pl.program_id(0)
        res = fn(i, *[r[...] for r in refs[:n_in]])
        refs = refs[:n_in] + refs[len(ins):]
        res = res if isinstance(res, (tuple, list)) else (res,)
        for o_ref, v in zip(refs[n_in:n_in + n_out], res[:n_out]):
            o_ref[...] = v.astype(o_ref.dtype)
        for a_ref, v in zip(refs[n_in + n_out:], res[n_out:]):
            @pl.when(i == 0)
            def _(a_ref=a_ref, v=v):
                a_ref[...] = v

            @pl.when(i > 0)
            def _(a_ref=a_ref, v=v):
                a_ref[...] += v

    acc_specs = [pl.BlockSpec(s.shape, lambda i, nd=len(s.shape): (0,) * nd) for s in accs]
    res = pl.pallas_call(
        body, name=name, grid=(n_rows // tr,),
        in_specs=[s for _, s in ins], out_specs=[s for _, s in outs] + acc_specs,
        out_shape=[s for s, _ in outs] + list(accs), compiler_params=_params(),
    )(*[a for a, _ in ins])
    return res


def _rms(x, g):
    return x * lax.rsqrt(jnp.mean(x * x, axis=-1, keepdims=True) + EPS) * g


def _sigmoid(x):
    return 1.0 / (1.0 + jnp.exp(-x))


def _silu(x):
    return x * _sigmoid(x)


def _rms_fwd(h, g, name):
    t, d = h.shape
    (u,) = _rowwise(lambda i, h, g: _rms(h, g), [(h, _rows(d)), (g, _whole(g))], [(SDS((t, d), BF16), _rows(d))], [],
                    name=name, n_rows=t)
    return u


def _rms_bwd(h, g, du, dres, name):
    t, d = h.shape

    def fn(i, h, g, du, dres):
        _, vjp = jax.vjp(_rms, h, g)
        dh, dg = vjp(du)
        dh = dh + dres
        return dh, dh, dg

    return _rowwise(fn, [(h, _rows(d)), (g, _whole(g)), (du, _rows(d)), (dres, _rows(d))],
                    [(SDS((t, d), F32), _rows(d)), (SDS((t, d), BF16), _rows(d))], [SDS((1, d), F32)],
                    name=name, n_rows=t)


def _conv_fwd(c0, w_dw, b_dw):
    t, ch = c0.shape
    tc = _tile(ch, 256)
    halo = 32
    shift = halo - (CONV_WIDTH - 1)

    def body(x_ref, w_ref, b_ref, o_ref, pad_ref):
        pad_ref[0:halo, :] = jnp.zeros((halo, tc), F32)
        pad_ref[halo:halo + t, :] = x_ref[...]
        for r0 in range(0, t, ROW_TILE):
            acc = jnp.zeros((ROW_TILE, tc), F32) + b_ref[...]
            for j in range(CONV_WIDTH):
                acc = acc + pad_ref[r0 + shift + j:r0 + shift + j + ROW_TILE, :] * w_ref[j:j + 1, :]
            o_ref[r0:r0 + ROW_TILE, :] = acc

    col = lambda i: (0, i)
    return pl.pallas_call(
        body, name="conv_fwd", grid=(ch // tc,),
        in_specs=[pl.BlockSpec((t, tc), col), pl.BlockSpec((CONV_WIDTH, tc), col), pl.BlockSpec((1, tc), col)],
        out_specs=pl.BlockSpec((t, tc), col), out_shape=SDS((t, ch), F32),
        scratch_shapes=[pltpu.VMEM((halo + t, tc), F32)], compiler_params=_params(),
    )(c0, w_dw, b_dw)


def _conv_bwd(dc1, c0, w_dw):
    t, ch = c0.shape
    tc = _tile(ch, 256)
    halo = 32
    shift = halo - (CONV_WIDTH - 1)

    def body(d_ref, x_ref, w_ref, dx_ref, dw_ref, db_ref, xpad_ref, dpad_ref):
        xpad_ref[0:halo, :] = jnp.zeros((halo, tc), F32)
        xpad_ref[halo:halo + t, :] = x_ref[...]
        dpad_ref[0:t, :] = d_ref[...]
        dpad_ref[t:t + halo, :] = jnp.zeros((halo, tc), F32)
        for r0 in range(0, t, ROW_TILE):
            acc = jnp.zeros((ROW_TILE, tc), F32)
            for j in range(CONV_WIDTH):
                off = r0 + (CONV_WIDTH - 1) - j
                acc = acc + dpad_ref[off:off + ROW_TILE, :] * w_ref[j:j + 1, :]
            dx_ref[r0:r0 + ROW_TILE, :] = acc
        for j in range(CONV_WIDTH):
            acc = jnp.zeros((1, tc), F32)
            for r0 in range(0, t, ROW_TILE):
                prod = d_ref[r0:r0 + ROW_TILE, :] * xpad_ref[r0 + shift + j:r0 + shift + j + ROW_TILE, :]
                acc = acc + jnp.sum(prod, axis=0, keepdims=True)
            dw_ref[j:j + 1, :] = acc
        db_ref[...] = jnp.sum(d_ref[...], axis=0, keepdims=True)

    col = lambda i: (0, i)
    return pl.pallas_call(
        body, name="conv_bwd", grid=(ch // tc,),
        in_specs=[pl.BlockSpec((t, tc), col), pl.BlockSpec((t, tc), col), pl.BlockSpec((CONV_WIDTH, tc), col)],
        out_specs=[pl.BlockSpec((t, tc), col), pl.BlockSpec((CONV_WIDTH, tc), col), pl.BlockSpec((1, tc), col)],
        out_shape=[SDS((t, ch), F32), SDS((CONV_WIDTH, ch), F32), SDS((1, ch), F32)],
        scratch_shapes=[pltpu.VMEM((halo + t, tc), F32), pltpu.VMEM((halo + t, tc), F32)], compiler_params=_params(),
    )(dc1, c0, w_dw)


def _rope(x1, x2, cos, sin):
    return x1 * cos - x2 * sin, x1 * sin + x2 * cos


def _attn_prep(q, kv, z_l, kr_col, cos, sin, n_heads):
    t = q.shape[0]
    hn = n_heads * QK_NOPE
    half = QK_ROPE // 2

    def body(q_ref, kv_ref, kr_ref, cos_ref, sin_ref, qo_ref, ko_ref, vo_ref):
        cos, sin = cos_ref[...], sin_ref[...]
        kr = kr_ref[...]
        k1, k2 = _rope(kr[:, 0:half], kr[:, half:QK_ROPE], cos, sin)
        for h in range(n_heads):
            b = h * QK
            q1, q2 = _rope(q_ref[:, b + QK_NOPE:b + QK_NOPE + half], q_ref[:, b + QK_NOPE + half:b + QK], cos, sin)
            qo_ref[h] = jnp.concatenate([q_ref[:, b:b + QK_NOPE], q1, q2], axis=-1).astype(BF16)
            ko_ref[h] = jnp.concatenate([kv_ref[:, h * QK_NOPE:(h + 1) * QK_NOPE], k1, k2], axis=-1).astype(BF16)
            vo_ref[h] = kv_ref[:, hn + h * V_HEAD:hn + (h + 1) * V_HEAD].astype(BF16)

    tr = ROW_TILE
    hm = lambda w: pl.BlockSpec((n_heads, tr, w), lambda i: (0, i, 0))
    return pl.pallas_call(
        body, name="attn_prep", grid=(t // tr,),
        in_specs=[_rows(q.shape[1]), _rows(kv.shape[1]), _rows(LANES, kr_col), _rows(half), _rows(half)],
        out_specs=[hm(QK), hm(QK), hm(V_HEAD)],
        out_shape=[SDS((n_heads, t, QK), BF16), SDS((n_heads, t, QK), BF16), SDS((n_heads, t, V_HEAD), BF16)],
        compiler_params=_params(),
    )(q, kv, z_l, cos, sin)


def _attn_post(dq_hm, dk_hm, dv_hm, cos, sin):
    n_heads, t, _ = dq_hm.shape
    hn = n_heads * QK_NOPE
    half = QK_ROPE // 2

    def unrope(d1, d2, cos, sin):
        return d1 * cos + d2 * sin, d2 * cos - d1 * sin

    def body(dq_ref, dk_ref, dv_ref, cos_ref, sin_ref, qo_ref, kvo_ref, kro_ref):
        cos, sin = cos_ref[...], sin_ref[...]
        dkr = jnp.zeros((ROW_TILE, QK_ROPE), F32)
        for h in range(n_heads):
            dq = dq_ref[h]
            d1, d2 = unrope(dq[:, QK_NOPE:QK_NOPE + half], dq[:, QK_NOPE + half:QK], cos, sin)
            qo_ref[:, h * QK:(h + 1) * QK] = jnp.concatenate([dq[:, 0:QK_NOPE], d1, d2], axis=-1).astype(BF16)
            dk = dk_ref[h]
            kvo_ref[:, h * QK_NOPE:(h + 1) * QK_NOPE] = dk[:, 0:QK_NOPE].astype(BF16)
            kvo_ref[:, hn + h * V_HEAD:hn + (h + 1) * V_HEAD] = dv_ref[h].astype(BF16)
            dkr = dkr + dk[:, QK_NOPE:QK]
        d1, d2 = unrope(dkr[:, 0:half], dkr[:, half:QK_ROPE], cos, sin)
        kro_ref[...] = jnp.concatenate([d1, d2, jnp.zeros((ROW_TILE, LANES - QK_ROPE), F32)], axis=-1)

    tr = ROW_TILE
    hm = lambda w: pl.BlockSpec((n_heads, tr, w), lambda i: (0, i, 0))
    return pl.pallas_call(
        body, name="attn_post", grid=(t // tr,),
        in_specs=[hm(QK), hm(QK), hm(V_HEAD), _rows(half), _rows(half)],
        out_specs=[_rows(n_heads * QK), _rows(2 * hn), _rows(LANES)],
        out_shape=[SDS((t, n_heads * QK), BF16), SDS((t, 2 * hn), BF16), SDS((t, LANES), F32)],
        compiler_params=_params(),
    )(dq_hm, dk_hm, dv_hm, cos, sin)


N_QBLK = 4
_NT = (((1,), (1,)), ((), ()))
_TN = (((0,), (0,)), ((), ()))


def _scores(q, k, r0, scale):
    s = lax.dot_general(q, k, _NT, preferred_element_type=F32) * scale
    row = r0 + lax.broadcasted_iota(jnp.int32, s.shape, 0)
    col = lax.broadcasted_iota(jnp.int32, s.shape, 1)
    return jnp.where(col <= row, s, -jnp.inf)


def _attn_fwd(q_hm, k_hm, v_hm):
    n_heads, t, _ = q_hm.shape
    bq = t // N_QBLK
    scale = QK ** -0.5

    def body(q_ref, k_ref, v_ref, o_ref, lse_ref):
        for i in range(N_QBLK):
            r0, n_k = i * bq, (i + 1) * bq
            s = _scores(q_ref[0, r0:r0 + bq, :], k_ref[0, 0:n_k, :], r0, scale)
            m = jnp.max(s, axis=-1, keepdims=True)
            p = jnp.exp(s - m)
            l = jnp.sum(p, axis=-1, keepdims=True)
            p = (p / l).astype(BF16)
            o_ref[r0:r0 + bq, :] = jnp.dot(p, v_ref[0, 0:n_k, :], preferred_element_type=F32).astype(BF16)
            lse_ref[0, r0:r0 + bq, :] = m + jnp.log(l)

    head = lambda w: pl.BlockSpec((1, t, w), lambda h: (h, 0, 0))
    return pl.pallas_call(
        body, name="attn_fwd", grid=(n_heads,),
        in_specs=[head(QK), head(QK), head(V_HEAD)],
        out_specs=[pl.BlockSpec((t, V_HEAD), lambda h: (0, h)), head(1)],
        out_shape=[SDS((t, n_heads * V_HEAD), BF16), SDS((n_heads, t, 1), F32)],
        compiler_params=_params(),
    )(q_hm, k_hm, v_hm)


def _attn_bwd(q_hm, k_hm, v_hm, lse, d_o):
    n_heads, t, _ = q_hm.shape
    bq = t // N_QBLK
    scale = QK ** -0.5

    def body(q_ref, k_ref, v_ref, lse_ref, do_ref, dq_ref, dk_ref, dv_ref):
        dk_ref[...] = jnp.zeros(dk_ref.shape, F32)
        dv_ref[...] = jnp.zeros(dv_ref.shape, F32)
        for i in range(N_QBLK):
            r0, n_k = i * bq, (i + 1) * bq
            q = q_ref[0, r0:r0 + bq, :]
            k = k_ref[0, 0:n_k, :]
            d_o = do_ref[r0:r0 + bq, :]
            s = _scores(q, k, r0, scale)
            p = jnp.exp(s - lse_ref[0, r0:r0 + bq, :])
            dp = lax.dot_general(d_o, v_ref[0, 0:n_k, :], _NT, preferred_element_type=F32)
            ds = (p * (dp - jnp.sum(dp * p, axis=-1, keepdims=True)) * scale).astype(BF16)
            dq_ref[0, r0:r0 + bq, :] = jnp.dot(ds, k, preferred_element_type=F32)
            dk_ref[0, 0:n_k, :] += lax.dot_general(ds, q, _TN, preferred_element_type=F32)
            dv_ref[0, 0:n_k, :] += lax.dot_general(p.astype(BF16), d_o, _TN, preferred_element_type=F32)

    head = lambda w: pl.BlockSpec((1, t, w), lambda h: (h, 0, 0))
    return pl.pallas_call(
        body, name="attn_bwd", grid=(n_heads,),
        in_specs=[head(QK), head(QK), head(V_HEAD), head(1), pl.BlockSpec((t, V_HEAD), lambda h: (0, h))],
        out_specs=[head(QK), head(QK), head(V_HEAD)],
        out_shape=[SDS((n_heads, t, QK), F32), SDS((n_heads, t, QK), F32), SDS((n_heads, t, V_HEAD), F32)],
        compiler_params=_params(),
    )(q_hm, k_hm, v_hm, lse, d_o)


def _glu(za, zb, ba, bb):
    return (za + ba) * _sigmoid(zb + bb)


def _ln_silu(c, g, b):
    mu = jnp.mean(c, axis=-1, keepdims=True)
    var = jnp.mean(jnp.square(c - mu), axis=-1, keepdims=True)
    return _silu((c - mu) * lax.rsqrt(var + EPS) * g + b)


def _mix(yc, bco, ya, zc, za, bgc, bga):
    return _sigmoid(zc + bgc) * (yc + bco) + _sigmoid(za + bga) * ya


def _swiglu(a, b):
    return _silu(a) * b


def _local_step(x, target, meta, w_dw, wts, sp, ship=None, fetch=None, dims=None):
    wts = dict(wts)
    if ship is None:
        ship = lambda group, grads: jnp.zeros((8, LANES), F32)
    if fetch is None:
        fetch = lambda group, after: {}
    seq, d = x.shape
    length = N_META + seq
    t = -(-length // BLOCK_Q) * BLOCK_Q
    ch = w_dw.shape[1]
    ql = sp["g_q_lora"].shape[1]
    n_heads = dims["n_heads"] if dims else wts["w_uq"].shape[1] // QK
    hn = n_heads * QK_NOPE
    dff = dims["dff"] if dims else wts["w_ffn_down"].shape[0]
    assert sp["g_kv_lora"].shape[1] == ql and ql % LANES == 0 and t % (N_QBLK * 16) == 0
    pad_rows = lambda a: jnp.concatenate([jnp.zeros((N_META, d), F32), a, jnp.zeros((t - length, d), F32)], axis=0)
    h0 = jnp.concatenate([meta, x, jnp.zeros((t - length, d), F32)], axis=0)
    target_p = pad_rows(target)

    pos = jnp.arange(t, dtype=F32)
    inv_freq = ROPE_THETA ** (-jnp.arange(0, QK_ROPE, 2, dtype=F32) / QK_ROPE)
    ang = pos[:, None] * inv_freq[None, :]
    cos, sin = jnp.cos(ang), jnp.sin(ang)

    b_glu_a, b_glu_b = sp["b_glu"][:, :ch], sp["b_glu"][:, ch:]
    b_gate_c, b_gate_a = sp["b_gate"][:, :d], sp["b_gate"][:, d:]
    kr_col = 2 * ql // LANES

    u = _rms_fwd(h0, sp["g_mix"], "rms_mix")
    z_glu = _mm(u, wts["w_glu"], mode="nn", out_dtype=F32, name="mm_z_glu")
    z_l = _mm(u, wts["w_lora"], mode="nn", out_dtype=F32, name="mm_z_lora")
    z_gate = _mm(u, wts["w_gate"], mode="nn", out_dtype=F32, name="mm_z_gate")

    glu_ins = [(z_glu, _rows(ch, 0)), (z_glu, _rows(ch, 1)), (b_glu_a, _whole(b_glu_a)), (b_glu_b, _whole(b_glu_b))]
    (c0,) = _rowwise(lambda i, za, zb, ba, bb: _glu(za, zb, ba, bb), glu_ins, [(SDS((t, ch), F32), _rows(ch))], [],
                     name="glu_fwd", n_rows=t)
    c1 = _conv_fwd(c0, w_dw, sp["b_dw"])
    ln_ins = [(c1, _rows(ch)), (sp["g_conv_ln"], _whole(sp["g_conv_ln"])), (sp["b_conv_ln"], _whole(sp["b_conv_ln"]))]
    (c3,) = _rowwise(lambda i, c, g, b: _ln_silu(c, g, b), ln_ins, [(SDS((t, ch), BF16), _rows(ch))], [],
                     name="ln_silu_fwd", n_rows=t)
    wts.update(fetch("mixers", c3))
    yc = _mm(c3, wts["w_conv_out"], mode="nn", out_dtype=F32, name="mm_conv_out")

    lora_ins = [(z_l, _rows(ql, 0)), (z_l, _rows(ql, 1)), (sp["g_q_lora"], _whole(sp["g_q_lora"])),
                (sp["g_kv_lora"], _whole(sp["g_kv_lora"]))]
    cq, ckv = _rowwise(lambda i, zq, zk, gq, gk: (_rms(zq, gq), _rms(zk, gk)), lora_ins,
                       [(SDS((t, ql), BF16), _rows(ql)), (SDS((t, ql), BF16), _rows(ql))], [],
                       name="lora_norm_fwd", n_rows=t)
    q = _mm(cq, wts["w_uq"], mode="nn", out_dtype=F32, name="mm_q")
    kv = _mm(ckv, wts["w_ukv"], mode="nn", out_dtype=F32, name="mm_kv")
    q_hm, k_hm, v_hm = _attn_prep(q, kv, z_l, kr_col, cos, sin, n_heads)
    o, lse = _attn_fwd(q_hm, k_hm, v_hm)
    ya = _mm(o, wts["w_attn_out"], mode="nn", out_dtype=F32, name="mm_attn_out")

    mix_ins = [(yc, _rows(d)), (sp["b_conv_out"], _whole(sp["b_conv_out"])), (ya, _rows(d)), (z_gate, _rows(d, 0)),
               (z_gate, _rows(d, 1)), (b_gate_c, _whole(b_gate_c)), (b_gate_a, _whole(b_gate_a))]
    (mix,) = _rowwise(lambda i, *a: _mix(*a), mix_ins, [(SDS((t, d), BF16), _rows(d))], [], name="mix_fwd", n_rows=t)
    h1 = _mm(mix, wts["w_out"], mode="nn", out_dtype=F32, name="mm_out", add=h0)

    hn_ = _rms_fwd(h1, sp["g_ffn"], "rms_ffn")
    wts.update(fetch("ffn", hn_))
    ab = _mm(hn_, wts["w_gu"], mode="nn", out_dtype=F32, name="mm_ffn_in")
    tr_ffn = 64
    (f,) = _rowwise(lambda i, a, b: _swiglu(a, b), [(ab, _rows(dff, 0, tr_ffn)), (ab, _rows(dff, 1, tr_ffn))],
                    [(SDS((t, dff), BF16), _rows(dff, 0, tr_ffn))], [], name="swiglu_fwd", n_rows=t, tr=tr_ffn)
    h2 = _mm(f, wts["w_ffn_down"], mode="nn", out_dtype=F32, name="mm_ffn_down", add=h1)

    def head(i, h, g, tgt):
        y, vjp = jax.vjp(_rms, h, g)
        row = i * ROW_TILE + lax.broadcasted_iota(jnp.int32, (ROW_TILE, 1), 0)
        valid = jnp.logical_and(row >= N_META, row < length)
        err = jnp.where(valid, y - tgt, 0.0)
        dh, dg = vjp(err / d)
        loss = 0.5 * jnp.sum(jnp.sum(err * err, axis=-1, keepdims=True), axis=0, keepdims=True) / d
        return dh, dh, dg, jnp.broadcast_to(loss, (1, LANES))

    dh2, dh2_b, g_final, loss_v = _rowwise(
        head, [(h2, _rows(d)), (sp["g_final"], _whole(sp["g_final"])), (target_p, _rows(d))],
        [(SDS((t, d), F32), _rows(d)), (SDS((t, d), BF16), _rows(d))], [SDS((1, d), F32), SDS((1, LANES), F32)],
        name="loss_head", n_rows=t)
    loss = loss_v[0, 0]

    g_ffn_down = _mm(f, dh2_b, mode="tn", out_dtype=BF16, name="mm_g_ffn_down")
    df = _mm(dh2_b, wts["w_ffn_down"], mode="nt", out_dtype=F32, name="mm_d_f")

    def swiglu_bwd(i, a, b, df):
        _, vjp = jax.vjp(_swiglu, a, b)
        da, db = vjp(df)
        return jnp.concatenate([da, db], axis=-1)

    (dab,) = _rowwise(swiglu_bwd, [(ab, _rows(dff, 0, tr_ffn)), (ab, _rows(dff, 1, tr_ffn)), (df, _rows(dff, 0, tr_ffn))],
                      [(SDS((t, 2 * dff), BF16), _rows(2 * dff, 0, tr_ffn))], [], name="swiglu_bwd", n_rows=t, tr=tr_ffn)
    g_gu = _mm(hn_, dab, mode="tn", out_dtype=BF16, name="mm_g_ffn_in")
    dhn = _mm(dab, wts["w_gu"], mode="nt", out_dtype=F32, name="mm_d_hn")
    sent = ship("ffn", {"w_gu": g_gu, "w_ffn_down": g_ffn_down})
    dh1, dh1_b, g_g_ffn = _rms_bwd(h1, sp["g_ffn"] + sent[0, 0], dhn, dh2, "rms_ffn_bwd")

    g_w_out = _mm(mix, dh1_b, mode="tn", out_dtype=BF16, name="mm_g_out")
    dmix = _mm(dh1_b, wts["w_out"], mode="nt", out_dtype=F32, name="mm_d_mix")

    def mix_bwd(i, yc, bco, ya, zc, za, bgc, bga, dmix):
        _, vjp = jax.vjp(_mix, yc, bco, ya, zc, za, bgc, bga)
        dyc, dbco, dya, dzc, dza, dbgc, dbga = vjp(dmix)
        return dyc, dya, jnp.concatenate([dzc, dza], axis=-1), dbco, dbgc, dbga

    dyc, dya, dz_gate, g_b_conv_out, g_bgc, g_bga = _rowwise(
        mix_bwd, mix_ins + [(dmix, _rows(d))],
        [(SDS((t, d), BF16), _rows(d)), (SDS((t, d), BF16), _rows(d)), (SDS((t, 2 * d), BF16), _rows(2 * d))],
        [SDS((1, d), F32)] * 3, name="mix_bwd", n_rows=t)

    g_attn_out = _mm(o, dya, mode="tn", out_dtype=BF16, name="mm_g_attn_out")
    d_o = _mm(dya, wts["w_attn_out"], mode="nt", out_dtype=BF16, name="mm_d_o")
    dq_hm, dk_hm, dv_hm = _attn_bwd(q_hm, k_hm, v_hm, lse, d_o)
    dq, dkv, dkr = _attn_post(dq_hm, dk_hm, dv_hm, cos, sin)
    g_uq = _mm(cq, dq, mode="tn", out_dtype=BF16, name="mm_g_uq")
    g_ukv = _mm(ckv, dkv, mode="tn", out_dtype=BF16, name="mm_g_ukv")
    dcq = _mm(dq, wts["w_uq"], mode="nt", out_dtype=F32, name="mm_d_cq")
    dckv = _mm(dkv, wts["w_ukv"], mode="nt", out_dtype=F32, name="mm_d_ckv")

    def lora_bwd(i, zq, zk, gq, gk, dcq, dckv, dkr):
        _, vq = jax.vjp(_rms, zq, gq)
        _, vk = jax.vjp(_rms, zk, gk)
        dzq, dgq = vq(dcq)
        dzk, dgk = vk(dckv)
        return jnp.concatenate([dzq, dzk, dkr], axis=-1), dgq, dgk

    dz_l, g_g_q, g_g_kv = _rowwise(
        lora_bwd, lora_ins + [(dcq, _rows(ql)), (dckv, _rows(ql)), (dkr, _rows(LANES))],
        [(SDS((t, 2 * ql + LANES), BF16), _rows(2 * ql + LANES))], [SDS((1, ql), F32)] * 2,
        name="lora_norm_bwd", n_rows=t)

    g_conv_out = _mm(c3, dyc, mode="tn", out_dtype=BF16, name="mm_g_conv_out")
    dc3 = _mm(dyc, wts["w_conv_out"], mode="nt", out_dtype=F32, name="mm_d_c3")
    sent = ship("mixers", {"w_conv_out": g_conv_out, "w_uq": g_uq, "w_ukv": g_ukv, "w_attn_out": g_attn_out,
                           "w_out": g_w_out})

    def ln_bwd(i, c, g, b, dc3):
        _, vjp = jax.vjp(_ln_silu, c, g, b)
        return vjp(dc3)

    g_ln_sent = sp["g_conv_ln"] + sent[0, 0]
    dc1, g_g_ln, g_b_ln = _rowwise(
        ln_bwd, [ln_ins[0], (g_ln_sent, _whole(g_ln_sent)), ln_ins[2], (dc3, _rows(ch))],
        [(SDS((t, ch), F32), _rows(ch))], [SDS((1, ch), F32)] * 2, name="ln_silu_bwd", n_rows=t)
    dc0, g_w_dw, g_b_dw = _conv_bwd(dc1, c0, w_dw)

    def glu_bwd(i, za, zb, ba, bb, dc0):
        _, vjp = jax.vjp(_glu, za, zb, ba, bb)
        dza, dzb, dba, dbb = vjp(dc0)
        return jnp.concatenate([dza, dzb], axis=-1), dba, dbb

    dz_glu, g_bga_, g_bgb_ = _rowwise(glu_bwd, glu_ins + [(dc0, _rows(ch))],
                                      [(SDS((t, 2 * ch), BF16), _rows(2 * ch))], [SDS((1, ch), F32)] * 2,
                                      name="glu_bwd", n_rows=t)

    du = _mm(dz_glu, wts["w_glu"], mode="nt", out_dtype=F32, name="mm_d_u0")
    du = _mm(dz_l, wts["w_lora"], mode="nt", out_dtype=F32, name="mm_d_u1", add=du)
    du = _mm(dz_gate, wts["w_gate"], mode="nt", out_dtype=F32, name="mm_d_u2", add=du)
    dh0, _, g_g_mix = _rms_bwd(h0, sp["g_mix"], du, dh1, "rms_mix_bwd")
    big = {"meta_tokens": dh0[:N_META], "w_dw": g_w_dw}
    small = {
        "g_mix": g_g_mix, "b_glu": jnp.concatenate([g_bga_, g_bgb_], axis=1),
        "b_gate": jnp.concatenate([g_bgc, g_bga], axis=1), "b_dw": g_b_dw, "g_conv_ln": g_g_ln, "b_conv_ln": g_b_ln,
        "b_conv_out": g_b_conv_out, "g_q_lora": g_g_q, "g_kv_lora": g_g_kv, "g_ffn": g_g_ffn, "g_final": g_final,
    }
    sent = ship("small", {**big, **small})
    g_glu = _mm(u, dz_glu, mode="tn", out_dtype=BF16, name="mm_g_w_glu", after=sent)
    g_lora = _mm(u, dz_l, mode="tn", out_dtype=BF16, name="mm_g_w_lora", after=sent)
    g_gate = _mm(u, dz_gate, mode="tn", out_dtype=BF16, name="mm_g_w_gate", after=sent)
    big.update({"w_glu": g_glu, "w_lora": g_lora, "w_gate": g_gate})
    ship("input", big)
    return loss, dh0[N_META:length], big, small


def _slot(ref, dev):
    return ref.at[dev]


def _row_window(rows):
    return lambda ref, dev: ref.at[pl.ds(pl.multiple_of(dev * rows, 16), rows)]


def _col_window(width, offset=0):
    return lambda ref, dev: ref.at[:, pl.ds(pl.multiple_of(offset + dev * width, LANES), width)]


def _dev_index(x, y, c):
    return 4 * x + 2 * y + c


def _gather_weights(items, out_shapes):
    srcs = [it[0] for it in items]
    n, n_out = len(srcs), len(out_shapes)

    def body(*refs):
        src, out = refs[:n], refs[n:n + n_out]
        send_sems, recv_sems, local_sems = refs[n + n_out:]
        x, y, c = lax.axis_index("x"), lax.axis_index("y"), lax.axis_index("c")
        me, sibling = (x, y, c), (x, y, 1 - c)
        chips = [(1 - x, y), (x, 1 - y), (1 - x, 1 - y)]

        def place(i, block):
            _, o, window = items[i]
            return window(out[o], _dev_index(*block))

        def copy(k, i, block, to, from_src=False):
            return pltpu.make_async_remote_copy(
                src_ref=src[i] if from_src else place(i, block), dst_ref=place(i, block),
                send_sem=send_sems.at[k * n + i], recv_sem=recv_sems.at[k * n + i],
                device_id=to, device_id_type=pl.DeviceIdType.MESH)

        mine = [pltpu.make_async_copy(src[i], place(i, me), local_sems.at[i]) for i in range(n)]
        first = [copy(0, i, me, sibling, True) for i in range(n)]
        first += [copy(1 + j, i, me, (*chip, c), True) for j, chip in enumerate(chips) for i in range(n)]
        for cp in mine + first:
            cp.start()
        passed = [[copy(4 + j, i, (*chip, c), sibling) for i in range(n)] for j, chip in enumerate(chips)]
        for j, chip in enumerate(chips):
            for i in range(n):
                copy(1 + j, i, (*chip, c), me).wait_recv()
            for cp in passed[j]:
                cp.start()
        for i in range(n):
            copy(0, i, sibling, me).wait_recv()
        for j, chip in enumerate(chips):
            for i in range(n):
                copy(4 + j, i, (*chip, 1 - c), me).wait_recv()
        for cp in first + [cp for row in passed for cp in row]:
            cp.wait_send()
        for cp in mine:
            cp.wait()

    any_spec = pl.BlockSpec(memory_space=pl.ANY)
    return pl.pallas_call(
        body, name="gather_weights", in_specs=[any_spec] * n, out_specs=[any_spec] * n_out, out_shape=out_shapes,
        scratch_shapes=[pltpu.SemaphoreType.DMA((7 * n,)), pltpu.SemaphoreType.DMA((7 * n,)),
                        pltpu.SemaphoreType.DMA((n,))],
    )(*srcs)


def _exchange(srcs, out_shapes, items, *, masks, name):
    n_src, n_out, n = len(srcs), len(out_shapes), len(items)

    def body(*refs):
        src, out = refs[:n_src], refs[n_src:n_src + n_out]
        send_sems, recv_sems = refs[n_src + n_out:]
        remote = _peer_copies(src, out, send_sems, recv_sems, items, masks)
        for cp in remote:
            cp.start()
        for cp in remote:
            cp.wait()

    any_spec = pl.BlockSpec(memory_space=pl.ANY)
    return pl.pallas_call(
        body, name=name, in_specs=[any_spec] * n_src, out_specs=[any_spec] * n_out, out_shape=out_shapes,
        scratch_shapes=[pltpu.SemaphoreType.DMA((len(masks) * n,)), pltpu.SemaphoreType.DMA((len(masks) * n,))],
    )(*srcs)


def _peer_copies(src, out, send_sems, recv_sems, items, masks):
    x, y, c = lax.axis_index("x"), lax.axis_index("y"), lax.axis_index("c")
    me = _dev_index(x, y, c)
    n = len(items)
    copies = []
    for k, mask in enumerate(masks):
        px, py, pc = x ^ ((mask >> 2) & 1), y ^ ((mask >> 1) & 1), c ^ (mask & 1)
        peer = _dev_index(px, py, pc)
        for i, (s, s_win, o, d_win) in enumerate(items):
            copies.append(pltpu.make_async_remote_copy(
                src_ref=s_win(src[s], peer), dst_ref=d_win(out[o], me),
                send_sem=send_sems.at[k * n + i], recv_sem=recv_sems.at[k * n + i],
                device_id=(px, py, pc), device_id_type=pl.DeviceIdType.MESH))
    return copies


def _exchange_start(srcs, out_shapes, items, *, masks, name):
    n_src, n_out, n = len(srcs), len(out_shapes), len(items)
    n_sem = len(masks) * n
    n_buf = n_src + n_out

    def body(*refs):
        src, land = refs[:n_src], refs[n_src:n_buf]
        send_sems, recv_sems = refs[n_buf], refs[n_buf + 1]
        token = refs[-1]
        for cp in _peer_copies(src, land, send_sems, recv_sems, items, masks):
            cp.start()
        token[...] = jnp.zeros_like(token)

    hbm = pl.BlockSpec(memory_space=pltpu.HBM)
    sem = pl.BlockSpec(memory_space=pltpu.SEMAPHORE)
    bufs = [pltpu.with_memory_space_constraint(a, pltpu.HBM) for a in srcs]
    bufs += [pltpu.with_memory_space_constraint(lax.empty(s.shape, s.dtype), pltpu.HBM) for s in out_shapes]
    res = pl.pallas_call(
        body, name=name,
        out_shape=(pltpu.SemaphoreType.DMA((n_sem,)), pltpu.SemaphoreType.DMA((n_sem,)),
                   *[pltpu.HBM(b.shape, b.dtype) for b in bufs], SDS((8, LANES), F32)),
        in_specs=[hbm] * n_buf, out_specs=(sem, sem, *[hbm] * n_buf, pl.BlockSpec(memory_space=pltpu.VMEM)),
        input_output_aliases={i: 2 + i for i in range(n_buf)},
        compiler_params=pltpu.CompilerParams(has_side_effects=pltpu.SideEffectType.DATAFLOW_SIDE_EFFECTING),
    )(*bufs)
    return res[0], res[1], list(res[2:2 + n_src]), list(res[2 + n_src:2 + n_buf]), res[-1]


def _exchange_wait(send_sems, recv_sems, srcs, lands, items, after, *, masks, name):
    n_src, n_out = len(srcs), len(lands)
    n_buf = n_src + n_out

    def body(*refs):
        src, land = refs[:n_src], refs[n_src:n_buf]
        send_sems, recv_sems = refs[n_buf], refs[n_buf + 1]
        for cp in _peer_copies(src, land, send_sems, recv_sems, items, masks):
            cp.wait_send()
            cp.wait_recv()

    hbm = pl.BlockSpec(memory_space=pltpu.HBM)
    sem = pl.BlockSpec(memory_space=pltpu.SEMAPHORE)
    bufs = list(srcs) + list(lands)
    res = pl.pallas_call(
        body, name=name, out_shape=tuple(pltpu.HBM(b.shape, b.dtype) for b in bufs),
        in_specs=[hbm] * n_buf + [sem, sem, pl.BlockSpec(memory_space=pl.ANY)], out_specs=tuple([hbm] * n_buf),
        input_output_aliases={i: i for i in range(n_buf)},
        compiler_params=pltpu.CompilerParams(has_side_effects=pltpu.SideEffectType.DATAFLOW_SIDE_EFFECTING),
    )(*bufs, send_sems, recv_sems, after)
    return list(res[:n_src]), list(res[n_src:])


def _regroup(srcs, dsts, *, name, tr=256, after=()):
    def segments(shape, valid):
        if len(shape) == 3:
            return [(p, shape[2]) for p in range(shape[0])]
        return [(None, valid)]

    src_arrays = [s[0] if isinstance(s, tuple) else s for s in srcs]
    src_valid = [s[1] if isinstance(s, tuple) else s.shape[-1] for s in srcs]
    k_rows = src_arrays[0].shape[-2]
    src_segs = [(i, p, w) for i, a in enumerate(src_arrays) for p, w in segments(a.shape, src_valid[i])]
    dst_segs = [(j, p, w) for j, (shape, _, valid) in enumerate(dsts) for p, w in segments(shape, valid)]
    pieces = []
    si, so, di, do = 0, 0, 0, 0
    while si < len(src_segs) and di < len(dst_segs):
        n = min(src_segs[si][2] - so, dst_segs[di][2] - do)
        pieces.append((src_segs[si][0], src_segs[si][1], so, dst_segs[di][0], dst_segs[di][1], do, n))
        so, do = so + n, do + n
        if so == src_segs[si][2]:
            si, so = si + 1, 0
        if do == dst_segs[di][2]:
            di, do = di + 1, 0
    assert si == len(src_segs) and di == len(dst_segs), "source and destination columns differ in number"
    n_src = len(src_arrays)

    def body(*refs):
        src, dst = refs[:n_src], refs[n_src + len(after):]
        for j, (shape, dtype, valid) in enumerate(dsts):
            if len(shape) == 2 and valid < shape[1]:
                dst[j][:, valid:shape[1]] = jnp.zeros((tr, shape[1] - valid), dtype)
        for i, sp, so, j, dp, do, n in pieces:
            val = src[i][:, so:so + n] if sp is None else src[i][sp, :, so:so + n]
            if dp is None:
                dst[j][:, do:do + n] = val.astype(dst[j].dtype)
            else:
                dst[j][dp, :, do:do + n] = val.astype(dst[j].dtype)

    def spec(shape):
        if len(shape) == 3:
            return pl.BlockSpec((shape[0], tr, shape[2]), lambda i: (0, i, 0))
        return pl.BlockSpec((tr, shape[1]), lambda i: (i, 0))

    n_after = len(after)
    return pl.pallas_call(
        body, name=name, grid=(k_rows // tr,),
        in_specs=[spec(a.shape) for a in src_arrays] + [pl.BlockSpec(memory_space=pl.ANY)] * n_after,
        out_specs=[spec(shape) for shape, _, _ in dsts], out_shape=[SDS(shape, dtype) for shape, dtype, _ in dsts],
        compiler_params=_params(),
    )(*src_arrays, *after)


def _cast_bf16(a, name, after=None):
    r, c = a.shape
    tr = _row_tile(r, c * 4)
    (out,) = _rowwise(lambda i, v: v, [(a, _rows(c, 0, tr))], [(SDS((r, c), BF16), _rows(c, 0, tr))], [], name=name,
                      n_rows=r, tr=tr, after=after)
    return out


def _own_spec(kind, tr, width, n_tiles):
    if kind[0] == "slot":
        return pl.BlockSpec((None, tr, width), lambda i, me: (me[0], i, 0))
    if kind[0] == "rows":
        return pl.BlockSpec((tr, width), lambda i, me: (me[0] * n_tiles + i, 0))
    if kind[0] == "cols":
        return pl.BlockSpec((tr, width), lambda i, me: (i, kind[1] + me[0]))
    return pl.BlockSpec((tr, width), lambda i, me: (i, 0))


def _cast_into(me, a, whole, kind, name, into=None, after=None):
    r, c = a.shape
    tr = _row_tile(r, c * 4)
    extra = [x for x in (into, after) if x is not None]

    def body(me_ref, a_ref, *rest):
        rest[len(extra)][...] = a_ref[...].astype(BF16)

    return pl.pallas_call(
        body, name=name,
        grid_spec=pltpu.PrefetchScalarGridSpec(
            num_scalar_prefetch=1, grid=(r // tr,),
            in_specs=[pl.BlockSpec((tr, c), lambda i, me: (i, 0))] + [pl.BlockSpec(memory_space=pl.ANY)] * len(extra),
            out_specs=_own_spec(kind, tr, c, r // tr)),
        out_shape=whole, input_output_aliases={2: 0} if into is not None else {}, compiler_params=_params(),
    )(me, a, *extra)


def _own_block_copies(land, send_sems, recv_sems, items, masks):
    x, y, c = lax.axis_index("x"), lax.axis_index("y"), lax.axis_index("c")
    me = _dev_index(x, y, c)
    n = len(items)
    copies = []
    for k, mask in enumerate(masks):
        px, py, pc = x ^ ((mask >> 2) & 1), y ^ ((mask >> 1) & 1), c ^ (mask & 1)
        for i, (o, win) in enumerate(items):
            copies.append(pltpu.make_async_remote_copy(
                src_ref=win(land[o], me), dst_ref=win(land[o], me),
                send_sem=send_sems.at[k * n + i], recv_sem=recv_sems.at[k * n + i],
                device_id=(px, py, pc), device_id_type=pl.DeviceIdType.MESH))
    return copies


def _spread_start(lands, items, *, masks, name):
    n_buf, n_sem = len(lands), len(masks) * len(items)

    def body(*refs):
        land, send_sems, recv_sems, token = refs[:n_buf], refs[n_buf], refs[n_buf + 1], refs[-1]
        for cp in _own_block_copies(land, send_sems, recv_sems, items, masks):
            cp.start()
        token[...] = jnp.zeros_like(token)

    hbm = pl.BlockSpec(memory_space=pltpu.HBM)
    sem = pl.BlockSpec(memory_space=pltpu.SEMAPHORE)
    bufs = [pltpu.with_memory_space_constraint(a, pltpu.HBM) for a in lands]
    res = pl.pallas_call(
        body, name=name,
        out_shape=(pltpu.SemaphoreType.DMA((n_sem,)), pltpu.SemaphoreType.DMA((n_sem,)),
                   *[pltpu.HBM(b.shape, b.dtype) for b in bufs], SDS((8, LANES), F32)),
        in_specs=[hbm] * n_buf, out_specs=(sem, sem, *[hbm] * n_buf, pl.BlockSpec(memory_space=pltpu.VMEM)),
        input_output_aliases={i: 2 + i for i in range(n_buf)},
        compiler_params=pltpu.CompilerParams(has_side_effects=pltpu.SideEffectType.DATAFLOW_SIDE_EFFECTING),
    )(*bufs)
    return res[0], res[1], list(res[2:2 + n_buf]), res[-1]


def _swap_with_sibling(lands, items, name):
    n_buf, n = len(lands), len(items)

    def body(*refs):
        land, send_sems, recv_sems = refs[n_buf:2 * n_buf], refs[2 * n_buf], refs[2 * n_buf + 1]
        x, y, c = lax.axis_index("x"), lax.axis_index("y"), lax.axis_index("c")
        copies = []
        for j, (px, py) in enumerate([(0, 0), (0, 1), (1, 0), (1, 1)]):
            block = _dev_index(px, py, c)
            for i, (o, win) in enumerate(items):
                copies.append(pltpu.make_async_remote_copy(
                    src_ref=win(land[o], block), dst_ref=win(land[o], block),
                    send_sem=send_sems.at[j * n + i], recv_sem=recv_sems.at[j * n + i],
                    device_id=(x, y, 1 - c), device_id_type=pl.DeviceIdType.MESH))
        for cp in copies:
            cp.start()
        for cp in copies:
            cp.wait()

    any_spec = pl.BlockSpec(memory_space=pl.ANY)
    return pl.pallas_call(
        body, name=name, in_specs=[any_spec] * n_buf, out_specs=[any_spec] * n_buf,
        out_shape=[SDS(a.shape, a.dtype) for a in lands], input_output_aliases={i: i for i in range(n_buf)},
        scratch_shapes=[pltpu.SemaphoreType.DMA((4 * n,)), pltpu.SemaphoreType.DMA((4 * n,))],
    )(*lands)


def _spread_wait(send_sems, recv_sems, lands, items, after, *, masks, name):
    n_buf = len(lands)

    def body(*refs):
        land, send_sems, recv_sems = refs[:n_buf], refs[n_buf], refs[n_buf + 1]
        for cp in _own_block_copies(land, send_sems, recv_sems, items, masks):
            cp.wait_send()
            cp.wait_recv()

    hbm = pl.BlockSpec(memory_space=pltpu.HBM)
    sem = pl.BlockSpec(memory_space=pltpu.SEMAPHORE)
    res = pl.pallas_call(
        body, name=name, out_shape=tuple(pltpu.HBM(b.shape, b.dtype) for b in lands),
        in_specs=[hbm] * n_buf + [sem, sem, pl.BlockSpec(memory_space=pl.ANY)], out_specs=tuple([hbm] * n_buf),
        input_output_aliases={i: i for i in range(n_buf)},
        compiler_params=pltpu.CompilerParams(has_side_effects=pltpu.SideEffectType.DATAFLOW_SIDE_EFFECTING),
    )(*lands, send_sems, recv_sems, after)
    return list(res)


def _row_tile(rows, row_bytes, limit=2 * 1024 * 1024):
    best = None
    for t in range(16, rows + 1, 16):
        if rows % t == 0 and t * row_bytes <= limit:
            best = t
    return best if best is not None else rows


def _adamw(me, recv, own, own_kind, w, m, v, *, name):
    n_rows, width = w.shape
    tr = _row_tile(n_rows, width * 4, limit=1024 * 1024)
    n_tiles = n_rows // tr
    own_spec = _own_spec(own_kind, tr, width, n_tiles)

    def body(me_ref, r_ref, own_ref, w_ref, m_ref, v_ref, g_ref, d_ref, mo_ref, vo_ref):
        mine = own_ref[...].astype(F32)
        g = None
        for q in range(N_DEV):
            term = jnp.where(me_ref[0] == q, mine, r_ref[q].astype(F32))
            g = term if g is None else g + term
        m_new = ADAM_B1 * m_ref[...] + (1.0 - ADAM_B1) * g
        v_new = ADAM_B2 * v_ref[...] + (1.0 - ADAM_B2) * jnp.square(g)
        m_hat = m_new / (1.0 - ADAM_B1 ** ADAM_STEP)
        v_hat = v_new / (1.0 - ADAM_B2 ** ADAM_STEP)
        g_ref[...] = g
        d_ref[...] = -ADAM_LR * (m_hat / (jnp.sqrt(v_hat) + ADAM_EPS) + ADAM_WD * w_ref[...])
        mo_ref[...] = m_new
        vo_ref[...] = v_new

    row = pl.BlockSpec((tr, width), lambda i, me: (i, 0))
    return pl.pallas_call(
        body, name=name,
        grid_spec=pltpu.PrefetchScalarGridSpec(
            num_scalar_prefetch=1, grid=(n_tiles,),
            in_specs=[pl.BlockSpec((N_DEV, tr, width), lambda i, me: (0, i, 0)), own_spec, row, row, row],
            out_specs=[row] * 4),
        out_shape=[SDS((n_rows, width), F32)] * 4, compiler_params=_params(),
    )(me, recv, own, w, m, v)


def _offsets(sizes):
    offs, o = [], 0
    for n in sizes:
        offs.append(o)
        o += n
    return offs, o


def kernel(x, meta_tokens, g_mix, w_in, b_glu, b_gate, w_dw, b_dw, g_conv_ln, b_conv_ln, w_conv_out, b_conv_out, g_q_lora, w_uq, g_kv_lora, w_uk, w_uv, w_attn_out, w_out, g_ffn, w_ffn_gate, w_ffn_up, w_ffn_down, g_final, loss_target, m_meta_tokens, m_g_mix, m_w_in, m_b_glu, m_b_gate, m_w_dw, m_b_dw, m_g_conv_ln, m_b_conv_ln, m_w_conv_out, m_b_conv_out, m_g_q_lora, m_w_uq, m_g_kv_lora, m_w_uk, m_w_uv, m_w_attn_out, m_w_out, m_g_ffn, m_w_ffn_gate, m_w_ffn_up, m_w_ffn_down, m_g_final, v_meta_tokens, v_g_mix, v_w_in, v_b_glu, v_b_gate, v_w_dw, v_b_dw, v_g_conv_ln, v_b_conv_ln, v_w_conv_out, v_b_conv_out, v_g_q_lora, v_w_uq, v_g_kv_lora, v_w_uk, v_w_uv, v_w_attn_out, v_w_out, v_g_ffn, v_w_ffn_gate, v_w_ffn_up, v_w_ffn_down, v_g_final):
    w_all = dict(meta_tokens=meta_tokens, g_mix=g_mix, w_in=w_in, b_glu=b_glu, b_gate=b_gate, w_dw=w_dw, b_dw=b_dw, g_conv_ln=g_conv_ln, b_conv_ln=b_conv_ln, w_conv_out=w_conv_out, b_conv_out=b_conv_out, g_q_lora=g_q_lora, w_uq=w_uq, g_kv_lora=g_kv_lora, w_uk=w_uk, w_uv=w_uv, w_attn_out=w_attn_out, w_out=w_out, g_ffn=g_ffn, w_ffn_gate=w_ffn_gate, w_ffn_up=w_ffn_up, w_ffn_down=w_ffn_down, g_final=g_final)
    m_all = dict(meta_tokens=m_meta_tokens, g_mix=m_g_mix, w_in=m_w_in, b_glu=m_b_glu, b_gate=m_b_gate, w_dw=m_w_dw, b_dw=m_b_dw, g_conv_ln=m_g_conv_ln, b_conv_ln=m_b_conv_ln, w_conv_out=m_w_conv_out, b_conv_out=m_b_conv_out, g_q_lora=m_g_q_lora, w_uq=m_w_uq, g_kv_lora=m_g_kv_lora, w_uk=m_w_uk, w_uv=m_w_uv, w_attn_out=m_w_attn_out, w_out=m_w_out, g_ffn=m_g_ffn, w_ffn_gate=m_w_ffn_gate, w_ffn_up=m_w_ffn_up, w_ffn_down=m_w_ffn_down, g_final=m_g_final)
    v_all = dict(meta_tokens=v_meta_tokens, g_mix=v_g_mix, w_in=v_w_in, b_glu=v_b_glu, b_gate=v_b_gate, w_dw=v_w_dw, b_dw=v_b_dw, g_conv_ln=v_g_conv_ln, b_conv_ln=v_b_conv_ln, w_conv_out=v_w_conv_out, b_conv_out=v_b_conv_out, g_q_lora=v_g_q_lora, w_uq=v_w_uq, g_kv_lora=v_g_kv_lora, w_uk=v_w_uk, w_uv=v_w_uv, w_attn_out=v_w_attn_out, w_out=v_w_out, g_ffn=v_g_ffn, w_ffn_gate=v_w_ffn_gate, w_ffn_up=v_w_ffn_up, w_ffn_down=v_w_ffn_down, g_final=v_g_final)

    two_d = lambda a: a.reshape(a.shape[-2:]) if a.ndim >= 2 else a.reshape(1, -1)
    sh = {n: two_d(w_all[n]) for n in SHARDED}
    d = x.shape[-1]
    k_in, c_in = sh["w_in"].shape
    r_co, r_ao, r_wo, r_fd = (sh[n].shape[0] for n in ROW_SHARDED)
    ql, c_uq = sh["w_uq"].shape
    c_uk = sh["w_uk"].shape[1]
    c_ff = sh["w_ffn_gate"].shape[1]
    n_meta, c_meta = sh["meta_tokens"].shape
    n_taps, c_dw = sh["w_dw"].shape
    ch, dff = N_DEV * c_dw, N_DEV * c_ff
    n_lora = 2 * ql + QK_ROPE

    masks = tuple(range(1, N_DEV))
    whole_ref = lambda ref, dev: ref
    gathered = _gather_weights(
        [(_cast_bf16(sh["w_in"], "cast_w_in"), 0, _slot), (sh["meta_tokens"], 1, _col_window(c_meta)),
         (sh["w_dw"], 2, _col_window(c_dw))],
        [SDS((N_DEV, k_in, c_in), BF16), SDS((n_meta, N_DEV * c_meta), F32), SDS((n_taps, ch), F32)])
    me = (4 * lax.axis_index("x") + 2 * lax.axis_index("y") + lax.axis_index("c")).astype(jnp.int32).reshape(1)

    def placed(n, whole, kind, after, into=None):
        return _cast_into(me, sh[n], whole, kind, "cast_" + n, into=into, after=None if into is not None else after)

    same_core = (2, 4, 6)
    coming = {}
    first = gathered[0]
    w_ukv0 = placed("w_uk", SDS((ql, 2 * N_DEV * c_uk), BF16), ("cols", 0), first)
    coming["mixers"] = _spread_start(
        [placed("w_conv_out", SDS((N_DEV * r_co, d), BF16), ("rows",), first),
         placed("w_uq", SDS((ql, N_DEV * c_uq), BF16), ("cols", 0), first),
         placed("w_uv", SDS((ql, 2 * N_DEV * c_uk), BF16), ("cols", N_DEV), first, into=w_ukv0),
         placed("w_attn_out", SDS((N_DEV * r_ao, d), BF16), ("rows",), first),
         placed("w_out", SDS((N_DEV * r_wo, d), BF16), ("rows",), first)],
        mixer_items := [(0, _row_window(r_co)), (1, _col_window(c_uq)), (2, _col_window(c_uk)),
                        (2, _col_window(c_uk, N_DEV * c_uk)), (3, _row_window(r_ao)), (4, _row_window(r_wo))],
        masks=same_core, name="send_weights_mixers") + (mixer_items,)
    second = coming["mixers"][3]
    coming["ffn"] = _spread_start(
        [placed("w_ffn_gate", SDS((N_DEV, d, c_ff), BF16), ("slot",), second),
         placed("w_ffn_up", SDS((N_DEV, d, c_ff), BF16), ("slot",), second),
         placed("w_ffn_down", SDS((N_DEV * r_fd, d), BF16), ("rows",), second)],
        ffn_items := [(0, _slot), (1, _slot), (2, _row_window(r_fd))],
        masks=same_core, name="send_weights_ffn") + (ffn_items,)

    def fetch(group, after):
        send_sems, recv_sems, lands, _, items = coming[group]
        lands = _spread_wait(send_sems, recv_sems, lands, items, after, masks=same_core,
                             name="wait_weights_" + group)
        lands = _swap_with_sibling(lands, items, "pass_weights_" + group)
        if group == "mixers":
            return {"w_conv_out": lands[0], "w_uq": lands[1], "w_ukv": lands[2], "w_attn_out": lands[3],
                    "w_out": lands[4]}
        (w_gu,) = _regroup([lands[0], lands[1]], [((d, 2 * dff), BF16, 2 * dff)], name="unpack_w_ffn_in")
        return {"w_gu": w_gu, "w_ffn_down": lands[2]}

    w_glu, w_lora, w_gate = _regroup(
        [gathered[0]], [((k_in, 2 * ch), BF16, 2 * ch), ((k_in, n_lora + LANES - QK_ROPE), BF16, n_lora),
                        ((k_in, 2 * d), BF16, 2 * d)], name="unpack_w_in",
        after=(coming["mixers"][3], coming["ffn"][3]))
    wts = {"w_glu": w_glu, "w_lora": w_lora, "w_gate": w_gate}
    sp = {n: w_all[n].reshape(1, -1) for n in SMALL}
    dims = {"n_heads": N_DEV * c_uq // QK, "dff": dff}

    recv_shape = lambda n: SDS((N_DEV,) + sh[n].shape, F32 if n in F32_GATHERED else BF16)
    in_flight = {}

    s_sizes = [w_all[n].size for n in SMALL]
    s_offs, n_s = _offsets(s_sizes)
    cat_small = lambda src: jnp.concatenate([src[n].reshape(1, -1) for n in SMALL], axis=1)
    small_names = ("meta_tokens", "w_dw")

    def ship(group, grads):
        if group == "small":
            srcs = [grads["meta_tokens"], grads["w_dw"], cat_small(grads)]
            recv = _exchange(
                srcs, [recv_shape(n) for n in small_names] + [SDS((N_DEV, 1, n_s), F32)],
                [(0, _col_window(c_meta), 0, _slot), (1, _col_window(c_dw), 1, _slot), (2, whole_ref, 2, _slot)],
                masks=masks, name="exchange_grads_small")
            in_flight[group] = (srcs, recv)
            return recv[2]
        if group == "input":
            (s_in,) = _regroup([grads["w_glu"], (grads["w_lora"], n_lora), grads["w_gate"]],
                               [((N_DEV, k_in, c_in), BF16, None)], name="pack_g_w_in")
            srcs, names, items, own = [s_in], ("w_in",), [(0, _slot, 0, _slot)], [(0, ("slot",))]
        elif group == "ffn":
            s_gate, s_up = _regroup([grads["w_gu"]], [((N_DEV, d, c_ff), BF16, None)] * 2, name="pack_g_w_ffn_in")
            srcs = [s_gate, s_up, grads["w_ffn_down"]]
            names = ("w_ffn_gate", "w_ffn_up", "w_ffn_down")
            items = [(0, _slot, 0, _slot), (1, _slot, 1, _slot), (2, _row_window(r_fd), 2, _slot)]
            own = [(0, ("slot",)), (1, ("slot",)), (2, ("rows",))]
        else:
            srcs = [grads["w_conv_out"], grads["w_uq"], grads["w_ukv"], grads["w_attn_out"], grads["w_out"]]
            names = ("w_conv_out", "w_uq", "w_uk", "w_uv", "w_attn_out", "w_out")
            items = [(0, _row_window(r_co), 0, _slot), (1, _col_window(c_uq), 1, _slot),
                     (2, _col_window(c_uk), 2, _slot), (2, _col_window(c_uk, N_DEV * c_uk), 3, _slot),
                     (3, _row_window(r_ao), 4, _slot), (4, _row_window(r_wo), 5, _slot)]
            own = [(0, ("rows",)), (1, ("cols", 0)), (2, ("cols", 0)), (2, ("cols", N_DEV)), (3, ("rows",)),
                   (4, ("rows",))]
        send_sems, recv_sems, srcs, lands, zero = _exchange_start(
            srcs, [recv_shape(n) for n in names], items, masks=masks, name="send_grads_" + group)
        in_flight[group] = (send_sems, recv_sems, srcs, lands, items, names, own, zero)
        return zero

    loss, grad_x, _, _ = _local_step(x[0], loss_target[0], gathered[1], gathered[2], wts, sp, ship, fetch, dims)

    by_name = {}

    def update(n, recv, own, kind):
        outs = _adamw(me, recv, own, kind, sh[n], two_d(m_all[n]), two_d(v_all[n]), name="adamw_" + n)
        by_name[n] = [o.reshape(w_all[n].shape) for o in outs]
        return outs[0]

    small_srcs, small_recv = in_flight["small"]
    for n, recv, src in zip(small_names, small_recv, small_srcs):
        update(n, recv, src, ("cols", 0))
    outs = _adamw(me, small_recv[2], small_srcs[2], ("whole",), cat_small(w_all), cat_small(m_all), cat_small(v_all),
                  name="adamw_replicated")
    for n, o, s in zip(SMALL, s_offs, s_sizes):
        by_name[n] = [out[:, o:o + s].reshape(w_all[n].shape) for out in outs]
    done = in_flight["input"][-1]
    for group in ("ffn", "mixers", "input"):
        send_sems, recv_sems, srcs, lands, items, names, own, _ = in_flight[group]
        srcs, lands = _exchange_wait(send_sems, recv_sems, srcs, lands, items, done, masks=masks,
                                     name="wait_grads_" + group)
        for n, land, (s, kind) in zip(names, lands, own):
            done = update(n, land, srcs[s], kind)
    result = [[by_name[n][k] for n in WEIGHTS] for k in range(4)]
    loss = lax.psum(loss, ("x", "y", "c"))
    return (loss, grad_x[None], *result[0], *result[1], *result[2], *result[3])
```

```python
import functools

import jax
import jax.numpy as jnp
from jax import lax
from jax.experimental import pallas as pl
from jax.experimental.pallas import tpu as pltpu

F32, BF16 = jnp.float32, jnp.bfloat16
SDS = jax.ShapeDtypeStruct

N_DEV = 8
N_META = 16
BLOCK_Q = 128
CONV_WIDTH = 31
QK_NOPE, QK_ROPE, V_HEAD = 128, 64, 128
QK = QK_NOPE + QK_ROPE
ROPE_THETA = 10000.0
EPS = 1e-6
ADAM_LR, ADAM_B1, ADAM_B2, ADAM_EPS, ADAM_WD, ADAM_STEP = 0.001, 0.9, 0.999, 1e-08, 0.01, 10

LANES = 128
ROW_TILE = 128
PACK_W = 1024
VMEM_LIMIT = 56 * 1024 * 1024

BIG = ("w_in", "w_conv_out", "w_uq", "w_uk", "w_uv", "w_attn_out", "w_out", "w_ffn_gate", "w_ffn_up", "w_ffn_down")
F32_GATHERED = ("meta_tokens", "w_dw")
SHARDED = BIG + F32_GATHERED
ROW_SHARDED = ("w_conv_out", "w_attn_out", "w_out", "w_ffn_down")
SMALL = ("g_mix", "b_glu", "b_gate", "b_dw", "g_conv_ln", "b_conv_ln", "b_conv_out", "g_q_lora", "g_kv_lora",
         "g_ffn", "g_final")
WEIGHTS = ("meta_tokens", "g_mix", "w_in", "b_glu", "b_gate", "w_dw", "b_dw", "g_conv_ln", "b_conv_ln", "w_conv_out",
           "b_conv_out", "g_q_lora", "w_uq", "g_kv_lora", "w_uk", "w_uv", "w_attn_out", "w_out", "g_ffn", "w_ffn_gate",
           "w_ffn_up", "w_ffn_down", "g_final")


def _params():
    return pltpu.CompilerParams(vmem_limit_bytes=VMEM_LIMIT)


def _tile(dim, limit):
    best = None
    t = LANES
    while t <= min(dim, limit):
        if dim % t == 0:
            best = t
        t += LANES
    return best if best is not None else dim


def _mm(a, b, *, mode, out_dtype, name, add=None, after=None):
    if mode == "nn":
        (m, kc), n = a.shape, b.shape[1]
    elif mode == "nt":
        (m, kc), n = a.shape, b.shape[0]
    else:
        (kc, m), n = a.shape, b.shape[1]
    if mode == "tn":
        tm, tn, tk = _tile(m, 1024), _tile(n, 512), kc
    else:
        tm, tn, tk = m, _tile(n, 512), _tile(kc, 512)
    nk = kc // tk
    if mode == "nn":
        a_spec = pl.BlockSpec((tm, tk), lambda i, j, k: (i, k))
        b_spec = pl.BlockSpec((tk, tn), lambda i, j, k: (k, j))
        dims = (((1,), (0,)), ((), ()))
    elif mode == "nt":
        a_spec = pl.BlockSpec((tm, tk), lambda i, j, k: (i, k))
        b_spec = pl.BlockSpec((tn, tk), lambda i, j, k: (j, k))
        dims = (((1,), (1,)), ((), ()))
    else:
        a_spec = pl.BlockSpec((tk, tm), lambda i, j, k: (k, i))
        b_spec = pl.BlockSpec((tk, tn), lambda i, j, k: (k, j))
        dims = (((0,), (0,)), ((), ()))
    o_spec = pl.BlockSpec((tm, tn), lambda i, j, k: (i, j))
    has_add = add is not None

    def body(*refs):
        if after is not None:
            refs = refs[:-3] + refs[-2:]
        if has_add:
            a_ref, b_ref, add_ref, o_ref, acc_ref = refs
        else:
            a_ref, b_ref, o_ref, acc_ref = refs
        k = pl.program_id(2)
        p = lax.dot_general(a_ref[...], b_ref[...], dims, preferred_element_type=F32)
        if nk == 1:
            o_ref[...] = ((p + add_ref[...]) if has_add else p).astype(o_ref.dtype)
            return

        @pl.when(k == 0)
        def _():
            acc_ref[...] = (p + add_ref[...]) if has_add else p

        @pl.when(jnp.logical_and(k > 0, k < nk - 1))
        def _():
            acc_ref[...] += p

        @pl.when(k == nk - 1)
        def _():
            o_ref[...] = (acc_ref[...] + p).astype(o_ref.dtype)

    in_specs = [a_spec, b_spec] + ([o_spec] if has_add else [])
    args = (a, b) + ((add,) if has_add else ())
    if after is not None:
        in_specs, args = in_specs + [pl.BlockSpec(memory_space=pl.ANY)], args + (after,)
    acc_shape = (tm, tn) if nk > 1 else (8, LANES)
    return pl.pallas_call(
        body, name=name, grid=(m // tm, n // tn, nk), in_specs=in_specs, out_specs=o_spec,
        out_shape=SDS((m, n), out_dtype), scratch_shapes=[pltpu.VMEM(acc_shape, F32)],
        compiler_params=_params(),
    )(*args)


def _rows(width, col=0, tr=ROW_TILE):
    return pl.BlockSpec((tr, width), lambda i: (i, col))


def _whole(arr):
    nd = arr.ndim
    return pl.BlockSpec(arr.shape, lambda i: (0,) * nd)


def _rowwise(fn, ins, outs, accs, *, name, n_rows, tr=ROW_TILE, after=None):
    n_in, n_out = len(ins), len(outs)
    if after is not None:
        ins = list(ins) + [(after, pl.BlockSpec(memory_space=pl.ANY))]

    def body(*refs):
        i = pl.program_id(0)
        res = fn(i, *[r[...] for r in refs[:n_in]])
        refs = refs[:n_in] + refs[len(ins):]
        res = res if isinstance(res, (tuple, list)) else (res,)
        for o_ref, v in zip(refs[n_in:n_in + n_out], res[:n_out]):
            o_ref[...] = v.astype(o_ref.dtype)
        for a_ref, v in zip(refs[n_in + n_out:], res[n_out:]):
            @pl.when(i == 0)
            def _(a_ref=a_ref, v=v):
                a_ref[...] = v

            @pl.when(i > 0)
            def _(a_ref=a_ref, v=v):
                a_ref[...] += v

    acc_specs = [pl.BlockSpec(s.shape, lambda i, nd=len(s.shape): (0,) * nd) for s in accs]
    res = pl.pallas_call(
        body, name=name, grid=(n_rows // tr,),
        in_specs=[s for _, s in ins], out_specs=[s for _, s in outs] + acc_specs,
        out_shape=[s for s, _ in outs] + list(accs), compiler_params=_params(),
    )(*[a for a, _ in ins])
    return res


def _rms(x, g):
    return x * lax.rsqrt(jnp.mean(x * x, axis=-1, keepdims=True) + EPS) * g


def _sigmoid(x):
    return 1.0 / (1.0 + jnp.exp(-x))


def _silu(x):
    return x * _sigmoid(x)


def _rms_fwd(h, g, name):
    t, d = h.shape
    (u,) = _rowwise(lambda i, h, g: _rms(h, g), [(h, _rows(d)), (g, _whole(g))], [(SDS((t, d), BF16), _rows(d))], [],
                    name=name, n_rows=t)
    return u


def _rms_bwd(h, g, du, dres, name):
    t, d = h.shape

    def fn(i, h, g, du, dres):
        _, vjp = jax.vjp(_rms, h, g)
        dh, dg = vjp(du)
        dh = dh + dres
        return dh, dh, dg

    return _rowwise(fn, [(h, _rows(d)), (g, _whole(g)), (du, _rows(d)), (dres, _rows(d))],
                    [(SDS((t, d), F32), _rows(d)), (SDS((t, d), BF16), _rows(d))], [SDS((1, d), F32)],
                    name=name, n_rows=t)


def _conv_fwd(c0, w_dw, b_dw):
    t, ch = c0.shape
    tc = _tile(ch, 256)
    halo = 32
    shift = halo - (CONV_WIDTH - 1)

    def body(x_ref, w_ref, b_ref, o_ref, pad_ref):
        pad_ref[0:halo, :] = jnp.zeros((halo, tc), F32)
        pad_ref[halo:halo + t, :] = x_ref[...]
        for r0 in range(0, t, ROW_TILE):
            acc = jnp.zeros((ROW_TILE, tc), F32) + b_ref[...]
            for j in range(CONV_WIDTH):
                acc = acc + pad_ref[r0 + shift + j:r0 + shift + j + ROW_TILE, :] * w_ref[j:j + 1, :]
            o_ref[r0:r0 + ROW_TILE, :] = acc

    col = lambda i: (0, i)
    return pl.pallas_call(
        body, name="conv_fwd", grid=(ch // tc,),
        in_specs=[pl.BlockSpec((t, tc), col), pl.BlockSpec((CONV_WIDTH, tc), col), pl.BlockSpec((1, tc), col)],
        out_specs=pl.BlockSpec((t, tc), col), out_shape=SDS((t, ch), F32),
        scratch_shapes=[pltpu.VMEM((halo + t, tc), F32)], compiler_params=_params(),
    )(c0, w_dw, b_dw)


def _conv_bwd(dc1, c0, w_dw):
    t, ch = c0.shape
    tc = _tile(ch, 256)
    halo = 32
    shift = halo - (CONV_WIDTH - 1)

    def body(d_ref, x_ref, w_ref, dx_ref, dw_ref, db_ref, xpad_ref, dpad_ref):
        xpad_ref[0:halo, :] = jnp.zeros((halo, tc), F32)
        xpad_ref[halo:halo + t, :] = x_ref[...]
        dpad_ref[0:t, :] = d_ref[...]
        dpad_ref[t:t + halo, :] = jnp.zeros((halo, tc), F32)
        for r0 in range(0, t, ROW_TILE):
            acc = jnp.zeros((ROW_TILE, tc), F32)
            for j in range(CONV_WIDTH):
                off = r0 + (CONV_WIDTH - 1) - j
                acc = acc + dpad_ref[off:off + ROW_TILE, :] * w_ref[j:j + 1, :]
            dx_ref[r0:r0 + ROW_TILE, :] = acc
        for j in range(CONV_WIDTH):
            acc = jnp.zeros((1, tc), F32)
            for r0 in range(0, t, ROW_TILE):
                prod = d_ref[r0:r0 + ROW_TILE, :] * xpad_ref[r0 + shift + j:r0 + shift + j + ROW_TILE, :]
                acc = acc + jnp.sum(prod, axis=0, keepdims=True)
            dw_ref[j:j + 1, :] = acc
        db_ref[...] = jnp.sum(d_ref[...], axis=0, keepdims=True)

    col = lambda i: (0, i)
    return pl.pallas_call(
        body, name="conv_bwd", grid=(ch // tc,),
        in_specs=[pl.BlockSpec((t, tc), col), pl.BlockSpec((t, tc), col), pl.BlockSpec((CONV_WIDTH, tc), col)],
        out_specs=[pl.BlockSpec((t, tc), col), pl.BlockSpec((CONV_WIDTH, tc), col), pl.BlockSpec((1, tc), col)],
        out_shape=[SDS((t, ch), F32), SDS((CONV_WIDTH, ch), F32), SDS((1, ch), F32)],
        scratch_shapes=[pltpu.VMEM((halo + t, tc), F32), pltpu.VMEM((halo + t, tc), F32)], compiler_params=_params(),
    )(dc1, c0, w_dw)


def _rope(x1, x2, cos, sin):
    return x1 * cos - x2 * sin, x1 * sin + x2 * cos


def _attn_prep(q, kv, z_l, kr_col, cos, sin, n_heads):
    t = q.shape[0]
    hn = n_heads * QK_NOPE
    half = QK_ROPE // 2

    def body(q_ref, kv_ref, kr_ref, cos_ref, sin_ref, qo_ref, ko_ref, vo_ref):
        cos, sin = cos_ref[...], sin_ref[...]
        kr = kr_ref[...]
        k1, k2 = _rope(kr[:, 0:half], kr[:, half:QK_ROPE], cos, sin)
        for h in range(n_heads):
            b = h * QK
            q1, q2 = _rope(q_ref[:, b + QK_NOPE:b + QK_NOPE + half], q_ref[:, b + QK_NOPE + half:b + QK], cos, sin)
            qo_ref[h] = jnp.concatenate([q_ref[:, b:b + QK_NOPE], q1, q2], axis=-1).astype(BF16)
            ko_ref[h] = jnp.concatenate([kv_ref[:, h * QK_NOPE:(h + 1) * QK_NOPE], k1, k2], axis=-1).astype(BF16)
            vo_ref[h] = kv_ref[:, hn + h * V_HEAD:hn + (h + 1) * V_HEAD].astype(BF16)

    tr = ROW_TILE
    hm = lambda w: pl.BlockSpec((n_heads, tr, w), lambda i: (0, i, 0))
    return pl.pallas_call(
        body, name="attn_prep", grid=(t // tr,),
        in_specs=[_rows(q.shape[1]), _rows(kv.shape[1]), _rows(LANES, kr_col), _rows(half), _rows(half)],
        out_specs=[hm(QK), hm(QK), hm(V_HEAD)],
        out_shape=[SDS((n_heads, t, QK), BF16), SDS((n_heads, t, QK), BF16), SDS((n_heads, t, V_HEAD), BF16)],
        compiler_params=_params(),
    )(q, kv, z_l, cos, sin)


def _attn_post(dq_hm, dk_hm, dv_hm, cos, sin):
    n_heads, t, _ = dq_hm.shape
    hn = n_heads * QK_NOPE
    half = QK_ROPE // 2

    def unrope(d1, d2, cos, sin):
        return d1 * cos + d2 * sin, d2 * cos - d1 * sin

    def body(dq_ref, dk_ref, dv_ref, cos_ref, sin_ref, qo_ref, kvo_ref, kro_ref):
        cos, sin = cos_ref[...], sin_ref[...]
        dkr = jnp.zeros((ROW_TILE, QK_ROPE), F32)
        for h in range(n_heads):
            dq = dq_ref[h]
            d1, d2 = unrope(dq[:, QK_NOPE:QK_NOPE + half], dq[:, QK_NOPE + half:QK], cos, sin)
            qo_ref[:, h * QK:(h + 1) * QK] = jnp.concatenate([dq[:, 0:QK_NOPE], d1, d2], axis=-1).astype(BF16)
            dk = dk_ref[h]
            kvo_ref[:, h * QK_NOPE:(h + 1) * QK_NOPE] = dk[:, 0:QK_NOPE].astype(BF16)
            kvo_ref[:, hn + h * V_HEAD:hn + (h + 1) * V_HEAD] = dv_ref[h].astype(BF16)
            dkr = dkr + dk[:, QK_NOPE:QK]
        d1, d2 = unrope(dkr[:, 0:half], dkr[:, half:QK_ROPE], cos, sin)
        kro_ref[...] = jnp.concatenate([d1, d2, jnp.zeros((ROW_TILE, LANES - QK_ROPE), F32)], axis=-1)

    tr = ROW_TILE
    hm = lambda w: pl.BlockSpec((n_heads, tr, w), lambda i: (0, i, 0))
    return pl.pallas_call(
        body, name="attn_post", grid=(t // tr,),
        in_specs=[hm(QK), hm(QK), hm(V_HEAD), _rows(half), _rows(half)],
        out_specs=[_rows(n_heads * QK), _rows(2 * hn), _rows(LANES)],
        out_shape=[SDS((t, n_heads * QK), BF16), SDS((t, 2 * hn), BF16), SDS((t, LANES), F32)],
        compiler_params=_params(),
    )(dq_hm, dk_hm, dv_hm, cos, sin)


N_QBLK = 4
_NT = (((1,), (1,)), ((), ()))
_TN = (((0,), (0,)), ((), ()))


def _scores(q, k, r0, scale):
    s = lax.dot_general(q, k, _NT, preferred_element_type=F32) * scale
    row = r0 + lax.broadcasted_iota(jnp.int32, s.shape, 0)
    col = lax.broadcasted_iota(jnp.int32, s.shape, 1)
    return jnp.where(col <= row, s, -jnp.inf)


def _attn_fwd(q_hm, k_hm, v_hm):
    n_heads, t, _ = q_hm.shape
    bq = t // N_QBLK
    scale = QK ** -0.5

    def body(q_ref, k_ref, v_ref, o_ref, lse_ref):
        for i in range(N_QBLK):
            r0, n_k = i * bq, (i + 1) * bq
            s = _scores(q_ref[0, r0:r0 + bq, :], k_ref[0, 0:n_k, :], r0, scale)
            m = jnp.max(s, axis=-1, keepdims=True)
            p = jnp.exp(s - m)
            l = jnp.sum(p, axis=-1, keepdims=True)
            p = (p / l).astype(BF16)
            o_ref[r0:r0 + bq, :] = jnp.dot(p, v_ref[0, 0:n_k, :], preferred_element_type=F32).astype(BF16)
            lse_ref[0, r0:r0 + bq, :] = m + jnp.log(l)

    head = lambda w: pl.BlockSpec((1, t, w), lambda h: (h, 0, 0))
    return pl.pallas_call(
        body, name="attn_fwd", grid=(n_heads,),
        in_specs=[head(QK), head(QK), head(V_HEAD)],
        out_specs=[pl.BlockSpec((t, V_HEAD), lambda h: (0, h)), head(1)],
        out_shape=[SDS((t, n_heads * V_HEAD), BF16), SDS((n_heads, t, 1), F32)],
        compiler_params=_params(),
    )(q_hm, k_hm, v_hm)


def _attn_bwd(q_hm, k_hm, v_hm, lse, d_o):
    n_heads, t, _ = q_hm.shape
    bq = t // N_QBLK
    scale = QK ** -0.5

    def body(q_ref, k_ref, v_ref, lse_ref, do_ref, dq_ref, dk_ref, dv_ref):
        dk_ref[...] = jnp.zeros(dk_ref.shape, F32)
        dv_ref[...] = jnp.zeros(dv_ref.shape, F32)
        for i in range(N_QBLK):
            r0, n_k = i * bq, (i + 1) * bq
            q = q_ref[0, r0:r0 + bq, :]
            k = k_ref[0, 0:n_k, :]
            d_o = do_ref[r0:r0 + bq, :]
            s = _scores(q, k, r0, scale)
            p = jnp.exp(s - lse_ref[0, r0:r0 + bq, :])
            dp = lax.dot_general(d_o, v_ref[0, 0:n_k, :], _NT, preferred_element_type=F32)
            ds = (p * (dp - jnp.sum(dp * p, axis=-1, keepdims=True)) * scale).astype(BF16)
            dq_ref[0, r0:r0 + bq, :] = jnp.dot(ds, k, preferred_element_type=F32)
            dk_ref[0, 0:n_k, :] += lax.dot_general(ds, q, _TN, preferred_element_type=F32)
            dv_ref[0, 0:n_k, :] += lax.dot_general(p.astype(BF16), d_o, _TN, preferred_element_type=F32)

    head = lambda w: pl.BlockSpec((1, t, w), lambda h: (h, 0, 0))
    return pl.pallas_call(
        body, name="attn_bwd", grid=(n_heads,),
        in_specs=[head(QK), head(QK), head(V_HEAD), head(1), pl.BlockSpec((t, V_HEAD), lambda h: (0, h))],
        out_specs=[head(QK), head(QK), head(V_HEAD)],
        out_shape=[SDS((n_heads, t, QK), F32), SDS((n_heads, t, QK), F32), SDS((n_heads, t, V_HEAD), F32)],
        compiler_params=_params(),
    )(q_hm, k_hm, v_hm, lse, d_o)


def _glu(za, zb, ba, bb):
    return (za + ba) * _sigmoid(zb + bb)


def _ln_silu(c, g, b):
    mu = jnp.mean(c, axis=-1, keepdims=True)
    var = jnp.mean(jnp.square(c - mu), axis=-1, keepdims=True)
    return _silu((c - mu) * lax.rsqrt(var + EPS) * g + b)


def _mix(yc, bco, ya, zc, za, bgc, bga):
    return _sigmoid(zc + bgc) * (yc + bco) + _sigmoid(za + bga) * ya


def _swiglu(a, b):
    return _silu(a) * b


def _local_step(x, target, meta, w_dw, wts, sp, ship=None, fetch=None, dims=None):
    wts = dict(wts)
    if ship is None:
        ship = lambda group, grads: jnp.zeros((8, LANES), F32)
    if fetch is None:
        fetch = lambda group, after: {}
    seq, d = x.shape
    length = N_META + seq
    t = -(-length // BLOCK_Q) * BLOCK_Q
    ch = w_dw.shape[1]
    ql = sp["g_q_lora"].shape[1]
    n_heads = dims["n_heads"] if dims else wts["w_uq"].shape[1] // QK
    hn = n_heads * QK_NOPE
    dff = dims["dff"] if dims else wts["w_ffn_down"].shape[0]
    assert sp["g_kv_lora"].shape[1] == ql and ql % LANES == 0 and t % (N_QBLK * 16) == 0
    pad_rows = lambda a: jnp.concatenate([jnp.zeros((N_META, d), F32), a, jnp.zeros((t - length, d), F32)], axis=0)
    h0 = jnp.concatenate([meta, x, jnp.zeros((t - length, d), F32)], axis=0)
    target_p = pad_rows(target)

    pos = jnp.arange(t, dtype=F32)
    inv_freq = ROPE_THETA ** (-jnp.arange(0, QK_ROPE, 2, dtype=F32) / QK_ROPE)
    ang = pos[:, None] * inv_freq[None, :]
    cos, sin = jnp.cos(ang), jnp.sin(ang)

    b_glu_a, b_glu_b = sp["b_glu"][:, :ch], sp["b_glu"][:, ch:]
    b_gate_c, b_gate_a = sp["b_gate"][:, :d], sp["b_gate"][:, d:]
    kr_col = 2 * ql // LANES

    u = _rms_fwd(h0, sp["g_mix"], "rms_mix")
    z_glu = _mm(u, wts["w_glu"], mode="nn", out_dtype=F32, name="mm_z_glu")
    z_l = _mm(u, wts["w_lora"], mode="nn", out_dtype=F32, name="mm_z_lora")
    z_gate = _mm(u, wts["w_gate"], mode="nn", out_dtype=F32, name="mm_z_gate")

    glu_ins = [(z_glu, _rows(ch, 0)), (z_glu, _rows(ch, 1)), (b_glu_a, _whole(b_glu_a)), (b_glu_b, _whole(b_glu_b))]
    (c0,) = _rowwise(lambda i, za, zb, ba, bb: _glu(za, zb, ba, bb), glu_ins, [(SDS((t, ch), F32), _rows(ch))], [],
                     name="glu_fwd", n_rows=t)
    c1 = _conv_fwd(c0, w_dw, sp["b_dw"])
    ln_ins = [(c1, _rows(ch)), (sp["g_conv_ln"], _whole(sp["g_conv_ln"])), (sp["b_conv_ln"], _whole(sp["b_conv_ln"]))]
    (c3,) = _rowwise(lambda i, c, g, b: _ln_silu(c, g, b), ln_ins, [(SDS((t, ch), BF16), _rows(ch))], [],
                     name="ln_silu_fwd", n_rows=t)
    wts.update(fetch("mixers", c3))
    yc = _mm(c3, wts["w_conv_out"], mode="nn", out_dtype=F32, name="mm_conv_out")

    lora_ins = [(z_l, _rows(ql, 0)), (z_l, _rows(ql, 1)), (sp["g_q_lora"], _whole(sp["g_q_lora"])),
                (sp["g_kv_lora"], _whole(sp["g_kv_lora"]))]
    cq, ckv = _rowwise(lambda i, zq, zk, gq, gk: (_rms(zq, gq), _rms(zk, gk)), lora_ins,
                       [(SDS((t, ql), BF16), _rows(ql)), (SDS((t, ql), BF16), _rows(ql))], [],
                       name="lora_norm_fwd", n_rows=t)
    q = _mm(cq, wts["w_uq"], mode="nn", out_dtype=F32, name="mm_q")
    kv = _mm(ckv, wts["w_ukv"], mode="nn", out_dtype=F32, name="mm_kv")
    q_hm, k_hm, v_hm = _attn_prep(q, kv, z_l, kr_col, cos, sin, n_heads)
    o, lse = _attn_fwd(q_hm, k_hm, v_hm)
    ya = _mm(o, wts["w_attn_out"], mode="nn", out_dtype=F32, name="mm_attn_out")

    mix_ins = [(yc, _rows(d)), (sp["b_conv_out"], _whole(sp["b_conv_out"])), (ya, _rows(d)), (z_gate, _rows(d, 0)),
               (z_gate, _rows(d, 1)), (b_gate_c, _whole(b_gate_c)), (b_gate_a, _whole(b_gate_a))]
    (mix,) = _rowwise(lambda i, *a: _mix(*a), mix_ins, [(SDS((t, d), BF16), _rows(d))], [], name="mix_fwd", n_rows=t)
    h1 = _mm(mix, wts["w_out"], mode="nn", out_dtype=F32, name="mm_out", add=h0)

    hn_ = _rms_fwd(h1, sp["g_ffn"], "rms_ffn")
    wts.update(fetch("ffn", hn_))
    ab = _mm(hn_, wts["w_gu"], mode="nn", out_dtype=F32, name="mm_ffn_in")
    tr_ffn = 64
    (f,) = _rowwise(lambda i, a, b: _swiglu(a, b), [(ab, _rows(dff, 0, tr_ffn)), (ab, _rows(dff, 1, tr_ffn))],
                    [(SDS((t, dff), BF16), _rows(dff, 0, tr_ffn))], [], name="swiglu_fwd", n_rows=t, tr=tr_ffn)
    h2 = _mm(f, wts["w_ffn_down"], mode="nn", out_dtype=F32, name="mm_ffn_down", add=h1)

    def head(i, h, g, tgt):
        y, vjp = jax.vjp(_rms, h, g)
        row = i * ROW_TILE + lax.broadcasted_iota(jnp.int32, (ROW_TILE, 1), 0)
        valid = jnp.logical_and(row >= N_META, row < length)
        err = jnp.where(valid, y - tgt, 0.0)
        dh, dg = vjp(err / d)
        loss = 0.5 * jnp.sum(jnp.sum(err * err, axis=-1, keepdims=True), axis=0, keepdims=True) / d
        return dh, dh, dg, jnp.broadcast_to(loss, (1, LANES))

    dh2, dh2_b, g_final, loss_v = _rowwise(
        head, [(h2, _rows(d)), (sp["g_final"], _whole(sp["g_final"])), (target_p, _rows(d))],
        [(SDS((t, d), F32), _rows(d)), (SDS((t, d), BF16), _rows(d))], [SDS((1, d), F32), SDS((1, LANES), F32)],
        name="loss_head", n_rows=t)
    loss = loss_v[0, 0]

    g_ffn_down = _mm(f, dh2_b, mode="tn", out_dtype=BF16, name="mm_g_ffn_down")
    df = _mm(dh2_b, wts["w_ffn_down"], mode="nt", out_dtype=F32, name="mm_d_f")

    def swiglu_bwd(i, a, b, df):
        _, vjp = jax.vjp(_swiglu, a, b)
        da, db = vjp(df)
        return jnp.concatenate([da, db], axis=-1)

    (dab,) = _rowwise(swiglu_bwd, [(ab, _rows(dff, 0, tr_ffn)), (ab, _rows(dff, 1, tr_ffn)), (df, _rows(dff, 0, tr_ffn))],
                      [(SDS((t, 2 * dff), BF16), _rows(2 * dff, 0, tr_ffn))], [], name="swiglu_bwd", n_rows=t, tr=tr_ffn)
    g_gu = _mm(hn_, dab, mode="tn", out_dtype=BF16, name="mm_g_ffn_in")
    dhn = _mm(dab, wts["w_gu"], mode="nt", out_dtype=F32, name="mm_d_hn")
    sent = ship("ffn", {"w_gu": g_gu, "w_ffn_down": g_ffn_down})
    dh1, dh1_b, g_g_ffn = _rms_bwd(h1, sp["g_ffn"] + sent[0, 0], dhn, dh2, "rms_ffn_bwd")

    g_w_out = _mm(mix, dh1_b, mode="tn", out_dtype=BF16, name="mm_g_out")
    dmix = _mm(dh1_b, wts["w_out"], mode="nt", out_dtype=F32, name="mm_d_mix")

    def mix_bwd(i, yc, bco, ya, zc, za, bgc, bga, dmix):
        _, vjp = jax.vjp(_mix, yc, bco, ya, zc, za, bgc, bga)
        dyc, dbco, dya, dzc, dza, dbgc, dbga = vjp(dmix)
        return dyc, dya, jnp.concatenate([dzc, dza], axis=-1), dbco, dbgc, dbga

    dyc, dya, dz_gate, g_b_conv_out, g_bgc, g_bga = _rowwise(
        mix_bwd, mix_ins + [(dmix, _rows(d))],
        [(SDS((t, d), BF16), _rows(d)), (SDS((t, d), BF16), _rows(d)), (SDS((t, 2 * d), BF16), _rows(2 * d))],
        [SDS((1, d), F32)] * 3, name="mix_bwd", n_rows=t)

    g_attn_out = _mm(o, dya, mode="tn", out_dtype=BF16, name="mm_g_attn_out")
    d_o = _mm(dya, wts["w_attn_out"], mode="nt", out_dtype=BF16, name="mm_d_o")
    dq_hm, dk_hm, dv_hm = _attn_bwd(q_hm, k_hm, v_hm, lse, d_o)
    dq, dkv, dkr = _attn_post(dq_hm, dk_hm, dv_hm, cos, sin)
    g_uq = _mm(cq, dq, mode="tn", out_dtype=BF16, name="mm_g_uq")
    g_ukv = _mm(ckv, dkv, mode="tn", out_dtype=BF16, name="mm_g_ukv")
    dcq = _mm(dq, wts["w_uq"], mode="nt", out_dtype=F32, name="mm_d_cq")
    dckv = _mm(dkv, wts["w_ukv"], mode="nt", out_dtype=F32, name="mm_d_ckv")

    def lora_bwd(i, zq, zk, gq, gk, dcq, dckv, dkr):
        _, vq = jax.vjp(_rms, zq, gq)
        _, vk = jax.vjp(_rms, zk, gk)
        dzq, dgq = vq(dcq)
        dzk, dgk = vk(dckv)
        return jnp.concatenate([dzq, dzk, dkr], axis=-1), dgq, dgk

    dz_l, g_g_q, g_g_kv = _rowwise(
        lora_bwd, lora_ins + [(dcq, _rows(ql)), (dckv, _rows(ql)), (dkr, _rows(LANES))],
        [(SDS((t, 2 * ql + LANES), BF16), _rows(2 * ql + LANES))], [SDS((1, ql), F32)] * 2,
        name="lora_norm_bwd", n_rows=t)

    g_conv_out = _mm(c3, dyc, mode="tn", out_dtype=BF16, name="mm_g_conv_out")
    dc3 = _mm(dyc, wts["w_conv_out"], mode="nt", out_dtype=F32, name="mm_d_c3")
    sent = ship("mixers", {"w_conv_out": g_conv_out, "w_uq": g_uq, "w_ukv": g_ukv, "w_attn_out": g_attn_out,
                           "w_out": g_w_out})

    def ln_bwd(i, c, g, b, dc3):
        _, vjp = jax.vjp(_ln_silu, c, g, b)
        return vjp(dc3)

    g_ln_sent = sp["g_conv_ln"] + sent[0, 0]
    dc1, g_g_ln, g_b_ln = _rowwise(
        ln_bwd, [ln_ins[0], (g_ln_sent, _whole(g_ln_sent)), ln_ins[2], (dc3, _rows(ch))],
        [(SDS((t, ch), F32), _rows(ch))], [SDS((1, ch), F32)] * 2, name="ln_silu_bwd", n_rows=t)
    dc0, g_w_dw, g_b_dw = _conv_bwd(dc1, c0, w_dw)

    def glu_bwd(i, za, zb, ba, bb, dc0):
        _, vjp = jax.vjp(_glu, za, zb, ba, bb)
        dza, dzb, dba, dbb = vjp(dc0)
        return jnp.concatenate([dza, dzb], axis=-1), dba, dbb

    dz_glu, g_bga_, g_bgb_ = _rowwise(glu_bwd, glu_ins + [(dc0, _rows(ch))],
                                      [(SDS((t, 2 * ch), BF16), _rows(2 * ch))], [SDS((1, ch), F32)] * 2,
                                      name="glu_bwd", n_rows=t)

    du = _mm(dz_glu, wts["w_glu"], mode="nt", out_dtype=F32, name="mm_d_u0")
    du = _mm(dz_l, wts["w_lora"], mode="nt", out_dtype=F32, name="mm_d_u1", add=du)
    du = _mm(dz_gate, wts["w_gate"], mode="nt", out_dtype=F32, name="mm_d_u2", add=du)
    dh0, _, g_g_mix = _rms_bwd(h0, sp["g_mix"], du, dh1, "rms_mix_bwd")
    big = {"meta_tokens": dh0[:N_META], "w_dw": g_w_dw}
    small = {
        "g_mix": g_g_mix, "b_glu": jnp.concatenate([g_bga_, g_bgb_], axis=1),
        "b_gate": jnp.concatenate([g_bgc, g_bga], axis=1), "b_dw": g_b_dw, "g_conv_ln": g_g_ln, "b_conv_ln": g_b_ln,
        "b_conv_out": g_b_conv_out, "g_q_lora": g_g_q, "g_kv_lora": g_g_kv, "g_ffn": g_g_ffn, "g_final": g_final,
    }
    sent = ship("small", {**big, **small})
    g_glu = _mm(u, dz_glu, mode="tn", out_dtype=BF16, name="mm_g_w_glu", after=sent)
    g_lora = _mm(u, dz_l, mode="tn", out_dtype=BF16, name="mm_g_w_lora", after=sent)
    g_gate = _mm(u, dz_gate, mode="tn", out_dtype=BF16, name="mm_g_w_gate", after=sent)
    big.update({"w_glu": g_glu, "w_lora": g_lora, "w_gate": g_gate})
    ship("input", big)
    return loss, dh0[N_META:length], big, small


def _slot(ref, dev):
    return ref.at[dev]


def _row_window(rows):
    return lambda ref, dev: ref.at[pl.ds(pl.multiple_of(dev * rows, 16), rows)]


def _col_window(width, offset=0):
    return lambda ref, dev: ref.at[:, pl.ds(pl.multiple_of(offset + dev * width, LANES), width)]


def _dev_index(x, y, c):
    return 4 * x + 2 * y + c


def _gather_weights(items, out_shapes):
    srcs = [it[0] for it in items]
    n, n_out = len(srcs), len(out_shapes)

    def body(*refs):
        src, out = refs[:n], refs[n:n + n_out]
        send_sems, recv_sems, local_sems = refs[n + n_out:]
        x, y, c = lax.axis_index("x"), lax.axis_index("y"), lax.axis_index("c")
        me, sibling = (x, y, c), (x, y, 1 - c)
        chips = [(1 - x, y), (x, 1 - y), (1 - x, 1 - y)]

        def place(i, block):
            _, o, window = items[i]
            return window(out[o], _dev_index(*block))

        def copy(k, i, block, to, from_src=False):
            return pltpu.make_async_remote_copy(
                src_ref=src[i] if from_src else place(i, block), dst_ref=place(i, block),
                send_sem=send_sems.at[k * n + i], recv_sem=recv_sems.at[k * n + i],
                device_id=to, device_id_type=pl.DeviceIdType.MESH)

        mine = [pltpu.make_async_copy(src[i], place(i, me), local_sems.at[i]) for i in range(n)]
        first = [copy(0, i, me, sibling, True) for i in range(n)]
        first += [copy(1 + j, i, me, (*chip, c), True) for j, chip in enumerate(chips) for i in range(n)]
        for cp in mine + first:
            cp.start()
        passed = [[copy(4 + j, i, (*chip, c), sibling) for i in range(n)] for j, chip in enumerate(chips)]
        for j, chip in enumerate(chips):
            for i in range(n):
                copy(1 + j, i, (*chip, c), me).wait_recv()
            for cp in passed[j]:
                cp.start()
        for i in range(n):
            copy(0, i, sibling, me).wait_recv()
        for j, chip in enumerate(chips):
            for i in range(n):
                copy(4 + j, i, (*chip, 1 - c), me).wait_recv()
        for cp in first + [cp for row in passed for cp in row]:
            cp.wait_send()
        for cp in mine:
            cp.wait()

    any_spec = pl.BlockSpec(memory_space=pl.ANY)
    return pl.pallas_call(
        body, name="gather_weights", in_specs=[any_spec] * n, out_specs=[any_spec] * n_out, out_shape=out_shapes,
        scratch_shapes=[pltpu.SemaphoreType.DMA((7 * n,)), pltpu.SemaphoreType.DMA((7 * n,)),
                        pltpu.SemaphoreType.DMA((n,))],
    )(*srcs)


def _exchange(srcs, out_shapes, items, *, masks, name):
    n_src, n_out, n = len(srcs), len(out_shapes), len(items)

    def body(*refs):
        src, out = refs[:n_src], refs[n_src:n_src + n_out]
        send_sems, recv_sems = refs[n_src + n_out:]
        remote = _peer_copies(src, out, send_sems, recv_sems, items, masks)
        for cp in remote:
            cp.start()
        for cp in remote:
            cp.wait()

    any_spec = pl.BlockSpec(memory_space=pl.ANY)
    return pl.pallas_call(
        body, name=name, in_specs=[any_spec] * n_src, out_specs=[any_spec] * n_out, out_shape=out_shapes,
        scratch_shapes=[pltpu.SemaphoreType.DMA((len(masks) * n,)), pltpu.SemaphoreType.DMA((len(masks) * n,))],
    )(*srcs)


def _peer_copies(src, out, send_sems, recv_sems, items, masks):
    x, y, c = lax.axis_index("x"), lax.axis_index("y"), lax.axis_index("c")
    me = _dev_index(x, y, c)
    n = len(items)
    copies = []
    for k, mask in enumerate(masks):
        px, py, pc = x ^ ((mask >> 2) & 1), y ^ ((mask >> 1) & 1), c ^ (mask & 1)
        peer = _dev_index(px, py, pc)
        for i, (s, s_win, o, d_win) in enumerate(items):
            copies.append(pltpu.make_async_remote_copy(
                src_ref=s_win(src[s], peer), dst_ref=d_win(out[o], me),
                send_sem=send_sems.at[k * n + i], recv_sem=recv_sems.at[k * n + i],
                device_id=(px, py, pc), device_id_type=pl.DeviceIdType.MESH))
    return copies


def _exchange_start(srcs, out_shapes, items, *, masks, name):
    n_src, n_out, n = len(srcs), len(out_shapes), len(items)
    n_sem = len(masks) * n
    n_buf = n_src + n_out

    def body(*refs):
        src, land = refs[:n_src], refs[n_src:n_buf]
        send_sems, recv_sems = refs[n_buf], refs[n_buf + 1]
        token = refs[-1]
        for cp in _peer_copies(src, land, send_sems, recv_sems, items, masks):
            cp.start()
        token[...] = jnp.zeros_like(token)

    hbm = pl.BlockSpec(memory_space=pltpu.HBM)
    sem = pl.BlockSpec(memory_space=pltpu.SEMAPHORE)
    bufs = [pltpu.with_memory_space_constraint(a, pltpu.HBM) for a in srcs]
    bufs += [pltpu.with_memory_space_constraint(lax.empty(s.shape, s.dtype), pltpu.HBM) for s in out_shapes]
    res = pl.pallas_call(
        body, name=name,
        out_shape=(pltpu.SemaphoreType.DMA((n_sem,)), pltpu.SemaphoreType.DMA((n_sem,)),
                   *[pltpu.HBM(b.shape, b.dtype) for b in bufs], SDS((8, LANES), F32)),
        in_specs=[hbm] * n_buf, out_specs=(sem, sem, *[hbm] * n_buf, pl.BlockSpec(memory_space=pltpu.VMEM)),
        input_output_aliases={i: 2 + i for i in range(n_buf)},
        compiler_params=pltpu.CompilerParams(has_side_effects=pltpu.SideEffectType.DATAFLOW_SIDE_EFFECTING),
    )(*bufs)
    return res[0], res[1], list(res[2:2 + n_src]), list(res[2 + n_src:2 + n_buf]), res[-1]


def _exchange_wait(send_sems, recv_sems, srcs, lands, items, after, *, masks, name):
    n_src, n_out = len(srcs), len(lands)
    n_buf = n_src + n_out

    def body(*refs):
        src, land = refs[:n_src], refs[n_src:n_buf]
        send_sems, recv_sems = refs[n_buf], refs[n_buf + 1]
        for cp in _peer_copies(src, land, send_sems, recv_sems, items, masks):
            cp.wait_send()
            cp.wait_recv()

    hbm = pl.BlockSpec(memory_space=pltpu.HBM)
    sem = pl.BlockSpec(memory_space=pltpu.SEMAPHORE)
    bufs = list(srcs) + list(lands)
    res = pl.pallas_call(
        body, name=name, out_shape=tuple(pltpu.HBM(b.shape, b.dtype) for b in bufs),
        in_specs=[hbm] * n_buf + [sem, sem, pl.BlockSpec(memory_space=pl.ANY)], out_specs=tuple([hbm] * n_buf),
        input_output_aliases={i: i for i in range(n_buf)},
        compiler_params=pltpu.CompilerParams(has_side_effects=pltpu.SideEffectType.DATAFLOW_SIDE_EFFECTING),
    )(*bufs, send_sems, recv_sems, after)
    return list(res[:n_src]), list(res[n_src:])


def _regroup(srcs, dsts, *, name, tr=256, after=()):
    def segments(shape, valid):
        if len(shape) == 3:
            return [(p, shape[2]) for p in range(shape[0])]
        return [(None, valid)]

    src_arrays = [s[0] if isinstance(s, tuple) else s for s in srcs]
    src_valid = [s[1] if isinstance(s, tuple) else s.shape[-1] for s in srcs]
    k_rows = src_arrays[0].shape[-2]
    src_segs = [(i, p, w) for i, a in enumerate(src_arrays) for p, w in segments(a.shape, src_valid[i])]
    dst_segs = [(j, p, w) for j, (shape, _, valid) in enumerate(dsts) for p, w in segments(shape, valid)]
    pieces = []
    si, so, di, do = 0, 0, 0, 0
    while si < len(src_segs) and di < len(dst_segs):
        n = min(src_segs[si][2] - so, dst_segs[di][2] - do)
        pieces.append((src_segs[si][0], src_segs[si][1], so, dst_segs[di][0], dst_segs[di][1], do, n))
        so, do = so + n, do + n
        if so == src_segs[si][2]:
            si, so = si + 1, 0
        if do == dst_segs[di][2]:
            di, do = di + 1, 0
    assert si == len(src_segs) and di == len(dst_segs), "source and destination columns differ in number"
    n_src = len(src_arrays)

    def body(*refs):
        src, dst = refs[:n_src], refs[n_src + len(after):]
        for j, (shape, dtype, valid) in enumerate(dsts):
            if len(shape) == 2 and valid < shape[1]:
                dst[j][:, valid:shape[1]] = jnp.zeros((tr, shape[1] - valid), dtype)
        for i, sp, so, j, dp, do, n in pieces:
            val = src[i][:, so:so + n] if sp is None else src[i][sp, :, so:so + n]
            if dp is None:
                dst[j][:, do:do + n] = val.astype(dst[j].dtype)
            else:
                dst[j][dp, :, do:do + n] = val.astype(dst[j].dtype)

    def spec(shape):
        if len(shape) == 3:
            return pl.BlockSpec((shape[0], tr, shape[2]), lambda i: (0, i, 0))
        return pl.BlockSpec((tr, shape[1]), lambda i: (i, 0))

    n_after = len(after)
    return pl.pallas_call(
        body, name=name, grid=(k_rows // tr,),
        in_specs=[spec(a.shape) for a in src_arrays] + [pl.BlockSpec(memory_space=pl.ANY)] * n_after,
        out_specs=[spec(shape) for shape, _, _ in dsts], out_shape=[SDS(shape, dtype) for shape, dtype, _ in dsts],
        compiler_params=_params(),
    )(*src_arrays, *after)


def _cast_bf16(a, name, after=None):
    r, c = a.shape
    tr = _row_tile(r, c * 4)
    (out,) = _rowwise(lambda i, v: v, [(a, _rows(c, 0, tr))], [(SDS((r, c), BF16), _rows(c, 0, tr))], [], name=name,
                      n_rows=r, tr=tr, after=after)
    return out


def _own_spec(kind, tr, width, n_tiles):
    if kind[0] == "slot":
        return pl.BlockSpec((None, tr, width), lambda i, me: (me[0], i, 0))
    if kind[0] == "chip_slot":
        return pl.BlockSpec((None, tr, width), lambda i, me: (me[0] // 2, i, 0))
    if kind[0] == "rows":
        return pl.BlockSpec((tr, width), lambda i, me: (me[0] * n_tiles + i, 0))
    if kind[0] == "cols":
        return pl.BlockSpec((tr, width), lambda i, me: (i, kind[1] + me[0]))
    return pl.BlockSpec((tr, width), lambda i, me: (i, 0))


def _cast_into(me, a, whole, kind, name, into=None, after=None):
    r, c = a.shape
    tr = _row_tile(r, c * 4)
    extra = [x for x in (into, after) if x is not None]

    def body(me_ref, a_ref, *rest):
        rest[len(extra)][...] = a_ref[...].astype(BF16)

    return pl.pallas_call(
        body, name=name,
        grid_spec=pltpu.PrefetchScalarGridSpec(
            num_scalar_prefetch=1, grid=(r // tr,),
            in_specs=[pl.BlockSpec((tr, c), lambda i, me: (i, 0))] + [pl.BlockSpec(memory_space=pl.ANY)] * len(extra),
            out_specs=_own_spec(kind, tr, c, r // tr)),
        out_shape=whole, input_output_aliases={2: 0} if into is not None else {}, compiler_params=_params(),
    )(me, a, *extra)


def _own_block_copies(land, send_sems, recv_sems, items, masks):
    x, y, c = lax.axis_index("x"), lax.axis_index("y"), lax.axis_index("c")
    me = _dev_index(x, y, c)
    n = len(items)
    copies = []
    for k, mask in enumerate(masks):
        px, py, pc = x ^ ((mask >> 2) & 1), y ^ ((mask >> 1) & 1), c ^ (mask & 1)
        for i, (o, win) in enumerate(items):
            copies.append(pltpu.make_async_remote_copy(
                src_ref=win(land[o], me), dst_ref=win(land[o], me),
                send_sem=send_sems.at[k * n + i], recv_sem=recv_sems.at[k * n + i],
                device_id=(px, py, pc), device_id_type=pl.DeviceIdType.MESH))
    return copies


def _spread_start(lands, items, *, masks, name):
    n_buf, n_sem = len(lands), len(masks) * len(items)

    def body(*refs):
        land, send_sems, recv_sems, token = refs[:n_buf], refs[n_buf], refs[n_buf + 1], refs[-1]
        for cp in _own_block_copies(land, send_sems, recv_sems, items, masks):
            cp.start()
        token[...] = jnp.zeros_like(token)

    hbm = pl.BlockSpec(memory_space=pltpu.HBM)
    sem = pl.BlockSpec(memory_space=pltpu.SEMAPHORE)
    bufs = [pltpu.with_memory_space_constraint(a, pltpu.HBM) for a in lands]
    res = pl.pallas_call(
        body, name=name,
        out_shape=(pltpu.SemaphoreType.DMA((n_sem,)), pltpu.SemaphoreType.DMA((n_sem,)),
                   *[pltpu.HBM(b.shape, b.dtype) for b in bufs], SDS((8, LANES), F32)),
        in_specs=[hbm] * n_buf, out_specs=(sem, sem, *[hbm] * n_buf, pl.BlockSpec(memory_space=pltpu.VMEM)),
        input_output_aliases={i: 2 + i for i in range(n_buf)},
        compiler_params=pltpu.CompilerParams(has_side_effects=pltpu.SideEffectType.DATAFLOW_SIDE_EFFECTING),
    )(*bufs)
    return res[0], res[1], list(res[2:2 + n_buf]), res[-1]


def _swap_with_sibling(lands, items, name):
    n_buf, n = len(lands), len(items)

    def body(*refs):
        land, send_sems, recv_sems = refs[n_buf:2 * n_buf], refs[2 * n_buf], refs[2 * n_buf + 1]
        x, y, c = lax.axis_index("x"), lax.axis_index("y"), lax.axis_index("c")
        copies = []
        for j, (px, py) in enumerate([(0, 0), (0, 1), (1, 0), (1, 1)]):
            block = _dev_index(px, py, c)
            for i, (o, win) in enumerate(items):
                copies.append(pltpu.make_async_remote_copy(
                    src_ref=win(land[o], block), dst_ref=win(land[o], block),
                    send_sem=send_sems.at[j * n + i], recv_sem=recv_sems.at[j * n + i],
                    device_id=(x, y, 1 - c), device_id_type=pl.DeviceIdType.MESH))
        for cp in copies:
            cp.start()
        for cp in copies:
            cp.wait()

    any_spec = pl.BlockSpec(memory_space=pl.ANY)
    return pl.pallas_call(
        body, name=name, in_specs=[any_spec] * n_buf, out_specs=[any_spec] * n_buf,
        out_shape=[SDS(a.shape, a.dtype) for a in lands], input_output_aliases={i: i for i in range(n_buf)},
        scratch_shapes=[pltpu.SemaphoreType.DMA((4 * n,)), pltpu.SemaphoreType.DMA((4 * n,))],
    )(*lands)


def _spread_wait(send_sems, recv_sems, lands, items, after, *, masks, name):
    n_buf = len(lands)

    def body(*refs):
        land, send_sems, recv_sems = refs[:n_buf], refs[n_buf], refs[n_buf + 1]
        for cp in _own_block_copies(land, send_sems, recv_sems, items, masks):
            cp.wait_send()
            cp.wait_recv()

    hbm = pl.BlockSpec(memory_space=pltpu.HBM)
    sem = pl.BlockSpec(memory_space=pltpu.SEMAPHORE)
    res = pl.pallas_call(
        body, name=name, out_shape=tuple(pltpu.HBM(b.shape, b.dtype) for b in lands),
        in_specs=[hbm] * n_buf + [sem, sem, pl.BlockSpec(memory_space=pl.ANY)], out_specs=tuple([hbm] * n_buf),
        input_output_aliases={i: i for i in range(n_buf)},
        compiler_params=pltpu.CompilerParams(has_side_effects=pltpu.SideEffectType.DATAFLOW_SIDE_EFFECTING),
    )(*lands, send_sems, recv_sems, after)
    return list(res)


def _row_tile(rows, row_bytes, limit=2 * 1024 * 1024):
    best = None
    for t in range(16, rows + 1, 16):
        if rows % t == 0 and t * row_bytes <= limit:
            best = t
    return best if best is not None else rows


def _sum_with_sibling(me, slabs, from_sibling, name):
    _, k, c = slabs.shape
    tr = _row_tile(k, c * 4)

    def body(me_ref, a_ref, b_ref, o_ref):
        o_ref[...] = (a_ref[...].astype(F32) + b_ref[...].astype(F32)).astype(BF16)

    by_chip = pl.BlockSpec((None, tr, c), lambda j, i, me: (j, i, 0))
    return pl.pallas_call(
        body, name=name,
        grid_spec=pltpu.PrefetchScalarGridSpec(
            num_scalar_prefetch=1, grid=(N_DEV // 2, k // tr),
            in_specs=[pl.BlockSpec((None, tr, c), lambda j, i, me: (2 * j + me[0] % 2, i, 0)), by_chip],
            out_specs=by_chip),
        out_shape=SDS((N_DEV // 2, k, c), BF16), compiler_params=_params(),
    )(me, slabs, from_sibling)


def _adamw(me, recv, own, own_kind, w, m, v, *, name):
    n_rows, width = w.shape
    tr = _row_tile(n_rows, width * 4, limit=1024 * 1024)
    n_tiles = n_rows // tr
    own_spec = _own_spec(own_kind, tr, width, n_tiles)

    n_slots = recv.shape[0]

    def body(me_ref, r_ref, own_ref, w_ref, m_ref, v_ref, g_ref, d_ref, mo_ref, vo_ref):
        mine = own_ref[...].astype(F32)
        my_slot = me_ref[0] if n_slots == N_DEV else me_ref[0] // 2
        g = None
        for q in range(n_slots):
            term = jnp.where(my_slot == q, mine, r_ref[q].astype(F32))
            g = term if g is None else g + term
        m_new = ADAM_B1 * m_ref[...] + (1.0 - ADAM_B1) * g
        v_new = ADAM_B2 * v_ref[...] + (1.0 - ADAM_B2) * jnp.square(g)
        m_hat = m_new / (1.0 - ADAM_B1 ** ADAM_STEP)
        v_hat = v_new / (1.0 - ADAM_B2 ** ADAM_STEP)
        g_ref[...] = g
        d_ref[...] = -ADAM_LR * (m_hat / (jnp.sqrt(v_hat) + ADAM_EPS) + ADAM_WD * w_ref[...])
        mo_ref[...] = m_new
        vo_ref[...] = v_new

    row = pl.BlockSpec((tr, width), lambda i, me: (i, 0))
    return pl.pallas_call(
        body, name=name,
        grid_spec=pltpu.PrefetchScalarGridSpec(
            num_scalar_prefetch=1, grid=(n_tiles,),
            in_specs=[pl.BlockSpec((n_slots, tr, width), lambda i, me: (0, i, 0)), own_spec, row, row, row],
            out_specs=[row] * 4),
        out_shape=[SDS((n_rows, width), F32)] * 4, compiler_params=_params(),
    )(me, recv, own, w, m, v)


def _offsets(sizes):
    offs, o = [], 0
    for n in sizes:
        offs.append(o)
        o += n
    return offs, o


def kernel(x, meta_tokens, g_mix, w_in, b_glu, b_gate, w_dw, b_dw, g_conv_ln, b_conv_ln, w_conv_out, b_conv_out, g_q_lora, w_uq, g_kv_lora, w_uk, w_uv, w_attn_out, w_out, g_ffn, w_ffn_gate, w_ffn_up, w_ffn_down, g_final, loss_target, m_meta_tokens, m_g_mix, m_w_in, m_b_glu, m_b_gate, m_w_dw, m_b_dw, m_g_conv_ln, m_b_conv_ln, m_w_conv_out, m_b_conv_out, m_g_q_lora, m_w_uq, m_g_kv_lora, m_w_uk, m_w_uv, m_w_attn_out, m_w_out, m_g_ffn, m_w_ffn_gate, m_w_ffn_up, m_w_ffn_down, m_g_final, v_meta_tokens, v_g_mix, v_w_in, v_b_glu, v_b_gate, v_w_dw, v_b_dw, v_g_conv_ln, v_b_conv_ln, v_w_conv_out, v_b_conv_out, v_g_q_lora, v_w_uq, v_g_kv_lora, v_w_uk, v_w_uv, v_w_attn_out, v_w_out, v_g_ffn, v_w_ffn_gate, v_w_ffn_up, v_w_ffn_down, v_g_final):
    w_all = dict(meta_tokens=meta_tokens, g_mix=g_mix, w_in=w_in, b_glu=b_glu, b_gate=b_gate, w_dw=w_dw, b_dw=b_dw, g_conv_ln=g_conv_ln, b_conv_ln=b_conv_ln, w_conv_out=w_conv_out, b_conv_out=b_conv_out, g_q_lora=g_q_lora, w_uq=w_uq, g_kv_lora=g_kv_lora, w_uk=w_uk, w_uv=w_uv, w_attn_out=w_attn_out, w_out=w_out, g_ffn=g_ffn, w_ffn_gate=w_ffn_gate, w_ffn_up=w_ffn_up, w_ffn_down=w_ffn_down, g_final=g_final)
    m_all = dict(meta_tokens=m_meta_tokens, g_mix=m_g_mix, w_in=m_w_in, b_glu=m_b_glu, b_gate=m_b_gate, w_dw=m_w_dw, b_dw=m_b_dw, g_conv_ln=m_g_conv_ln, b_conv_ln=m_b_conv_ln, w_conv_out=m_w_conv_out, b_conv_out=m_b_conv_out, g_q_lora=m_g_q_lora, w_uq=m_w_uq, g_kv_lora=m_g_kv_lora, w_uk=m_w_uk, w_uv=m_w_uv, w_attn_out=m_w_attn_out, w_out=m_w_out, g_ffn=m_g_ffn, w_ffn_gate=m_w_ffn_gate, w_ffn_up=m_w_ffn_up, w_ffn_down=m_w_ffn_down, g_final=m_g_final)
    v_all = dict(meta_tokens=v_meta_tokens, g_mix=v_g_mix, w_in=v_w_in, b_glu=v_b_glu, b_gate=v_b_gate, w_dw=v_w_dw, b_dw=v_b_dw, g_conv_ln=v_g_conv_ln, b_conv_ln=v_b_conv_ln, w_conv_out=v_w_conv_out, b_conv_out=v_b_conv_out, g_q_lora=v_g_q_lora, w_uq=v_w_uq, g_kv_lora=v_g_kv_lora, w_uk=v_w_uk, w_uv=v_w_uv, w_attn_out=v_w_attn_out, w_out=v_w_out, g_ffn=v_g_ffn, w_ffn_gate=v_w_ffn_gate, w_ffn_up=v_w_ffn_up, w_ffn_down=v_w_ffn_down, g_final=v_g_final)

    two_d = lambda a: a.reshape(a.shape[-2:]) if a.ndim >= 2 else a.reshape(1, -1)
    sh = {n: two_d(w_all[n]) for n in SHARDED}
    d = x.shape[-1]
    k_in, c_in = sh["w_in"].shape
    r_co, r_ao, r_wo, r_fd = (sh[n].shape[0] for n in ROW_SHARDED)
    ql, c_uq = sh["w_uq"].shape
    c_uk = sh["w_uk"].shape[1]
    c_ff = sh["w_ffn_gate"].shape[1]
    n_meta, c_meta = sh["meta_tokens"].shape
    n_taps, c_dw = sh["w_dw"].shape
    ch, dff = N_DEV * c_dw, N_DEV * c_ff
    n_lora = 2 * ql + QK_ROPE

    masks = tuple(range(1, N_DEV))
    whole_ref = lambda ref, dev: ref
    gathered = _gather_weights(
        [(_cast_bf16(sh["w_in"], "cast_w_in"), 0, _slot), (sh["meta_tokens"], 1, _col_window(c_meta)),
         (sh["w_dw"], 2, _col_window(c_dw))],
        [SDS((N_DEV, k_in, c_in), BF16), SDS((n_meta, N_DEV * c_meta), F32), SDS((n_taps, ch), F32)])
    me = (4 * lax.axis_index("x") + 2 * lax.axis_index("y") + lax.axis_index("c")).astype(jnp.int32).reshape(1)

    def placed(n, whole, kind, after, into=None):
        return _cast_into(me, sh[n], whole, kind, "cast_" + n, into=into, after=None if into is not None else after)

    same_core = (2, 4, 6)
    coming = {}
    first = gathered[0]
    w_ukv0 = placed("w_uk", SDS((ql, 2 * N_DEV * c_uk), BF16), ("cols", 0), first)
    coming["mixers"] = _spread_start(
        [placed("w_conv_out", SDS((N_DEV * r_co, d), BF16), ("rows",), first),
         placed("w_uq", SDS((ql, N_DEV * c_uq), BF16), ("cols", 0), first),
         placed("w_uv", SDS((ql, 2 * N_DEV * c_uk), BF16), ("cols", N_DEV), first, into=w_ukv0),
         placed("w_attn_out", SDS((N_DEV * r_ao, d), BF16), ("rows",), first),
         placed("w_out", SDS((N_DEV * r_wo, d), BF16), ("rows",), first)],
        mixer_items := [(0, _row_window(r_co)), (1, _col_window(c_uq)), (2, _col_window(c_uk)),
                        (2, _col_window(c_uk, N_DEV * c_uk)), (3, _row_window(r_ao)), (4, _row_window(r_wo))],
        masks=same_core, name="send_weights_mixers") + (mixer_items,)
    second = coming["mixers"][3]
    coming["ffn"] = _spread_start(
        [placed("w_ffn_gate", SDS((N_DEV, d, c_ff), BF16), ("slot",), second),
         placed("w_ffn_up", SDS((N_DEV, d, c_ff), BF16), ("slot",), second),
         placed("w_ffn_down", SDS((N_DEV * r_fd, d), BF16), ("rows",), second)],
        ffn_items := [(0, _slot), (1, _slot), (2, _row_window(r_fd))],
        masks=same_core, name="send_weights_ffn") + (ffn_items,)

    def fetch(group, after):
        send_sems, recv_sems, lands, _, items = coming[group]
        lands = _spread_wait(send_sems, recv_sems, lands, items, after, masks=same_core,
                             name="wait_weights_" + group)
        lands = _swap_with_sibling(lands, items, "pass_weights_" + group)
        if group == "mixers":
            return {"w_conv_out": lands[0], "w_uq": lands[1], "w_ukv": lands[2], "w_attn_out": lands[3],
                    "w_out": lands[4]}
        (w_gu,) = _regroup([lands[0], lands[1]], [((d, 2 * dff), BF16, 2 * dff)], name="unpack_w_ffn_in")
        return {"w_gu": w_gu, "w_ffn_down": lands[2]}

    w_glu, w_lora, w_gate = _regroup(
        [gathered[0]], [((k_in, 2 * ch), BF16, 2 * ch), ((k_in, n_lora + LANES - QK_ROPE), BF16, n_lora),
                        ((k_in, 2 * d), BF16, 2 * d)], name="unpack_w_in",
        after=(coming["mixers"][3], coming["ffn"][3]))
    wts = {"w_glu": w_glu, "w_lora": w_lora, "w_gate": w_gate}
    sp = {n: w_all[n].reshape(1, -1) for n in SMALL}
    dims = {"n_heads": N_DEV * c_uq // QK, "dff": dff}

    recv_shape = lambda n: SDS((N_DEV,) + sh[n].shape, F32 if n in F32_GATHERED else BF16)
    in_flight = {}

    s_sizes = [w_all[n].size for n in SMALL]
    s_offs, n_s = _offsets(s_sizes)
    cat_small = lambda src: jnp.concatenate([src[n].reshape(1, -1) for n in SMALL], axis=1)
    small_names = ("meta_tokens", "w_dw")

    def ship(group, grads):
        if group == "small":
            srcs = [grads["meta_tokens"], grads["w_dw"], cat_small(grads)]
            recv = _exchange(
                srcs, [recv_shape(n) for n in small_names] + [SDS((N_DEV, 1, n_s), F32)],
                [(0, _col_window(c_meta), 0, _slot), (1, _col_window(c_dw), 1, _slot), (2, whole_ref, 2, _slot)],
                masks=masks, name="exchange_grads_small")
            in_flight[group] = (srcs, recv)
            return recv[2]
        if group == "input":
            (s_in,) = _regroup([grads["w_glu"], (grads["w_lora"], n_lora), grads["w_gate"]],
                               [((N_DEV, k_in, c_in), BF16, None)], name="pack_g_w_in")
            per_chip = SDS((N_DEV // 2, k_in, c_in), BF16)
            (halves,) = _exchange(
                [s_in], [per_chip],
                [(0, lambda ref, peer, j=j: ref.at[2 * j + peer % 2], 0, lambda ref, dev, j=j: ref.at[j])
                 for j in range(N_DEV // 2)], masks=(1,), name="swap_grads_input")
            chip_sums = _sum_with_sibling(me, s_in, halves, "sum_grads_input")
            send_sems, recv_sems, srcs, lands, zero = _exchange_start(
                [chip_sums], [per_chip],
                items := [(0, lambda ref, peer: ref.at[peer // 2], 0, lambda ref, dev: ref.at[dev // 2])],
                masks=(2, 4, 6), name="send_grads_input")
            in_flight[group] = (send_sems, recv_sems, srcs, lands, items, ("w_in",), [(0, ("chip_slot",))], zero,
                                (2, 4, 6))
            return zero
        if group == "ffn":
            s_gate, s_up = _regroup([grads["w_gu"]], [((N_DEV, d, c_ff), BF16, None)] * 2, name="pack_g_w_ffn_in")
            srcs = [s_gate, s_up, grads["w_ffn_down"]]
            names = ("w_ffn_gate", "w_ffn_up", "w_ffn_down")
            items = [(0, _slot, 0, _slot), (1, _slot, 1, _slot), (2, _row_window(r_fd), 2, _slot)]
            own = [(0, ("slot",)), (1, ("slot",)), (2, ("rows",))]
        else:
            srcs = [grads["w_conv_out"], grads["w_uq"], grads["w_ukv"], grads["w_attn_out"], grads["w_out"]]
            names = ("w_conv_out", "w_uq", "w_uk", "w_uv", "w_attn_out", "w_out")
            items = [(0, _row_window(r_co), 0, _slot), (1, _col_window(c_uq), 1, _slot),
                     (2, _col_window(c_uk), 2, _slot), (2, _col_window(c_uk, N_DEV * c_uk), 3, _slot),
                     (3, _row_window(r_ao), 4, _slot), (4, _row_window(r_wo), 5, _slot)]
            own = [(0, ("rows",)), (1, ("cols", 0)), (2, ("cols", 0)), (2, ("cols", N_DEV)), (3, ("rows",)),
                   (4, ("rows",))]
        send_sems, recv_sems, srcs, lands, zero = _exchange_start(
            srcs, [recv_shape(n) for n in names], items, masks=masks, name="send_grads_" + group)
        in_flight[group] = (send_sems, recv_sems, srcs, lands, items, names, own, zero, masks)
        return zero

    loss, grad_x, _, _ = _local_step(x[0], loss_target[0], gathered[1], gathered[2], wts, sp, ship, fetch, dims)

    by_name = {}

    def update(n, recv, own, kind):
        outs = _adamw(me, recv, own, kind, sh[n], two_d(m_all[n]), two_d(v_all[n]), name="adamw_" + n)
        by_name[n] = [o.reshape(w_all[n].shape) for o in outs]
        return outs[0]

    small_srcs, small_recv = in_flight["small"]
    for n, recv, src in zip(small_names, small_recv, small_srcs):
        update(n, recv, src, ("cols", 0))
    outs = _adamw(me, small_recv[2], small_srcs[2], ("whole",), cat_small(w_all), cat_small(m_all), cat_small(v_all),
                  name="adamw_replicated")
    for n, o, s in zip(SMALL, s_offs, s_sizes):
        by_name[n] = [out[:, o:o + s].reshape(w_all[n].shape) for out in outs]
    done = in_flight["input"][7]
    for group in ("ffn", "mixers", "input"):
        send_sems, recv_sems, srcs, lands, items, names, own, _, group_masks = in_flight[group]
        srcs, lands = _exchange_wait(send_sems, recv_sems, srcs, lands, items, done, masks=group_masks,
                                     name="wait_grads_" + group)
        for n, land, (s, kind) in zip(names, lands, own):
            done = update(n, land, srcs[s], kind)
    result = [[by_name[n][k] for n in WEIGHTS] for k in range(4)]
    loss = lax.psum(loss, ("x", "y", "c"))
    return (loss, grad_x[None], *result[0], *result[1], *result[2], *result[3])
```

```python
import functools

import jax
import jax.numpy as jnp
from jax import lax
from jax.experimental import pallas as pl
from jax.experimental.pallas import tpu as pltpu

F32, BF16 = jnp.float32, jnp.bfloat16
SDS = jax.ShapeDtypeStruct

N_DEV = 8
N_META = 16
BLOCK_Q = 128
CONV_WIDTH = 31
QK_NOPE, QK_ROPE, V_HEAD = 128, 64, 128
QK = QK_NOPE + QK_ROPE
ROPE_THETA = 10000.0
EPS = 1e-6
ADAM_LR, ADAM_B1, ADAM_B2, ADAM_EPS, ADAM_WD, ADAM_STEP = 0.001, 0.9, 0.999, 1e-08, 0.01, 10

LANES = 128
ROW_TILE = 128
PACK_W = 1024
VMEM_LIMIT = 56 * 1024 * 1024

BIG = ("w_in", "w_conv_out", "w_uq", "w_uk", "w_uv", "w_attn_out", "w_out", "w_ffn_gate", "w_ffn_up", "w_ffn_down")
F32_GATHERED = ("meta_tokens", "w_dw")
SHARDED = BIG + F32_GATHERED
ROW_SHARDED = ("w_conv_out", "w_attn_out", "w_out", "w_ffn_down")
SMALL = ("g_mix", "b_glu", "b_gate", "b_dw", "g_conv_ln", "b_conv_ln", "b_conv_out", "g_q_lora", "g_kv_lora",
         "g_ffn", "g_final")
WEIGHTS = ("meta_tokens", "g_mix", "w_in", "b_glu", "b_gate", "w_dw", "b_dw", "g_conv_ln", "b_conv_ln", "w_conv_out",
           "b_conv_out", "g_q_lora", "w_uq", "g_kv_lora", "w_uk", "w_uv", "w_attn_out", "w_out", "g_ffn", "w_ffn_gate",
           "w_ffn_up", "w_ffn_down", "g_final")


def _params():
    return pltpu.CompilerParams(vmem_limit_bytes=VMEM_LIMIT)


def _tile(dim, limit):
    best = None
    t = LANES
    while t <= min(dim, limit):
        if dim % t == 0:
            best = t
        t += LANES
    return best if best is not None else dim


def _mm(a, b, *, mode, out_dtype, name, add=None, after=None):
    if mode == "nn":
        (m, kc), n = a.shape, b.shape[1]
    elif mode == "nt":
        (m, kc), n = a.shape, b.shape[0]
    else:
        (kc, m), n = a.shape, b.shape[1]
    if mode == "tn":
        tm, tn, tk = _tile(m, 1024), _tile(n, 512), kc
    else:
        tm, tn, tk = m, _tile(n, 512), _tile(kc, 512)
    nk = kc // tk
    if mode == "nn":
        a_spec = pl.BlockSpec((tm, tk), lambda i, j, k: (i, k))
        b_spec = pl.BlockSpec((tk, tn), lambda i, j, k: (k, j))
        dims = (((1,), (0,)), ((), ()))
    elif mode == "nt":
        a_spec = pl.BlockSpec((tm, tk), lambda i, j, k: (i, k))
        b_spec = pl.BlockSpec((tn, tk), lambda i, j, k: (j, k))
        dims = (((1,), (1,)), ((), ()))
    else:
        a_spec = pl.BlockSpec((tk, tm), lambda i, j, k: (k, i))
        b_spec = pl.BlockSpec((tk, tn), lambda i, j, k: (k, j))
        dims = (((0,), (0,)), ((), ()))
    o_spec = pl.BlockSpec((tm, tn), lambda i, j, k: (i, j))
    has_add = add is not None

    def body(*refs):
        if after is not None:
            refs = refs[:-3] + refs[-2:]
        if has_add:
            a_ref, b_ref, add_ref, o_ref, acc_ref = refs
        else:
            a_ref, b_ref, o_ref, acc_ref = refs
        k = pl.program_id(2)
        p = lax.dot_general(a_ref[...], b_ref[...], dims, preferred_element_type=F32)
        if nk == 1:
            o_ref[...] = ((p + add_ref[...]) if has_add else p).astype(o_ref.dtype)
            return

        @pl.when(k == 0)
        def _():
            acc_ref[...] = (p + add_ref[...]) if has_add else p

        @pl.when(jnp.logical_and(k > 0, k < nk - 1))
        def _():
            acc_ref[...] += p

        @pl.when(k == nk - 1)
        def _():
            o_ref[...] = (acc_ref[...] + p).astype(o_ref.dtype)

    in_specs = [a_spec, b_spec] + ([o_spec] if has_add else [])
    args = (a, b) + ((add,) if has_add else ())
    if after is not None:
        in_specs, args = in_specs + [pl.BlockSpec(memory_space=pl.ANY)], args + (after,)
    acc_shape = (tm, tn) if nk > 1 else (8, LANES)
    return pl.pallas_call(
        body, name=name, grid=(m // tm, n // tn, nk), in_specs=in_specs, out_specs=o_spec,
        out_shape=SDS((m, n), out_dtype), scratch_shapes=[pltpu.VMEM(acc_shape, F32)],
        compiler_params=_params(),
    )(*args)


def _rows(width, col=0, tr=ROW_TILE):
    return pl.BlockSpec((tr, width), lambda i: (i, col))


def _whole(arr):
    nd = arr.ndim
    return pl.BlockSpec(arr.shape, lambda i: (0,) * nd)


def _rowwise(fn, ins, outs, accs, *, name, n_rows, tr=ROW_TILE, after=None):
    n_in, n_out = len(ins), len(outs)
    if after is not None:
        ins = list(ins) + [(after, pl.BlockSpec(memory_space=pl.ANY))]

    def body(*refs):
        i = pl.program_id(0)
        res = fn(i, *[r[...] for r in refs[:n_in]])
        refs = refs[:n_in] + refs[len(ins):]
        res = res if isinstance(res, (tuple, list)) else (res,)
        for o_ref, v in zip(refs[n_in:n_in + n_out], res[:n_out]):
            o_ref[...] = v.astype(o_ref.dtype)
        for a_ref, v in zip(refs[n_in + n_out:], res[n_out:]):
            @pl.when(i == 0)
            def _(a_ref=a_ref, v=v):
                a_ref[...] = v

            @pl.when(i > 0)
            def _(a_ref=a_ref, v=v):
                a_ref[...] += v

    acc_specs = [pl.BlockSpec(s.shape, lambda i, nd=len(s.shape): (0,) * nd) for s in accs]
    res = pl.pallas_call(
        body, name=name, grid=(n_rows // tr,),
        in_specs=[s for _, s in ins], out_specs=[s for _, s in outs] + acc_specs,
        out_shape=[s for s, _ in outs] + list(accs), compiler_params=_params(),
    )(*[a for a, _ in ins])
    return res


def _rms(x, g):
    return x * lax.rsqrt(jnp.mean(x * x, axis=-1, keepdims=True) + EPS) * g


def _sigmoid(x):
    return 1.0 / (1.0 + jnp.exp(-x))


def _silu(x):
    return x * _sigmoid(x)


def _rms_fwd(h, g, name):
    t, d = h.shape
    (u,) = _rowwise(lambda i, h, g: _rms(h, g), [(h, _rows(d)), (g, _whole(g))], [(SDS((t, d), BF16), _rows(d))], [],
                    name=name, n_rows=t)
    return u


def _rms_bwd(h, g, du, dres, name):
    t, d = h.shape

    def fn(i, h, g, du, dres):
        _, vjp = jax.vjp(_rms, h, g)
        dh, dg = vjp(du)
        dh = dh + dres
        return dh, dh, dg

    return _rowwise(fn, [(h, _rows(d)), (g, _whole(g)), (du, _rows(d)), (dres, _rows(d))],
                    [(SDS((t, d), F32), _rows(d)), (SDS((t, d), BF16), _rows(d))], [SDS((1, d), F32)],
                    name=name, n_rows=t)


def _conv_fwd(c0, w_dw, b_dw):
    t, ch = c0.shape
    tc = _tile(ch, 256)
    halo = 32
    shift = halo - (CONV_WIDTH - 1)

    def body(x_ref, w_ref, b_ref, o_ref, pad_ref):
        pad_ref[0:halo, :] = jnp.zeros((halo, tc), F32)
        pad_ref[halo:halo + t, :] = x_ref[...]
        for r0 in range(0, t, ROW_TILE):
            acc = jnp.zeros((ROW_TILE, tc), F32) + b_ref[...]
            for j in range(CONV_WIDTH):
                acc = acc + pad_ref[r0 + shift + j:r0 + shift + j + ROW_TILE, :] * w_ref[j:j + 1, :]
            o_ref[r0:r0 + ROW_TILE, :] = acc

    col = lambda i: (0, i)
    return pl.pallas_call(
        body, name="conv_fwd", grid=(ch // tc,),
        in_specs=[pl.BlockSpec((t, tc), col), pl.BlockSpec((CONV_WIDTH, tc), col), pl.BlockSpec((1, tc), col)],
        out_specs=pl.BlockSpec((t, tc), col), out_shape=SDS((t, ch), F32),
        scratch_shapes=[pltpu.VMEM((halo + t, tc), F32)], compiler_params=_params(),
    )(c0, w_dw, b_dw)


def _conv_bwd(dc1, c0, w_dw):
    t, ch = c0.shape
    tc = _tile(ch, 256)
    halo = 32
    shift = halo - (CONV_WIDTH - 1)

    def body(d_ref, x_ref, w_ref, dx_ref, dw_ref, db_ref, xpad_ref, dpad_ref):
        xpad_ref[0:halo, :] = jnp.zeros((halo, tc), F32)
        xpad_ref[halo:halo + t, :] = x_ref[...]
        dpad_ref[0:t, :] = d_ref[...]
        dpad_ref[t:t + halo, :] = jnp.zeros((halo, tc), F32)
        for r0 in range(0, t, ROW_TILE):
            acc = jnp.zeros((ROW_TILE, tc), F32)
            for j in range(CONV_WIDTH):
                off = r0 + (CONV_WIDTH - 1) - j
                acc = acc + dpad_ref[off:off + ROW_TILE, :] * w_ref[j:j + 1, :]
            dx_ref[r0:r0 + ROW_TILE, :] = acc
        for j in range(CONV_WIDTH):
            acc = jnp.zeros((1, tc), F32)
            for r0 in range(0, t, ROW_TILE):
                prod = d_ref[r0:r0 + ROW_TILE, :] * xpad_ref[r0 + shift + j:r0 + shift + j + ROW_TILE, :]
                acc = acc + jnp.sum(prod, axis=0, keepdims=True)
            dw_ref[j:j + 1, :] = acc
        db_ref[...] = jnp.sum(d_ref[...], axis=0, keepdims=True)

    col = lambda i: (0, i)
    return pl.pallas_call(
        body, name="conv_bwd", grid=(ch // tc,),
        in_specs=[pl.BlockSpec((t, tc), col), pl.BlockSpec((t, tc), col), pl.BlockSpec((CONV_WIDTH, tc), col)],
        out_specs=[pl.BlockSpec((t, tc), col), pl.BlockSpec((CONV_WIDTH, tc), col), pl.BlockSpec((1, tc), col)],
        out_shape=[SDS((t, ch), F32), SDS((CONV_WIDTH, ch), F32), SDS((1, ch), F32)],
        scratch_shapes=[pltpu.VMEM((halo + t, tc), F32), pltpu.VMEM((halo + t, tc), F32)], compiler_params=_params(),
    )(dc1, c0, w_dw)


def _rope(x1, x2, cos, sin):
    return x1 * cos - x2 * sin, x1 * sin + x2 * cos


def _attn_prep(q, kv, z_l, kr_col, cos, sin, n_heads):
    t = q.shape[0]
    hn = n_heads * QK_NOPE
    half = QK_ROPE // 2

    def body(q_ref, kv_ref, kr_ref, cos_ref, sin_ref, qo_ref, ko_ref, vo_ref):
        cos, sin = cos_ref[...], sin_ref[...]
        kr = kr_ref[...]
        k1, k2 = _rope(kr[:, 0:half], kr[:, half:QK_ROPE], cos, sin)
        for h in range(n_heads):
            b = h * QK
            q1, q2 = _rope(q_ref[:, b + QK_NOPE:b + QK_NOPE + half], q_ref[:, b + QK_NOPE + half:b + QK], cos, sin)
            qo_ref[h] = jnp.concatenate([q_ref[:, b:b + QK_NOPE], q1, q2], axis=-1).astype(BF16)
            ko_ref[h] = jnp.concatenate([kv_ref[:, h * QK_NOPE:(h + 1) * QK_NOPE], k1, k2], axis=-1).astype(BF16)
            vo_ref[h] = kv_ref[:, hn + h * V_HEAD:hn + (h + 1) * V_HEAD].astype(BF16)

    tr = ROW_TILE
    hm = lambda w: pl.BlockSpec((n_heads, tr, w), lambda i: (0, i, 0))
    return pl.pallas_call(
        body, name="attn_prep", grid=(t // tr,),
        in_specs=[_rows(q.shape[1]), _rows(kv.shape[1]), _rows(LANES, kr_col), _rows(half), _rows(half)],
        out_specs=[hm(QK), hm(QK), hm(V_HEAD)],
        out_shape=[SDS((n_heads, t, QK), BF16), SDS((n_heads, t, QK), BF16), SDS((n_heads, t, V_HEAD), BF16)],
        compiler_params=_params(),
    )(q, kv, z_l, cos, sin)


def _attn_post(dq_hm, dk_hm, dv_hm, cos, sin):
    n_heads, t, _ = dq_hm.shape
    hn = n_heads * QK_NOPE
    half = QK_ROPE // 2

    def unrope(d1, d2, cos, sin):
        return d1 * cos + d2 * sin, d2 * cos - d1 * sin

    def body(dq_ref, dk_ref, dv_ref, cos_ref, sin_ref, qo_ref, kvo_ref, kro_ref):
        cos, sin = cos_ref[...], sin_ref[...]
        dkr = jnp.zeros((ROW_TILE, QK_ROPE), F32)
        for h in range(n_heads):
            dq = dq_ref[h]
            d1, d2 = unrope(dq[:, QK_NOPE:QK_NOPE + half], dq[:, QK_NOPE + half:QK], cos, sin)
            qo_ref[:, h * QK:(h + 1) * QK] = jnp.concatenate([dq[:, 0:QK_NOPE], d1, d2], axis=-1).astype(BF16)
            dk = dk_ref[h]
            kvo_ref[:, h * QK_NOPE:(h + 1) * QK_NOPE] = dk[:, 0:QK_NOPE].astype(BF16)
            kvo_ref[:, hn + h * V_HEAD:hn + (h + 1) * V_HEAD] = dv_ref[h].astype(BF16)
            dkr = dkr + dk[:, QK_NOPE:QK]
        d1, d2 = unrope(dkr[:, 0:half], dkr[:, half:QK_ROPE], cos, sin)
        kro_ref[...] = jnp.concatenate([d1, d2, jnp.zeros((ROW_TILE, LANES - QK_ROPE), F32)], axis=-1)

    tr = ROW_TILE
    hm = lambda w: pl.BlockSpec((n_heads, tr, w), lambda i: (0, i, 0))
    return pl.pallas_call(
        body, name="attn_post", grid=(t // tr,),
        in_specs=[hm(QK), hm(QK), hm(V_HEAD), _rows(half), _rows(half)],
        out_specs=[_rows(n_heads * QK), _rows(2 * hn), _rows(LANES)],
        out_shape=[SDS((t, n_heads * QK), BF16), SDS((t, 2 * hn), BF16), SDS((t, LANES), F32)],
        compiler_params=_params(),
    )(dq_hm, dk_hm, dv_hm, cos, sin)


N_QBLK = 4
_NT = (((1,), (1,)), ((), ()))
_TN = (((0,), (0,)), ((), ()))


def _scores(q, k, r0, scale):
    s = lax.dot_general(q, k, _NT, preferred_element_type=F32) * scale
    row = r0 + lax.broadcasted_iota(jnp.int32, s.shape, 0)
    col = lax.broadcasted_iota(jnp.int32, s.shape, 1)
    return jnp.where(col <= row, s, -jnp.inf)


def _attn_fwd(q_hm, k_hm, v_hm):
    n_heads, t, _ = q_hm.shape
    bq = t // N_QBLK
    scale = QK ** -0.5

    def body(q_ref, k_ref, v_ref, o_ref, lse_ref):
        for i in range(N_QBLK):
            r0, n_k = i * bq, (i + 1) * bq
            s = _scores(q_ref[0, r0:r0 + bq, :], k_ref[0, 0:n_k, :], r0, scale)
            m = jnp.max(s, axis=-1, keepdims=True)
            p = jnp.exp(s - m)
            l = jnp.sum(p, axis=-1, keepdims=True)
            p = (p / l).astype(BF16)
            o_ref[r0:r0 + bq, :] = jnp.dot(p, v_ref[0, 0:n_k, :], preferred_element_type=F32).astype(BF16)
            lse_ref[0, r0:r0 + bq, :] = m + jnp.log(l)

    head = lambda w: pl.BlockSpec((1, t, w), lambda h: (h, 0, 0))
    return pl.pallas_call(
        body, name="attn_fwd", grid=(n_heads,),
        in_specs=[head(QK), head(QK), head(V_HEAD)],
        out_specs=[pl.BlockSpec((t, V_HEAD), lambda h: (0, h)), head(1)],
        out_shape=[SDS((t, n_heads * V_HEAD), BF16), SDS((n_heads, t, 1), F32)],
        compiler_params=_params(),
    )(q_hm, k_hm, v_hm)


def _attn_bwd(q_hm, k_hm, v_hm, lse, d_o):
    n_heads, t, _ = q_hm.shape
    bq = t // N_QBLK
    scale = QK ** -0.5

    def body(q_ref, k_ref, v_ref, lse_ref, do_ref, dq_ref, dk_ref, dv_ref):
        dk_ref[...] = jnp.zeros(dk_ref.shape, F32)
        dv_ref[...] = jnp.zeros(dv_ref.shape, F32)
        for i in range(N_QBLK):
            r0, n_k = i * bq, (i + 1) * bq
            q = q_ref[0, r0:r0 + bq, :]
            k = k_ref[0, 0:n_k, :]
            d_o = do_ref[r0:r0 + bq, :]
            s = _scores(q, k, r0, scale)
            p = jnp.exp(s - lse_ref[0, r0:r0 + bq, :])
            dp = lax.dot_general(d_o, v_ref[0, 0:n_k, :], _NT, preferred_element_type=F32)
            ds = (p * (dp - jnp.sum(dp * p, axis=-1, keepdims=True)) * scale).astype(BF16)
            dq_ref[0, r0:r0 + bq, :] = jnp.dot(ds, k, preferred_element_type=F32)
            dk_ref[0, 0:n_k, :] += lax.dot_general(ds, q, _TN, preferred_element_type=F32)
            dv_ref[0, 0:n_k, :] += lax.dot_general(p.astype(BF16), d_o, _TN, preferred_element_type=F32)

    head = lambda w: pl.BlockSpec((1, t, w), lambda h: (h, 0, 0))
    return pl.pallas_call(
        body, name="attn_bwd", grid=(n_heads,),
        in_specs=[head(QK), head(QK), head(V_HEAD), head(1), pl.BlockSpec((t, V_HEAD), lambda h: (0, h))],
        out_specs=[head(QK), head(QK), head(V_HEAD)],
        out_shape=[SDS((n_heads, t, QK), F32), SDS((n_heads, t, QK), F32), SDS((n_heads, t, V_HEAD), F32)],
        compiler_params=_params(),
    )(q_hm, k_hm, v_hm, lse, d_o)


def _glu(za, zb, ba, bb):
    return (za + ba) * _sigmoid(zb + bb)


def _ln_silu(c, g, b):
    mu = jnp.mean(c, axis=-1, keepdims=True)
    var = jnp.mean(jnp.square(c - mu), axis=-1, keepdims=True)
    return _silu((c - mu) * lax.rsqrt(var + EPS) * g + b)


def _mix(yc, bco, ya, zc, za, bgc, bga):
    return _sigmoid(zc + bgc) * (yc + bco) + _sigmoid(za + bga) * ya


def _swiglu(a, b):
    return _silu(a) * b


def _local_step(x, target, meta, w_dw, wts, sp, ship=None, fetch=None, dims=None):
    wts = dict(wts)
    if ship is None:
        ship = lambda group, grads: jnp.zeros((8, LANES), F32)
    if fetch is None:
        fetch = lambda group, after: {}
    seq, d = x.shape
    length = N_META + seq
    t = -(-length // BLOCK_Q) * BLOCK_Q
    ch = w_dw.shape[1]
    ql = sp["g_q_lora"].shape[1]
    n_heads = dims["n_heads"] if dims else wts["w_uq"].shape[1] // QK
    hn = n_heads * QK_NOPE
    dff = dims["dff"] if dims else wts["w_ffn_down"].shape[0]
    assert sp["g_kv_lora"].shape[1] == ql and ql % LANES == 0 and t % (N_QBLK * 16) == 0
    pad_rows = lambda a: jnp.concatenate([jnp.zeros((N_META, d), F32), a, jnp.zeros((t - length, d), F32)], axis=0)
    h0 = jnp.concatenate([meta, x, jnp.zeros((t - length, d), F32)], axis=0)
    target_p = pad_rows(target)

    pos = jnp.arange(t, dtype=F32)
    inv_freq = ROPE_THETA ** (-jnp.arange(0, QK_ROPE, 2, dtype=F32) / QK_ROPE)
    ang = pos[:, None] * inv_freq[None, :]
    cos, sin = jnp.cos(ang), jnp.sin(ang)

    b_glu_a, b_glu_b = sp["b_glu"][:, :ch], sp["b_glu"][:, ch:]
    b_gate_c, b_gate_a = sp["b_gate"][:, :d], sp["b_gate"][:, d:]
    kr_col = 2 * ql // LANES

    u = _rms_fwd(h0, sp["g_mix"], "rms_mix")
    z_glu = _mm(u, wts["w_glu"], mode="nn", out_dtype=F32, name="mm_z_glu")
    z_l = _mm(u, wts["w_lora"], mode="nn", out_dtype=F32, name="mm_z_lora")
    z_gate = _mm(u, wts["w_gate"], mode="nn", out_dtype=F32, name="mm_z_gate")

    glu_ins = [(z_glu, _rows(ch, 0)), (z_glu, _rows(ch, 1)), (b_glu_a, _whole(b_glu_a)), (b_glu_b, _whole(b_glu_b))]
    (c0,) = _rowwise(lambda i, za, zb, ba, bb: _glu(za, zb, ba, bb), glu_ins, [(SDS((t, ch), F32), _rows(ch))], [],
                     name="glu_fwd", n_rows=t)
    c1 = _conv_fwd(c0, w_dw, sp["b_dw"])
    ln_ins = [(c1, _rows(ch)), (sp["g_conv_ln"], _whole(sp["g_conv_ln"])), (sp["b_conv_ln"], _whole(sp["b_conv_ln"]))]
    (c3,) = _rowwise(lambda i, c, g, b: _ln_silu(c, g, b), ln_ins, [(SDS((t, ch), BF16), _rows(ch))], [],
                     name="ln_silu_fwd", n_rows=t)
    wts.update(fetch("mixers", c3))
    yc = _mm(c3, wts["w_conv_out"], mode="nn", out_dtype=F32, name="mm_conv_out")

    lora_ins = [(z_l, _rows(ql, 0)), (z_l, _rows(ql, 1)), (sp["g_q_lora"], _whole(sp["g_q_lora"])),
                (sp["g_kv_lora"], _whole(sp["g_kv_lora"]))]
    cq, ckv = _rowwise(lambda i, zq, zk, gq, gk: (_rms(zq, gq), _rms(zk, gk)), lora_ins,
                       [(SDS((t, ql), BF16), _rows(ql)), (SDS((t, ql), BF16), _rows(ql))], [],
                       name="lora_norm_fwd", n_rows=t)
    q = _mm(cq, wts["w_uq"], mode="nn", out_dtype=F32, name="mm_q")
    kv = _mm(ckv, wts["w_ukv"], mode="nn", out_dtype=F32, name="mm_kv")
    q_hm, k_hm, v_hm = _attn_prep(q, kv, z_l, kr_col, cos, sin, n_heads)
    o, lse = _attn_fwd(q_hm, k_hm, v_hm)
    ya = _mm(o, wts["w_attn_out"], mode="nn", out_dtype=F32, name="mm_attn_out")

    mix_ins = [(yc, _rows(d)), (sp["b_conv_out"], _whole(sp["b_conv_out"])), (ya, _rows(d)), (z_gate, _rows(d, 0)),
               (z_gate, _rows(d, 1)), (b_gate_c, _whole(b_gate_c)), (b_gate_a, _whole(b_gate_a))]
    (mix,) = _rowwise(lambda i, *a: _mix(*a), mix_ins, [(SDS((t, d), BF16), _rows(d))], [], name="mix_fwd", n_rows=t)
    h1 = _mm(mix, wts["w_out"], mode="nn", out_dtype=F32, name="mm_out", add=h0)

    hn_ = _rms_fwd(h1, sp["g_ffn"], "rms_ffn")
    wts.update(fetch("ffn", hn_))
    ab = _mm(hn_, wts["w_gu"], mode="nn", out_dtype=F32, name="mm_ffn_in")
    tr_ffn = 64
    (f,) = _rowwise(lambda i, a, b: _swiglu(a, b), [(ab, _rows(dff, 0, tr_ffn)), (ab, _rows(dff, 1, tr_ffn))],
                    [(SDS((t, dff), BF16), _rows(dff, 0, tr_ffn))], [], name="swiglu_fwd", n_rows=t, tr=tr_ffn)
    h2 = _mm(f, wts["w_ffn_down"], mode="nn", out_dtype=F32, name="mm_ffn_down", add=h1)

    def head(i, h, g, tgt):
        y, vjp = jax.vjp(_rms, h, g)
        row = i * ROW_TILE + lax.broadcasted_iota(jnp.int32, (ROW_TILE, 1), 0)
        valid = jnp.logical_and(row >= N_META, row < length)
        err = jnp.where(valid, y - tgt, 0.0)
        dh, dg = vjp(err / d)
        loss = 0.5 * jnp.sum(jnp.sum(err * err, axis=-1, keepdims=True), axis=0, keepdims=True) / d
        return dh, dh, dg, jnp.broadcast_to(loss, (1, LANES))

    dh2, dh2_b, g_final, loss_v = _rowwise(
        head, [(h2, _rows(d)), (sp["g_final"], _whole(sp["g_final"])), (target_p, _rows(d))],
        [(SDS((t, d), F32), _rows(d)), (SDS((t, d), BF16), _rows(d))], [SDS((1, d), F32), SDS((1, LANES), F32)],
        name="loss_head", n_rows=t)
    loss = loss_v[0, 0]

    g_ffn_down = _mm(f, dh2_b, mode="tn", out_dtype=BF16, name="mm_g_ffn_down")
    df = _mm(dh2_b, wts["w_ffn_down"], mode="nt", out_dtype=F32, name="mm_d_f")

    def swiglu_bwd(i, a, b, df):
        _, vjp = jax.vjp(_swiglu, a, b)
        da, db = vjp(df)
        return jnp.concatenate([da, db], axis=-1)

    (dab,) = _rowwise(swiglu_bwd, [(ab, _rows(dff, 0, tr_ffn)), (ab, _rows(dff, 1, tr_ffn)), (df, _rows(dff, 0, tr_ffn))],
                      [(SDS((t, 2 * dff), BF16), _rows(2 * dff, 0, tr_ffn))], [], name="swiglu_bwd", n_rows=t, tr=tr_ffn)
    g_gu = _mm(hn_, dab, mode="tn", out_dtype=BF16, name="mm_g_ffn_in")
    dhn = _mm(dab, wts["w_gu"], mode="nt", out_dtype=F32, name="mm_d_hn")
    sent = ship("ffn", {"w_gu": g_gu, "w_ffn_down": g_ffn_down})
    dh1, dh1_b, g_g_ffn = _rms_bwd(h1, sp["g_ffn"] + sent[0, 0], dhn, dh2, "rms_ffn_bwd")

    g_w_out = _mm(mix, dh1_b, mode="tn", out_dtype=BF16, name="mm_g_out")
    dmix = _mm(dh1_b, wts["w_out"], mode="nt", out_dtype=F32, name="mm_d_mix")

    def mix_bwd(i, yc, bco, ya, zc, za, bgc, bga, dmix):
        _, vjp = jax.vjp(_mix, yc, bco, ya, zc, za, bgc, bga)
        dyc, dbco, dya, dzc, dza, dbgc, dbga = vjp(dmix)
        return dyc, dya, jnp.concatenate([dzc, dza], axis=-1), dbco, dbgc, dbga

    dyc, dya, dz_gate, g_b_conv_out, g_bgc, g_bga = _rowwise(
        mix_bwd, mix_ins + [(dmix, _rows(d))],
        [(SDS((t, d), BF16), _rows(d)), (SDS((t, d), BF16), _rows(d)), (SDS((t, 2 * d), BF16), _rows(2 * d))],
        [SDS((1, d), F32)] * 3, name="mix_bwd", n_rows=t)

    g_attn_out = _mm(o, dya, mode="tn", out_dtype=BF16, name="mm_g_attn_out")
    d_o = _mm(dya, wts["w_attn_out"], mode="nt", out_dtype=BF16, name="mm_d_o")
    dq_hm, dk_hm, dv_hm = _attn_bwd(q_hm, k_hm, v_hm, lse, d_o)
    dq, dkv, dkr = _attn_post(dq_hm, dk_hm, dv_hm, cos, sin)
    g_uq = _mm(cq, dq, mode="tn", out_dtype=BF16, name="mm_g_uq")
    g_ukv = _mm(ckv, dkv, mode="tn", out_dtype=BF16, name="mm_g_ukv")
    dcq = _mm(dq, wts["w_uq"], mode="nt", out_dtype=F32, name="mm_d_cq")
    dckv = _mm(dkv, wts["w_ukv"], mode="nt", out_dtype=F32, name="mm_d_ckv")

    def lora_bwd(i, zq, zk, gq, gk, dcq, dckv, dkr):
        _, vq = jax.vjp(_rms, zq, gq)
        _, vk = jax.vjp(_rms, zk, gk)
        dzq, dgq = vq(dcq)
        dzk, dgk = vk(dckv)
        return jnp.concatenate([dzq, dzk, dkr], axis=-1), dgq, dgk

    dz_l, g_g_q, g_g_kv = _rowwise(
        lora_bwd, lora_ins + [(dcq, _rows(ql)), (dckv, _rows(ql)), (dkr, _rows(LANES))],
        [(SDS((t, 2 * ql + LANES), BF16), _rows(2 * ql + LANES))], [SDS((1, ql), F32)] * 2,
        name="lora_norm_bwd", n_rows=t)

    g_conv_out = _mm(c3, dyc, mode="tn", out_dtype=BF16, name="mm_g_conv_out")
    dc3 = _mm(dyc, wts["w_conv_out"], mode="nt", out_dtype=F32, name="mm_d_c3")
    sent = ship("mixers", {"w_conv_out": g_conv_out, "w_uq": g_uq, "w_ukv": g_ukv, "w_attn_out": g_attn_out,
                           "w_out": g_w_out})

    def ln_bwd(i, c, g, b, dc3):
        _, vjp = jax.vjp(_ln_silu, c, g, b)
        return vjp(dc3)

    g_ln_sent = sp["g_conv_ln"] + sent[0, 0]
    dc1, g_g_ln, g_b_ln = _rowwise(
        ln_bwd, [ln_ins[0], (g_ln_sent, _whole(g_ln_sent)), ln_ins[2], (dc3, _rows(ch))],
        [(SDS((t, ch), F32), _rows(ch))], [SDS((1, ch), F32)] * 2, name="ln_silu_bwd", n_rows=t)
    dc0, g_w_dw, g_b_dw = _conv_bwd(dc1, c0, w_dw)

    def glu_bwd(i, za, zb, ba, bb, dc0):
        _, vjp = jax.vjp(_glu, za, zb, ba, bb)
        dza, dzb, dba, dbb = vjp(dc0)
        return jnp.concatenate([dza, dzb], axis=-1), dba, dbb

    dz_glu, g_bga_, g_bgb_ = _rowwise(glu_bwd, glu_ins + [(dc0, _rows(ch))],
                                      [(SDS((t, 2 * ch), BF16), _rows(2 * ch))], [SDS((1, ch), F32)] * 2,
                                      name="glu_bwd", n_rows=t)

    du = _mm(dz_glu, wts["w_glu"], mode="nt", out_dtype=F32, name="mm_d_u0")
    du = _mm(dz_l, wts["w_lora"], mode="nt", out_dtype=F32, name="mm_d_u1", add=du)
    du = _mm(dz_gate, wts["w_gate"], mode="nt", out_dtype=F32, name="mm_d_u2", add=du)
    dh0, _, g_g_mix = _rms_bwd(h0, sp["g_mix"], du, dh1, "rms_mix_bwd")
    big = {"meta_tokens": dh0[:N_META], "w_dw": g_w_dw}
    small = {
        "g_mix": g_g_mix, "b_glu": jnp.concatenate([g_bga_, g_bgb_], axis=1),
        "b_gate": jnp.concatenate([g_bgc, g_bga], axis=1), "b_dw": g_b_dw, "g_conv_ln": g_g_ln, "b_conv_ln": g_b_ln,
        "b_conv_out": g_b_conv_out, "g_q_lora": g_g_q, "g_kv_lora": g_g_kv, "g_ffn": g_g_ffn, "g_final": g_final,
    }
    sent = ship("small", {**big, **small})
    g_glu = _mm(u, dz_glu, mode="tn", out_dtype=BF16, name="mm_g_w_glu", after=sent)
    g_lora = _mm(u, dz_l, mode="tn", out_dtype=BF16, name="mm_g_w_lora", after=sent)
    g_gate = _mm(u, dz_gate, mode="tn", out_dtype=BF16, name="mm_g_w_gate", after=sent)
    big.update({"w_glu": g_glu, "w_lora": g_lora, "w_gate": g_gate})
    ship("input", big)
    return loss, dh0[N_META:length], big, small


def _slot(ref, dev):
    return ref.at[dev]


def _row_window(rows):
    return lambda ref, dev: ref.at[pl.ds(pl.multiple_of(dev * rows, 16), rows)]


def _col_window(width, offset=0):
    return lambda ref, dev: ref.at[:, pl.ds(pl.multiple_of(offset + dev * width, LANES), width)]


def _dev_index(x, y, c):
    return 4 * x + 2 * y + c


def _gather_weights(items, out_shapes):
    srcs = [it[0] for it in items]
    n, n_out = len(srcs), len(out_shapes)

    def body(*refs):
        src, out = refs[:n], refs[n:n + n_out]
        send_sems, recv_sems, local_sems = refs[n + n_out:]
        x, y, c = lax.axis_index("x"), lax.axis_index("y"), lax.axis_index("c")
        me, sibling = (x, y, c), (x, y, 1 - c)
        chips = [(1 - x, y), (x, 1 - y), (1 - x, 1 - y)]

        def place(i, block):
            _, o, window = items[i]
            return window(out[o], _dev_index(*block))

        def copy(k, i, block, to, from_src=False):
            return pltpu.make_async_remote_copy(
                src_ref=src[i] if from_src else place(i, block), dst_ref=place(i, block),
                send_sem=send_sems.at[k * n + i], recv_sem=recv_sems.at[k * n + i],
                device_id=to, device_id_type=pl.DeviceIdType.MESH)

        mine = [pltpu.make_async_copy(src[i], place(i, me), local_sems.at[i]) for i in range(n)]
        first = [copy(0, i, me, sibling, True) for i in range(n)]
        first += [copy(1 + j, i, me, (*chip, c), True) for j, chip in enumerate(chips) for i in range(n)]
        for cp in mine + first:
            cp.start()
        passed = [[copy(4 + j, i, (*chip, c), sibling) for i in range(n)] for j, chip in enumerate(chips)]
        for j, chip in enumerate(chips):
            for i in range(n):
                copy(1 + j, i, (*chip, c), me).wait_recv()
            for cp in passed[j]:
                cp.start()
        for i in range(n):
            copy(0, i, sibling, me).wait_recv()
        for j, chip in enumerate(chips):
            for i in range(n):
                copy(4 + j, i, (*chip, 1 - c), me).wait_recv()
        for cp in first + [cp for row in passed for cp in row]:
            cp.wait_send()
        for cp in mine:
            cp.wait()

    any_spec = pl.BlockSpec(memory_space=pl.ANY)
    return pl.pallas_call(
        body, name="gather_weights", in_specs=[any_spec] * n, out_specs=[any_spec] * n_out, out_shape=out_shapes,
        scratch_shapes=[pltpu.SemaphoreType.DMA((7 * n,)), pltpu.SemaphoreType.DMA((7 * n,)),
                        pltpu.SemaphoreType.DMA((n,))],
    )(*srcs)


def _exchange(srcs, out_shapes, items, *, masks, name):
    n_src, n_out, n = len(srcs), len(out_shapes), len(items)

    def body(*refs):
        src, out = refs[:n_src], refs[n_src:n_src + n_out]
        send_sems, recv_sems = refs[n_src + n_out:]
        remote = _peer_copies(src, out, send_sems, recv_sems, items, masks)
        for cp in remote:
            cp.start()
        for cp in remote:
            cp.wait()

    any_spec = pl.BlockSpec(memory_space=pl.ANY)
    return pl.pallas_call(
        body, name=name, in_specs=[any_spec] * n_src, out_specs=[any_spec] * n_out, out_shape=out_shapes,
        scratch_shapes=[pltpu.SemaphoreType.DMA((len(masks) * n,)), pltpu.SemaphoreType.DMA((len(masks) * n,))],
    )(*srcs)


def _peer_copies(src, out, send_sems, recv_sems, items, masks):
    x, y, c = lax.axis_index("x"), lax.axis_index("y"), lax.axis_index("c")
    me = _dev_index(x, y, c)
    n = len(items)
    copies = []
    for k, mask in enumerate(masks):
        px, py, pc = x ^ ((mask >> 2) & 1), y ^ ((mask >> 1) & 1), c ^ (mask & 1)
        peer = _dev_index(px, py, pc)
        for i, (s, s_win, o, d_win) in enumerate(items):
            copies.append(pltpu.make_async_remote_copy(
                src_ref=s_win(src[s], peer), dst_ref=d_win(out[o], me),
                send_sem=send_sems.at[k * n + i], recv_sem=recv_sems.at[k * n + i],
                device_id=(px, py, pc), device_id_type=pl.DeviceIdType.MESH))
    return copies


def _exchange_start(srcs, out_shapes, items, *, masks, name):
    n_src, n_out, n = len(srcs), len(out_shapes), len(items)
    n_sem = len(masks) * n
    n_buf = n_src + n_out

    def body(*refs):
        src, land = refs[:n_src], refs[n_src:n_buf]
        send_sems, recv_sems = refs[n_buf], refs[n_buf + 1]
        token = refs[-1]
        for cp in _peer_copies(src, land, send_sems, recv_sems, items, masks):
            cp.start()
        token[...] = jnp.zeros_like(token)

    hbm = pl.BlockSpec(memory_space=pltpu.HBM)
    sem = pl.BlockSpec(memory_space=pltpu.SEMAPHORE)
    bufs = [pltpu.with_memory_space_constraint(a, pltpu.HBM) for a in srcs]
    bufs += [pltpu.with_memory_space_constraint(lax.empty(s.shape, s.dtype), pltpu.HBM) for s in out_shapes]
    res = pl.pallas_call(
        body, name=name,
        out_shape=(pltpu.SemaphoreType.DMA((n_sem,)), pltpu.SemaphoreType.DMA((n_sem,)),
                   *[pltpu.HBM(b.shape, b.dtype) for b in bufs], SDS((8, LANES), F32)),
        in_specs=[hbm] * n_buf, out_specs=(sem, sem, *[hbm] * n_buf, pl.BlockSpec(memory_space=pltpu.VMEM)),
        input_output_aliases={i: 2 + i for i in range(n_buf)},
        compiler_params=pltpu.CompilerParams(has_side_effects=pltpu.SideEffectType.DATAFLOW_SIDE_EFFECTING),
    )(*bufs)
    return res[0], res[1], list(res[2:2 + n_src]), list(res[2 + n_src:2 + n_buf]), res[-1]


def _exchange_wait(send_sems, recv_sems, srcs, lands, items, after, *, masks, name):
    n_src, n_out = len(srcs), len(lands)
    n_buf = n_src + n_out

    def body(*refs):
        src, land = refs[:n_src], refs[n_src:n_buf]
        send_sems, recv_sems = refs[n_buf], refs[n_buf + 1]
        for cp in _peer_copies(src, land, send_sems, recv_sems, items, masks):
            cp.wait_send()
            cp.wait_recv()

    hbm = pl.BlockSpec(memory_space=pltpu.HBM)
    sem = pl.BlockSpec(memory_space=pltpu.SEMAPHORE)
    bufs = list(srcs) + list(lands)
    res = pl.pallas_call(
        body, name=name, out_shape=tuple(pltpu.HBM(b.shape, b.dtype) for b in bufs),
        in_specs=[hbm] * n_buf + [sem, sem, pl.BlockSpec(memory_space=pl.ANY)], out_specs=tuple([hbm] * n_buf),
        input_output_aliases={i: i for i in range(n_buf)},
        compiler_params=pltpu.CompilerParams(has_side_effects=pltpu.SideEffectType.DATAFLOW_SIDE_EFFECTING),
    )(*bufs, send_sems, recv_sems, after)
    return list(res[:n_src]), list(res[n_src:])


def _regroup(srcs, dsts, *, name, tr=256, after=()):
    def segments(shape, valid):
        if len(shape) == 3:
            return [(p, shape[2]) for p in range(shape[0])]
        return [(None, valid)]

    src_arrays = [s[0] if isinstance(s, tuple) else s for s in srcs]
    src_valid = [s[1] if isinstance(s, tuple) else s.shape[-1] for s in srcs]
    k_rows = src_arrays[0].shape[-2]
    src_segs = [(i, p, w) for i, a in enumerate(src_arrays) for p, w in segments(a.shape, src_valid[i])]
    dst_segs = [(j, p, w) for j, (shape, _, valid) in enumerate(dsts) for p, w in segments(shape, valid)]
    pieces = []
    si, so, di, do = 0, 0, 0, 0
    while si < len(src_segs) and di < len(dst_segs):
        n = min(src_segs[si][2] - so, dst_segs[di][2] - do)
        pieces.append((src_segs[si][0], src_segs[si][1], so, dst_segs[di][0], dst_segs[di][1], do, n))
        so, do = so + n, do + n
        if so == src_segs[si][2]:
            si, so = si + 1, 0
        if do == dst_segs[di][2]:
            di, do = di + 1, 0
    assert si == len(src_segs) and di == len(dst_segs), "source and destination columns differ in number"
    n_src = len(src_arrays)

    def body(*refs):
        src, dst = refs[:n_src], refs[n_src + len(after):]
        for j, (shape, dtype, valid) in enumerate(dsts):
            if len(shape) == 2 and valid < shape[1]:
                dst[j][:, valid:shape[1]] = jnp.zeros((tr, shape[1] - valid), dtype)
        for i, sp, so, j, dp, do, n in pieces:
            val = src[i][:, so:so + n] if sp is None else src[i][sp, :, so:so + n]
            if dp is None:
                dst[j][:, do:do + n] = val.astype(dst[j].dtype)
            else:
                dst[j][dp, :, do:do + n] = val.astype(dst[j].dtype)

    def spec(shape):
        if len(shape) == 3:
            return pl.BlockSpec((shape[0], tr, shape[2]), lambda i: (0, i, 0))
        return pl.BlockSpec((tr, shape[1]), lambda i: (i, 0))

    n_after = len(after)
    return pl.pallas_call(
        body, name=name, grid=(k_rows // tr,),
        in_specs=[spec(a.shape) for a in src_arrays] + [pl.BlockSpec(memory_space=pl.ANY)] * n_after,
        out_specs=[spec(shape) for shape, _, _ in dsts], out_shape=[SDS(shape, dtype) for shape, dtype, _ in dsts],
        compiler_params=_params(),
    )(*src_arrays, *after)


def _cast_bf16(a, name, after=None):
    r, c = a.shape
    tr = _row_tile(r, c * 4)
    (out,) = _rowwise(lambda i, v: v, [(a, _rows(c, 0, tr))], [(SDS((r, c), BF16), _rows(c, 0, tr))], [], name=name,
                      n_rows=r, tr=tr, after=after)
    return out


def _own_spec(kind, tr, width, n_tiles):
    if kind[0] == "slot":
        return pl.BlockSpec((None, tr, width), lambda i, me: (me[0], i, 0))
    if kind[0] == "chip_slot":
        return pl.BlockSpec((None, tr, width), lambda i, me: (me[0] // 2, i, 0))
    if kind[0] == "rows":
        return pl.BlockSpec((tr, width), lambda i, me: (me[0] * n_tiles + i, 0))
    if kind[0] == "cols":
        return pl.BlockSpec((tr, width), lambda i, me: (i, kind[1] + me[0]))
    return pl.BlockSpec((tr, width), lambda i, me: (i, 0))


def _cast_into(me, a, whole, kind, name, into=None, after=None):
    r, c = a.shape
    tr = _row_tile(r, c * 4)
    extra = [x for x in (into, after) if x is not None]

    def body(me_ref, a_ref, *rest):
        rest[len(extra)][...] = a_ref[...].astype(BF16)

    return pl.pallas_call(
        body, name=name,
        grid_spec=pltpu.PrefetchScalarGridSpec(
            num_scalar_prefetch=1, grid=(r // tr,),
            in_specs=[pl.BlockSpec((tr, c), lambda i, me: (i, 0))] + [pl.BlockSpec(memory_space=pl.ANY)] * len(extra),
            out_specs=_own_spec(kind, tr, c, r // tr)),
        out_shape=whole, input_output_aliases={2: 0} if into is not None else {}, compiler_params=_params(),
    )(me, a, *extra)


def _own_block_copies(land, send_sems, recv_sems, items, masks):
    x, y, c = lax.axis_index("x"), lax.axis_index("y"), lax.axis_index("c")
    me = _dev_index(x, y, c)
    n = len(items)
    copies = []
    for k, mask in enumerate(masks):
        px, py, pc = x ^ ((mask >> 2) & 1), y ^ ((mask >> 1) & 1), c ^ (mask & 1)
        for i, (o, win) in enumerate(items):
            copies.append(pltpu.make_async_remote_copy(
                src_ref=win(land[o], me), dst_ref=win(land[o], me),
                send_sem=send_sems.at[k * n + i], recv_sem=recv_sems.at[k * n + i],
                device_id=(px, py, pc), device_id_type=pl.DeviceIdType.MESH))
    return copies


def _spread_start(lands, items, *, masks, name):
    n_buf, n_sem = len(lands), len(masks) * len(items)

    def body(*refs):
        land, send_sems, recv_sems, token = refs[:n_buf], refs[n_buf], refs[n_buf + 1], refs[-1]
        for cp in _own_block_copies(land, send_sems, recv_sems, items, masks):
            cp.start()
        token[...] = jnp.zeros_like(token)

    hbm = pl.BlockSpec(memory_space=pltpu.HBM)
    sem = pl.BlockSpec(memory_space=pltpu.SEMAPHORE)
    bufs = [pltpu.with_memory_space_constraint(a, pltpu.HBM) for a in lands]
    res = pl.pallas_call(
        body, name=name,
        out_shape=(pltpu.SemaphoreType.DMA((n_sem,)), pltpu.SemaphoreType.DMA((n_sem,)),
                   *[pltpu.HBM(b.shape, b.dtype) for b in bufs], SDS((8, LANES), F32)),
        in_specs=[hbm] * n_buf, out_specs=(sem, sem, *[hbm] * n_buf, pl.BlockSpec(memory_space=pltpu.VMEM)),
        input_output_aliases={i: 2 + i for i in range(n_buf)},
        compiler_params=pltpu.CompilerParams(has_side_effects=pltpu.SideEffectType.DATAFLOW_SIDE_EFFECTING),
    )(*bufs)
    return res[0], res[1], list(res[2:2 + n_buf]), res[-1]


def _swap_with_sibling(lands, items, name):
    n_buf, n = len(lands), len(items)

    def body(*refs):
        land, send_sems, recv_sems = refs[n_buf:2 * n_buf], refs[2 * n_buf], refs[2 * n_buf + 1]
        x, y, c = lax.axis_index("x"), lax.axis_index("y"), lax.axis_index("c")
        copies = []
        for j, (px, py) in enumerate([(0, 0), (0, 1), (1, 0), (1, 1)]):
            block = _dev_index(px, py, c)
            for i, (o, win) in enumerate(items):
                copies.append(pltpu.make_async_remote_copy(
                    src_ref=win(land[o], block), dst_ref=win(land[o], block),
                    send_sem=send_sems.at[j * n + i], recv_sem=recv_sems.at[j * n + i],
                    device_id=(x, y, 1 - c), device_id_type=pl.DeviceIdType.MESH))
        for cp in copies:
            cp.start()
        for cp in copies:
            cp.wait()

    any_spec = pl.BlockSpec(memory_space=pl.ANY)
    return pl.pallas_call(
        body, name=name, in_specs=[any_spec] * n_buf, out_specs=[any_spec] * n_buf,
        out_shape=[SDS(a.shape, a.dtype) for a in lands], input_output_aliases={i: i for i in range(n_buf)},
        scratch_shapes=[pltpu.SemaphoreType.DMA((4 * n,)), pltpu.SemaphoreType.DMA((4 * n,))],
    )(*lands)


def _spread_wait(send_sems, recv_sems, lands, items, after, *, masks, name):
    n_buf = len(lands)

    def body(*refs):
        land, send_sems, recv_sems = refs[:n_buf], refs[n_buf], refs[n_buf + 1]
        for cp in _own_block_copies(land, send_sems, recv_sems, items, masks):
            cp.wait_send()
            cp.wait_recv()

    hbm = pl.BlockSpec(memory_space=pltpu.HBM)
    sem = pl.BlockSpec(memory_space=pltpu.SEMAPHORE)
    res = pl.pallas_call(
        body, name=name, out_shape=tuple(pltpu.HBM(b.shape, b.dtype) for b in lands),
        in_specs=[hbm] * n_buf + [sem, sem, pl.BlockSpec(memory_space=pl.ANY)], out_specs=tuple([hbm] * n_buf),
        input_output_aliases={i: i for i in range(n_buf)},
        compiler_params=pltpu.CompilerParams(has_side_effects=pltpu.SideEffectType.DATAFLOW_SIDE_EFFECTING),
    )(*lands, send_sems, recv_sems, after)
    return list(res)


def _row_tile(rows, row_bytes, limit=2 * 1024 * 1024):
    best = None
    for t in range(16, rows + 1, 16):
        if rows % t == 0 and t * row_bytes <= limit:
            best = t
    return best if best is not None else rows


def _sum_with_sibling(me, slabs, from_sibling, name):
    _, k, c = slabs.shape
    tr = _row_tile(k, c * 4)

    def body(me_ref, a_ref, b_ref, o_ref):
        o_ref[...] = (a_ref[...].astype(F32) + b_ref[...].astype(F32)).astype(BF16)

    by_chip = pl.BlockSpec((None, tr, c), lambda j, i, me: (j, i, 0))
    return pl.pallas_call(
        body, name=name,
        grid_spec=pltpu.PrefetchScalarGridSpec(
            num_scalar_prefetch=1, grid=(N_DEV // 2, k // tr),
            in_specs=[pl.BlockSpec((None, tr, c), lambda j, i, me: (2 * j + me[0] % 2, i, 0)), by_chip],
            out_specs=by_chip),
        out_shape=SDS((N_DEV // 2, k, c), BF16), compiler_params=_params(),
    )(me, slabs, from_sibling)


def _adamw(me, recv, own, own_kind, w, m, v, *, name):
    n_rows, width = w.shape
    tr = _row_tile(n_rows, width * 4, limit=1024 * 1024)
    n_tiles = n_rows // tr
    own_spec = _own_spec(own_kind, tr, width, n_tiles)

    n_slots = recv.shape[0]

    def body(me_ref, r_ref, own_ref, w_ref, m_ref, v_ref, g_ref, d_ref, mo_ref, vo_ref):
        mine = own_ref[...].astype(F32)
        my_slot = me_ref[0] if n_slots == N_DEV else me_ref[0] // 2
        g = None
        for q in range(n_slots):
            term = jnp.where(my_slot == q, mine, r_ref[q].astype(F32))
            g = term if g is None else g + term
        m_new = ADAM_B1 * m_ref[...] + (1.0 - ADAM_B1) * g
        v_new = ADAM_B2 * v_ref[...] + (1.0 - ADAM_B2) * jnp.square(g)
        m_hat = m_new / (1.0 - ADAM_B1 ** ADAM_STEP)
        v_hat = v_new / (1.0 - ADAM_B2 ** ADAM_STEP)
        g_ref[...] = g
        d_ref[...] = -ADAM_LR * (m_hat / (jnp.sqrt(v_hat) + ADAM_EPS) + ADAM_WD * w_ref[...])
        mo_ref[...] = m_new
        vo_ref[...] = v_new

    row = pl.BlockSpec((tr, width), lambda i, me: (i, 0))
    return pl.pallas_call(
        body, name=name,
        grid_spec=pltpu.PrefetchScalarGridSpec(
            num_scalar_prefetch=1, grid=(n_tiles,),
            in_specs=[pl.BlockSpec((n_slots, tr, width), lambda i, me: (0, i, 0)), own_spec, row, row, row],
            out_specs=[row] * 4),
        out_shape=[SDS((n_rows, width), F32)] * 4, compiler_params=_params(),
    )(me, recv, own, w, m, v)


def _offsets(sizes):
    offs, o = [], 0
    for n in sizes:
        offs.append(o)
        o += n
    return offs, o


def kernel(x, meta_tokens, g_mix, w_in, b_glu, b_gate, w_dw, b_dw, g_conv_ln, b_conv_ln, w_conv_out, b_conv_out, g_q_lora, w_uq, g_kv_lora, w_uk, w_uv, w_attn_out, w_out, g_ffn, w_ffn_gate, w_ffn_up, w_ffn_down, g_final, loss_target, m_meta_tokens, m_g_mix, m_w_in, m_b_glu, m_b_gate, m_w_dw, m_b_dw, m_g_conv_ln, m_b_conv_ln, m_w_conv_out, m_b_conv_out, m_g_q_lora, m_w_uq, m_g_kv_lora, m_w_uk, m_w_uv, m_w_attn_out, m_w_out, m_g_ffn, m_w_ffn_gate, m_w_ffn_up, m_w_ffn_down, m_g_final, v_meta_tokens, v_g_mix, v_w_in, v_b_glu, v_b_gate, v_w_dw, v_b_dw, v_g_conv_ln, v_b_conv_ln, v_w_conv_out, v_b_conv_out, v_g_q_lora, v_w_uq, v_g_kv_lora, v_w_uk, v_w_uv, v_w_attn_out, v_w_out, v_g_ffn, v_w_ffn_gate, v_w_ffn_up, v_w_ffn_down, v_g_final):
    w_all = dict(meta_tokens=meta_tokens, g_mix=g_mix, w_in=w_in, b_glu=b_glu, b_gate=b_gate, w_dw=w_dw, b_dw=b_dw, g_conv_ln=g_conv_ln, b_conv_ln=b_conv_ln, w_conv_out=w_conv_out, b_conv_out=b_conv_out, g_q_lora=g_q_lora, w_uq=w_uq, g_kv_lora=g_kv_lora, w_uk=w_uk, w_uv=w_uv, w_attn_out=w_attn_out, w_out=w_out, g_ffn=g_ffn, w_ffn_gate=w_ffn_gate, w_ffn_up=w_ffn_up, w_ffn_down=w_ffn_down, g_final=g_final)
    m_all = dict(meta_tokens=m_meta_tokens, g_mix=m_g_mix, w_in=m_w_in, b_glu=m_b_glu, b_gate=m_b_gate, w_dw=m_w_dw, b_dw=m_b_dw, g_conv_ln=m_g_conv_ln, b_conv_ln=m_b_conv_ln, w_conv_out=m_w_conv_out, b_conv_out=m_b_conv_out, g_q_lora=m_g_q_lora, w_uq=m_w_uq, g_kv_lora=m_g_kv_lora, w_uk=m_w_uk, w_uv=m_w_uv, w_attn_out=m_w_attn_out, w_out=m_w_out, g_ffn=m_g_ffn, w_ffn_gate=m_w_ffn_gate, w_ffn_up=m_w_ffn_up, w_ffn_down=m_w_ffn_down, g_final=m_g_final)
    v_all = dict(meta_tokens=v_meta_tokens, g_mix=v_g_mix, w_in=v_w_in, b_glu=v_b_glu, b_gate=v_b_gate, w_dw=v_w_dw, b_dw=v_b_dw, g_conv_ln=v_g_conv_ln, b_conv_ln=v_b_conv_ln, w_conv_out=v_w_conv_out, b_conv_out=v_b_conv_out, g_q_lora=v_g_q_lora, w_uq=v_w_uq, g_kv_lora=v_g_kv_lora, w_uk=v_w_uk, w_uv=v_w_uv, w_attn_out=v_w_attn_out, w_out=v_w_out, g_ffn=v_g_ffn, w_ffn_gate=v_w_ffn_gate, w_ffn_up=v_w_ffn_up, w_ffn_down=v_w_ffn_down, g_final=v_g_final)

    two_d = lambda a: a.reshape(a.shape[-2:]) if a.ndim >= 2 else a.reshape(1, -1)
    sh = {n: two_d(w_all[n]) for n in SHARDED}
    d = x.shape[-1]
    k_in, c_in = sh["w_in"].shape
    r_co, r_ao, r_wo, r_fd = (sh[n].shape[0] for n in ROW_SHARDED)
    ql, c_uq = sh["w_uq"].shape
    c_uk = sh["w_uk"].shape[1]
    c_ff = sh["w_ffn_gate"].shape[1]
    n_meta, c_meta = sh["meta_tokens"].shape
    n_taps, c_dw = sh["w_dw"].shape
    ch, dff = N_DEV * c_dw, N_DEV * c_ff
    n_lora = 2 * ql + QK_ROPE

    masks = tuple(range(1, N_DEV))
    whole_ref = lambda ref, dev: ref
    gathered = _gather_weights(
        [(_cast_bf16(sh["w_in"], "cast_w_in"), 0, _slot), (sh["meta_tokens"], 1, _col_window(c_meta)),
         (sh["w_dw"], 2, _col_window(c_dw))],
        [SDS((N_DEV, k_in, c_in), BF16), SDS((n_meta, N_DEV * c_meta), F32), SDS((n_taps, ch), F32)])
    me = (4 * lax.axis_index("x") + 2 * lax.axis_index("y") + lax.axis_index("c")).astype(jnp.int32).reshape(1)

    def placed(n, whole, kind, after, into=None):
        return _cast_into(me, sh[n], whole, kind, "cast_" + n, into=into, after=None if into is not None else after)

    same_core = (2, 4, 6)
    coming = {}
    first = gathered[0]
    w_ukv0 = placed("w_uk", SDS((ql, 2 * N_DEV * c_uk), BF16), ("cols", 0), first)
    coming["mixers"] = _spread_start(
        [placed("w_conv_out", SDS((N_DEV * r_co, d), BF16), ("rows",), first),
         placed("w_uq", SDS((ql, N_DEV * c_uq), BF16), ("cols", 0), first),
         placed("w_uv", SDS((ql, 2 * N_DEV * c_uk), BF16), ("cols", N_DEV), first, into=w_ukv0),
         placed("w_attn_out", SDS((N_DEV * r_ao, d), BF16), ("rows",), first),
         placed("w_out", SDS((N_DEV * r_wo, d), BF16), ("rows",), first)],
        mixer_items := [(0, _row_window(r_co)), (1, _col_window(c_uq)), (2, _col_window(c_uk)),
                        (2, _col_window(c_uk, N_DEV * c_uk)), (3, _row_window(r_ao)), (4, _row_window(r_wo))],
        masks=same_core, name="send_weights_mixers") + (mixer_items,)
    second = coming["mixers"][3]
    coming["ffn"] = _spread_start(
        [placed("w_ffn_gate", SDS((N_DEV, d, c_ff), BF16), ("slot",), second),
         placed("w_ffn_up", SDS((N_DEV, d, c_ff), BF16), ("slot",), second),
         placed("w_ffn_down", SDS((N_DEV * r_fd, d), BF16), ("rows",), second)],
        ffn_items := [(0, _slot), (1, _slot), (2, _row_window(r_fd))],
        masks=same_core, name="send_weights_ffn") + (ffn_items,)

    def fetch(group, after):
        send_sems, recv_sems, lands, _, items = coming[group]
        lands = _spread_wait(send_sems, recv_sems, lands, items, after, masks=same_core,
                             name="wait_weights_" + group)
        lands = _swap_with_sibling(lands, items, "pass_weights_" + group)
        if group == "mixers":
            return {"w_conv_out": lands[0], "w_uq": lands[1], "w_ukv": lands[2], "w_attn_out": lands[3],
                    "w_out": lands[4]}
        (w_gu,) = _regroup([lands[0], lands[1]], [((d, 2 * dff), BF16, 2 * dff)], name="unpack_w_ffn_in")
        return {"w_gu": w_gu, "w_ffn_down": lands[2]}

    w_glu, w_lora, w_gate = _regroup(
        [gathered[0]], [((k_in, 2 * ch), BF16, 2 * ch), ((k_in, n_lora + LANES - QK_ROPE), BF16, n_lora),
                        ((k_in, 2 * d), BF16, 2 * d)], name="unpack_w_in",
        after=(coming["mixers"][3], coming["ffn"][3]))
    wts = {"w_glu": w_glu, "w_lora": w_lora, "w_gate": w_gate}
    sp = {n: w_all[n].reshape(1, -1) for n in SMALL}
    dims = {"n_heads": N_DEV * c_uq // QK, "dff": dff}

    recv_shape = lambda n: SDS((N_DEV,) + sh[n].shape, F32 if n in F32_GATHERED else BF16)
    in_flight = {}

    s_sizes = [w_all[n].size for n in SMALL]
    s_offs, n_s = _offsets(s_sizes)
    cat_small = lambda src: jnp.concatenate([src[n].reshape(1, -1) for n in SMALL], axis=1)
    small_names = ("meta_tokens", "w_dw")

    def ship(group, grads):
        if group == "small":
            srcs = [grads["meta_tokens"], grads["w_dw"], cat_small(grads)]
            items = [(0, _col_window(c_meta), 0, _slot), (1, _col_window(c_dw), 1, _slot), (2, whole_ref, 2, _slot)]
            send_sems, recv_sems, srcs, lands, zero = _exchange_start(
                srcs, [recv_shape(n) for n in small_names] + [SDS((N_DEV, 1, n_s), F32)], items, masks=masks,
                name="send_grads_small")
            in_flight[group] = (send_sems, recv_sems, srcs, lands, items)
            return zero
        if group == "input":
            (s_in,) = _regroup([grads["w_glu"], (grads["w_lora"], n_lora), grads["w_gate"]],
                               [((N_DEV, k_in, c_in), BF16, None)], name="pack_g_w_in")
            per_chip = SDS((N_DEV // 2, k_in, c_in), BF16)
            (halves,) = _exchange(
                [s_in], [per_chip],
                [(0, lambda ref, peer, j=j: ref.at[2 * j + peer % 2], 0, lambda ref, dev, j=j: ref.at[j])
                 for j in range(N_DEV // 2)], masks=(1,), name="swap_grads_input")
            chip_sums = _sum_with_sibling(me, s_in, halves, "sum_grads_input")
            send_sems, recv_sems, srcs, lands, zero = _exchange_start(
                [chip_sums], [per_chip],
                items := [(0, lambda ref, peer: ref.at[peer // 2], 0, lambda ref, dev: ref.at[dev // 2])],
                masks=(2, 4, 6), name="send_grads_input")
            in_flight[group] = (send_sems, recv_sems, srcs, lands, items, ("w_in",), [(0, ("chip_slot",))], zero,
                                (2, 4, 6))
            return zero
        if group == "ffn":
            s_gate, s_up = _regroup([grads["w_gu"]], [((N_DEV, d, c_ff), BF16, None)] * 2, name="pack_g_w_ffn_in")
            srcs = [s_gate, s_up, grads["w_ffn_down"]]
            names = ("w_ffn_gate", "w_ffn_up", "w_ffn_down")
            items = [(0, _slot, 0, _slot), (1, _slot, 1, _slot), (2, _row_window(r_fd), 2, _slot)]
            own = [(0, ("slot",)), (1, ("slot",)), (2, ("rows",))]
        else:
            srcs = [grads["w_conv_out"], grads["w_uq"], grads["w_ukv"], grads["w_attn_out"], grads["w_out"]]
            names = ("w_conv_out", "w_uq", "w_uk", "w_uv", "w_attn_out", "w_out")
            items = [(0, _row_window(r_co), 0, _slot), (1, _col_window(c_uq), 1, _slot),
                     (2, _col_window(c_uk), 2, _slot), (2, _col_window(c_uk, N_DEV * c_uk), 3, _slot),
                     (3, _row_window(r_ao), 4, _slot), (4, _row_window(r_wo), 5, _slot)]
            own = [(0, ("rows",)), (1, ("cols", 0)), (2, ("cols", 0)), (2, ("cols", N_DEV)), (3, ("rows",)),
                   (4, ("rows",))]
        send_sems, recv_sems, srcs, lands, zero = _exchange_start(
            srcs, [recv_shape(n) for n in names], items, masks=masks, name="send_grads_" + group)
        in_flight[group] = (send_sems, recv_sems, srcs, lands, items, names, own, zero, masks)
        return zero

    loss, grad_x, _, _ = _local_step(x[0], loss_target[0], gathered[1], gathered[2], wts, sp, ship, fetch, dims)

    by_name = {}

    def update(n, recv, own, kind):
        outs = _adamw(me, recv, own, kind, sh[n], two_d(m_all[n]), two_d(v_all[n]), name="adamw_" + n)
        by_name[n] = [o.reshape(w_all[n].shape) for o in outs]
        return outs[0]

    done = in_flight["input"][7]
    send_sems, recv_sems, srcs, lands, items = in_flight["small"]
    small_srcs, small_recv = _exchange_wait(send_sems, recv_sems, srcs, lands, items, done, masks=masks,
                                            name="wait_grads_small")
    for n, recv, src in zip(small_names, small_recv, small_srcs):
        update(n, recv, src, ("cols", 0))
    outs = _adamw(me, small_recv[2], small_srcs[2], ("whole",), cat_small(w_all), cat_small(m_all), cat_small(v_all),
                  name="adamw_replicated")
    for n, o, s in zip(SMALL, s_offs, s_sizes):
        by_name[n] = [out[:, o:o + s].reshape(w_all[n].shape) for out in outs]
    done = outs[0]
    for group in ("ffn", "mixers", "input"):
        send_sems, recv_sems, srcs, lands, items, names, own, _, group_masks = in_flight[group]
        srcs, lands = _exchange_wait(send_sems, recv_sems, srcs, lands, items, done, masks=group_masks,
                                     name="wait_grads_" + group)
        for n, land, (s, kind) in zip(names, lands, own):
            done = update(n, land, srcs[s], kind)
    result = [[by_name[n][k] for n in WEIGHTS] for k in range(4)]
    loss = lax.psum(loss, ("x", "y", "c"))
    return (loss, grad_x[None], *result[0], *result[1], *result[2], *result[3])
```

```python
import functools

import jax
import jax.numpy as jnp
from jax import lax
from jax.experimental import pallas as pl
from jax.experimental.pallas import tpu as pltpu

F32, BF16 = jnp.float32, jnp.bfloat16
SDS = jax.ShapeDtypeStruct

N_DEV = 8
N_META = 16
BLOCK_Q = 128
CONV_WIDTH = 31
QK_NOPE, QK_ROPE, V_HEAD = 128, 64, 128
QK = QK_NOPE + QK_ROPE
ROPE_THETA = 10000.0
EPS = 1e-6
ADAM_LR, ADAM_B1, ADAM_B2, ADAM_EPS, ADAM_WD, ADAM_STEP = 0.001, 0.9, 0.999, 1e-08, 0.01, 10

LANES = 128
ROW_TILE = 128
PACK_W = 1024
VMEM_LIMIT = 56 * 1024 * 1024

BIG = ("w_in", "w_conv_out", "w_uq", "w_uk", "w_uv", "w_attn_out", "w_out", "w_ffn_gate", "w_ffn_up", "w_ffn_down")
F32_GATHERED = ("meta_tokens", "w_dw")
SHARDED = BIG + F32_GATHERED
ROW_SHARDED = ("w_conv_out", "w_attn_out", "w_out", "w_ffn_down")
SMALL = ("g_mix", "b_glu", "b_gate", "b_dw", "g_conv_ln", "b_conv_ln", "b_conv_out", "g_q_lora", "g_kv_lora",
         "g_ffn", "g_final")
WEIGHTS = ("meta_tokens", "g_mix", "w_in", "b_glu", "b_gate", "w_dw", "b_dw", "g_conv_ln", "b_conv_ln", "w_conv_out",
           "b_conv_out", "g_q_lora", "w_uq", "g_kv_lora", "w_uk", "w_uv", "w_attn_out", "w_out", "g_ffn", "w_ffn_gate",
           "w_ffn_up", "w_ffn_down", "g_final")


def _params():
    return pltpu.CompilerParams(vmem_limit_bytes=VMEM_LIMIT)


def _tile(dim, limit):
    best = None
    t = LANES
    while t <= min(dim, limit):
        if dim % t == 0:
            best = t
        t += LANES
    return best if best is not None else dim


def _mm(a, b, *, mode, out_dtype, name, add=None, after=None):
    if mode == "nn":
        (m, kc), n = a.shape, b.shape[1]
    elif mode == "nt":
        (m, kc), n = a.shape, b.shape[0]
    else:
        (kc, m), n = a.shape, b.shape[1]
    if mode == "tn":
        tm, tn, tk = _tile(m, 1024), _tile(n, 512), kc
    else:
        tm, tn, tk = m, _tile(n, 512), _tile(kc, 1024)
    nk = kc // tk
    if mode == "nn":
        a_spec = pl.BlockSpec((tm, tk), lambda i, j, k: (i, k))
        b_spec = pl.BlockSpec((tk, tn), lambda i, j, k: (k, j))
        dims = (((1,), (0,)), ((), ()))
    elif mode == "nt":
        a_spec = pl.BlockSpec((tm, tk), lambda i, j, k: (i, k))
        b_spec = pl.BlockSpec((tn, tk), lambda i, j, k: (j, k))
        dims = (((1,), (1,)), ((), ()))
    else:
        a_spec = pl.BlockSpec((tk, tm), lambda i, j, k: (k, i))
        b_spec = pl.BlockSpec((tk, tn), lambda i, j, k: (k, j))
        dims = (((0,), (0,)), ((), ()))
    o_spec = pl.BlockSpec((tm, tn), lambda i, j, k: (i, j))
    has_add = add is not None

    def body(*refs):
        if after is not None:
            refs = refs[:-3] + refs[-2:]
        if has_add:
            a_ref, b_ref, add_ref, o_ref, acc_ref = refs
        else:
            a_ref, b_ref, o_ref, acc_ref = refs
        k = pl.program_id(2)
        p = lax.dot_general(a_ref[...], b_ref[...], dims, preferred_element_type=F32)
        if nk == 1:
            o_ref[...] = ((p + add_ref[...]) if has_add else p).astype(o_ref.dtype)
            return

        @pl.when(k == 0)
        def _():
            acc_ref[...] = (p + add_ref[...]) if has_add else p

        @pl.when(jnp.logical_and(k > 0, k < nk - 1))
        def _():
            acc_ref[...] += p

        @pl.when(k == nk - 1)
        def _():
            o_ref[...] = (acc_ref[...] + p).astype(o_ref.dtype)

    in_specs = [a_spec, b_spec] + ([o_spec] if has_add else [])
    args = (a, b) + ((add,) if has_add else ())
    if after is not None:
        in_specs, args = in_specs + [pl.BlockSpec(memory_space=pl.ANY)], args + (after,)
    acc_shape = (tm, tn) if nk > 1 else (8, LANES)
    return pl.pallas_call(
        body, name=name, grid=(m // tm, n // tn, nk), in_specs=in_specs, out_specs=o_spec,
        out_shape=SDS((m, n), out_dtype), scratch_shapes=[pltpu.VMEM(acc_shape, F32)],
        compiler_params=_params(),
    )(*args)


def _rows(width, col=0, tr=ROW_TILE):
    return pl.BlockSpec((tr, width), lambda i: (i, col))


def _whole(arr):
    nd = arr.ndim
    return pl.BlockSpec(arr.shape, lambda i: (0,) * nd)


def _rowwise(fn, ins, outs, accs, *, name, n_rows, tr=ROW_TILE, after=None):
    n_in, n_out = len(ins), len(outs)
    if after is not None:
        ins = list(ins) + [(after, pl.BlockSpec(memory_space=pl.ANY))]

    def body(*refs):
        i = pl.program_id(0)
        res = fn(i, *[r[...] for r in refs[:n_in]])
        refs = refs[:n_in] + refs[len(ins):]
        res = res if isinstance(res, (tuple, list)) else (res,)
        for o_ref, v in zip(refs[n_in:n_in + n_out], res[:n_out]):
            o_ref[...] = v.astype(o_ref.dtype)
        for a_ref, v in zip(refs[n_in + n_out:], res[n_out:]):
            @pl.when(i == 0)
            def _(a_ref=a_ref, v=v):
                a_ref[...] = v

            @pl.when(i > 0)
            def _(a_ref=a_ref, v=v):
                a_ref[...] += v

    acc_specs = [pl.BlockSpec(s.shape, lambda i, nd=len(s.shape): (0,) * nd) for s in accs]
    res = pl.pallas_call(
        body, name=name, grid=(n_rows // tr,),
        in_specs=[s for _, s in ins], out_specs=[s for _, s in outs] + acc_specs,
        out_shape=[s for s, _ in outs] + list(accs), compiler_params=_params(),
    )(*[a for a, _ in ins])
    return res


def _rms(x, g):
    return x * lax.rsqrt(jnp.mean(x * x, axis=-1, keepdims=True) + EPS) * g


def _sigmoid(x):
    return 1.0 / (1.0 + jnp.exp(-x))


def _silu(x):
    return x * _sigmoid(x)


def _rms_fwd(h, g, name):
    t, d = h.shape
    (u,) = _rowwise(lambda i, h, g: _rms(h, g), [(h, _rows(d)), (g, _whole(g))], [(SDS((t, d), BF16), _rows(d))], [],
                    name=name, n_rows=t)
    return u


def _rms_bwd(h, g, du, dres, name):
    t, d = h.shape

    def fn(i, h, g, du, dres):
        _, vjp = jax.vjp(_rms, h, g)
        dh, dg = vjp(du)
        dh = dh + dres
        return dh, dh, dg

    return _rowwise(fn, [(h, _rows(d)), (g, _whole(g)), (du, _rows(d)), (dres, _rows(d))],
                    [(SDS((t, d), F32), _rows(d)), (SDS((t, d), BF16), _rows(d))], [SDS((1, d), F32)],
                    name=name, n_rows=t)


def _conv_fwd(c0, w_dw, b_dw):
    t, ch = c0.shape
    tc = _tile(ch, 256)
    halo = 32
    shift = halo - (CONV_WIDTH - 1)

    def body(x_ref, w_ref, b_ref, o_ref, pad_ref):
        pad_ref[0:halo, :] = jnp.zeros((halo, tc), F32)
        pad_ref[halo:halo + t, :] = x_ref[...]
        for r0 in range(0, t, ROW_TILE):
            acc = jnp.zeros((ROW_TILE, tc), F32) + b_ref[...]
            for j in range(CONV_WIDTH):
                acc = acc + pad_ref[r0 + shift + j:r0 + shift + j + ROW_TILE, :] * w_ref[j:j + 1, :]
            o_ref[r0:r0 + ROW_TILE, :] = acc

    col = lambda i: (0, i)
    return pl.pallas_call(
        body, name="conv_fwd", grid=(ch // tc,),
        in_specs=[pl.BlockSpec((t, tc), col), pl.BlockSpec((CONV_WIDTH, tc), col), pl.BlockSpec((1, tc), col)],
        out_specs=pl.BlockSpec((t, tc), col), out_shape=SDS((t, ch), F32),
        scratch_shapes=[pltpu.VMEM((halo + t, tc), F32)], compiler_params=_params(),
    )(c0, w_dw, b_dw)


def _conv_bwd(dc1, c0, w_dw):
    t, ch = c0.shape
    tc = _tile(ch, 256)
    halo = 32
    shift = halo - (CONV_WIDTH - 1)

    def body(d_ref, x_ref, w_ref, dx_ref, dw_ref, db_ref, xpad_ref, dpad_ref):
        xpad_ref[0:halo, :] = jnp.zeros((halo, tc), F32)
        xpad_ref[halo:halo + t, :] = x_ref[...]
        dpad_ref[0:t, :] = d_ref[...]
        dpad_ref[t:t + halo, :] = jnp.zeros((halo, tc), F32)
        for r0 in range(0, t, ROW_TILE):
            acc = jnp.zeros((ROW_TILE, tc), F32)
            for j in range(CONV_WIDTH):
                off = r0 + (CONV_WIDTH - 1) - j
                acc = acc + dpad_ref[off:off + ROW_TILE, :] * w_ref[j:j + 1, :]
            dx_ref[r0:r0 + ROW_TILE, :] = acc
        for j in range(CONV_WIDTH):
            acc = jnp.zeros((1, tc), F32)
            for r0 in range(0, t, ROW_TILE):
                prod = d_ref[r0:r0 + ROW_TILE, :] * xpad_ref[r0 + shift + j:r0 + shift + j + ROW_TILE, :]
                acc = acc + jnp.sum(prod, axis=0, keepdims=True)
            dw_ref[j:j + 1, :] = acc
        db_ref[...] = jnp.sum(d_ref[...], axis=0, keepdims=True)

    col = lambda i: (0, i)
    return pl.pallas_call(
        body, name="conv_bwd", grid=(ch // tc,),
        in_specs=[pl.BlockSpec((t, tc), col), pl.BlockSpec((t, tc), col), pl.BlockSpec((CONV_WIDTH, tc), col)],
        out_specs=[pl.BlockSpec((t, tc), col), pl.BlockSpec((CONV_WIDTH, tc), col), pl.BlockSpec((1, tc), col)],
        out_shape=[SDS((t, ch), F32), SDS((CONV_WIDTH, ch), F32), SDS((1, ch), F32)],
        scratch_shapes=[pltpu.VMEM((halo + t, tc), F32), pltpu.VMEM((halo + t, tc), F32)], compiler_params=_params(),
    )(dc1, c0, w_dw)


def _rope(x1, x2, cos, sin):
    return x1 * cos - x2 * sin, x1 * sin + x2 * cos


def _attn_prep(q, kv, z_l, kr_col, cos, sin, n_heads):
    t = q.shape[0]
    hn = n_heads * QK_NOPE
    half = QK_ROPE // 2

    def body(q_ref, kv_ref, kr_ref, cos_ref, sin_ref, qo_ref, ko_ref, vo_ref):
        cos, sin = cos_ref[...], sin_ref[...]
        kr = kr_ref[...]
        k1, k2 = _rope(kr[:, 0:half], kr[:, half:QK_ROPE], cos, sin)
        for h in range(n_heads):
            b = h * QK
            q1, q2 = _rope(q_ref[:, b + QK_NOPE:b + QK_NOPE + half], q_ref[:, b + QK_NOPE + half:b + QK], cos, sin)
            qo_ref[h] = jnp.concatenate([q_ref[:, b:b + QK_NOPE], q1, q2], axis=-1).astype(BF16)
            ko_ref[h] = jnp.concatenate([kv_ref[:, h * QK_NOPE:(h + 1) * QK_NOPE], k1, k2], axis=-1).astype(BF16)
            vo_ref[h] = kv_ref[:, hn + h * V_HEAD:hn + (h + 1) * V_HEAD].astype(BF16)

    tr = ROW_TILE
    hm = lambda w: pl.BlockSpec((n_heads, tr, w), lambda i: (0, i, 0))
    return pl.pallas_call(
        body, name="attn_prep", grid=(t // tr,),
        in_specs=[_rows(q.shape[1]), _rows(kv.shape[1]), _rows(LANES, kr_col), _rows(half), _rows(half)],
        out_specs=[hm(QK), hm(QK), hm(V_HEAD)],
        out_shape=[SDS((n_heads, t, QK), BF16), SDS((n_heads, t, QK), BF16), SDS((n_heads, t, V_HEAD), BF16)],
        compiler_params=_params(),
    )(q, kv, z_l, cos, sin)


def _attn_post(dq_hm, dk_hm, dv_hm, cos, sin):
    n_heads, t, _ = dq_hm.shape
    hn = n_heads * QK_NOPE
    half = QK_ROPE // 2

    def unrope(d1, d2, cos, sin):
        return d1 * cos + d2 * sin, d2 * cos - d1 * sin

    def body(dq_ref, dk_ref, dv_ref, cos_ref, sin_ref, qo_ref, kvo_ref, kro_ref):
        cos, sin = cos_ref[...], sin_ref[...]
        dkr = jnp.zeros((ROW_TILE, QK_ROPE), F32)
        for h in range(n_heads):
            dq = dq_ref[h]
            d1, d2 = unrope(dq[:, QK_NOPE:QK_NOPE + half], dq[:, QK_NOPE + half:QK], cos, sin)
            qo_ref[:, h * QK:(h + 1) * QK] = jnp.concatenate([dq[:, 0:QK_NOPE], d1, d2], axis=-1).astype(BF16)
            dk = dk_ref[h]
            kvo_ref[:, h * QK_NOPE:(h + 1) * QK_NOPE] = dk[:, 0:QK_NOPE].astype(BF16)
            kvo_ref[:, hn + h * V_HEAD:hn + (h + 1) * V_HEAD] = dv_ref[h].astype(BF16)
            dkr = dkr + dk[:, QK_NOPE:QK]
        d1, d2 = unrope(dkr[:, 0:half], dkr[:, half:QK_ROPE], cos, sin)
        kro_ref[...] = jnp.concatenate([d1, d2, jnp.zeros((ROW_TILE, LANES - QK_ROPE), F32)], axis=-1)

    tr = ROW_TILE
    hm = lambda w: pl.BlockSpec((n_heads, tr, w), lambda i: (0, i, 0))
    return pl.pallas_call(
        body, name="attn_post", grid=(t // tr,),
        in_specs=[hm(QK), hm(QK), hm(V_HEAD), _rows(half), _rows(half)],
        out_specs=[_rows(n_heads * QK), _rows(2 * hn), _rows(LANES)],
        out_shape=[SDS((t, n_heads * QK), BF16), SDS((t, 2 * hn), BF16), SDS((t, LANES), F32)],
        compiler_params=_params(),
    )(dq_hm, dk_hm, dv_hm, cos, sin)


N_QBLK = 4
_NT = (((1,), (1,)), ((), ()))
_TN = (((0,), (0,)), ((), ()))


def _scores(q, k, r0, scale):
    s = lax.dot_general(q, k, _NT, preferred_element_type=F32) * scale
    row = r0 + lax.broadcasted_iota(jnp.int32, s.shape, 0)
    col = lax.broadcasted_iota(jnp.int32, s.shape, 1)
    return jnp.where(col <= row, s, -jnp.inf)


def _attn_fwd(q_hm, k_hm, v_hm):
    n_heads, t, _ = q_hm.shape
    bq = t // N_QBLK
    scale = QK ** -0.5

    def body(q_ref, k_ref, v_ref, o_ref, lse_ref):
        for i in range(N_QBLK):
            r0, n_k = i * bq, (i + 1) * bq
            s = _scores(q_ref[0, r0:r0 + bq, :], k_ref[0, 0:n_k, :], r0, scale)
            m = jnp.max(s, axis=-1, keepdims=True)
            p = jnp.exp(s - m)
            l = jnp.sum(p, axis=-1, keepdims=True)
            p = (p / l).astype(BF16)
            o_ref[r0:r0 + bq, :] = jnp.dot(p, v_ref[0, 0:n_k, :], preferred_element_type=F32).astype(BF16)
            lse_ref[0, r0:r0 + bq, :] = m + jnp.log(l)

    head = lambda w: pl.BlockSpec((1, t, w), lambda h: (h, 0, 0))
    return pl.pallas_call(
        body, name="attn_fwd", grid=(n_heads,),
        in_specs=[head(QK), head(QK), head(V_HEAD)],
        out_specs=[pl.BlockSpec((t, V_HEAD), lambda h: (0, h)), head(1)],
        out_shape=[SDS((t, n_heads * V_HEAD), BF16), SDS((n_heads, t, 1), F32)],
        compiler_params=_params(),
    )(q_hm, k_hm, v_hm)


def _attn_bwd(q_hm, k_hm, v_hm, lse, d_o):
    n_heads, t, _ = q_hm.shape
    bq = t // N_QBLK
    scale = QK ** -0.5

    def body(q_ref, k_ref, v_ref, lse_ref, do_ref, dq_ref, dk_ref, dv_ref):
        dk_ref[...] = jnp.zeros(dk_ref.shape, F32)
        dv_ref[...] = jnp.zeros(dv_ref.shape, F32)
        for i in range(N_QBLK):
            r0, n_k = i * bq, (i + 1) * bq
            q = q_ref[0, r0:r0 + bq, :]
            k = k_ref[0, 0:n_k, :]
            d_o = do_ref[r0:r0 + bq, :]
            s = _scores(q, k, r0, scale)
            p = jnp.exp(s - lse_ref[0, r0:r0 + bq, :])
            dp = lax.dot_general(d_o, v_ref[0, 0:n_k, :], _NT, preferred_element_type=F32)
            ds = (p * (dp - jnp.sum(dp * p, axis=-1, keepdims=True)) * scale).astype(BF16)
            dq_ref[0, r0:r0 + bq, :] = jnp.dot(ds, k, preferred_element_type=F32)
            dk_ref[0, 0:n_k, :] += lax.dot_general(ds, q, _TN, preferred_element_type=F32)
            dv_ref[0, 0:n_k, :] += lax.dot_general(p.astype(BF16), d_o, _TN, preferred_element_type=F32)

    head = lambda w: pl.BlockSpec((1, t, w), lambda h: (h, 0, 0))
    return pl.pallas_call(
        body, name="attn_bwd", grid=(n_heads,),
        in_specs=[head(QK), head(QK), head(V_HEAD), head(1), pl.BlockSpec((t, V_HEAD), lambda h: (0, h))],
        out_specs=[head(QK), head(QK), head(V_HEAD)],
        out_shape=[SDS((n_heads, t, QK), F32), SDS((n_heads, t, QK), F32), SDS((n_heads, t, V_HEAD), F32)],
        compiler_params=_params(),
    )(q_hm, k_hm, v_hm, lse, d_o)


def _glu(za, zb, ba, bb):
    return (za + ba) * _sigmoid(zb + bb)


def _ln_silu(c, g, b):
    mu = jnp.mean(c, axis=-1, keepdims=True)
    var = jnp.mean(jnp.square(c - mu), axis=-1, keepdims=True)
    return _silu((c - mu) * lax.rsqrt(var + EPS) * g + b)


def _mix(yc, bco, ya, zc, za, bgc, bga):
    return _sigmoid(zc + bgc) * (yc + bco) + _sigmoid(za + bga) * ya


def _swiglu(a, b):
    return _silu(a) * b


def _local_step(x, target, meta, w_dw, wts, sp, ship=None, fetch=None, dims=None):
    wts = dict(wts)
    if ship is None:
        ship = lambda group, grads: jnp.zeros((8, LANES), F32)
    if fetch is None:
        fetch = lambda group, after: {}
    seq, d = x.shape
    length = N_META + seq
    t = -(-length // BLOCK_Q) * BLOCK_Q
    ch = w_dw.shape[1]
    ql = sp["g_q_lora"].shape[1]
    n_heads = dims["n_heads"] if dims else wts["w_uq"].shape[1] // QK
    hn = n_heads * QK_NOPE
    dff = dims["dff"] if dims else wts["w_ffn_down"].shape[0]
    assert sp["g_kv_lora"].shape[1] == ql and ql % LANES == 0 and t % (N_QBLK * 16) == 0
    pad_rows = lambda a: jnp.concatenate([jnp.zeros((N_META, d), F32), a, jnp.zeros((t - length, d), F32)], axis=0)
    h0 = jnp.concatenate([meta, x, jnp.zeros((t - length, d), F32)], axis=0)
    target_p = pad_rows(target)

    pos = jnp.arange(t, dtype=F32)
    inv_freq = ROPE_THETA ** (-jnp.arange(0, QK_ROPE, 2, dtype=F32) / QK_ROPE)
    ang = pos[:, None] * inv_freq[None, :]
    cos, sin = jnp.cos(ang), jnp.sin(ang)

    b_glu_a, b_glu_b = sp["b_glu"][:, :ch], sp["b_glu"][:, ch:]
    b_gate_c, b_gate_a = sp["b_gate"][:, :d], sp["b_gate"][:, d:]
    kr_col = 2 * ql // LANES

    u = _rms_fwd(h0, sp["g_mix"], "rms_mix")
    z_glu = _mm(u, wts["w_glu"], mode="nn", out_dtype=F32, name="mm_z_glu")
    z_l = _mm(u, wts["w_lora"], mode="nn", out_dtype=F32, name="mm_z_lora")
    z_gate = _mm(u, wts["w_gate"], mode="nn", out_dtype=F32, name="mm_z_gate")

    glu_ins = [(z_glu, _rows(ch, 0)), (z_glu, _rows(ch, 1)), (b_glu_a, _whole(b_glu_a)), (b_glu_b, _whole(b_glu_b))]
    (c0,) = _rowwise(lambda i, za, zb, ba, bb: _glu(za, zb, ba, bb), glu_ins, [(SDS((t, ch), F32), _rows(ch))], [],
                     name="glu_fwd", n_rows=t)
    c1 = _conv_fwd(c0, w_dw, sp["b_dw"])
    ln_ins = [(c1, _rows(ch)), (sp["g_conv_ln"], _whole(sp["g_conv_ln"])), (sp["b_conv_ln"], _whole(sp["b_conv_ln"]))]
    (c3,) = _rowwise(lambda i, c, g, b: _ln_silu(c, g, b), ln_ins, [(SDS((t, ch), BF16), _rows(ch))], [],
                     name="ln_silu_fwd", n_rows=t)
    wts.update(fetch("mixers", c3))
    yc = _mm(c3, wts["w_conv_out"], mode="nn", out_dtype=F32, name="mm_conv_out")

    lora_ins = [(z_l, _rows(ql, 0)), (z_l, _rows(ql, 1)), (sp["g_q_lora"], _whole(sp["g_q_lora"])),
                (sp["g_kv_lora"], _whole(sp["g_kv_lora"]))]
    cq, ckv = _rowwise(lambda i, zq, zk, gq, gk: (_rms(zq, gq), _rms(zk, gk)), lora_ins,
                       [(SDS((t, ql), BF16), _rows(ql)), (SDS((t, ql), BF16), _rows(ql))], [],
                       name="lora_norm_fwd", n_rows=t)
    q = _mm(cq, wts["w_uq"], mode="nn", out_dtype=F32, name="mm_q")
    kv = _mm(ckv, wts["w_ukv"], mode="nn", out_dtype=F32, name="mm_kv")
    q_hm, k_hm, v_hm = _attn_prep(q, kv, z_l, kr_col, cos, sin, n_heads)
    o, lse = _attn_fwd(q_hm, k_hm, v_hm)
    ya = _mm(o, wts["w_attn_out"], mode="nn", out_dtype=F32, name="mm_attn_out")

    mix_ins = [(yc, _rows(d)), (sp["b_conv_out"], _whole(sp["b_conv_out"])), (ya, _rows(d)), (z_gate, _rows(d, 0)),
               (z_gate, _rows(d, 1)), (b_gate_c, _whole(b_gate_c)), (b_gate_a, _whole(b_gate_a))]
    (mix,) = _rowwise(lambda i, *a: _mix(*a), mix_ins, [(SDS((t, d), BF16), _rows(d))], [], name="mix_fwd", n_rows=t)
    h1 = _mm(mix, wts["w_out"], mode="nn", out_dtype=F32, name="mm_out", add=h0)

    hn_ = _rms_fwd(h1, sp["g_ffn"], "rms_ffn")
    wts.update(fetch("ffn", hn_))
    ab = _mm(hn_, wts["w_gu"], mode="nn", out_dtype=F32, name="mm_ffn_in")
    tr_ffn = 64
    (f,) = _rowwise(lambda i, a, b: _swiglu(a, b), [(ab, _rows(dff, 0, tr_ffn)), (ab, _rows(dff, 1, tr_ffn))],
                    [(SDS((t, dff), BF16), _rows(dff, 0, tr_ffn))], [], name="swiglu_fwd", n_rows=t, tr=tr_ffn)
    h2 = _mm(f, wts["w_ffn_down"], mode="nn", out_dtype=F32, name="mm_ffn_down", add=h1)

    def head(i, h, g, tgt):
        y, vjp = jax.vjp(_rms, h, g)
        row = i * ROW_TILE + lax.broadcasted_iota(jnp.int32, (ROW_TILE, 1), 0)
        valid = jnp.logical_and(row >= N_META, row < length)
        err = jnp.where(valid, y - tgt, 0.0)
        dh, dg = vjp(err / d)
        loss = 0.5 * jnp.sum(jnp.sum(err * err, axis=-1, keepdims=True), axis=0, keepdims=True) / d
        return dh, dh, dg, jnp.broadcast_to(loss, (1, LANES))

    dh2, dh2_b, g_final, loss_v = _rowwise(
        head, [(h2, _rows(d)), (sp["g_final"], _whole(sp["g_final"])), (target_p, _rows(d))],
        [(SDS((t, d), F32), _rows(d)), (SDS((t, d), BF16), _rows(d))], [SDS((1, d), F32), SDS((1, LANES), F32)],
        name="loss_head", n_rows=t)
    loss = loss_v[0, 0]

    g_ffn_down = _mm(f, dh2_b, mode="tn", out_dtype=BF16, name="mm_g_ffn_down")
    df = _mm(dh2_b, wts["w_ffn_down"], mode="nt", out_dtype=F32, name="mm_d_f")

    def swiglu_bwd(i, a, b, df):
        _, vjp = jax.vjp(_swiglu, a, b)
        da, db = vjp(df)
        return jnp.concatenate([da, db], axis=-1)

    (dab,) = _rowwise(swiglu_bwd, [(ab, _rows(dff, 0, tr_ffn)), (ab, _rows(dff, 1, tr_ffn)), (df, _rows(dff, 0, tr_ffn))],
                      [(SDS((t, 2 * dff), BF16), _rows(2 * dff, 0, tr_ffn))], [], name="swiglu_bwd", n_rows=t, tr=tr_ffn)
    g_gu = _mm(hn_, dab, mode="tn", out_dtype=BF16, name="mm_g_ffn_in")
    dhn = _mm(dab, wts["w_gu"], mode="nt", out_dtype=F32, name="mm_d_hn")
    sent = ship("ffn", {"w_gu": g_gu, "w_ffn_down": g_ffn_down})
    dh1, dh1_b, g_g_ffn = _rms_bwd(h1, sp["g_ffn"] + sent[0, 0], dhn, dh2, "rms_ffn_bwd")

    g_w_out = _mm(mix, dh1_b, mode="tn", out_dtype=BF16, name="mm_g_out")
    dmix = _mm(dh1_b, wts["w_out"], mode="nt", out_dtype=F32, name="mm_d_mix")

    def mix_bwd(i, yc, bco, ya, zc, za, bgc, bga, dmix):
        _, vjp = jax.vjp(_mix, yc, bco, ya, zc, za, bgc, bga)
        dyc, dbco, dya, dzc, dza, dbgc, dbga = vjp(dmix)
        return dyc, dya, jnp.concatenate([dzc, dza], axis=-1), dbco, dbgc, dbga

    dyc, dya, dz_gate, g_b_conv_out, g_bgc, g_bga = _rowwise(
        mix_bwd, mix_ins + [(dmix, _rows(d))],
        [(SDS((t, d), BF16), _rows(d)), (SDS((t, d), BF16), _rows(d)), (SDS((t, 2 * d), BF16), _rows(2 * d))],
        [SDS((1, d), F32)] * 3, name="mix_bwd", n_rows=t)

    g_attn_out = _mm(o, dya, mode="tn", out_dtype=BF16, name="mm_g_attn_out")
    d_o = _mm(dya, wts["w_attn_out"], mode="nt", out_dtype=BF16, name="mm_d_o")
    dq_hm, dk_hm, dv_hm = _attn_bwd(q_hm, k_hm, v_hm, lse, d_o)
    dq, dkv, dkr = _attn_post(dq_hm, dk_hm, dv_hm, cos, sin)
    g_uq = _mm(cq, dq, mode="tn", out_dtype=BF16, name="mm_g_uq")
    g_ukv = _mm(ckv, dkv, mode="tn", out_dtype=BF16, name="mm_g_ukv")
    dcq = _mm(dq, wts["w_uq"], mode="nt", out_dtype=F32, name="mm_d_cq")
    dckv = _mm(dkv, wts["w_ukv"], mode="nt", out_dtype=F32, name="mm_d_ckv")

    def lora_bwd(i, zq, zk, gq, gk, dcq, dckv, dkr):
        _, vq = jax.vjp(_rms, zq, gq)
        _, vk = jax.vjp(_rms, zk, gk)
        dzq, dgq = vq(dcq)
        dzk, dgk = vk(dckv)
        return jnp.concatenate([dzq, dzk, dkr], axis=-1), dgq, dgk

    dz_l, g_g_q, g_g_kv = _rowwise(
        lora_bwd, lora_ins + [(dcq, _rows(ql)), (dckv, _rows(ql)), (dkr, _rows(LANES))],
        [(SDS((t, 2 * ql + LANES), BF16), _rows(2 * ql + LANES))], [SDS((1, ql), F32)] * 2,
        name="lora_norm_bwd", n_rows=t)

    g_conv_out = _mm(c3, dyc, mode="tn", out_dtype=BF16, name="mm_g_conv_out")
    dc3 = _mm(dyc, wts["w_conv_out"], mode="nt", out_dtype=F32, name="mm_d_c3")
    sent = ship("mixers", {"w_conv_out": g_conv_out, "w_uq": g_uq, "w_ukv": g_ukv, "w_attn_out": g_attn_out,
                           "w_out": g_w_out})

    def ln_bwd(i, c, g, b, dc3):
        _, vjp = jax.vjp(_ln_silu, c, g, b)
        return vjp(dc3)

    g_ln_sent = sp["g_conv_ln"] + sent[0, 0]
    dc1, g_g_ln, g_b_ln = _rowwise(
        ln_bwd, [ln_ins[0], (g_ln_sent, _whole(g_ln_sent)), ln_ins[2], (dc3, _rows(ch))],
        [(SDS((t, ch), F32), _rows(ch))], [SDS((1, ch), F32)] * 2, name="ln_silu_bwd", n_rows=t)
    dc0, g_w_dw, g_b_dw = _conv_bwd(dc1, c0, w_dw)

    def glu_bwd(i, za, zb, ba, bb, dc0):
        _, vjp = jax.vjp(_glu, za, zb, ba, bb)
        dza, dzb, dba, dbb = vjp(dc0)
        return jnp.concatenate([dza, dzb], axis=-1), dba, dbb

    dz_glu, g_bga_, g_bgb_ = _rowwise(glu_bwd, glu_ins + [(dc0, _rows(ch))],
                                      [(SDS((t, 2 * ch), BF16), _rows(2 * ch))], [SDS((1, ch), F32)] * 2,
                                      name="glu_bwd", n_rows=t)

    du = _mm(dz_glu, wts["w_glu"], mode="nt", out_dtype=F32, name="mm_d_u0")
    du = _mm(dz_l, wts["w_lora"], mode="nt", out_dtype=F32, name="mm_d_u1", add=du)
    du = _mm(dz_gate, wts["w_gate"], mode="nt", out_dtype=F32, name="mm_d_u2", add=du)
    dh0, _, g_g_mix = _rms_bwd(h0, sp["g_mix"], du, dh1, "rms_mix_bwd")
    big = {"meta_tokens": dh0[:N_META], "w_dw": g_w_dw}
    small = {
        "g_mix": g_g_mix, "b_glu": jnp.concatenate([g_bga_, g_bgb_], axis=1),
        "b_gate": jnp.concatenate([g_bgc, g_bga], axis=1), "b_dw": g_b_dw, "g_conv_ln": g_g_ln, "b_conv_ln": g_b_ln,
        "b_conv_out": g_b_conv_out, "g_q_lora": g_g_q, "g_kv_lora": g_g_kv, "g_ffn": g_g_ffn, "g_final": g_final,
    }
    sent = ship("small", {**big, **small})
    g_glu = _mm(u, dz_glu, mode="tn", out_dtype=BF16, name="mm_g_w_glu", after=sent)
    g_lora = _mm(u, dz_l, mode="tn", out_dtype=BF16, name="mm_g_w_lora", after=sent)
    g_gate = _mm(u, dz_gate, mode="tn", out_dtype=BF16, name="mm_g_w_gate", after=sent)
    big.update({"w_glu": g_glu, "w_lora": g_lora, "w_gate": g_gate})
    ship("input", big)
    return loss, dh0[N_META:length], big, small


def _slot(ref, dev):
    return ref.at[dev]


def _row_window(rows):
    return lambda ref, dev: ref.at[pl.ds(pl.multiple_of(dev * rows, 16), rows)]


def _col_window(width, offset=0):
    return lambda ref, dev: ref.at[:, pl.ds(pl.multiple_of(offset + dev * width, LANES), width)]


def _dev_index(x, y, c):
    return 4 * x + 2 * y + c


def _gather_weights(items, out_shapes):
    srcs = [it[0] for it in items]
    n, n_out = len(srcs), len(out_shapes)

    def body(*refs):
        src, out = refs[:n], refs[n:n + n_out]
        send_sems, recv_sems, local_sems = refs[n + n_out:]
        x, y, c = lax.axis_index("x"), lax.axis_index("y"), lax.axis_index("c")
        me, sibling = (x, y, c), (x, y, 1 - c)
        chips = [(1 - x, y), (x, 1 - y), (1 - x, 1 - y)]

        def place(i, block):
            _, o, window = items[i]
            return window(out[o], _dev_index(*block))

        def copy(k, i, block, to, from_src=False):
            return pltpu.make_async_remote_copy(
                src_ref=src[i] if from_src else place(i, block), dst_ref=place(i, block),
                send_sem=send_sems.at[k * n + i], recv_sem=recv_sems.at[k * n + i],
                device_id=to, device_id_type=pl.DeviceIdType.MESH)

        mine = [pltpu.make_async_copy(src[i], place(i, me), local_sems.at[i]) for i in range(n)]
        first = [copy(0, i, me, sibling, True) for i in range(n)]
        first += [copy(1 + j, i, me, (*chip, c), True) for j, chip in enumerate(chips) for i in range(n)]
        for cp in mine + first:
            cp.start()
        passed = [[copy(4 + j, i, (*chip, c), sibling) for i in range(n)] for j, chip in enumerate(chips)]
        for j, chip in enumerate(chips):
            for i in range(n):
                copy(1 + j, i, (*chip, c), me).wait_recv()
            for cp in passed[j]:
                cp.start()
        for i in range(n):
            copy(0, i, sibling, me).wait_recv()
        for j, chip in enumerate(chips):
            for i in range(n):
                copy(4 + j, i, (*chip, 1 - c), me).wait_recv()
        for cp in first + [cp for row in passed for cp in row]:
            cp.wait_send()
        for cp in mine:
            cp.wait()

    any_spec = pl.BlockSpec(memory_space=pl.ANY)
    return pl.pallas_call(
        body, name="gather_weights", in_specs=[any_spec] * n, out_specs=[any_spec] * n_out, out_shape=out_shapes,
        scratch_shapes=[pltpu.SemaphoreType.DMA((7 * n,)), pltpu.SemaphoreType.DMA((7 * n,)),
                        pltpu.SemaphoreType.DMA((n,))],
    )(*srcs)


def _exchange(srcs, out_shapes, items, *, masks, name):
    n_src, n_out, n = len(srcs), len(out_shapes), len(items)

    def body(*refs):
        src, out = refs[:n_src], refs[n_src:n_src + n_out]
        send_sems, recv_sems = refs[n_src + n_out:]
        remote = _peer_copies(src, out, send_sems, recv_sems, items, masks)
        for cp in remote:
            cp.start()
        for cp in remote:
            cp.wait()

    any_spec = pl.BlockSpec(memory_space=pl.ANY)
    return pl.pallas_call(
        body, name=name, in_specs=[any_spec] * n_src, out_specs=[any_spec] * n_out, out_shape=out_shapes,
        scratch_shapes=[pltpu.SemaphoreType.DMA((len(masks) * n,)), pltpu.SemaphoreType.DMA((len(masks) * n,))],
    )(*srcs)


def _peer_copies(src, out, send_sems, recv_sems, items, masks):
    x, y, c = lax.axis_index("x"), lax.axis_index("y"), lax.axis_index("c")
    me = _dev_index(x, y, c)
    n = len(items)
    copies = []
    for k, mask in enumerate(masks):
        px, py, pc = x ^ ((mask >> 2) & 1), y ^ ((mask >> 1) & 1), c ^ (mask & 1)
        peer = _dev_index(px, py, pc)
        for i, (s, s_win, o, d_win) in enumerate(items):
            copies.append(pltpu.make_async_remote_copy(
                src_ref=s_win(src[s], peer), dst_ref=d_win(out[o], me),
                send_sem=send_sems.at[k * n + i], recv_sem=recv_sems.at[k * n + i],
                device_id=(px, py, pc), device_id_type=pl.DeviceIdType.MESH))
    return copies


def _exchange_start(srcs, out_shapes, items, *, masks, name):
    n_src, n_out, n = len(srcs), len(out_shapes), len(items)
    n_sem = len(masks) * n
    n_buf = n_src + n_out

    def body(*refs):
        src, land = refs[:n_src], refs[n_src:n_buf]
        send_sems, recv_sems = refs[n_buf], refs[n_buf + 1]
        token = refs[-1]
        for cp in _peer_copies(src, land, send_sems, recv_sems, items, masks):
            cp.start()
        token[...] = jnp.zeros_like(token)

    hbm = pl.BlockSpec(memory_space=pltpu.HBM)
    sem = pl.BlockSpec(memory_space=pltpu.SEMAPHORE)
    bufs = [pltpu.with_memory_space_constraint(a, pltpu.HBM) for a in srcs]
    bufs += [pltpu.with_memory_space_constraint(lax.empty(s.shape, s.dtype), pltpu.HBM) for s in out_shapes]
    res = pl.pallas_call(
        body, name=name,
        out_shape=(pltpu.SemaphoreType.DMA((n_sem,)), pltpu.SemaphoreType.DMA((n_sem,)),
                   *[pltpu.HBM(b.shape, b.dtype) for b in bufs], SDS((8, LANES), F32)),
        in_specs=[hbm] * n_buf, out_specs=(sem, sem, *[hbm] * n_buf, pl.BlockSpec(memory_space=pltpu.VMEM)),
        input_output_aliases={i: 2 + i for i in range(n_buf)},
        compiler_params=pltpu.CompilerParams(has_side_effects=pltpu.SideEffectType.DATAFLOW_SIDE_EFFECTING),
    )(*bufs)
    return res[0], res[1], list(res[2:2 + n_src]), list(res[2 + n_src:2 + n_buf]), res[-1]


def _exchange_wait(send_sems, recv_sems, srcs, lands, items, after, *, masks, name):
    n_src, n_out = len(srcs), len(lands)
    n_buf = n_src + n_out

    def body(*refs):
        src, land = refs[:n_src], refs[n_src:n_buf]
        send_sems, recv_sems = refs[n_buf], refs[n_buf + 1]
        for cp in _peer_copies(src, land, send_sems, recv_sems, items, masks):
            cp.wait_send()
            cp.wait_recv()

    hbm = pl.BlockSpec(memory_space=pltpu.HBM)
    sem = pl.BlockSpec(memory_space=pltpu.SEMAPHORE)
    bufs = list(srcs) + list(lands)
    res = pl.pallas_call(
        body, name=name, out_shape=tuple(pltpu.HBM(b.shape, b.dtype) for b in bufs),
        in_specs=[hbm] * n_buf + [sem, sem, pl.BlockSpec(memory_space=pl.ANY)], out_specs=tuple([hbm] * n_buf),
        input_output_aliases={i: i for i in range(n_buf)},
        compiler_params=pltpu.CompilerParams(has_side_effects=pltpu.SideEffectType.DATAFLOW_SIDE_EFFECTING),
    )(*bufs, send_sems, recv_sems, after)
    return list(res[:n_src]), list(res[n_src:])


def _regroup(srcs, dsts, *, name, tr=256, after=()):
    def segments(shape, valid):
        if len(shape) == 3:
            return [(p, shape[2]) for p in range(shape[0])]
        return [(None, valid)]

    src_arrays = [s[0] if isinstance(s, tuple) else s for s in srcs]
    src_valid = [s[1] if isinstance(s, tuple) else s.shape[-1] for s in srcs]
    k_rows = src_arrays[0].shape[-2]
    src_segs = [(i, p, w) for i, a in enumerate(src_arrays) for p, w in segments(a.shape, src_valid[i])]
    dst_segs = [(j, p, w) for j, (shape, _, valid) in enumerate(dsts) for p, w in segments(shape, valid)]
    pieces = []
    si, so, di, do = 0, 0, 0, 0
    while si < len(src_segs) and di < len(dst_segs):
        n = min(src_segs[si][2] - so, dst_segs[di][2] - do)
        pieces.append((src_segs[si][0], src_segs[si][1], so, dst_segs[di][0], dst_segs[di][1], do, n))
        so, do = so + n, do + n
        if so == src_segs[si][2]:
            si, so = si + 1, 0
        if do == dst_segs[di][2]:
            di, do = di + 1, 0
    assert si == len(src_segs) and di == len(dst_segs), "source and destination columns differ in number"
    n_src = len(src_arrays)

    def body(*refs):
        src, dst = refs[:n_src], refs[n_src + len(after):]
        for j, (shape, dtype, valid) in enumerate(dsts):
            if len(shape) == 2 and valid < shape[1]:
                dst[j][:, valid:shape[1]] = jnp.zeros((tr, shape[1] - valid), dtype)
        for i, sp, so, j, dp, do, n in pieces:
            val = src[i][:, so:so + n] if sp is None else src[i][sp, :, so:so + n]
            if dp is None:
                dst[j][:, do:do + n] = val.astype(dst[j].dtype)
            else:
                dst[j][dp, :, do:do + n] = val.astype(dst[j].dtype)

    def spec(shape):
        if len(shape) == 3:
            return pl.BlockSpec((shape[0], tr, shape[2]), lambda i: (0, i, 0))
        return pl.BlockSpec((tr, shape[1]), lambda i: (i, 0))

    n_after = len(after)
    return pl.pallas_call(
        body, name=name, grid=(k_rows // tr,),
        in_specs=[spec(a.shape) for a in src_arrays] + [pl.BlockSpec(memory_space=pl.ANY)] * n_after,
        out_specs=[spec(shape) for shape, _, _ in dsts], out_shape=[SDS(shape, dtype) for shape, dtype, _ in dsts],
        compiler_params=_params(),
    )(*src_arrays, *after)


def _cast_bf16(a, name, after=None):
    r, c = a.shape
    tr = _row_tile(r, c * 4)
    (out,) = _rowwise(lambda i, v: v, [(a, _rows(c, 0, tr))], [(SDS((r, c), BF16), _rows(c, 0, tr))], [], name=name,
                      n_rows=r, tr=tr, after=after)
    return out


def _own_spec(kind, tr, width, n_tiles):
    if kind[0] == "slot":
        return pl.BlockSpec((None, tr, width), lambda i, me: (me[0], i, 0))
    if kind[0] == "chip_slot":
        return pl.BlockSpec((None, tr, width), lambda i, me: (me[0] // 2, i, 0))
    if kind[0] == "rows":
        return pl.BlockSpec((tr, width), lambda i, me: (me[0] * n_tiles + i, 0))
    if kind[0] == "cols":
        return pl.BlockSpec((tr, width), lambda i, me: (i, kind[1] + me[0]))
    return pl.BlockSpec((tr, width), lambda i, me: (i, 0))


def _cast_into(me, a, whole, kind, name, into=None, after=None):
    r, c = a.shape
    tr = _row_tile(r, c * 4)
    extra = [x for x in (into, after) if x is not None]

    def body(me_ref, a_ref, *rest):
        rest[len(extra)][...] = a_ref[...].astype(BF16)

    return pl.pallas_call(
        body, name=name,
        grid_spec=pltpu.PrefetchScalarGridSpec(
            num_scalar_prefetch=1, grid=(r // tr,),
            in_specs=[pl.BlockSpec((tr, c), lambda i, me: (i, 0))] + [pl.BlockSpec(memory_space=pl.ANY)] * len(extra),
            out_specs=_own_spec(kind, tr, c, r // tr)),
        out_shape=whole, input_output_aliases={2: 0} if into is not None else {}, compiler_params=_params(),
    )(me, a, *extra)


def _own_block_copies(land, send_sems, recv_sems, items, masks):
    x, y, c = lax.axis_index("x"), lax.axis_index("y"), lax.axis_index("c")
    me = _dev_index(x, y, c)
    n = len(items)
    copies = []
    for k, mask in enumerate(masks):
        px, py, pc = x ^ ((mask >> 2) & 1), y ^ ((mask >> 1) & 1), c ^ (mask & 1)
        for i, (o, win) in enumerate(items):
            copies.append(pltpu.make_async_remote_copy(
                src_ref=win(land[o], me), dst_ref=win(land[o], me),
                send_sem=send_sems.at[k * n + i], recv_sem=recv_sems.at[k * n + i],
                device_id=(px, py, pc), device_id_type=pl.DeviceIdType.MESH))
    return copies


def _spread_start(lands, items, *, masks, name):
    n_buf, n_sem = len(lands), len(masks) * len(items)

    def body(*refs):
        land, send_sems, recv_sems, token = refs[:n_buf], refs[n_buf], refs[n_buf + 1], refs[-1]
        for cp in _own_block_copies(land, send_sems, recv_sems, items, masks):
            cp.start()
        token[...] = jnp.zeros_like(token)

    hbm = pl.BlockSpec(memory_space=pltpu.HBM)
    sem = pl.BlockSpec(memory_space=pltpu.SEMAPHORE)
    bufs = [pltpu.with_memory_space_constraint(a, pltpu.HBM) for a in lands]
    res = pl.pallas_call(
        body, name=name,
        out_shape=(pltpu.SemaphoreType.DMA((n_sem,)), pltpu.SemaphoreType.DMA((n_sem,)),
                   *[pltpu.HBM(b.shape, b.dtype) for b in bufs], SDS((8, LANES), F32)),
        in_specs=[hbm] * n_buf, out_specs=(sem, sem, *[hbm] * n_buf, pl.BlockSpec(memory_space=pltpu.VMEM)),
        input_output_aliases={i: 2 + i for i in range(n_buf)},
        compiler_params=pltpu.CompilerParams(has_side_effects=pltpu.SideEffectType.DATAFLOW_SIDE_EFFECTING),
    )(*bufs)
    return res[0], res[1], list(res[2:2 + n_buf]), res[-1]


def _swap_with_sibling(lands, items, name):
    n_buf, n = len(lands), len(items)

    def body(*refs):
        land, send_sems, recv_sems = refs[n_buf:2 * n_buf], refs[2 * n_buf], refs[2 * n_buf + 1]
        x, y, c = lax.axis_index("x"), lax.axis_index("y"), lax.axis_index("c")
        copies = []
        for j, (px, py) in enumerate([(0, 0), (0, 1), (1, 0), (1, 1)]):
            block = _dev_index(px, py, c)
            for i, (o, win) in enumerate(items):
                copies.append(pltpu.make_async_remote_copy(
                    src_ref=win(land[o], block), dst_ref=win(land[o], block),
                    send_sem=send_sems.at[j * n + i], recv_sem=recv_sems.at[j * n + i],
                    device_id=(x, y, 1 - c), device_id_type=pl.DeviceIdType.MESH))
        for cp in copies:
            cp.start()
        for cp in copies:
            cp.wait()

    any_spec = pl.BlockSpec(memory_space=pl.ANY)
    return pl.pallas_call(
        body, name=name, in_specs=[any_spec] * n_buf, out_specs=[any_spec] * n_buf,
        out_shape=[SDS(a.shape, a.dtype) for a in lands], input_output_aliases={i: i for i in range(n_buf)},
        scratch_shapes=[pltpu.SemaphoreType.DMA((4 * n,)), pltpu.SemaphoreType.DMA((4 * n,))],
    )(*lands)


def _spread_wait(send_sems, recv_sems, lands, items, after, *, masks, name):
    n_buf = len(lands)

    def body(*refs):
        land, send_sems, recv_sems = refs[:n_buf], refs[n_buf], refs[n_buf + 1]
        for cp in _own_block_copies(land, send_sems, recv_sems, items, masks):
            cp.wait_send()
            cp.wait_recv()

    hbm = pl.BlockSpec(memory_space=pltpu.HBM)
    sem = pl.BlockSpec(memory_space=pltpu.SEMAPHORE)
    res = pl.pallas_call(
        body, name=name, out_shape=tuple(pltpu.HBM(b.shape, b.dtype) for b in lands),
        in_specs=[hbm] * n_buf + [sem, sem, pl.BlockSpec(memory_space=pl.ANY)], out_specs=tuple([hbm] * n_buf),
        input_output_aliases={i: i for i in range(n_buf)},
        compiler_params=pltpu.CompilerParams(has_side_effects=pltpu.SideEffectType.DATAFLOW_SIDE_EFFECTING),
    )(*lands, send_sems, recv_sems, after)
    return list(res)


def _row_tile(rows, row_bytes, limit=2 * 1024 * 1024):
    best = None
    for t in range(16, rows + 1, 16):
        if rows % t == 0 and t * row_bytes <= limit:
            best = t
    return best if best is not None else rows


def _sum_with_sibling(me, slabs, from_sibling, name):
    _, k, c = slabs.shape
    tr = _row_tile(k, c * 4)

    def body(me_ref, a_ref, b_ref, o_ref):
        o_ref[...] = (a_ref[...].astype(F32) + b_ref[...].astype(F32)).astype(BF16)

    by_chip = pl.BlockSpec((None, tr, c), lambda j, i, me: (j, i, 0))
    return pl.pallas_call(
        body, name=name,
        grid_spec=pltpu.PrefetchScalarGridSpec(
            num_scalar_prefetch=1, grid=(N_DEV // 2, k // tr),
            in_specs=[pl.BlockSpec((None, tr, c), lambda j, i, me: (2 * j + me[0] % 2, i, 0)), by_chip],
            out_specs=by_chip),
        out_shape=SDS((N_DEV // 2, k, c), BF16), compiler_params=_params(),
    )(me, slabs, from_sibling)


def _adamw(me, recv, own, own_kind, w, m, v, *, name):
    n_rows, width = w.shape
    tr = _row_tile(n_rows, width * 4, limit=1024 * 1024)
    n_tiles = n_rows // tr
    own_spec = _own_spec(own_kind, tr, width, n_tiles)

    n_slots = recv.shape[0]

    def body(me_ref, r_ref, own_ref, w_ref, m_ref, v_ref, g_ref, d_ref, mo_ref, vo_ref):
        mine = own_ref[...].astype(F32)
        my_slot = me_ref[0] if n_slots == N_DEV else me_ref[0] // 2
        g = None
        for q in range(n_slots):
            term = jnp.where(my_slot == q, mine, r_ref[q].astype(F32))
            g = term if g is None else g + term
        m_new = ADAM_B1 * m_ref[...] + (1.0 - ADAM_B1) * g
        v_new = ADAM_B2 * v_ref[...] + (1.0 - ADAM_B2) * jnp.square(g)
        m_hat = m_new / (1.0 - ADAM_B1 ** ADAM_STEP)
        v_hat = v_new / (1.0 - ADAM_B2 ** ADAM_STEP)
        g_ref[...] = g
        d_ref[...] = -ADAM_LR * (m_hat / (jnp.sqrt(v_hat) + ADAM_EPS) + ADAM_WD * w_ref[...])
        mo_ref[...] = m_new
        vo_ref[...] = v_new

    row = pl.BlockSpec((tr, width), lambda i, me: (i, 0))
    return pl.pallas_call(
        body, name=name,
        grid_spec=pltpu.PrefetchScalarGridSpec(
            num_scalar_prefetch=1, grid=(n_tiles,),
            in_specs=[pl.BlockSpec((n_slots, tr, width), lambda i, me: (0, i, 0)), own_spec, row, row, row],
            out_specs=[row] * 4),
        out_shape=[SDS((n_rows, width), F32)] * 4, compiler_params=_params(),
    )(me, recv, own, w, m, v)


def _offsets(sizes):
    offs, o = [], 0
    for n in sizes:
        offs.append(o)
        o += n
    return offs, o


def kernel(x, meta_tokens, g_mix, w_in, b_glu, b_gate, w_dw, b_dw, g_conv_ln, b_conv_ln, w_conv_out, b_conv_out, g_q_lora, w_uq, g_kv_lora, w_uk, w_uv, w_attn_out, w_out, g_ffn, w_ffn_gate, w_ffn_up, w_ffn_down, g_final, loss_target, m_meta_tokens, m_g_mix, m_w_in, m_b_glu, m_b_gate, m_w_dw, m_b_dw, m_g_conv_ln, m_b_conv_ln, m_w_conv_out, m_b_conv_out, m_g_q_lora, m_w_uq, m_g_kv_lora, m_w_uk, m_w_uv, m_w_attn_out, m_w_out, m_g_ffn, m_w_ffn_gate, m_w_ffn_up, m_w_ffn_down, m_g_final, v_meta_tokens, v_g_mix, v_w_in, v_b_glu, v_b_gate, v_w_dw, v_b_dw, v_g_conv_ln, v_b_conv_ln, v_w_conv_out, v_b_conv_out, v_g_q_lora, v_w_uq, v_g_kv_lora, v_w_uk, v_w_uv, v_w_attn_out, v_w_out, v_g_ffn, v_w_ffn_gate, v_w_ffn_up, v_w_ffn_down, v_g_final):
    w_all = dict(meta_tokens=meta_tokens, g_mix=g_mix, w_in=w_in, b_glu=b_glu, b_gate=b_gate, w_dw=w_dw, b_dw=b_dw, g_conv_ln=g_conv_ln, b_conv_ln=b_conv_ln, w_conv_out=w_conv_out, b_conv_out=b_conv_out, g_q_lora=g_q_lora, w_uq=w_uq, g_kv_lora=g_kv_lora, w_uk=w_uk, w_uv=w_uv, w_attn_out=w_attn_out, w_out=w_out, g_ffn=g_ffn, w_ffn_gate=w_ffn_gate, w_ffn_up=w_ffn_up, w_ffn_down=w_ffn_down, g_final=g_final)
    m_all = dict(meta_tokens=m_meta_tokens, g_mix=m_g_mix, w_in=m_w_in, b_glu=m_b_glu, b_gate=m_b_gate, w_dw=m_w_dw, b_dw=m_b_dw, g_conv_ln=m_g_conv_ln, b_conv_ln=m_b_conv_ln, w_conv_out=m_w_conv_out, b_conv_out=m_b_conv_out, g_q_lora=m_g_q_lora, w_uq=m_w_uq, g_kv_lora=m_g_kv_lora, w_uk=m_w_uk, w_uv=m_w_uv, w_attn_out=m_w_attn_out, w_out=m_w_out, g_ffn=m_g_ffn, w_ffn_gate=m_w_ffn_gate, w_ffn_up=m_w_ffn_up, w_ffn_down=m_w_ffn_down, g_final=m_g_final)
    v_all = dict(meta_tokens=v_meta_tokens, g_mix=v_g_mix, w_in=v_w_in, b_glu=v_b_glu, b_gate=v_b_gate, w_dw=v_w_dw, b_dw=v_b_dw, g_conv_ln=v_g_conv_ln, b_conv_ln=v_b_conv_ln, w_conv_out=v_w_conv_out, b_conv_out=v_b_conv_out, g_q_lora=v_g_q_lora, w_uq=v_w_uq, g_kv_lora=v_g_kv_lora, w_uk=v_w_uk, w_uv=v_w_uv, w_attn_out=v_w_attn_out, w_out=v_w_out, g_ffn=v_g_ffn, w_ffn_gate=v_w_ffn_gate, w_ffn_up=v_w_ffn_up, w_ffn_down=v_w_ffn_down, g_final=v_g_final)

    two_d = lambda a: a.reshape(a.shape[-2:]) if a.ndim >= 2 else a.reshape(1, -1)
    sh = {n: two_d(w_all[n]) for n in SHARDED}
    d = x.shape[-1]
    k_in, c_in = sh["w_in"].shape
    r_co, r_ao, r_wo, r_fd = (sh[n].shape[0] for n in ROW_SHARDED)
    ql, c_uq = sh["w_uq"].shape
    c_uk = sh["w_uk"].shape[1]
    c_ff = sh["w_ffn_gate"].shape[1]
    n_meta, c_meta = sh["meta_tokens"].shape
    n_taps, c_dw = sh["w_dw"].shape
    ch, dff = N_DEV * c_dw, N_DEV * c_ff
    n_lora = 2 * ql + QK_ROPE

    masks = tuple(range(1, N_DEV))
    whole_ref = lambda ref, dev: ref
    gathered = _gather_weights(
        [(_cast_bf16(sh["w_in"], "cast_w_in"), 0, _slot), (sh["meta_tokens"], 1, _col_window(c_meta)),
         (sh["w_dw"], 2, _col_window(c_dw))],
        [SDS((N_DEV, k_in, c_in), BF16), SDS((n_meta, N_DEV * c_meta), F32), SDS((n_taps, ch), F32)])
    me = (4 * lax.axis_index("x") + 2 * lax.axis_index("y") + lax.axis_index("c")).astype(jnp.int32).reshape(1)

    def placed(n, whole, kind, after, into=None):
        return _cast_into(me, sh[n], whole, kind, "cast_" + n, into=into, after=None if into is not None else after)

    same_core = (2, 4, 6)
    coming = {}
    first = gathered[0]
    w_ukv0 = placed("w_uk", SDS((ql, 2 * N_DEV * c_uk), BF16), ("cols", 0), first)
    coming["mixers"] = _spread_start(
        [placed("w_conv_out", SDS((N_DEV * r_co, d), BF16), ("rows",), first),
         placed("w_uq", SDS((ql, N_DEV * c_uq), BF16), ("cols", 0), first),
         placed("w_uv", SDS((ql, 2 * N_DEV * c_uk), BF16), ("cols", N_DEV), first, into=w_ukv0),
         placed("w_attn_out", SDS((N_DEV * r_ao, d), BF16), ("rows",), first),
         placed("w_out", SDS((N_DEV * r_wo, d), BF16), ("rows",), first)],
        mixer_items := [(0, _row_window(r_co)), (1, _col_window(c_uq)), (2, _col_window(c_uk)),
                        (2, _col_window(c_uk, N_DEV * c_uk)), (3, _row_window(r_ao)), (4, _row_window(r_wo))],
        masks=same_core, name="send_weights_mixers") + (mixer_items,)
    second = coming["mixers"][3]
    coming["ffn"] = _spread_start(
        [placed("w_ffn_gate", SDS((N_DEV, d, c_ff), BF16), ("slot",), second),
         placed("w_ffn_up", SDS((N_DEV, d, c_ff), BF16), ("slot",), second),
         placed("w_ffn_down", SDS((N_DEV * r_fd, d), BF16), ("rows",), second)],
        ffn_items := [(0, _slot), (1, _slot), (2, _row_window(r_fd))],
        masks=same_core, name="send_weights_ffn") + (ffn_items,)

    def fetch(group, after):
        send_sems, recv_sems, lands, _, items = coming[group]
        lands = _spread_wait(send_sems, recv_sems, lands, items, after, masks=same_core,
                             name="wait_weights_" + group)
        lands = _swap_with_sibling(lands, items, "pass_weights_" + group)
        if group == "mixers":
            return {"w_conv_out": lands[0], "w_uq": lands[1], "w_ukv": lands[2], "w_attn_out": lands[3],
                    "w_out": lands[4]}
        (w_gu,) = _regroup([lands[0], lands[1]], [((d, 2 * dff), BF16, 2 * dff)], name="unpack_w_ffn_in")
        return {"w_gu": w_gu, "w_ffn_down": lands[2]}

    w_glu, w_lora, w_gate = _regroup(
        [gathered[0]], [((k_in, 2 * ch), BF16, 2 * ch), ((k_in, n_lora + LANES - QK_ROPE), BF16, n_lora),
                        ((k_in, 2 * d), BF16, 2 * d)], name="unpack_w_in",
        after=(coming["mixers"][3], coming["ffn"][3]))
    wts = {"w_glu": w_glu, "w_lora": w_lora, "w_gate": w_gate}
    sp = {n: w_all[n].reshape(1, -1) for n in SMALL}
    dims = {"n_heads": N_DEV * c_uq // QK, "dff": dff}

    recv_shape = lambda n: SDS((N_DEV,) + sh[n].shape, F32 if n in F32_GATHERED else BF16)
    in_flight = {}

    s_sizes = [w_all[n].size for n in SMALL]
    s_offs, n_s = _offsets(s_sizes)
    cat_small = lambda src: jnp.concatenate([src[n].reshape(1, -1) for n in SMALL], axis=1)
    small_names = ("meta_tokens", "w_dw")

    def ship(group, grads):
        if group == "small":
            srcs = [grads["meta_tokens"], grads["w_dw"], cat_small(grads)]
            items = [(0, _col_window(c_meta), 0, _slot), (1, _col_window(c_dw), 1, _slot), (2, whole_ref, 2, _slot)]
            send_sems, recv_sems, srcs, lands, zero = _exchange_start(
                srcs, [recv_shape(n) for n in small_names] + [SDS((N_DEV, 1, n_s), F32)], items, masks=masks,
                name="send_grads_small")
            in_flight[group] = (send_sems, recv_sems, srcs, lands, items)
            return zero
        if group == "input":
            (s_in,) = _regroup([grads["w_glu"], (grads["w_lora"], n_lora), grads["w_gate"]],
                               [((N_DEV, k_in, c_in), BF16, None)], name="pack_g_w_in")
            per_chip = SDS((N_DEV // 2, k_in, c_in), BF16)
            (halves,) = _exchange(
                [s_in], [per_chip],
                [(0, lambda ref, peer, j=j: ref.at[2 * j + peer % 2], 0, lambda ref, dev, j=j: ref.at[j])
                 for j in range(N_DEV // 2)], masks=(1,), name="swap_grads_input")
            chip_sums = _sum_with_sibling(me, s_in, halves, "sum_grads_input")
            send_sems, recv_sems, srcs, lands, zero = _exchange_start(
                [chip_sums], [per_chip],
                items := [(0, lambda ref, peer: ref.at[peer // 2], 0, lambda ref, dev: ref.at[dev // 2])],
                masks=(2, 4, 6), name="send_grads_input")
            in_flight[group] = (send_sems, recv_sems, srcs, lands, items, ("w_in",), [(0, ("chip_slot",))], zero,
                                (2, 4, 6))
            return zero
        if group == "ffn":
            s_gate, s_up = _regroup([grads["w_gu"]], [((N_DEV, d, c_ff), BF16, None)] * 2, name="pack_g_w_ffn_in")
            srcs = [s_gate, s_up, grads["w_ffn_down"]]
            names = ("w_ffn_gate", "w_ffn_up", "w_ffn_down")
            items = [(0, _slot, 0, _slot), (1, _slot, 1, _slot), (2, _row_window(r_fd), 2, _slot)]
            own = [(0, ("slot",)), (1, ("slot",)), (2, ("rows",))]
        else:
            srcs = [grads["w_conv_out"], grads["w_uq"], grads["w_ukv"], grads["w_attn_out"], grads["w_out"]]
            names = ("w_conv_out", "w_uq", "w_uk", "w_uv", "w_attn_out", "w_out")
            items = [(0, _row_window(r_co), 0, _slot), (1, _col_window(c_uq), 1, _slot),
                     (2, _col_window(c_uk), 2, _slot), (2, _col_window(c_uk, N_DEV * c_uk), 3, _slot),
                     (3, _row_window(r_ao), 4, _slot), (4, _row_window(r_wo), 5, _slot)]
            own = [(0, ("rows",)), (1, ("cols", 0)), (2, ("cols", 0)), (2, ("cols", N_DEV)), (3, ("rows",)),
                   (4, ("rows",))]
        send_sems, recv_sems, srcs, lands, zero = _exchange_start(
            srcs, [recv_shape(n) for n in names], items, masks=masks, name="send_grads_" + group)
        in_flight[group] = (send_sems, recv_sems, srcs, lands, items, names, own, zero, masks)
        return zero

    loss, grad_x, _, _ = _local_step(x[0], loss_target[0], gathered[1], gathered[2], wts, sp, ship, fetch, dims)

    by_name = {}

    def update(n, recv, own, kind):
        outs = _adamw(me, recv, own, kind, sh[n], two_d(m_all[n]), two_d(v_all[n]), name="adamw_" + n)
        by_name[n] = [o.reshape(w_all[n].shape) for o in outs]
        return outs[0]

    done = in_flight["input"][7]
    send_sems, recv_sems, srcs, lands, items = in_flight["small"]
    small_srcs, small_recv = _exchange_wait(send_sems, recv_sems, srcs, lands, items, done, masks=masks,
                                            name="wait_grads_small")
    for n, recv, src in zip(small_names, small_recv, small_srcs):
        update(n, recv, src, ("cols", 0))
    outs = _adamw(me, small_recv[2], small_srcs[2], ("whole",), cat_small(w_all), cat_small(m_all), cat_small(v_all),
                  name="adamw_replicated")
    for n, o, s in zip(SMALL, s_offs, s_sizes):
        by_name[n] = [out[:, o:o + s].reshape(w_all[n].shape) for out in outs]
    done = outs[0]
    for group in ("ffn", "mixers", "input"):
        send_sems, recv_sems, srcs, lands, items, names, own, _, group_masks = in_flight[group]
        srcs, lands = _exchange_wait(send_sems, recv_sems, srcs, lands, items, done, masks=group_masks,
                                     name="wait_grads_" + group)
        for n, land, (s, kind) in zip(names, lands, own):
            done = update(n, land, srcs[s], kind)
    result = [[by_name[n][k] for n in WEIGHTS] for k in range(4)]
    loss = lax.psum(loss, ("x", "y", "c"))
    return (loss, grad_x[None], *result[0], *result[1], *result[2], *result[3])
```

```python
import functools

import jax
import jax.numpy as jnp
from jax import lax
from jax.experimental import pallas as pl
from jax.experimental.pallas import tpu as pltpu

F32, BF16 = jnp.float32, jnp.bfloat16
SDS = jax.ShapeDtypeStruct

N_DEV = 8
N_META = 16
BLOCK_Q = 128
CONV_WIDTH = 31
QK_NOPE, QK_ROPE, V_HEAD = 128, 64, 128
QK = QK_NOPE + QK_ROPE
ROPE_THETA = 10000.0
EPS = 1e-6
ADAM_LR, ADAM_B1, ADAM_B2, ADAM_EPS, ADAM_WD, ADAM_STEP = 0.001, 0.9, 0.999, 1e-08, 0.01, 10

LANES = 128
ROW_TILE = 128
PACK_W = 1024
VMEM_LIMIT = 56 * 1024 * 1024

BIG = ("w_in", "w_conv_out", "w_uq", "w_uk", "w_uv", "w_attn_out", "w_out", "w_ffn_gate", "w_ffn_up", "w_ffn_down")
F32_GATHERED = ("meta_tokens", "w_dw")
SHARDED = BIG + F32_GATHERED
ROW_SHARDED = ("w_conv_out", "w_attn_out", "w_out", "w_ffn_down")
SMALL = ("g_mix", "b_glu", "b_gate", "b_dw", "g_conv_ln", "b_conv_ln", "b_conv_out", "g_q_lora", "g_kv_lora",
         "g_ffn", "g_final")
WEIGHTS = ("meta_tokens", "g_mix", "w_in", "b_glu", "b_gate", "w_dw", "b_dw", "g_conv_ln", "b_conv_ln", "w_conv_out",
           "b_conv_out", "g_q_lora", "w_uq", "g_kv_lora", "w_uk", "w_uv", "w_attn_out", "w_out", "g_ffn", "w_ffn_gate",
           "w_ffn_up", "w_ffn_down", "g_final")


def _params():
    return pltpu.CompilerParams(vmem_limit_bytes=VMEM_LIMIT)


def _tile(dim, limit):
    best = None
    t = LANES
    while t <= min(dim, limit):
        if dim % t == 0:
            best = t
        t += LANES
    return best if best is not None else dim


def _mm(a, b, *, mode, out_dtype, name, add=None, after=None):
    if mode == "nn":
        (m, kc), n = a.shape, b.shape[1]
    elif mode == "nt":
        (m, kc), n = a.shape, b.shape[0]
    else:
        (kc, m), n = a.shape, b.shape[1]
    if mode == "tn":
        tm, tn, tk = _tile(m, 1024), _tile(n, 512), kc
    else:
        tm, tn, tk = m, _tile(n, 512), _tile(kc, 2048)
    nk = kc // tk
    if mode == "nn":
        a_spec = pl.BlockSpec((tm, tk), lambda i, j, k: (i, k))
        b_spec = pl.BlockSpec((tk, tn), lambda i, j, k: (k, j))
        dims = (((1,), (0,)), ((), ()))
    elif mode == "nt":
        a_spec = pl.BlockSpec((tm, tk), lambda i, j, k: (i, k))
        b_spec = pl.BlockSpec((tn, tk), lambda i, j, k: (j, k))
        dims = (((1,), (1,)), ((), ()))
    else:
        a_spec = pl.BlockSpec((tk, tm), lambda i, j, k: (k, i))
        b_spec = pl.BlockSpec((tk, tn), lambda i, j, k: (k, j))
        dims = (((0,), (0,)), ((), ()))
    o_spec = pl.BlockSpec((tm, tn), lambda i, j, k: (i, j))
    has_add = add is not None

    def body(*refs):
        if after is not None:
            refs = refs[:-3] + refs[-2:]
        if has_add:
            a_ref, b_ref, add_ref, o_ref, acc_ref = refs
        else:
            a_ref, b_ref, o_ref, acc_ref = refs
        k = pl.program_id(2)
        p = lax.dot_general(a_ref[...], b_ref[...], dims, preferred_element_type=F32)
        if nk == 1:
            o_ref[...] = ((p + add_ref[...]) if has_add else p).astype(o_ref.dtype)
            return

        @pl.when(k == 0)
        def _():
            acc_ref[...] = (p + add_ref[...]) if has_add else p

        @pl.when(jnp.logical_and(k > 0, k < nk - 1))
        def _():
            acc_ref[...] += p

        @pl.when(k == nk - 1)
        def _():
            o_ref[...] = (acc_ref[...] + p).astype(o_ref.dtype)

    in_specs = [a_spec, b_spec] + ([o_spec] if has_add else [])
    args = (a, b) + ((add,) if has_add else ())
    if after is not None:
        in_specs, args = in_specs + [pl.BlockSpec(memory_space=pl.ANY)], args + (after,)
    acc_shape = (tm, tn) if nk > 1 else (8, LANES)
    return pl.pallas_call(
        body, name=name, grid=(m // tm, n // tn, nk), in_specs=in_specs, out_specs=o_spec,
        out_shape=SDS((m, n), out_dtype), scratch_shapes=[pltpu.VMEM(acc_shape, F32)],
        compiler_params=_params(),
    )(*args)


def _rows(width, col=0, tr=ROW_TILE):
    return pl.BlockSpec((tr, width), lambda i: (i, col))


def _whole(arr):
    nd = arr.ndim
    return pl.BlockSpec(arr.shape, lambda i: (0,) * nd)


def _rowwise(fn, ins, outs, accs, *, name, n_rows, tr=ROW_TILE, after=None):
    n_in, n_out = len(ins), len(outs)
    if after is not None:
        ins = list(ins) + [(after, pl.BlockSpec(memory_space=pl.ANY))]

    def body(*refs):
        i = pl.program_id(0)
        res = fn(i, *[r[...] for r in refs[:n_in]])
        refs = refs[:n_in] + refs[len(ins):]
        res = res if isinstance(res, (tuple, list)) else (res,)
        for o_ref, v in zip(refs[n_in:n_in + n_out], res[:n_out]):
            o_ref[...] = v.astype(o_ref.dtype)
        for a_ref, v in zip(refs[n_in + n_out:], res[n_out:]):
            @pl.when(i == 0)
            def _(a_ref=a_ref, v=v):
                a_ref[...] = v

            @pl.when(i > 0)
            def _(a_ref=a_ref, v=v):
                a_ref[...] += v

    acc_specs = [pl.BlockSpec(s.shape, lambda i, nd=len(s.shape): (0,) * nd) for s in accs]
    res = pl.pallas_call(
        body, name=name, grid=(n_rows // tr,),
        in_specs=[s for _, s in ins], out_specs=[s for _, s in outs] + acc_specs,
        out_shape=[s for s, _ in outs] + list(accs), compiler_params=_params(),
    )(*[a for a, _ in ins])
    return res


def _rms(x, g):
    return x * lax.rsqrt(jnp.mean(x * x, axis=-1, keepdims=True) + EPS) * g


def _sigmoid(x):
    return 1.0 / (1.0 + jnp.exp(-x))


def _silu(x):
    return x * _sigmoid(x)


def _rms_fwd(h, g, name):
    t, d = h.shape
    (u,) = _rowwise(lambda i, h, g: _rms(h, g), [(h, _rows(d)), (g, _whole(g))], [(SDS((t, d), BF16), _rows(d))], [],
                    name=name, n_rows=t)
    return u


def _rms_bwd(h, g, du, dres, name):
    t, d = h.shape

    def fn(i, h, g, du, dres):
        _, vjp = jax.vjp(_rms, h, g)
        dh, dg = vjp(du)
        dh = dh + dres
        return dh, dh, dg

    return _rowwise(fn, [(h, _rows(d)), (g, _whole(g)), (du, _rows(d)), (dres, _rows(d))],
                    [(SDS((t, d), F32), _rows(d)), (SDS((t, d), BF16), _rows(d))], [SDS((1, d), F32)],
                    name=name, n_rows=t)


def _conv_fwd(c0, w_dw, b_dw):
    t, ch = c0.shape
    tc = _tile(ch, 256)
    halo = 32
    shift = halo - (CONV_WIDTH - 1)

    def body(x_ref, w_ref, b_ref, o_ref, pad_ref):
        pad_ref[0:halo, :] = jnp.zeros((halo, tc), F32)
        pad_ref[halo:halo + t, :] = x_ref[...]
        for r0 in range(0, t, ROW_TILE):
            acc = jnp.zeros((ROW_TILE, tc), F32) + b_ref[...]
            for j in range(CONV_WIDTH):
                acc = acc + pad_ref[r0 + shift + j:r0 + shift + j + ROW_TILE, :] * w_ref[j:j + 1, :]
            o_ref[r0:r0 + ROW_TILE, :] = acc

    col = lambda i: (0, i)
    return pl.pallas_call(
        body, name="conv_fwd", grid=(ch // tc,),
        in_specs=[pl.BlockSpec((t, tc), col), pl.BlockSpec((CONV_WIDTH, tc), col), pl.BlockSpec((1, tc), col)],
        out_specs=pl.BlockSpec((t, tc), col), out_shape=SDS((t, ch), F32),
        scratch_shapes=[pltpu.VMEM((halo + t, tc), F32)], compiler_params=_params(),
    )(c0, w_dw, b_dw)


def _conv_bwd(dc1, c0, w_dw):
    t, ch = c0.shape
    tc = _tile(ch, 256)
    halo = 32
    shift = halo - (CONV_WIDTH - 1)

    def body(d_ref, x_ref, w_ref, dx_ref, dw_ref, db_ref, xpad_ref, dpad_ref):
        xpad_ref[0:halo, :] = jnp.zeros((halo, tc), F32)
        xpad_ref[halo:halo + t, :] = x_ref[...]
        dpad_ref[0:t, :] = d_ref[...]
        dpad_ref[t:t + halo, :] = jnp.zeros((halo, tc), F32)
        for r0 in range(0, t, ROW_TILE):
            acc = jnp.zeros((ROW_TILE, tc), F32)
            for j in range(CONV_WIDTH):
                off = r0 + (CONV_WIDTH - 1) - j
                acc = acc + dpad_ref[off:off + ROW_TILE, :] * w_ref[j:j + 1, :]
            dx_ref[r0:r0 + ROW_TILE, :] = acc
        for j in range(CONV_WIDTH):
            acc = jnp.zeros((1, tc), F32)
            for r0 in range(0, t, ROW_TILE):
                prod = d_ref[r0:r0 + ROW_TILE, :] * xpad_ref[r0 + shift + j:r0 + shift + j + ROW_TILE, :]
                acc = acc + jnp.sum(prod, axis=0, keepdims=True)
            dw_ref[j:j + 1, :] = acc
        db_ref[...] = jnp.sum(d_ref[...], axis=0, keepdims=True)

    col = lambda i: (0, i)
    return pl.pallas_call(
        body, name="conv_bwd", grid=(ch // tc,),
        in_specs=[pl.BlockSpec((t, tc), col), pl.BlockSpec((t, tc), col), pl.BlockSpec((CONV_WIDTH, tc), col)],
        out_specs=[pl.BlockSpec((t, tc), col), pl.BlockSpec((CONV_WIDTH, tc), col), pl.BlockSpec((1, tc), col)],
        out_shape=[SDS((t, ch), F32), SDS((CONV_WIDTH, ch), F32), SDS((1, ch), F32)],
        scratch_shapes=[pltpu.VMEM((halo + t, tc), F32), pltpu.VMEM((halo + t, tc), F32)], compiler_params=_params(),
    )(dc1, c0, w_dw)


def _rope(x1, x2, cos, sin):
    return x1 * cos - x2 * sin, x1 * sin + x2 * cos


def _attn_prep(q, kv, z_l, kr_col, cos, sin, n_heads):
    t = q.shape[0]
    hn = n_heads * QK_NOPE
    half = QK_ROPE // 2

    def body(q_ref, kv_ref, kr_ref, cos_ref, sin_ref, qo_ref, ko_ref, vo_ref):
        cos, sin = cos_ref[...], sin_ref[...]
        kr = kr_ref[...]
        k1, k2 = _rope(kr[:, 0:half], kr[:, half:QK_ROPE], cos, sin)
        for h in range(n_heads):
            b = h * QK
            q1, q2 = _rope(q_ref[:, b + QK_NOPE:b + QK_NOPE + half], q_ref[:, b + QK_NOPE + half:b + QK], cos, sin)
            qo_ref[h] = jnp.concatenate([q_ref[:, b:b + QK_NOPE], q1, q2], axis=-1).astype(BF16)
            ko_ref[h] = jnp.concatenate([kv_ref[:, h * QK_NOPE:(h + 1) * QK_NOPE], k1, k2], axis=-1).astype(BF16)
            vo_ref[h] = kv_ref[:, hn + h * V_HEAD:hn + (h + 1) * V_HEAD].astype(BF16)

    tr = ROW_TILE
    hm = lambda w: pl.BlockSpec((n_heads, tr, w), lambda i: (0, i, 0))
    return pl.pallas_call(
        body, name="attn_prep", grid=(t // tr,),
        in_specs=[_rows(q.shape[1]), _rows(kv.shape[1]), _rows(LANES, kr_col), _rows(half), _rows(half)],
        out_specs=[hm(QK), hm(QK), hm(V_HEAD)],
        out_shape=[SDS((n_heads, t, QK), BF16), SDS((n_heads, t, QK), BF16), SDS((n_heads, t, V_HEAD), BF16)],
        compiler_params=_params(),
    )(q, kv, z_l, cos, sin)


def _attn_post(dq_hm, dk_hm, dv_hm, cos, sin):
    n_heads, t, _ = dq_hm.shape
    hn = n_heads * QK_NOPE
    half = QK_ROPE // 2

    def unrope(d1, d2, cos, sin):
        return d1 * cos + d2 * sin, d2 * cos - d1 * sin

    def body(dq_ref, dk_ref, dv_ref, cos_ref, sin_ref, qo_ref, kvo_ref, kro_ref):
        cos, sin = cos_ref[...], sin_ref[...]
        dkr = jnp.zeros((ROW_TILE, QK_ROPE), F32)
        for h in range(n_heads):
            dq = dq_ref[h]
            d1, d2 = unrope(dq[:, QK_NOPE:QK_NOPE + half], dq[:, QK_NOPE + half:QK], cos, sin)
            qo_ref[:, h * QK:(h + 1) * QK] = jnp.concatenate([dq[:, 0:QK_NOPE], d1, d2], axis=-1).astype(BF16)
            dk = dk_ref[h]
            kvo_ref[:, h * QK_NOPE:(h + 1) * QK_NOPE] = dk[:, 0:QK_NOPE].astype(BF16)
            kvo_ref[:, hn + h * V_HEAD:hn + (h + 1) * V_HEAD] = dv_ref[h].astype(BF16)
            dkr = dkr + dk[:, QK_NOPE:QK]
        d1, d2 = unrope(dkr[:, 0:half], dkr[:, half:QK_ROPE], cos, sin)
        kro_ref[...] = jnp.concatenate([d1, d2, jnp.zeros((ROW_TILE, LANES - QK_ROPE), F32)], axis=-1)

    tr = ROW_TILE
    hm = lambda w: pl.BlockSpec((n_heads, tr, w), lambda i: (0, i, 0))
    return pl.pallas_call(
        body, name="attn_post", grid=(t // tr,),
        in_specs=[hm(QK), hm(QK), hm(V_HEAD), _rows(half), _rows(half)],
        out_specs=[_rows(n_heads * QK), _rows(2 * hn), _rows(LANES)],
        out_shape=[SDS((t, n_heads * QK), BF16), SDS((t, 2 * hn), BF16), SDS((t, LANES), F32)],
        compiler_params=_params(),
    )(dq_hm, dk_hm, dv_hm, cos, sin)


N_QBLK = 4
_NT = (((1,), (1,)), ((), ()))
_TN = (((0,), (0,)), ((), ()))


def _scores(q, k, r0, scale):
    s = lax.dot_general(q, k, _NT, preferred_element_type=F32) * scale
    row = r0 + lax.broadcasted_iota(jnp.int32, s.shape, 0)
    col = lax.broadcasted_iota(jnp.int32, s.shape, 1)
    return jnp.where(col <= row, s, -jnp.inf)


def _attn_fwd(q_hm, k_hm, v_hm):
    n_heads, t, _ = q_hm.shape
    bq = t // N_QBLK
    scale = QK ** -0.5

    def body(q_ref, k_ref, v_ref, o_ref, lse_ref):
        for i in range(N_QBLK):
            r0, n_k = i * bq, (i + 1) * bq
            s = _scores(q_ref[0, r0:r0 + bq, :], k_ref[0, 0:n_k, :], r0, scale)
            m = jnp.max(s, axis=-1, keepdims=True)
            p = jnp.exp(s - m)
            l = jnp.sum(p, axis=-1, keepdims=True)
            p = (p / l).astype(BF16)
            o_ref[r0:r0 + bq, :] = jnp.dot(p, v_ref[0, 0:n_k, :], preferred_element_type=F32).astype(BF16)
            lse_ref[0, r0:r0 + bq, :] = m + jnp.log(l)

    head = lambda w: pl.BlockSpec((1, t, w), lambda h: (h, 0, 0))
    return pl.pallas_call(
        body, name="attn_fwd", grid=(n_heads,),
        in_specs=[head(QK), head(QK), head(V_HEAD)],
        out_specs=[pl.BlockSpec((t, V_HEAD), lambda h: (0, h)), head(1)],
        out_shape=[SDS((t, n_heads * V_HEAD), BF16), SDS((n_heads, t, 1), F32)],
        compiler_params=_params(),
    )(q_hm, k_hm, v_hm)


def _attn_bwd(q_hm, k_hm, v_hm, lse, d_o):
    n_heads, t, _ = q_hm.shape
    bq = t // N_QBLK
    scale = QK ** -0.5

    def body(q_ref, k_ref, v_ref, lse_ref, do_ref, dq_ref, dk_ref, dv_ref):
        dk_ref[...] = jnp.zeros(dk_ref.shape, F32)
        dv_ref[...] = jnp.zeros(dv_ref.shape, F32)
        for i in range(N_QBLK):
            r0, n_k = i * bq, (i + 1) * bq
            q = q_ref[0, r0:r0 + bq, :]
            k = k_ref[0, 0:n_k, :]
            d_o = do_ref[r0:r0 + bq, :]
            s = _scores(q, k, r0, scale)
            p = jnp.exp(s - lse_ref[0, r0:r0 + bq, :])
            dp = lax.dot_general(d_o, v_ref[0, 0:n_k, :], _NT, preferred_element_type=F32)
            ds = (p * (dp - jnp.sum(dp * p, axis=-1, keepdims=True)) * scale).astype(BF16)
            dq_ref[0, r0:r0 + bq, :] = jnp.dot(ds, k, preferred_element_type=F32)
            dk_ref[0, 0:n_k, :] += lax.dot_general(ds, q, _TN, preferred_element_type=F32)
            dv_ref[0, 0:n_k, :] += lax.dot_general(p.astype(BF16), d_o, _TN, preferred_element_type=F32)

    head = lambda w: pl.BlockSpec((1, t, w), lambda h: (h, 0, 0))
    return pl.pallas_call(
        body, name="attn_bwd", grid=(n_heads,),
        in_specs=[head(QK), head(QK), head(V_HEAD), head(1), pl.BlockSpec((t, V_HEAD), lambda h: (0, h))],
        out_specs=[head(QK), head(QK), head(V_HEAD)],
        out_shape=[SDS((n_heads, t, QK), F32), SDS((n_heads, t, QK), F32), SDS((n_heads, t, V_HEAD), F32)],
        compiler_params=_params(),
    )(q_hm, k_hm, v_hm, lse, d_o)


def _glu(za, zb, ba, bb):
    return (za + ba) * _sigmoid(zb + bb)


def _ln_silu(c, g, b):
    mu = jnp.mean(c, axis=-1, keepdims=True)
    var = jnp.mean(jnp.square(c - mu), axis=-1, keepdims=True)
    return _silu((c - mu) * lax.rsqrt(var + EPS) * g + b)


def _mix(yc, bco, ya, zc, za, bgc, bga):
    return _sigmoid(zc + bgc) * (yc + bco) + _sigmoid(za + bga) * ya


def _swiglu(a, b):
    return _silu(a) * b


def _local_step(x, target, meta, w_dw, wts, sp, ship=None, fetch=None, dims=None):
    wts = dict(wts)
    if ship is None:
        ship = lambda group, grads: jnp.zeros((8, LANES), F32)
    if fetch is None:
        fetch = lambda group, after: {}
    seq, d = x.shape
    length = N_META + seq
    t = -(-length // BLOCK_Q) * BLOCK_Q
    ch = w_dw.shape[1]
    ql = sp["g_q_lora"].shape[1]
    n_heads = dims["n_heads"] if dims else wts["w_uq"].shape[1] // QK
    hn = n_heads * QK_NOPE
    dff = dims["dff"] if dims else wts["w_ffn_down"].shape[0]
    assert sp["g_kv_lora"].shape[1] == ql and ql % LANES == 0 and t % (N_QBLK * 16) == 0
    pad_rows = lambda a: jnp.concatenate([jnp.zeros((N_META, d), F32), a, jnp.zeros((t - length, d), F32)], axis=0)
    h0 = jnp.concatenate([meta, x, jnp.zeros((t - length, d), F32)], axis=0)
    target_p = pad_rows(target)

    pos = jnp.arange(t, dtype=F32)
    inv_freq = ROPE_THETA ** (-jnp.arange(0, QK_ROPE, 2, dtype=F32) / QK_ROPE)
    ang = pos[:, None] * inv_freq[None, :]
    cos, sin = jnp.cos(ang), jnp.sin(ang)

    b_glu_a, b_glu_b = sp["b_glu"][:, :ch], sp["b_glu"][:, ch:]
    b_gate_c, b_gate_a = sp["b_gate"][:, :d], sp["b_gate"][:, d:]
    kr_col = 2 * ql // LANES

    u = _rms_fwd(h0, sp["g_mix"], "rms_mix")
    z_glu = _mm(u, wts["w_glu"], mode="nn", out_dtype=F32, name="mm_z_glu")
    z_l = _mm(u, wts["w_lora"], mode="nn", out_dtype=F32, name="mm_z_lora")
    z_gate = _mm(u, wts["w_gate"], mode="nn", out_dtype=F32, name="mm_z_gate")

    glu_ins = [(z_glu, _rows(ch, 0)), (z_glu, _rows(ch, 1)), (b_glu_a, _whole(b_glu_a)), (b_glu_b, _whole(b_glu_b))]
    (c0,) = _rowwise(lambda i, za, zb, ba, bb: _glu(za, zb, ba, bb), glu_ins, [(SDS((t, ch), F32), _rows(ch))], [],
                     name="glu_fwd", n_rows=t)
    c1 = _conv_fwd(c0, w_dw, sp["b_dw"])
    ln_ins = [(c1, _rows(ch)), (sp["g_conv_ln"], _whole(sp["g_conv_ln"])), (sp["b_conv_ln"], _whole(sp["b_conv_ln"]))]
    (c3,) = _rowwise(lambda i, c, g, b: _ln_silu(c, g, b), ln_ins, [(SDS((t, ch), BF16), _rows(ch))], [],
                     name="ln_silu_fwd", n_rows=t)
    wts.update(fetch("mixers", c3))
    yc = _mm(c3, wts["w_conv_out"], mode="nn", out_dtype=F32, name="mm_conv_out")

    lora_ins = [(z_l, _rows(ql, 0)), (z_l, _rows(ql, 1)), (sp["g_q_lora"], _whole(sp["g_q_lora"])),
                (sp["g_kv_lora"], _whole(sp["g_kv_lora"]))]
    cq, ckv = _rowwise(lambda i, zq, zk, gq, gk: (_rms(zq, gq), _rms(zk, gk)), lora_ins,
                       [(SDS((t, ql), BF16), _rows(ql)), (SDS((t, ql), BF16), _rows(ql))], [],
                       name="lora_norm_fwd", n_rows=t)
    q = _mm(cq, wts["w_uq"], mode="nn", out_dtype=F32, name="mm_q")
    kv = _mm(ckv, wts["w_ukv"], mode="nn", out_dtype=F32, name="mm_kv")
    q_hm, k_hm, v_hm = _attn_prep(q, kv, z_l, kr_col, cos, sin, n_heads)
    o, lse = _attn_fwd(q_hm, k_hm, v_hm)
    ya = _mm(o, wts["w_attn_out"], mode="nn", out_dtype=F32, name="mm_attn_out")

    mix_ins = [(yc, _rows(d)), (sp["b_conv_out"], _whole(sp["b_conv_out"])), (ya, _rows(d)), (z_gate, _rows(d, 0)),
               (z_gate, _rows(d, 1)), (b_gate_c, _whole(b_gate_c)), (b_gate_a, _whole(b_gate_a))]
    (mix,) = _rowwise(lambda i, *a: _mix(*a), mix_ins, [(SDS((t, d), BF16), _rows(d))], [], name="mix_fwd", n_rows=t)
    h1 = _mm(mix, wts["w_out"], mode="nn", out_dtype=F32, name="mm_out", add=h0)

    hn_ = _rms_fwd(h1, sp["g_ffn"], "rms_ffn")
    wts.update(fetch("ffn", hn_))
    ab = _mm(hn_, wts["w_gu"], mode="nn", out_dtype=F32, name="mm_ffn_in")
    tr_ffn = 64
    (f,) = _rowwise(lambda i, a, b: _swiglu(a, b), [(ab, _rows(dff, 0, tr_ffn)), (ab, _rows(dff, 1, tr_ffn))],
                    [(SDS((t, dff), BF16), _rows(dff, 0, tr_ffn))], [], name="swiglu_fwd", n_rows=t, tr=tr_ffn)
    h2 = _mm(f, wts["w_ffn_down"], mode="nn", out_dtype=F32, name="mm_ffn_down", add=h1)

    def head(i, h, g, tgt):
        y, vjp = jax.vjp(_rms, h, g)
        row = i * ROW_TILE + lax.broadcasted_iota(jnp.int32, (ROW_TILE, 1), 0)
        valid = jnp.logical_and(row >= N_META, row < length)
        err = jnp.where(valid, y - tgt, 0.0)
        dh, dg = vjp(err / d)
        loss = 0.5 * jnp.sum(jnp.sum(err * err, axis=-1, keepdims=True), axis=0, keepdims=True) / d
        return dh, dh, dg, jnp.broadcast_to(loss, (1, LANES))

    dh2, dh2_b, g_final, loss_v = _rowwise(
        head, [(h2, _rows(d)), (sp["g_final"], _whole(sp["g_final"])), (target_p, _rows(d))],
        [(SDS((t, d), F32), _rows(d)), (SDS((t, d), BF16), _rows(d))], [SDS((1, d), F32), SDS((1, LANES), F32)],
        name="loss_head", n_rows=t)
    loss = loss_v[0, 0]

    g_ffn_down = _mm(f, dh2_b, mode="tn", out_dtype=BF16, name="mm_g_ffn_down")
    df = _mm(dh2_b, wts["w_ffn_down"], mode="nt", out_dtype=F32, name="mm_d_f")

    def swiglu_bwd(i, a, b, df):
        _, vjp = jax.vjp(_swiglu, a, b)
        da, db = vjp(df)
        return jnp.concatenate([da, db], axis=-1)

    (dab,) = _rowwise(swiglu_bwd, [(ab, _rows(dff, 0, tr_ffn)), (ab, _rows(dff, 1, tr_ffn)), (df, _rows(dff, 0, tr_ffn))],
                      [(SDS((t, 2 * dff), BF16), _rows(2 * dff, 0, tr_ffn))], [], name="swiglu_bwd", n_rows=t, tr=tr_ffn)
    g_gu = _mm(hn_, dab, mode="tn", out_dtype=BF16, name="mm_g_ffn_in")
    dhn = _mm(dab, wts["w_gu"], mode="nt", out_dtype=F32, name="mm_d_hn")
    sent = ship("ffn", {"w_gu": g_gu, "w_ffn_down": g_ffn_down})
    dh1, dh1_b, g_g_ffn = _rms_bwd(h1, sp["g_ffn"] + sent[0, 0], dhn, dh2, "rms_ffn_bwd")

    g_w_out = _mm(mix, dh1_b, mode="tn", out_dtype=BF16, name="mm_g_out")
    dmix = _mm(dh1_b, wts["w_out"], mode="nt", out_dtype=F32, name="mm_d_mix")

    def mix_bwd(i, yc, bco, ya, zc, za, bgc, bga, dmix):
        _, vjp = jax.vjp(_mix, yc, bco, ya, zc, za, bgc, bga)
        dyc, dbco, dya, dzc, dza, dbgc, dbga = vjp(dmix)
        return dyc, dya, jnp.concatenate([dzc, dza], axis=-1), dbco, dbgc, dbga

    dyc, dya, dz_gate, g_b_conv_out, g_bgc, g_bga = _rowwise(
        mix_bwd, mix_ins + [(dmix, _rows(d))],
        [(SDS((t, d), BF16), _rows(d)), (SDS((t, d), BF16), _rows(d)), (SDS((t, 2 * d), BF16), _rows(2 * d))],
        [SDS((1, d), F32)] * 3, name="mix_bwd", n_rows=t)

    g_attn_out = _mm(o, dya, mode="tn", out_dtype=BF16, name="mm_g_attn_out")
    d_o = _mm(dya, wts["w_attn_out"], mode="nt", out_dtype=BF16, name="mm_d_o")
    dq_hm, dk_hm, dv_hm = _attn_bwd(q_hm, k_hm, v_hm, lse, d_o)
    dq, dkv, dkr = _attn_post(dq_hm, dk_hm, dv_hm, cos, sin)
    g_uq = _mm(cq, dq, mode="tn", out_dtype=BF16, name="mm_g_uq")
    g_ukv = _mm(ckv, dkv, mode="tn", out_dtype=BF16, name="mm_g_ukv")
    dcq = _mm(dq, wts["w_uq"], mode="nt", out_dtype=F32, name="mm_d_cq")
    dckv = _mm(dkv, wts["w_ukv"], mode="nt", out_dtype=F32, name="mm_d_ckv")

    def lora_bwd(i, zq, zk, gq, gk, dcq, dckv, dkr):
        _, vq = jax.vjp(_rms, zq, gq)
        _, vk = jax.vjp(_rms, zk, gk)
        dzq, dgq = vq(dcq)
        dzk, dgk = vk(dckv)
        return jnp.concatenate([dzq, dzk, dkr], axis=-1), dgq, dgk

    dz_l, g_g_q, g_g_kv = _rowwise(
        lora_bwd, lora_ins + [(dcq, _rows(ql)), (dckv, _rows(ql)), (dkr, _rows(LANES))],
        [(SDS((t, 2 * ql + LANES), BF16), _rows(2 * ql + LANES))], [SDS((1, ql), F32)] * 2,
        name="lora_norm_bwd", n_rows=t)

    g_conv_out = _mm(c3, dyc, mode="tn", out_dtype=BF16, name="mm_g_conv_out")
    dc3 = _mm(dyc, wts["w_conv_out"], mode="nt", out_dtype=F32, name="mm_d_c3")
    sent = ship("mixers", {"w_conv_out": g_conv_out, "w_uq": g_uq, "w_ukv": g_ukv, "w_attn_out": g_attn_out,
                           "w_out": g_w_out})

    def ln_bwd(i, c, g, b, dc3):
        _, vjp = jax.vjp(_ln_silu, c, g, b)
        return vjp(dc3)

    g_ln_sent = sp["g_conv_ln"] + sent[0, 0]
    dc1, g_g_ln, g_b_ln = _rowwise(
        ln_bwd, [ln_ins[0], (g_ln_sent, _whole(g_ln_sent)), ln_ins[2], (dc3, _rows(ch))],
        [(SDS((t, ch), F32), _rows(ch))], [SDS((1, ch), F32)] * 2, name="ln_silu_bwd", n_rows=t)
    dc0, g_w_dw, g_b_dw = _conv_bwd(dc1, c0, w_dw)

    def glu_bwd(i, za, zb, ba, bb, dc0):
        _, vjp = jax.vjp(_glu, za, zb, ba, bb)
        dza, dzb, dba, dbb = vjp(dc0)
        return jnp.concatenate([dza, dzb], axis=-1), dba, dbb

    dz_glu, g_bga_, g_bgb_ = _rowwise(glu_bwd, glu_ins + [(dc0, _rows(ch))],
                                      [(SDS((t, 2 * ch), BF16), _rows(2 * ch))], [SDS((1, ch), F32)] * 2,
                                      name="glu_bwd", n_rows=t)

    du = _mm(dz_glu, wts["w_glu"], mode="nt", out_dtype=F32, name="mm_d_u0")
    du = _mm(dz_l, wts["w_lora"], mode="nt", out_dtype=F32, name="mm_d_u1", add=du)
    du = _mm(dz_gate, wts["w_gate"], mode="nt", out_dtype=F32, name="mm_d_u2", add=du)
    dh0, _, g_g_mix = _rms_bwd(h0, sp["g_mix"], du, dh1, "rms_mix_bwd")
    big = {"meta_tokens": dh0[:N_META], "w_dw": g_w_dw}
    small = {
        "g_mix": g_g_mix, "b_glu": jnp.concatenate([g_bga_, g_bgb_], axis=1),
        "b_gate": jnp.concatenate([g_bgc, g_bga], axis=1), "b_dw": g_b_dw, "g_conv_ln": g_g_ln, "b_conv_ln": g_b_ln,
        "b_conv_out": g_b_conv_out, "g_q_lora": g_g_q, "g_kv_lora": g_g_kv, "g_ffn": g_g_ffn, "g_final": g_final,
    }
    sent = ship("small", {**big, **small})
    g_glu = _mm(u, dz_glu, mode="tn", out_dtype=BF16, name="mm_g_w_glu", after=sent)
    g_lora = _mm(u, dz_l, mode="tn", out_dtype=BF16, name="mm_g_w_lora", after=sent)
    g_gate = _mm(u, dz_gate, mode="tn", out_dtype=BF16, name="mm_g_w_gate", after=sent)
    big.update({"w_glu": g_glu, "w_lora": g_lora, "w_gate": g_gate})
    ship("input", big)
    return loss, dh0[N_META:length], big, small


def _slot(ref, dev):
    return ref.at[dev]


def _row_window(rows):
    return lambda ref, dev: ref.at[pl.ds(pl.multiple_of(dev * rows, 16), rows)]


def _col_window(width, offset=0):
    return lambda ref, dev: ref.at[:, pl.ds(pl.multiple_of(offset + dev * width, LANES), width)]


def _dev_index(x, y, c):
    return 4 * x + 2 * y + c


def _gather_weights(items, out_shapes):
    srcs = [it[0] for it in items]
    n, n_out = len(srcs), len(out_shapes)

    def body(*refs):
        src, out = refs[:n], refs[n:n + n_out]
        send_sems, recv_sems, local_sems = refs[n + n_out:]
        x, y, c = lax.axis_index("x"), lax.axis_index("y"), lax.axis_index("c")
        me, sibling = (x, y, c), (x, y, 1 - c)
        chips = [(1 - x, y), (x, 1 - y), (1 - x, 1 - y)]

        def place(i, block):
            _, o, window = items[i]
            return window(out[o], _dev_index(*block))

        def copy(k, i, block, to, from_src=False):
            return pltpu.make_async_remote_copy(
                src_ref=src[i] if from_src else place(i, block), dst_ref=place(i, block),
                send_sem=send_sems.at[k * n + i], recv_sem=recv_sems.at[k * n + i],
                device_id=to, device_id_type=pl.DeviceIdType.MESH)

        mine = [pltpu.make_async_copy(src[i], place(i, me), local_sems.at[i]) for i in range(n)]
        first = [copy(0, i, me, sibling, True) for i in range(n)]
        first += [copy(1 + j, i, me, (*chip, c), True) for j, chip in enumerate(chips) for i in range(n)]
        for cp in mine + first:
            cp.start()
        passed = [[copy(4 + j, i, (*chip, c), sibling) for i in range(n)] for j, chip in enumerate(chips)]
        for j, chip in enumerate(chips):
            for i in range(n):
                copy(1 + j, i, (*chip, c), me).wait_recv()
            for cp in passed[j]:
                cp.start()
        for i in range(n):
            copy(0, i, sibling, me).wait_recv()
        for j, chip in enumerate(chips):
            for i in range(n):
                copy(4 + j, i, (*chip, 1 - c), me).wait_recv()
        for cp in first + [cp for row in passed for cp in row]:
            cp.wait_send()
        for cp in mine:
            cp.wait()

    any_spec = pl.BlockSpec(memory_space=pl.ANY)
    return pl.pallas_call(
        body, name="gather_weights", in_specs=[any_spec] * n, out_specs=[any_spec] * n_out, out_shape=out_shapes,
        scratch_shapes=[pltpu.SemaphoreType.DMA((7 * n,)), pltpu.SemaphoreType.DMA((7 * n,)),
                        pltpu.SemaphoreType.DMA((n,))],
    )(*srcs)


def _exchange(srcs, out_shapes, items, *, masks, name):
    n_src, n_out, n = len(srcs), len(out_shapes), len(items)

    def body(*refs):
        src, out = refs[:n_src], refs[n_src:n_src + n_out]
        send_sems, recv_sems = refs[n_src + n_out:]
        remote = _peer_copies(src, out, send_sems, recv_sems, items, masks)
        for cp in remote:
            cp.start()
        for cp in remote:
            cp.wait()

    any_spec = pl.BlockSpec(memory_space=pl.ANY)
    return pl.pallas_call(
        body, name=name, in_specs=[any_spec] * n_src, out_specs=[any_spec] * n_out, out_shape=out_shapes,
        scratch_shapes=[pltpu.SemaphoreType.DMA((len(masks) * n,)), pltpu.SemaphoreType.DMA((len(masks) * n,))],
    )(*srcs)


def _peer_copies(src, out, send_sems, recv_sems, items, masks):
    x, y, c = lax.axis_index("x"), lax.axis_index("y"), lax.axis_index("c")
    me = _dev_index(x, y, c)
    n = len(items)
    copies = []
    for k, mask in enumerate(masks):
        px, py, pc = x ^ ((mask >> 2) & 1), y ^ ((mask >> 1) & 1), c ^ (mask & 1)
        peer = _dev_index(px, py, pc)
        for i, (s, s_win, o, d_win) in enumerate(items):
            copies.append(pltpu.make_async_remote_copy(
                src_ref=s_win(src[s], peer), dst_ref=d_win(out[o], me),
                send_sem=send_sems.at[k * n + i], recv_sem=recv_sems.at[k * n + i],
                device_id=(px, py, pc), device_id_type=pl.DeviceIdType.MESH))
    return copies


def _exchange_start(srcs, out_shapes, items, *, masks, name):
    n_src, n_out, n = len(srcs), len(out_shapes), len(items)
    n_sem = len(masks) * n
    n_buf = n_src + n_out

    def body(*refs):
        src, land = refs[:n_src], refs[n_src:n_buf]
        send_sems, recv_sems = refs[n_buf], refs[n_buf + 1]
        token = refs[-1]
        for cp in _peer_copies(src, land, send_sems, recv_sems, items, masks):
            cp.start()
        token[...] = jnp.zeros_like(token)

    hbm = pl.BlockSpec(memory_space=pltpu.HBM)
    sem = pl.BlockSpec(memory_space=pltpu.SEMAPHORE)
    bufs = [pltpu.with_memory_space_constraint(a, pltpu.HBM) for a in srcs]
    bufs += [pltpu.with_memory_space_constraint(lax.empty(s.shape, s.dtype), pltpu.HBM) for s in out_shapes]
    res = pl.pallas_call(
        body, name=name,
        out_shape=(pltpu.SemaphoreType.DMA((n_sem,)), pltpu.SemaphoreType.DMA((n_sem,)),
                   *[pltpu.HBM(b.shape, b.dtype) for b in bufs], SDS((8, LANES), F32)),
        in_specs=[hbm] * n_buf, out_specs=(sem, sem, *[hbm] * n_buf, pl.BlockSpec(memory_space=pltpu.VMEM)),
        input_output_aliases={i: 2 + i for i in range(n_buf)},
        compiler_params=pltpu.CompilerParams(has_side_effects=pltpu.SideEffectType.DATAFLOW_SIDE_EFFECTING),
    )(*bufs)
    return res[0], res[1], list(res[2:2 + n_src]), list(res[2 + n_src:2 + n_buf]), res[-1]


def _exchange_wait(send_sems, recv_sems, srcs, lands, items, after, *, masks, name):
    n_src, n_out = len(srcs), len(lands)
    n_buf = n_src + n_out

    def body(*refs):
        src, land = refs[:n_src], refs[n_src:n_buf]
        send_sems, recv_sems = refs[n_buf], refs[n_buf + 1]
        for cp in _peer_copies(src, land, send_sems, recv_sems, items, masks):
            cp.wait_send()
            cp.wait_recv()

    hbm = pl.BlockSpec(memory_space=pltpu.HBM)
    sem = pl.BlockSpec(memory_space=pltpu.SEMAPHORE)
    bufs = list(srcs) + list(lands)
    res = pl.pallas_call(
        body, name=name, out_shape=tuple(pltpu.HBM(b.shape, b.dtype) for b in bufs),
        in_specs=[hbm] * n_buf + [sem, sem, pl.BlockSpec(memory_space=pl.ANY)], out_specs=tuple([hbm] * n_buf),
        input_output_aliases={i: i for i in range(n_buf)},
        compiler_params=pltpu.CompilerParams(has_side_effects=pltpu.SideEffectType.DATAFLOW_SIDE_EFFECTING),
    )(*bufs, send_sems, recv_sems, after)
    return list(res[:n_src]), list(res[n_src:])


def _regroup(srcs, dsts, *, name, tr=256, after=()):
    def segments(shape, valid):
        if len(shape) == 3:
            return [(p, shape[2]) for p in range(shape[0])]
        return [(None, valid)]

    src_arrays = [s[0] if isinstance(s, tuple) else s for s in srcs]
    src_valid = [s[1] if isinstance(s, tuple) else s.shape[-1] for s in srcs]
    k_rows = src_arrays[0].shape[-2]
    src_segs = [(i, p, w) for i, a in enumerate(src_arrays) for p, w in segments(a.shape, src_valid[i])]
    dst_segs = [(j, p, w) for j, (shape, _, valid) in enumerate(dsts) for p, w in segments(shape, valid)]
    pieces = []
    si, so, di, do = 0, 0, 0, 0
    while si < len(src_segs) and di < len(dst_segs):
        n = min(src_segs[si][2] - so, dst_segs[di][2] - do)
        pieces.append((src_segs[si][0], src_segs[si][1], so, dst_segs[di][0], dst_segs[di][1], do, n))
        so, do = so + n, do + n
        if so == src_segs[si][2]:
            si, so = si + 1, 0
        if do == dst_segs[di][2]:
            di, do = di + 1, 0
    assert si == len(src_segs) and di == len(dst_segs), "source and destination columns differ in number"
    n_src = len(src_arrays)

    def body(*refs):
        src, dst = refs[:n_src], refs[n_src + len(after):]
        for j, (shape, dtype, valid) in enumerate(dsts):
            if len(shape) == 2 and valid < shape[1]:
                dst[j][:, valid:shape[1]] = jnp.zeros((tr, shape[1] - valid), dtype)
        for i, sp, so, j, dp, do, n in pieces:
            val = src[i][:, so:so + n] if sp is None else src[i][sp, :, so:so + n]
            if dp is None:
                dst[j][:, do:do + n] = val.astype(dst[j].dtype)
            else:
                dst[j][dp, :, do:do + n] = val.astype(dst[j].dtype)

    def spec(shape):
        if len(shape) == 3:
            return pl.BlockSpec((shape[0], tr, shape[2]), lambda i: (0, i, 0))
        return pl.BlockSpec((tr, shape[1]), lambda i: (i, 0))

    n_after = len(after)
    return pl.pallas_call(
        body, name=name, grid=(k_rows // tr,),
        in_specs=[spec(a.shape) for a in src_arrays] + [pl.BlockSpec(memory_space=pl.ANY)] * n_after,
        out_specs=[spec(shape) for shape, _, _ in dsts], out_shape=[SDS(shape, dtype) for shape, dtype, _ in dsts],
        compiler_params=_params(),
    )(*src_arrays, *after)


def _cast_bf16(a, name, after=None):
    r, c = a.shape
    tr = _row_tile(r, c * 4)
    (out,) = _rowwise(lambda i, v: v, [(a, _rows(c, 0, tr))], [(SDS((r, c), BF16), _rows(c, 0, tr))], [], name=name,
                      n_rows=r, tr=tr, after=after)
    return out


def _own_spec(kind, tr, width, n_tiles):
    if kind[0] == "slot":
        return pl.BlockSpec((None, tr, width), lambda i, me: (me[0], i, 0))
    if kind[0] == "chip_slot":
        return pl.BlockSpec((None, tr, width), lambda i, me: (me[0] // 2, i, 0))
    if kind[0] == "rows":
        return pl.BlockSpec((tr, width), lambda i, me: (me[0] * n_tiles + i, 0))
    if kind[0] == "cols":
        return pl.BlockSpec((tr, width), lambda i, me: (i, kind[1] + me[0]))
    return pl.BlockSpec((tr, width), lambda i, me: (i, 0))


def _cast_into(me, a, whole, kind, name, into=None, after=None):
    r, c = a.shape
    tr = _row_tile(r, c * 4)
    extra = [x for x in (into, after) if x is not None]

    def body(me_ref, a_ref, *rest):
        rest[len(extra)][...] = a_ref[...].astype(BF16)

    return pl.pallas_call(
        body, name=name,
        grid_spec=pltpu.PrefetchScalarGridSpec(
            num_scalar_prefetch=1, grid=(r // tr,),
            in_specs=[pl.BlockSpec((tr, c), lambda i, me: (i, 0))] + [pl.BlockSpec(memory_space=pl.ANY)] * len(extra),
            out_specs=_own_spec(kind, tr, c, r // tr)),
        out_shape=whole, input_output_aliases={2: 0} if into is not None else {}, compiler_params=_params(),
    )(me, a, *extra)


def _own_block_copies(land, send_sems, recv_sems, items, masks):
    x, y, c = lax.axis_index("x"), lax.axis_index("y"), lax.axis_index("c")
    me = _dev_index(x, y, c)
    n = len(items)
    copies = []
    for k, mask in enumerate(masks):
        px, py, pc = x ^ ((mask >> 2) & 1), y ^ ((mask >> 1) & 1), c ^ (mask & 1)
        for i, (o, win) in enumerate(items):
            copies.append(pltpu.make_async_remote_copy(
                src_ref=win(land[o], me), dst_ref=win(land[o], me),
                send_sem=send_sems.at[k * n + i], recv_sem=recv_sems.at[k * n + i],
                device_id=(px, py, pc), device_id_type=pl.DeviceIdType.MESH))
    return copies


def _spread_start(lands, items, *, masks, name):
    n_buf, n_sem = len(lands), len(masks) * len(items)

    def body(*refs):
        land, send_sems, recv_sems, token = refs[:n_buf], refs[n_buf], refs[n_buf + 1], refs[-1]
        for cp in _own_block_copies(land, send_sems, recv_sems, items, masks):
            cp.start()
        token[...] = jnp.zeros_like(token)

    hbm = pl.BlockSpec(memory_space=pltpu.HBM)
    sem = pl.BlockSpec(memory_space=pltpu.SEMAPHORE)
    bufs = [pltpu.with_memory_space_constraint(a, pltpu.HBM) for a in lands]
    res = pl.pallas_call(
        body, name=name,
        out_shape=(pltpu.SemaphoreType.DMA((n_sem,)), pltpu.SemaphoreType.DMA((n_sem,)),
                   *[pltpu.HBM(b.shape, b.dtype) for b in bufs], SDS((8, LANES), F32)),
        in_specs=[hbm] * n_buf, out_specs=(sem, sem, *[hbm] * n_buf, pl.BlockSpec(memory_space=pltpu.VMEM)),
        input_output_aliases={i: 2 + i for i in range(n_buf)},
        compiler_params=pltpu.CompilerParams(has_side_effects=pltpu.SideEffectType.DATAFLOW_SIDE_EFFECTING),
    )(*bufs)
    return res[0], res[1], list(res[2:2 + n_buf]), res[-1]


def _swap_with_sibling(lands, items, name):
    n_buf, n = len(lands), len(items)

    def body(*refs):
        land, send_sems, recv_sems = refs[n_buf:2 * n_buf], refs[2 * n_buf], refs[2 * n_buf + 1]
        x, y, c = lax.axis_index("x"), lax.axis_index("y"), lax.axis_index("c")
        copies = []
        for j, (px, py) in enumerate([(0, 0), (0, 1), (1, 0), (1, 1)]):
            block = _dev_index(px, py, c)
            for i, (o, win) in enumerate(items):
                copies.append(pltpu.make_async_remote_copy(
                    src_ref=win(land[o], block), dst_ref=win(land[o], block),
                    send_sem=send_sems.at[j * n + i], recv_sem=recv_sems.at[j * n + i],
                    device_id=(x, y, 1 - c), device_id_type=pl.DeviceIdType.MESH))
        for cp in copies:
            cp.start()
        for cp in copies:
            cp.wait()

    any_spec = pl.BlockSpec(memory_space=pl.ANY)
    return pl.pallas_call(
        body, name=name, in_specs=[any_spec] * n_buf, out_specs=[any_spec] * n_buf,
        out_shape=[SDS(a.shape, a.dtype) for a in lands], input_output_aliases={i: i for i in range(n_buf)},
        scratch_shapes=[pltpu.SemaphoreType.DMA((4 * n,)), pltpu.SemaphoreType.DMA((4 * n,))],
    )(*lands)


def _spread_wait(send_sems, recv_sems, lands, items, after, *, masks, name):
    n_buf = len(lands)

    def body(*refs):
        land, send_sems, recv_sems = refs[:n_buf], refs[n_buf], refs[n_buf + 1]
        for cp in _own_block_copies(land, send_sems, recv_sems, items, masks):
            cp.wait_send()
            cp.wait_recv()

    hbm = pl.BlockSpec(memory_space=pltpu.HBM)
    sem = pl.BlockSpec(memory_space=pltpu.SEMAPHORE)
    res = pl.pallas_call(
        body, name=name, out_shape=tuple(pltpu.HBM(b.shape, b.dtype) for b in lands),
        in_specs=[hbm] * n_buf + [sem, sem, pl.BlockSpec(memory_space=pl.ANY)], out_specs=tuple([hbm] * n_buf),
        input_output_aliases={i: i for i in range(n_buf)},
        compiler_params=pltpu.CompilerParams(has_side_effects=pltpu.SideEffectType.DATAFLOW_SIDE_EFFECTING),
    )(*lands, send_sems, recv_sems, after)
    return list(res)


def _row_tile(rows, row_bytes, limit=2 * 1024 * 1024):
    best = None
    for t in range(16, rows + 1, 16):
        if rows % t == 0 and t * row_bytes <= limit:
            best = t
    return best if best is not None else rows


def _sum_with_sibling(me, slabs, from_sibling, name):
    _, k, c = slabs.shape
    tr = _row_tile(k, c * 4)

    def body(me_ref, a_ref, b_ref, o_ref):
        o_ref[...] = (a_ref[...].astype(F32) + b_ref[...].astype(F32)).astype(BF16)

    by_chip = pl.BlockSpec((None, tr, c), lambda j, i, me: (j, i, 0))
    return pl.pallas_call(
        body, name=name,
        grid_spec=pltpu.PrefetchScalarGridSpec(
            num_scalar_prefetch=1, grid=(N_DEV // 2, k // tr),
            in_specs=[pl.BlockSpec((None, tr, c), lambda j, i, me: (2 * j + me[0] % 2, i, 0)), by_chip],
            out_specs=by_chip),
        out_shape=SDS((N_DEV // 2, k, c), BF16), compiler_params=_params(),
    )(me, slabs, from_sibling)


def _adamw(me, recv, own, own_kind, w, m, v, *, name):
    n_rows, width = w.shape
    tr = _row_tile(n_rows, width * 4, limit=1024 * 1024)
    n_tiles = n_rows // tr
    own_spec = _own_spec(own_kind, tr, width, n_tiles)

    n_slots = recv.shape[0]

    def body(me_ref, r_ref, own_ref, w_ref, m_ref, v_ref, g_ref, d_ref, mo_ref, vo_ref):
        mine = own_ref[...].astype(F32)
        my_slot = me_ref[0] if n_slots == N_DEV else me_ref[0] // 2
        g = None
        for q in range(n_slots):
            term = jnp.where(my_slot == q, mine, r_ref[q].astype(F32))
            g = term if g is None else g + term
        m_new = ADAM_B1 * m_ref[...] + (1.0 - ADAM_B1) * g
        v_new = ADAM_B2 * v_ref[...] + (1.0 - ADAM_B2) * jnp.square(g)
        m_hat = m_new / (1.0 - ADAM_B1 ** ADAM_STEP)
        v_hat = v_new / (1.0 - ADAM_B2 ** ADAM_STEP)
        g_ref[...] = g
        d_ref[...] = -ADAM_LR * (m_hat / (jnp.sqrt(v_hat) + ADAM_EPS) + ADAM_WD * w_ref[...])
        mo_ref[...] = m_new
        vo_ref[...] = v_new

    row = pl.BlockSpec((tr, width), lambda i, me: (i, 0))
    return pl.pallas_call(
        body, name=name,
        grid_spec=pltpu.PrefetchScalarGridSpec(
            num_scalar_prefetch=1, grid=(n_tiles,),
            in_specs=[pl.BlockSpec((n_slots, tr, width), lambda i, me: (0, i, 0)), own_spec, row, row, row],
            out_specs=[row] * 4),
        out_shape=[SDS((n_rows, width), F32)] * 4, compiler_params=_params(),
    )(me, recv, own, w, m, v)


def _offsets(sizes):
    offs, o = [], 0
    for n in sizes:
        offs.append(o)
        o += n
    return offs, o


def kernel(x, meta_tokens, g_mix, w_in, b_glu, b_gate, w_dw, b_dw, g_conv_ln, b_conv_ln, w_conv_out, b_conv_out, g_q_lora, w_uq, g_kv_lora, w_uk, w_uv, w_attn_out, w_out, g_ffn, w_ffn_gate, w_ffn_up, w_ffn_down, g_final, loss_target, m_meta_tokens, m_g_mix, m_w_in, m_b_glu, m_b_gate, m_w_dw, m_b_dw, m_g_conv_ln, m_b_conv_ln, m_w_conv_out, m_b_conv_out, m_g_q_lora, m_w_uq, m_g_kv_lora, m_w_uk, m_w_uv, m_w_attn_out, m_w_out, m_g_ffn, m_w_ffn_gate, m_w_ffn_up, m_w_ffn_down, m_g_final, v_meta_tokens, v_g_mix, v_w_in, v_b_glu, v_b_gate, v_w_dw, v_b_dw, v_g_conv_ln, v_b_conv_ln, v_w_conv_out, v_b_conv_out, v_g_q_lora, v_w_uq, v_g_kv_lora, v_w_uk, v_w_uv, v_w_attn_out, v_w_out, v_g_ffn, v_w_ffn_gate, v_w_ffn_up, v_w_ffn_down, v_g_final):
    w_all = dict(meta_tokens=meta_tokens, g_mix=g_mix, w_in=w_in, b_glu=b_glu, b_gate=b_gate, w_dw=w_dw, b_dw=b_dw, g_conv_ln=g_conv_ln, b_conv_ln=b_conv_ln, w_conv_out=w_conv_out, b_conv_out=b_conv_out, g_q_lora=g_q_lora, w_uq=w_uq, g_kv_lora=g_kv_lora, w_uk=w_uk, w_uv=w_uv, w_attn_out=w_attn_out, w_out=w_out, g_ffn=g_ffn, w_ffn_gate=w_ffn_gate, w_ffn_up=w_ffn_up, w_ffn_down=w_ffn_down, g_final=g_final)
    m_all = dict(meta_tokens=m_meta_tokens, g_mix=m_g_mix, w_in=m_w_in, b_glu=m_b_glu, b_gate=m_b_gate, w_dw=m_w_dw, b_dw=m_b_dw, g_conv_ln=m_g_conv_ln, b_conv_ln=m_b_conv_ln, w_conv_out=m_w_conv_out, b_conv_out=m_b_conv_out, g_q_lora=m_g_q_lora, w_uq=m_w_uq, g_kv_lora=m_g_kv_lora, w_uk=m_w_uk, w_uv=m_w_uv, w_attn_out=m_w_attn_out, w_out=m_w_out, g_ffn=m_g_ffn, w_ffn_gate=m_w_ffn_gate, w_ffn_up=m_w_ffn_up, w_ffn_down=m_w_ffn_down, g_final=m_g_final)
    v_all = dict(meta_tokens=v_meta_tokens, g_mix=v_g_mix, w_in=v_w_in, b_glu=v_b_glu, b_gate=v_b_gate, w_dw=v_w_dw, b_dw=v_b_dw, g_conv_ln=v_g_conv_ln, b_conv_ln=v_b_conv_ln, w_conv_out=v_w_conv_out, b_conv_out=v_b_conv_out, g_q_lora=v_g_q_lora, w_uq=v_w_uq, g_kv_lora=v_g_kv_lora, w_uk=v_w_uk, w_uv=v_w_uv, w_attn_out=v_w_attn_out, w_out=v_w_out, g_ffn=v_g_ffn, w_ffn_gate=v_w_ffn_gate, w_ffn_up=v_w_ffn_up, w_ffn_down=v_w_ffn_down, g_final=v_g_final)

    two_d = lambda a: a.reshape(a.shape[-2:]) if a.ndim >= 2 else a.reshape(1, -1)
    sh = {n: two_d(w_all[n]) for n in SHARDED}
    d = x.shape[-1]
    k_in, c_in = sh["w_in"].shape
    r_co, r_ao, r_wo, r_fd = (sh[n].shape[0] for n in ROW_SHARDED)
    ql, c_uq = sh["w_uq"].shape
    c_uk = sh["w_uk"].shape[1]
    c_ff = sh["w_ffn_gate"].shape[1]
    n_meta, c_meta = sh["meta_tokens"].shape
    n_taps, c_dw = sh["w_dw"].shape
    ch, dff = N_DEV * c_dw, N_DEV * c_ff
    n_lora = 2 * ql + QK_ROPE

    masks = tuple(range(1, N_DEV))
    whole_ref = lambda ref, dev: ref
    gathered = _gather_weights(
        [(_cast_bf16(sh["w_in"], "cast_w_in"), 0, _slot), (sh["meta_tokens"], 1, _col_window(c_meta)),
         (sh["w_dw"], 2, _col_window(c_dw))],
        [SDS((N_DEV, k_in, c_in), BF16), SDS((n_meta, N_DEV * c_meta), F32), SDS((n_taps, ch), F32)])
    me = (4 * lax.axis_index("x") + 2 * lax.axis_index("y") + lax.axis_index("c")).astype(jnp.int32).reshape(1)

    def placed(n, whole, kind, after, into=None):
        return _cast_into(me, sh[n], whole, kind, "cast_" + n, into=into, after=None if into is not None else after)

    same_core = (2, 4, 6)
    coming = {}
    first = gathered[0]
    w_ukv0 = placed("w_uk", SDS((ql, 2 * N_DEV * c_uk), BF16), ("cols", 0), first)
    coming["mixers"] = _spread_start(
        [placed("w_conv_out", SDS((N_DEV * r_co, d), BF16), ("rows",), first),
         placed("w_uq", SDS((ql, N_DEV * c_uq), BF16), ("cols", 0), first),
         placed("w_uv", SDS((ql, 2 * N_DEV * c_uk), BF16), ("cols", N_DEV), first, into=w_ukv0),
         placed("w_attn_out", SDS((N_DEV * r_ao, d), BF16), ("rows",), first),
         placed("w_out", SDS((N_DEV * r_wo, d), BF16), ("rows",), first)],
        mixer_items := [(0, _row_window(r_co)), (1, _col_window(c_uq)), (2, _col_window(c_uk)),
                        (2, _col_window(c_uk, N_DEV * c_uk)), (3, _row_window(r_ao)), (4, _row_window(r_wo))],
        masks=same_core, name="send_weights_mixers") + (mixer_items,)
    second = coming["mixers"][3]
    coming["ffn"] = _spread_start(
        [placed("w_ffn_gate", SDS((N_DEV, d, c_ff), BF16), ("slot",), second),
         placed("w_ffn_up", SDS((N_DEV, d, c_ff), BF16), ("slot",), second),
         placed("w_ffn_down", SDS((N_DEV * r_fd, d), BF16), ("rows",), second)],
        ffn_items := [(0, _slot), (1, _slot), (2, _row_window(r_fd))],
        masks=same_core, name="send_weights_ffn") + (ffn_items,)

    def fetch(group, after):
        send_sems, recv_sems, lands, _, items = coming[group]
        lands = _spread_wait(send_sems, recv_sems, lands, items, after, masks=same_core,
                             name="wait_weights_" + group)
        lands = _swap_with_sibling(lands, items, "pass_weights_" + group)
        if group == "mixers":
            return {"w_conv_out": lands[0], "w_uq": lands[1], "w_ukv": lands[2], "w_attn_out": lands[3],
                    "w_out": lands[4]}
        (w_gu,) = _regroup([lands[0], lands[1]], [((d, 2 * dff), BF16, 2 * dff)], name="unpack_w_ffn_in")
        return {"w_gu": w_gu, "w_ffn_down": lands[2]}

    w_glu, w_lora, w_gate = _regroup(
        [gathered[0]], [((k_in, 2 * ch), BF16, 2 * ch), ((k_in, n_lora + LANES - QK_ROPE), BF16, n_lora),
                        ((k_in, 2 * d), BF16, 2 * d)], name="unpack_w_in",
        after=(coming["mixers"][3], coming["ffn"][3]))
    wts = {"w_glu": w_glu, "w_lora": w_lora, "w_gate": w_gate}
    sp = {n: w_all[n].reshape(1, -1) for n in SMALL}
    dims = {"n_heads": N_DEV * c_uq // QK, "dff": dff}

    recv_shape = lambda n: SDS((N_DEV,) + sh[n].shape, F32 if n in F32_GATHERED else BF16)
    in_flight = {}

    s_sizes = [w_all[n].size for n in SMALL]
    s_offs, n_s = _offsets(s_sizes)
    cat_small = lambda src: jnp.concatenate([src[n].reshape(1, -1) for n in SMALL], axis=1)
    small_names = ("meta_tokens", "w_dw")

    def ship(group, grads):
        if group == "small":
            srcs = [grads["meta_tokens"], grads["w_dw"], cat_small(grads)]
            items = [(0, _col_window(c_meta), 0, _slot), (1, _col_window(c_dw), 1, _slot), (2, whole_ref, 2, _slot)]
            send_sems, recv_sems, srcs, lands, zero = _exchange_start(
                srcs, [recv_shape(n) for n in small_names] + [SDS((N_DEV, 1, n_s), F32)], items, masks=masks,
                name="send_grads_small")
            in_flight[group] = (send_sems, recv_sems, srcs, lands, items)
            return zero
        if group == "input":
            (s_in,) = _regroup([grads["w_glu"], (grads["w_lora"], n_lora), grads["w_gate"]],
                               [((N_DEV, k_in, c_in), BF16, None)], name="pack_g_w_in")
            per_chip = SDS((N_DEV // 2, k_in, c_in), BF16)
            (halves,) = _exchange(
                [s_in], [per_chip],
                [(0, lambda ref, peer, j=j: ref.at[2 * j + peer % 2], 0, lambda ref, dev, j=j: ref.at[j])
                 for j in range(N_DEV // 2)], masks=(1,), name="swap_grads_input")
            chip_sums = _sum_with_sibling(me, s_in, halves, "sum_grads_input")
            send_sems, recv_sems, srcs, lands, zero = _exchange_start(
                [chip_sums], [per_chip],
                items := [(0, lambda ref, peer: ref.at[peer // 2], 0, lambda ref, dev: ref.at[dev // 2])],
                masks=(2, 4, 6), name="send_grads_input")
            in_flight[group] = (send_sems, recv_sems, srcs, lands, items, ("w_in",), [(0, ("chip_slot",))], zero,
                                (2, 4, 6))
            return zero
        if group == "ffn":
            s_gate, s_up = _regroup([grads["w_gu"]], [((N_DEV, d, c_ff), BF16, None)] * 2, name="pack_g_w_ffn_in")
            srcs = [s_gate, s_up, grads["w_ffn_down"]]
            names = ("w_ffn_gate", "w_ffn_up", "w_ffn_down")
            items = [(0, _slot, 0, _slot), (1, _slot, 1, _slot), (2, _row_window(r_fd), 2, _slot)]
            own = [(0, ("slot",)), (1, ("slot",)), (2, ("rows",))]
        else:
            srcs = [grads["w_conv_out"], grads["w_uq"], grads["w_ukv"], grads["w_attn_out"], grads["w_out"]]
            names = ("w_conv_out", "w_uq", "w_uk", "w_uv", "w_attn_out", "w_out")
            items = [(0, _row_window(r_co), 0, _slot), (1, _col_window(c_uq), 1, _slot),
                     (2, _col_window(c_uk), 2, _slot), (2, _col_window(c_uk, N_DEV * c_uk), 3, _slot),
                     (3, _row_window(r_ao), 4, _slot), (4, _row_window(r_wo), 5, _slot)]
            own = [(0, ("rows",)), (1, ("cols", 0)), (2, ("cols", 0)), (2, ("cols", N_DEV)), (3, ("rows",)),
                   (4, ("rows",))]
        send_sems, recv_sems, srcs, lands, zero = _exchange_start(
            srcs, [recv_shape(n) for n in names], items, masks=masks, name="send_grads_" + group)
        in_flight[group] = (send_sems, recv_sems, srcs, lands, items, names, own, zero, masks)
        return zero

    loss, grad_x, _, _ = _local_step(x[0], loss_target[0], gathered[1], gathered[2], wts, sp, ship, fetch, dims)

    by_name = {}

    def update(n, recv, own, kind):
        outs = _adamw(me, recv, own, kind, sh[n], two_d(m_all[n]), two_d(v_all[n]), name="adamw_" + n)
        by_name[n] = [o.reshape(w_all[n].shape) for o in outs]
        return outs[0]

    done = in_flight["input"][7]
    send_sems, recv_sems, srcs, lands, items = in_flight["small"]
    small_srcs, small_recv = _exchange_wait(send_sems, recv_sems, srcs, lands, items, done, masks=masks,
                                            name="wait_grads_small")
    for n, recv, src in zip(small_names, small_recv, small_srcs):
        update(n, recv, src, ("cols", 0))
    outs = _adamw(me, small_recv[2], small_srcs[2], ("whole",), cat_small(w_all), cat_small(m_all), cat_small(v_all),
                  name="adamw_replicated")
    for n, o, s in zip(SMALL, s_offs, s_sizes):
        by_name[n] = [out[:, o:o + s].reshape(w_all[n].shape) for out in outs]
    done = outs[0]
    for group in ("ffn", "mixers", "input"):
        send_sems, recv_sems, srcs, lands, items, names, own, _, group_masks = in_flight[group]
        srcs, lands = _exchange_wait(send_sems, recv_sems, srcs, lands, items, done, masks=group_masks,
                                     name="wait_grads_" + group)
        for n, land, (s, kind) in zip(names, lands, own):
            done = update(n, land, srcs[s], kind)
    result = [[by_name[n][k] for n in WEIGHTS] for k in range(4)]
    loss = lax.psum(loss, ("x", "y", "c"))
    return (loss, grad_x[None], *result[0], *result[1], *result[2], *result[3])
```
